```python
import math
import jax, jax.numpy as jnp
from jax import lax
import numpy as np

D_MODEL = 1024
BATCH = 8
SEQ = 2048
DEPTH = 2

GRID_W = 64
CTX_LEN = 256
ROPE_BASE = 10000.0
EPS = 1e-6

MLA_HEADS = 4
MLA_NOPE = 128
MLA_ROPE = 64
MLA_V = 128
MLA_QK = MLA_NOPE + MLA_ROPE
MLA_Q_RANK = 384
MLA_KV_RANK = 256
Q_BLOCK = 128
CONV_CH = 256
CONV_WIDTH = 31
SWA_HEADS = 4
SWA_KV_HEADS = 2
SWA_HEAD_DIM = 64
SWA_WINDOW = 128
SWA_BLOCK = 128
N_EXPERTS = 32
TOP_K = 4
D_EXPERT = D_MODEL
SWIGLU_LIMIT = 7.0
SWIGLU_ALPHA = 1.702
EXPERT_BLOCK = 256

O_CKV = 0
O_KROPE = O_CKV + MLA_KV_RANK
O_SK = O_KROPE + MLA_ROPE
O_SV = O_SK + SWA_KV_HEADS * SWA_HEAD_DIM
O_CQ = O_SV + SWA_KV_HEADS * SWA_HEAD_DIM
KV_COLS = O_CQ
O_SQ = O_CQ + MLA_Q_RANK
O_GLU = O_SQ + SWA_HEADS * SWA_HEAD_DIM
IN_COLS = O_GLU + 2 * CONV_CH
MIX_WIDTH = MLA_HEADS * MLA_V + CONV_CH + SWA_HEADS * SWA_HEAD_DIM

kernel_name = 'hybrid_mla_conformer_swa_moe_dit'


def rms_norm(x, g):
    xf = x.astype(jnp.float32)
    y = xf * lax.rsqrt(jnp.mean(xf * xf, axis=-1, keepdims=True) + EPS)
    return (y * g.astype(jnp.float32)).astype(x.dtype)


def layer_norm(x, g, b):
    xf = x.astype(jnp.float32)
    mu = jnp.mean(xf, axis=-1, keepdims=True)
    var = jnp.mean(jnp.square(xf - mu), axis=-1, keepdims=True)
    y = (xf - mu) * lax.rsqrt(var + EPS) * g.astype(jnp.float32) + b.astype(jnp.float32)
    return y.astype(x.dtype)


def modulate(h, shift, scale):
    return h * (1 + scale) + shift


def axial_rope_tables(n_rows, rot_dim):
    q = rot_dim // 4
    row = jnp.repeat(jnp.arange(n_rows, dtype=jnp.float32), GRID_W)
    col = jnp.tile(jnp.arange(GRID_W, dtype=jnp.float32), n_rows)
    inv = ROPE_BASE ** (-jnp.arange(q, dtype=jnp.float32) / q)
    ang = jnp.stack([row[:, None] * inv, col[:, None] * inv], axis=1)
    return jnp.cos(ang), jnp.sin(ang)


def apply_rope(x, cos, sin):
    q = x.shape[-1] // 4
    xr = x.astype(jnp.float32).reshape(x.shape[:-1] + (2, 2, q))
    x1, x2 = xr[..., 0, :], xr[..., 1, :]
    cs, sn = cos[:, None], sin[:, None]
    out = jnp.stack([x1 * cs - x2 * sn, x1 * sn + x2 * cs], axis=-2)
    return out.reshape(x.shape).astype(x.dtype)


def mla_keys(ckv, krope, kv_norm_g, w_ukv, k_head_g, rope):
    bsz, n, _ = ckv.shape
    kv = (rms_norm(ckv, kv_norm_g) @ w_ukv).reshape(bsz, n, MLA_HEADS, MLA_NOPE + MLA_V)
    k_nope, v = kv[..., :MLA_NOPE], kv[..., MLA_NOPE:]
    k_pe = jnp.broadcast_to(krope[:, :, None, :], (bsz, n, MLA_HEADS, MLA_ROPE))
    k = rms_norm(jnp.concatenate([k_nope, k_pe], axis=-1), k_head_g)
    if rope is not None:
        k = jnp.concatenate([k[..., :MLA_NOPE], apply_rope(k[..., MLA_NOPE:], *rope)], axis=-1)
    return k, v


def mla_queries(cq, q_norm_g, w_uq, q_head_g, rope):
    bsz, n, _ = cq.shape
    q = (rms_norm(cq, q_norm_g) @ w_uq).reshape(bsz, n, MLA_HEADS, MLA_QK)
    q = rms_norm(q, q_head_g)
    if rope is not None:
        q = jnp.concatenate([q[..., :MLA_NOPE], apply_rope(q[..., MLA_NOPE:], *rope)], axis=-1)
    return q


def mla_latent_attention(q, k_lat, v_lat, k_ctx, v_ctx):
    bsz, n, nh, dq = q.shape
    k = jnp.concatenate([k_ctx, k_lat], axis=1)
    v = jnp.concatenate([v_ctx, v_lat], axis=1)
    nb = n // Q_BLOCK
    qb = q.reshape(bsz, nb, Q_BLOCK, nh, dq).transpose(1, 0, 2, 3, 4)
    scale = MLA_QK ** -0.5

    def block(qi):
        s = jnp.einsum('bqhd,bkhd->bhqk', qi, k, preferred_element_type=jnp.float32) * scale
        p = jax.nn.softmax(s, axis=-1).astype(v.dtype)
        return jnp.einsum('bhqk,bkhd->bqhd', p, v)

    out = lax.map(block, qb)
    return out.transpose(1, 0, 2, 3, 4).reshape(bsz, n, nh * MLA_V)


def mla_ctx_attention(q, k, v):
    bsz, n, nh, _ = q.shape
    s = jnp.einsum('bqhd,bkhd->bhqk', q, k, preferred_element_type=jnp.float32) * (MLA_QK ** -0.5)
    p = jax.nn.softmax(s, axis=-1).astype(v.dtype)
    return jnp.einsum('bhqk,bkhd->bqhd', p, v).reshape(bsz, n, nh * MLA_V)


def swa_keys(sk, sv, k_g, rope):
    bsz, n, _ = sk.shape
    k = rms_norm(sk.reshape(bsz, n, SWA_KV_HEADS, SWA_HEAD_DIM), k_g)
    if rope is not None:
        k = apply_rope(k, *rope)
    return k, sv.reshape(bsz, n, SWA_KV_HEADS, SWA_HEAD_DIM)


def swa_queries(sq, q_g, rope):
    bsz, n, _ = sq.shape
    q = rms_norm(sq.reshape(bsz, n, SWA_HEADS, SWA_HEAD_DIM), q_g)
    if rope is not None:
        q = apply_rope(q, *rope)
    return q


def swa_latent_attention(q, k, v, k_ctx, v_ctx, sink):
    bsz, n, _, dh = q.shape
    grp = SWA_HEADS // SWA_KV_HEADS
    nb = n // SWA_BLOCK
    span = SWA_BLOCK + 2 * SWA_WINDOW
    pad = ((0, 0), (SWA_WINDOW, SWA_WINDOW), (0, 0), (0, 0))
    kp, vp = jnp.pad(k, pad), jnp.pad(v, pad)
    idx = jnp.arange(nb)[:, None] * SWA_BLOCK + jnp.arange(span)[None, :]
    kb, vb = kp[:, idx], vp[:, idx]
    qb = q.reshape(bsz, nb, SWA_BLOCK, SWA_KV_HEADS, grp, dh)
    scale = dh ** -0.5
    s_loc = jnp.einsum('bnqhgd,bnkhd->bnhgqk', qb, kb, preferred_element_type=jnp.float32) * scale
    qpos = jnp.arange(nb)[:, None] * SWA_BLOCK + jnp.arange(SWA_BLOCK)[None, :]
    kpos = idx - SWA_WINDOW
    rel = kpos[:, None, :] - qpos[:, :, None]
    valid = (jnp.abs(rel) <= SWA_WINDOW) & (kpos[:, None, :] >= 0) & (kpos[:, None, :] < n)
    s_loc = jnp.where(valid[None, :, None, None], s_loc, -jnp.inf)
    s_ctx = jnp.einsum('bnqhgd,bchd->bnhgqc', qb, k_ctx, preferred_element_type=jnp.float32) * scale
    s_sink = jnp.broadcast_to(sink.astype(jnp.float32).reshape(SWA_KV_HEADS, grp)[None, None, :, :, None, None], s_loc.shape[:-1] + (1,))
    p = jax.nn.softmax(jnp.concatenate([s_loc, s_ctx, s_sink], axis=-1), axis=-1).astype(v.dtype)
    n_c = k_ctx.shape[1]
    out = jnp.einsum('bnhgqk,bnkhd->bnqhgd', p[..., :span], vb) + jnp.einsum('bnhgqc,bchd->bnqhgd', p[..., span:span + n_c], v_ctx)
    return out.reshape(bsz, n, SWA_HEADS * dh)


def swa_ctx_attention(q, k, v, sink):
    bsz, n, _, dh = q.shape
    grp = SWA_HEADS // SWA_KV_HEADS
    qg = q.reshape(bsz, n, SWA_KV_HEADS, grp, dh)
    s = jnp.einsum('bqhgd,bkhd->bhgqk', qg, k, preferred_element_type=jnp.float32) * (dh ** -0.5)
    s_sink = jnp.broadcast_to(sink.astype(jnp.float32).reshape(SWA_KV_HEADS, grp)[None, :, :, None, None], s.shape[:-1] + (1,))
    p = jax.nn.softmax(jnp.concatenate([s, s_sink], axis=-1), axis=-1)[..., :n].astype(v.dtype)
    return jnp.einsum('bhgqk,bkhd->bqhgd', p, v).reshape(bsz, n, SWA_HEADS * dh)


def conformer_conv(glu_in, w, b, ln_g, ln_b):
    a, gate = glu_in[..., :CONV_CH], glu_in[..., CONV_CH:]
    u = a * jax.nn.sigmoid(gate)
    y = lax.conv_general_dilated(u, w[:, None, :], window_strides=(1,), padding=[(CONV_WIDTH // 2, CONV_WIDTH // 2)], dimension_numbers=('NWC', 'WIO', 'NWC'), feature_group_count=CONV_CH) + b
    return jax.nn.silu(layer_norm(y, ln_g, ln_b))


def moe(h, w_r, b_r, w1, b1, w2, b2):
    n_tok, dm = h.shape
    logits = (h @ w_r + b_r).astype(jnp.float32)
    top_v, top_i = lax.top_k(logits, TOP_K)
    gates = jax.nn.softmax(top_v, axis=-1)
    nk = n_tok * TOP_K
    e_flat = top_i.reshape(-1)
    tok_flat = jnp.arange(nk, dtype=jnp.int32) // TOP_K
    g_flat = gates.reshape(-1)
    order = jnp.argsort(e_flat)
    se, st, sg = e_flat[order], tok_flat[order], g_flat[order]
    counts = jax.ops.segment_sum(jnp.ones_like(e_flat), e_flat, num_segments=N_EXPERTS)
    padded = (counts + EXPERT_BLOCK - 1) // EXPERT_BLOCK * EXPERT_BLOCK
    start = jnp.cumsum(counts) - counts
    pend = jnp.cumsum(padded)
    pstart = pend - padded
    dest = pstart[se] + jnp.arange(nk, dtype=jnp.int32) - start[se]
    n_buf = -(-nk // EXPERT_BLOCK) * EXPERT_BLOCK + N_EXPERTS * EXPERT_BLOCK
    nb = n_buf // EXPERT_BLOCK
    row_tok = jnp.full((n_buf,), n_tok, jnp.int32).at[dest].set(st)
    row_w = jnp.zeros((n_buf,), jnp.float32).at[dest].set(sg)
    block_e = jnp.minimum(jnp.searchsorted(pend, jnp.arange(nb, dtype=jnp.int32) * EXPERT_BLOCK, side='right'), N_EXPERTS - 1)
    h_pad = jnp.concatenate([h, jnp.zeros((1, dm), h.dtype)], axis=0)
    xb = h_pad[row_tok].reshape(nb, EXPERT_BLOCK, dm)

    def expert_block(args):
        xi, e = args
        u = xi @ w1[e] + b1[e]
        x_glu = jnp.minimum(u[..., ::2], SWIGLU_LIMIT)
        x_lin = jnp.clip(u[..., 1::2], -SWIGLU_LIMIT, SWIGLU_LIMIT)
        act = x_glu * jax.nn.sigmoid(SWIGLU_ALPHA * x_glu) * (x_lin + 1)
        return act @ w2[e] + b2[e]

    yb = lax.map(expert_block, (xb, block_e)).reshape(n_buf, dm)
    y = jax.ops.segment_sum(yb * row_w[:, None].astype(yb.dtype), row_tok, num_segments=n_tok + 1)
    return y[:n_tok]


def setup_inputs(seed: int = 0) -> dict:
    key = jax.random.key(seed)
    ks = jax.random.split(key, 32)
    f32 = jnp.float32
    L = DEPTH

    def nrm(k, shape, s):
        return jax.random.normal(k, shape, f32) * s

    def gain(k, shape):
        return 1.0 + 0.02 * jax.random.normal(k, shape, f32)

    return {
        'x': nrm(ks[0], (BATCH, SEQ, D_MODEL), 1.0),
        'c': nrm(ks[1], (BATCH, D_MODEL), 1.0),
        'ctx': nrm(ks[2], (BATCH, CTX_LEN, D_MODEL), 1.0),
        'c_ctx': nrm(ks[3], (D_MODEL,), 1.0),
        'norm1_g': gain(ks[4], (L, D_MODEL)),
        'norm2_g': gain(ks[5], (L, D_MODEL)),
        'w_ada': nrm(ks[6], (L, D_MODEL, 6 * D_MODEL), 0.5 * D_MODEL ** -0.5),
        'b_ada': nrm(ks[7], (L, 6 * D_MODEL), 0.02),
        'w_in': nrm(ks[8], (L, D_MODEL, IN_COLS), D_MODEL ** -0.5),
        'mla_q_norm': gain(ks[9], (L, MLA_Q_RANK)),
        'mla_kv_norm': gain(ks[10], (L, MLA_KV_RANK)),
        'mla_w_uq': nrm(ks[11], (L, MLA_Q_RANK, MLA_HEADS * MLA_QK), MLA_Q_RANK ** -0.5),
        'mla_w_ukv': nrm(ks[12], (L, MLA_KV_RANK, MLA_HEADS * (MLA_NOPE + MLA_V)), MLA_KV_RANK ** -0.5),
        'mla_q_head_norm': gain(ks[13], (L, MLA_QK)),
        'mla_k_head_norm': gain(ks[14], (L, MLA_QK)),
        'conv_w': nrm(ks[15], (L, CONV_WIDTH, CONV_CH), CONV_WIDTH ** -0.5),
        'conv_b': nrm(ks[16], (L, CONV_CH), 0.02),
        'conv_ln_g': gain(ks[17], (L, CONV_CH)),
        'conv_ln_b': nrm(ks[18], (L, CONV_CH), 0.02),
        'swa_q_norm': gain(ks[19], (L, SWA_HEAD_DIM)),
        'swa_k_norm': gain(ks[20], (L, SWA_HEAD_DIM)),
        'swa_sink': nrm(ks[21], (L, SWA_HEADS), 0.5),
        'w_out': nrm(ks[22], (L, MIX_WIDTH, D_MODEL), MIX_WIDTH ** -0.5),
        'router_w': nrm(ks[23], (L, D_MODEL, N_EXPERTS), D_MODEL ** -0.5),
        'router_b': nrm(ks[24], (L, N_EXPERTS), 0.01),
        'exp_w1': nrm(ks[25], (L, N_EXPERTS, D_MODEL, 2 * D_EXPERT), D_MODEL ** -0.5),
        'exp_b1': nrm(ks[26], (L, N_EXPERTS, 2 * D_EXPERT), 0.02),
        'exp_w2': nrm(ks[27], (L, N_EXPERTS, D_EXPERT, D_MODEL), D_EXPERT ** -0.5),
        'exp_b2': nrm(ks[28], (L, N_EXPERTS, D_MODEL), 0.02),
    }


def reference(x, c, ctx, c_ctx, norm1_g, norm2_g, w_ada, b_ada, w_in, mla_q_norm, mla_kv_norm, mla_w_uq, mla_w_ukv, mla_q_head_norm, mla_k_head_norm, conv_w, conv_b, conv_ln_g, conv_ln_b, swa_q_norm, swa_k_norm, swa_sink, w_out, router_w, router_b, exp_w1, exp_b1, exp_w2, exp_b2):
    bsz, n_lat, dm = x.shape
    n_ctx = ctx.shape[1]
    n_rows = n_lat // GRID_W
    rope_mla = axial_rope_tables(n_rows, MLA_ROPE)
    rope_swa = axial_rope_tables(n_rows, SWA_HEAD_DIM)
    s_lat = jax.nn.silu(c)
    s_ctx = jax.nn.silu(c_ctx)
    xc = ctx
    for l in range(DEPTH):
        last = l == DEPTH - 1
        m_lat = (s_lat @ w_ada[l] + b_ada[l])[:, None, :]
        m_ctx = s_ctx @ w_ada[l] + b_ada[l]
        sh1, sc1, g1, sh2, sc2, g2 = jnp.split(m_lat, 6, axis=-1)
        csh1, csc1, cg1, csh2, csc2, cg2 = jnp.split(m_ctx, 6, axis=-1)

        h = modulate(rms_norm(x, norm1_g[l]), sh1, sc1)
        hc = modulate(rms_norm(xc, norm1_g[l]), csh1, csc1)
        p = h @ w_in[l]
        pc = hc @ (w_in[l][:, :KV_COLS] if last else w_in[l])

        k_mla_c, v_mla_c = mla_keys(pc[..., O_CKV:O_KROPE], pc[..., O_KROPE:O_SK], mla_kv_norm[l], mla_w_ukv[l], mla_k_head_norm[l], None)
        k_swa_c, v_swa_c = swa_keys(pc[..., O_SK:O_SV], pc[..., O_SV:O_CQ], swa_k_norm[l], None)

        k_mla, v_mla = mla_keys(p[..., O_CKV:O_KROPE], p[..., O_KROPE:O_SK], mla_kv_norm[l], mla_w_ukv[l], mla_k_head_norm[l], rope_mla)
        q_mla = mla_queries(p[..., O_CQ:O_SQ], mla_q_norm[l], mla_w_uq[l], mla_q_head_norm[l], rope_mla)
        o_mla = mla_latent_attention(q_mla, k_mla, v_mla, k_mla_c, v_mla_c)
        o_conv = conformer_conv(p[..., O_GLU:], conv_w[l], conv_b[l], conv_ln_g[l], conv_ln_b[l])
        k_swa, v_swa = swa_keys(p[..., O_SK:O_SV], p[..., O_SV:O_CQ], swa_k_norm[l], rope_swa)
        q_swa = swa_queries(p[..., O_SQ:O_GLU], swa_q_norm[l], rope_swa)
        o_swa = swa_latent_attention(q_swa, k_swa, v_swa, k_swa_c, v_swa_c, swa_sink[l])
        mix = jnp.concatenate([o_mla, o_conv, o_swa], axis=-1) @ w_out[l]

        if not last:
            q_mla_c = mla_queries(pc[..., O_CQ:O_SQ], mla_q_norm[l], mla_w_uq[l], mla_q_head_norm[l], None)
            oc_mla = mla_ctx_attention(q_mla_c, k_mla_c, v_mla_c)
            oc_conv = conformer_conv(pc[..., O_GLU:], conv_w[l], conv_b[l], conv_ln_g[l], conv_ln_b[l])
            q_swa_c = swa_queries(pc[..., O_SQ:O_GLU], swa_q_norm[l], None)
            oc_swa = swa_ctx_attention(q_swa_c, k_swa_c, v_swa_c, swa_sink[l])
            xc = xc + cg1 * (jnp.concatenate([oc_mla, oc_conv, oc_swa], axis=-1) @ w_out[l])
        x = x + g1 * mix

        hf = modulate(rms_norm(x, norm2_g[l]), sh2, sc2)
        if last:
            y = moe(hf.reshape(-1, dm), router_w[l], router_b[l], exp_w1[l], exp_b1[l], exp_w2[l], exp_b2[l])
            x = x + g2 * y.reshape(bsz, n_lat, dm)
        else:
            hfc = modulate(rms_norm(xc, norm2_g[l]), csh2, csc2)
            tok = jnp.concatenate([hfc, hf], axis=1).reshape(-1, dm)
            y = moe(tok, router_w[l], router_b[l], exp_w1[l], exp_b1[l], exp_w2[l], exp_b2[l]).reshape(bsz, n_ctx + n_lat, dm)
            xc = xc + cg2 * y[:, :n_ctx]
            x = x + g2 * y[:, n_ctx:]
    return x
```

```python
import functools

import numpy as np
import jax
import jax.numpy as jnp
from jax import lax
from jax.experimental import pallas as pl
from jax.experimental.pallas import tpu as pltpu

F32 = jnp.float32
BF16 = jnp.bfloat16
I32 = jnp.int32

GRID_W = 64
ROPE_BASE = 10000.0
EPS = 1e-6
MLA_HEADS = 4
MLA_NOPE = 128
MLA_ROPE = 64
MLA_V = 128
MLA_QK = MLA_NOPE + MLA_ROPE
MLA_Q_RANK = 384
MLA_KV_RANK = 256
CONV_CH = 256
CONV_WIDTH = 31
SWA_HEADS = 4
SWA_KV_HEADS = 2
SWA_HEAD_DIM = 64
SWA_WINDOW = 128
N_EXPERTS = 32
TOP_K = 4
SWIGLU_LIMIT = 7.0
SWIGLU_ALPHA = 1.702
EXPERT_BLOCK = 256

LANES = 128
TM = 256
CONV_HALO = 16
VMEM_LIMIT = 48 * 1024 * 1024
NEG = -1e30

_PI = np.array([i + 16 if (i % 32) < 16 else i - 16 for i in range(64)])


def _cp(sem):
    return pltpu.CompilerParams(dimension_semantics=sem, vmem_limit_bytes=VMEM_LIMIT)


def _full(shape):
    n = len(shape)
    return pl.BlockSpec(shape, lambda *a, _n=n: (0,) * _n)


def _split(x):
    hi = x.astype(BF16)
    lo = (x - hi.astype(F32)).astype(BF16)
    return hi, lo


def _dot3(a, b):
    ah, al = _split(a)
    bh, bl = _split(b)
    d = functools.partial(jnp.dot, preferred_element_type=F32)
    return d(ah, bh) + d(ah, bl) + d(al, bh)


def _ada_kernel(s_ref, w_ref, b_ref, o_ref):
    s = s_ref[...]
    s = s * jax.nn.sigmoid(s)
    o_ref[0] = _dot3(s, w_ref[0]) + b_ref[0]


def _ada(s_in, w_ada, b_ada):
    L, D, N = w_ada.shape
    tn = 1536
    return pl.pallas_call(
        _ada_kernel,
        grid=(L, N // tn),
        in_specs=[
            pl.BlockSpec((16, D), lambda l, j: (0, 0)),
            pl.BlockSpec((1, D, tn), lambda l, j: (l, 0, j)),
            pl.BlockSpec((1, 1, tn), lambda l, j: (l, 0, j)),
        ],
        out_specs=pl.BlockSpec((1, 16, tn), lambda l, j: (l, 0, j)),
        out_shape=jax.ShapeDtypeStruct((L, 16, N), F32),
        compiler_params=_cp(("arbitrary", "arbitrary")),
        name="ada",
    )(s_in, w_ada, b_ada.reshape(L, 1, N))


def _rope(x, c, s):
    return x * c + pltpu.roll(x, 64, axis=1) * s


def _prep_kernel(x_ref, mod_ref, n1_ref, win_ref, kvg_ref, wukv_ref, qg_ref, wuq_ref, gq_ref, gkn_ref, gkp_ref,
                 gsq_ref, gsk_ref, cos_ref, sin_ref, qm_ref, km_ref, vm_ref, qs_ref, ks_ref, vs_ref, u_ref):
    x = x_ref[0]
    mod = mod_ref[0, 0]
    sh, sc = mod[0:1], mod[1:2]
    y = x * lax.rsqrt(jnp.mean(x * x, axis=-1, keepdims=True) + EPS) * n1_ref[...]
    h = y * (1.0 + sc) + sh
    p = jnp.dot(h.astype(BF16), win_ref[...], preferred_element_type=F32)
    c = cos_ref[...]
    s = sin_ref[...]
    lane = lax.broadcasted_iota(I32, (TM, LANES), 1)
    lo = lane < 64

    def ss_lo(v):
        return jnp.sum(jnp.where(lo, v * v, 0.0), axis=-1, keepdims=True)

    ckv = p[:, 0:256]
    ckvn = ckv * lax.rsqrt(jnp.mean(ckv * ckv, axis=-1, keepdims=True) + EPS) * kvg_ref[...]
    kv = jnp.dot(ckvn.astype(BF16), wukv_ref[...], preferred_element_type=F32)
    kpe = p[:, 256:384]
    ss_pe = ss_lo(kpe)
    kpe_rot = _rope(kpe * gkp_ref[...], c, s)
    for hh in range(MLA_HEADS):
        kn = kv[:, 128 * hh:128 * hh + 128]
        r = lax.rsqrt((jnp.sum(kn * kn, axis=-1, keepdims=True) + ss_pe) * (1.0 / MLA_QK) + EPS)
        km_ref[0, hh, :, 0:128] = (kn * r * gkn_ref[...]).astype(BF16)
        km_ref[0, hh, :, 128:256] = (kpe_rot * r).astype(BF16)
        vm_ref[0, hh] = kv[:, 512 + 128 * hh:640 + 128 * hh].astype(BF16)

    cq = p[:, 896:1280]
    cqn = cq * lax.rsqrt(jnp.mean(cq * cq, axis=-1, keepdims=True) + EPS) * qg_ref[...]
    q = jnp.dot(cqn.astype(BF16), wuq_ref[...], preferred_element_type=F32)
    gq = gq_ref[...]
    for hh in range(MLA_HEADS):
        qn = q[:, 256 * hh:256 * hh + 128]
        qp = q[:, 256 * hh + 128:256 * hh + 256]
        r = lax.rsqrt((jnp.sum(qn * qn, axis=-1, keepdims=True) + ss_lo(qp)) * (1.0 / MLA_QK) + EPS)
        qm_ref[0, hh, :, 0:128] = (qn * r * gq[:, 0:128]).astype(BF16)
        qm_ref[0, hh, :, 128:256] = _rope(qp * r * gq[:, 128:256], c, s).astype(BF16)

    for g in range(SWA_KV_HEADS):
        xk = p[:, 384 + 128 * g:512 + 128 * g]
        r = lax.rsqrt(ss_lo(xk) * (1.0 / SWA_HEAD_DIM) + EPS)
        ks_ref[0, g] = _rope(xk * r * gsk_ref[...], c, s).astype(BF16)
        vs_ref[0, g] = p[:, 640 + 128 * g:768 + 128 * g].astype(BF16)
    for hh in range(SWA_HEADS):
        xq = p[:, 1280 + 128 * hh:1408 + 128 * hh]
        r = lax.rsqrt(ss_lo(xq) * (1.0 / SWA_HEAD_DIM) + EPS)
        qs_ref[0, hh] = _rope(xq * r * gsq_ref[...], c, s).astype(BF16)

    u_ref[0] = p[:, 1792:2048] * jax.nn.sigmoid(p[:, 2048:2304])


def _prep(xu, modsel, lw, cos_t, sin_t, t_off):
    B, T, D = xu.shape
    nt = T // TM - t_off
    ncol = lw["w_in"].shape[1]
    row = lambda b, i: (b, i + t_off, 0)
    head = lambda b, i: (b, 0, i + t_off, 0)
    in_specs = [
        pl.BlockSpec((1, TM, D), row),
        pl.BlockSpec((1, 1, 6, D), lambda b, i: (b, jnp.minimum(i + t_off, 1), 0, 0)),
        _full((1, D)),
        _full((D, ncol)),
        _full((1, MLA_KV_RANK)),
        _full((MLA_KV_RANK, 1024)),
        _full((1, MLA_Q_RANK)),
        _full((MLA_Q_RANK, 1024)),
        _full((1, 256)),
        _full((1, 128)),
        _full((1, 128)),
        _full((1, 128)),
        _full((1, 128)),
        pl.BlockSpec((TM, LANES), lambda b, i: (i + t_off, 0)),
        pl.BlockSpec((TM, LANES), lambda b, i: (i + t_off, 0)),
    ]
    out_shape = [
        jax.ShapeDtypeStruct((B, MLA_HEADS, T, 256), BF16),
        jax.ShapeDtypeStruct((B, MLA_HEADS, T, 256), BF16),
        jax.ShapeDtypeStruct((B, MLA_HEADS, T, 128), BF16),
        jax.ShapeDtypeStruct((B, SWA_HEADS, T, 128), BF16),
        jax.ShapeDtypeStruct((B, SWA_KV_HEADS, T, 128), BF16),
        jax.ShapeDtypeStruct((B, SWA_KV_HEADS, T, 128), BF16),
        jax.ShapeDtypeStruct((B, T, CONV_CH), F32),
    ]
    out_specs = [
        pl.BlockSpec((1, MLA_HEADS, TM, 256), head),
        pl.BlockSpec((1, MLA_HEADS, TM, 256), head),
        pl.BlockSpec((1, MLA_HEADS, TM, 128), head),
        pl.BlockSpec((1, SWA_HEADS, TM, 128), head),
        pl.BlockSpec((1, SWA_KV_HEADS, TM, 128), head),
        pl.BlockSpec((1, SWA_KV_HEADS, TM, 128), head),
        pl.BlockSpec((1, TM, CONV_CH), row),
    ]
    return pl.pallas_call(
        _prep_kernel,
        grid=(B, nt),
        in_specs=in_specs,
        out_specs=out_specs,
        out_shape=out_shape,
        compiler_params=_cp(("arbitrary", "arbitrary")),
        name="prep",
    )(xu, modsel, lw["n1"], lw["w_in"], lw["kvg"], lw["w_ukv"], lw["qg"], lw["w_uq"], lw["gq"], lw["gkn"], lw["gkp"],
      lw["gsq"], lw["gsk"], cos_t, sin_t)


def _mla_kernel(q_ref, k_ref, v_ref, o_ref, *, q_off, n_keys):
    qi = pl.program_id(1) + q_off

    def attend(nk):
        for h in range(MLA_HEADS):
            q = q_ref[0, h]
            k = k_ref[0, h, 0:nk, :]
            v = v_ref[0, h, 0:nk, :]
            s = lax.dot_general(q, k, (((1,), (1,)), ((), ())), preferred_element_type=F32)
            m = jnp.max(s, axis=-1, keepdims=True)
            p = jnp.exp(s - m)
            l = jnp.sum(p, axis=-1, keepdims=True)
            o = jnp.dot(p.astype(BF16), v, preferred_element_type=F32) / l
            o_ref[0, :, 128 * h:128 * h + 128] = o.astype(BF16)

    if q_off == 0:
        @pl.when(qi == 0)
        def _():
            attend(TM)

        @pl.when(qi > 0)
        def _():
            attend(n_keys)
    else:
        attend(n_keys)


def _mla(qm, km, vm, t_off):
    B, H, T, _ = qm.shape
    nt = T // TM - t_off
    return pl.pallas_call(
        functools.partial(_mla_kernel, q_off=t_off, n_keys=T),
        grid=(B, nt),
        in_specs=[
            pl.BlockSpec((1, H, TM, 256), lambda b, i: (b, 0, i + t_off, 0)),
            pl.BlockSpec((1, H, T, 256), lambda b, i: (b, 0, 0, 0)),
            pl.BlockSpec((1, H, T, 128), lambda b, i: (b, 0, 0, 0)),
        ],
        out_specs=pl.BlockSpec((1, TM, H * MLA_V), lambda b, i: (b, i, 0)),
        out_shape=jax.ShapeDtypeStruct((B, nt * TM, H * MLA_V), BF16),
        compiler_params=_cp(("arbitrary", "arbitrary")),
        name="mla_attn",
    )(qm, km, vm)


SWA_SPAN = TM + 2 * SWA_WINDOW


def _swa_kernel(sink_ref, q_ref, k_ref, v_ref, o_ref, *, q_off, n_rows):
    qi = pl.program_id(1) + q_off
    nt_dims = (((1,), (1,)), ((), ()))

    def latent():
        start = jnp.clip(qi * TM - SWA_WINDOW, 0, n_rows - SWA_SPAN)
        start = pl.multiple_of(start, SWA_WINDOW)
        qpos = qi * TM + lax.broadcasted_iota(I32, (TM, SWA_SPAN), 0)
        kpos = start + lax.broadcasted_iota(I32, (TM, SWA_SPAN), 1)
        valid = (jnp.abs(qpos - kpos) <= SWA_WINDOW) & (kpos >= TM)
        for h in range(SWA_HEADS):
            g = h // (SWA_HEADS // SWA_KV_HEADS)
            q = q_ref[0, h]
            kl = k_ref[0, g, pl.ds(start, SWA_SPAN), :]
            vl = v_ref[0, g, pl.ds(start, SWA_SPAN), :]
            kc = k_ref[0, g, 0:TM, :]
            vc = v_ref[0, g, 0:TM, :]
            sl = lax.dot_general(q, kl, nt_dims, preferred_element_type=F32)
            sl = jnp.where(valid, sl, NEG)
            scx = lax.dot_general(q, kc, nt_dims, preferred_element_type=F32)
            sink = sink_ref[h]
            m = jnp.maximum(jnp.maximum(jnp.max(sl, axis=-1, keepdims=True), jnp.max(scx, axis=-1, keepdims=True)), sink)
            pl_ = jnp.exp(sl - m)
            pc = jnp.exp(scx - m)
            l = jnp.sum(pl_, axis=-1, keepdims=True) + jnp.sum(pc, axis=-1, keepdims=True) + jnp.exp(sink - m)
            o = (jnp.dot(pl_.astype(BF16), vl, preferred_element_type=F32)
                 + jnp.dot(pc.astype(BF16), vc, preferred_element_type=F32)) / l
            o_ref[0, :, 128 * h:128 * h + 128] = o.astype(BF16)

    def context():
        for h in range(SWA_HEADS):
            g = h // (SWA_HEADS // SWA_KV_HEADS)
            q = q_ref[0, h]
            kc = k_ref[0, g, 0:TM, :]
            vc = v_ref[0, g, 0:TM, :]
            scx = lax.dot_general(q, kc, nt_dims, preferred_element_type=F32)
            sink = sink_ref[h]
            m = jnp.maximum(jnp.max(scx, axis=-1, keepdims=True), sink)
            pc = jnp.exp(scx - m)
            l = jnp.sum(pc, axis=-1, keepdims=True) + jnp.exp(sink - m)
            o = jnp.dot(pc.astype(BF16), vc, preferred_element_type=F32) / l
            o_ref[0, :, 128 * h:128 * h + 128] = o.astype(BF16)

    if q_off == 0:
        pl.when(qi == 0)(context)
        pl.when(qi > 0)(latent)
    else:
        latent()


def _swa(sink, qs, ks, vs, t_off):
    B, H, T, _ = qs.shape
    G = ks.shape[1]
    nt = T // TM - t_off
    return pl.pallas_call(
        functools.partial(_swa_kernel, q_off=t_off, n_rows=T),
        grid=(B, nt),
        in_specs=[
            pl.BlockSpec(memory_space=pltpu.SMEM),
            pl.BlockSpec((1, H, TM, 128), lambda b, i: (b, 0, i + t_off, 0)),
            pl.BlockSpec((1, G, T, 128), lambda b, i: (b, 0, 0, 0)),
            pl.BlockSpec((1, G, T, 128), lambda b, i: (b, 0, 0, 0)),
        ],
        out_specs=pl.BlockSpec((1, TM, H * 128), lambda b, i: (b, i, 0)),
        out_shape=jax.ShapeDtypeStruct((B, nt * TM, H * 128), BF16),
        compiler_params=_cp(("arbitrary", "arbitrary")),
        name="swa_attn",
    )(sink, qs, ks, vs)


def _conv_kernel(u_ref, w_ref, b_ref, g_ref, bb_ref, o_ref, buf_ref, *, t_off, n_rows):
    i = pl.program_id(1) + t_off
    nt_all = n_rows // TM
    start = pl.multiple_of(i * TM, TM)
    ps = pl.multiple_of(jnp.maximum(start - CONV_HALO, 0), 8)
    ns = pl.multiple_of(jnp.minimum(start + TM, n_rows - CONV_HALO), 8)
    keep_prev = jnp.where(i <= 1, 0.0, 1.0)
    keep_next = jnp.where((i == 0) | (i == nt_all - 1), 0.0, 1.0)
    buf_ref[0:CONV_HALO, :] = u_ref[0, pl.ds(ps, CONV_HALO), :] * keep_prev
    buf_ref[CONV_HALO:CONV_HALO + TM, :] = u_ref[0, pl.ds(start, TM), :]
    buf_ref[CONV_HALO + TM:2 * CONV_HALO + TM, :] = u_ref[0, pl.ds(ns, CONV_HALO), :] * keep_next
    off = CONV_HALO - CONV_WIDTH // 2
    acc = jnp.zeros((TM, CONV_CH), F32)
    for j in range(CONV_WIDTH):
        acc = acc + buf_ref[off + j:off + j + TM, :] * w_ref[j:j + 1, :]
    y = acc + b_ref[...]
    mu = jnp.mean(y, axis=-1, keepdims=True)
    d = y - mu
    var = jnp.mean(d * d, axis=-1, keepdims=True)
    z = d * lax.rsqrt(var + EPS) * g_ref[...] + bb_ref[...]
    o_ref[0] = (z * jax.nn.sigmoid(z)).astype(BF16)


def _conv(u, lw, t_off):
    B, T, C = u.shape
    nt = T // TM - t_off
    return pl.pallas_call(
        functools.partial(_conv_kernel, t_off=t_off, n_rows=T),
        grid=(B, nt),
        in_specs=[
            pl.BlockSpec((1, T, C), lambda b, i: (b, 0, 0)),
            _full((CONV_WIDTH, C)),
            _full((1, C)),
            _full((1, C)),
            _full((1, C)),
        ],
        out_specs=pl.BlockSpec((1, TM, C), lambda b, i: (b, i, 0)),
        out_shape=jax.ShapeDtypeStruct((B, nt * TM, C), BF16),
        scratch_shapes=[pltpu.VMEM((TM + 2 * CONV_HALO, C), F32)],
        compiler_params=_cp(("arbitrary", "arbitrary")),
        name="conv",
    )(u, lw["conv_w"], lw["conv_b"], lw["conv_g"], lw["conv_bb"])


def _out_kernel(om_ref, oc_ref, os_ref, x_ref, mod_ref, w1_ref, w2_ref, w3_ref, n2_ref, wrh_ref, wrl_ref, br_ref,
                tri_ref, xn_ref, hf_ref, e_ref, rk_ref, gt_ref, cnt_ref, run_ref):
    first = (pl.program_id(0) == 0) & (pl.program_id(1) == 0)

    @pl.when(first)
    def _():
        run_ref[...] = jnp.zeros_like(run_ref)

    d = functools.partial(jnp.dot, preferred_element_type=F32)
    mix = d(om_ref[0], w1_ref[...]) + d(oc_ref[0], w2_ref[...]) + d(os_ref[0], w3_ref[...])
    mod = mod_ref[0, 0]
    g1, sh2, sc2 = mod[2:3], mod[3:4], mod[4:5]
    xn = x_ref[0] + g1 * mix
    xn_ref[0] = xn
    hf = xn * lax.rsqrt(jnp.mean(xn * xn, axis=-1, keepdims=True) + EPS) * n2_ref[...]
    hf = hf * (1.0 + sc2) + sh2
    hf_ref[0] = hf

    hi, lo = _split(hf)
    logits = d(hi, wrh_ref[...]) + d(hi, wrl_ref[...]) + d(lo, wrh_ref[...]) + br_ref[...]
    lane = lax.broadcasted_iota(I32, (TM, LANES), 1)
    l = logits
    ohs, vals, idxs = [], [], []
    for _ in range(TOP_K):
        m = jnp.max(l, axis=-1, keepdims=True)
        idx = jnp.min(jnp.where(l == m, lane, LANES), axis=-1, keepdims=True)
        oh = lane == idx
        ohs.append(oh)
        vals.append(m)
        idxs.append(idx)
        l = jnp.where(oh, -jnp.inf, l)
    ex = [jnp.exp(v - vals[0]) for v in vals]
    den = ex[0] + ex[1] + ex[2] + ex[3]
    gates = [e / den for e in ex]

    oa = jnp.zeros((TM, LANES), F32)
    for oh in ohs:
        oa = oa + oh.astype(F32)
    tot = d(tri_ref[...], oa.astype(BF16)) + run_ref[0:1, :]
    rks = [jnp.sum(jnp.where(oh, tot, 0.0), axis=-1, keepdims=True).astype(I32) for oh in ohs]
    run_ref[...] = run_ref[...] + jnp.sum(oa, axis=0, keepdims=True)
    cnt_ref[...] = run_ref[...]

    e_out = jnp.zeros((TM, LANES), I32)
    r_out = jnp.zeros((TM, LANES), I32)
    g_out = jnp.zeros((TM, LANES), F32)
    for k in range(TOP_K):
        e_out = jnp.where(lane == k, idxs[k], e_out)
        r_out = jnp.where(lane == k, rks[k], r_out)
        g_out = jnp.where(lane == k, gates[k], g_out)
    e_ref[0] = e_out
    rk_ref[0] = r_out
    gt_ref[0] = g_out


def _outproj(om, oc, osw, xu, modsel, lw, tri, t_off):
    B, T, D = xu.shape
    nt = T // TM - t_off
    row = lambda b, i: (b, i, 0)
    Tq = nt * TM
    in_specs = [
        pl.BlockSpec((1, TM, 512), row),
        pl.BlockSpec((1, TM, 256), row),
        pl.BlockSpec((1, TM, 512), row),
        pl.BlockSpec((1, TM, D), lambda b, i: (b, i + t_off, 0)),
        pl.BlockSpec((1, 1, 6, D), lambda b, i: (b, jnp.minimum(i + t_off, 1), 0, 0)),
        _full((512, D)),
        _full((256, D)),
        _full((512, D)),
        _full((1, D)),
        _full((D, LANES)),
        _full((D, LANES)),
        _full((1, LANES)),
        _full((TM, TM)),
    ]
    out_shape = [
        jax.ShapeDtypeStruct((B, Tq, D), F32),
        jax.ShapeDtypeStruct((B, Tq, D), F32),
        jax.ShapeDtypeStruct((B, Tq, LANES), I32),
        jax.ShapeDtypeStruct((B, Tq, LANES), I32),
        jax.ShapeDtypeStruct((B, Tq, LANES), F32),
        jax.ShapeDtypeStruct((8, LANES), F32),
    ]
    out_specs = [
        pl.BlockSpec((1, TM, D), row),
        pl.BlockSpec((1, TM, D), row),
        pl.BlockSpec((1, TM, LANES), row),
        pl.BlockSpec((1, TM, LANES), row),
        pl.BlockSpec((1, TM, LANES), row),
        pl.BlockSpec((8, LANES), lambda b, i: (0, 0)),
    ]
    return pl.pallas_call(
        _out_kernel,
        grid=(B, nt),
        in_specs=in_specs,
        out_specs=out_specs,
        out_shape=out_shape,
        scratch_shapes=[pltpu.VMEM((8, LANES), F32)],
        compiler_params=_cp(("arbitrary", "arbitrary")),
        name="outproj_router",
    )(om, oc, osw, xu, modsel, lw["w_o1"], lw["w_o2"], lw["w_o3"], lw["n2"], lw["wr_hi"], lw["wr_lo"], lw["br"], tri)


def _row_copy(src_ref, src_row, dst_ref, dst_row, sem):
    return pltpu.make_async_copy(src_ref.at[pl.ds(src_row, 1), :], dst_ref.at[pl.ds(dst_row, 1), :], sem)


def _disp_kernel(pst_ref, cnt_ref, na_ref, e_ref, rk_ref, hf_ref, xb_ref, zrow_ref, sem, zsem, *, n_blocks):
    first = (pl.program_id(0) == 0) & (pl.program_id(1) == 0)

    @pl.when(first)
    def _():
        zrow_ref[...] = jnp.zeros_like(zrow_ref)

        def zblock(j):
            return pltpu.make_async_copy(zrow_ref, xb_ref.at[pl.ds(pl.multiple_of(j * EXPERT_BLOCK, EXPERT_BLOCK), EXPERT_BLOCK), :], zsem)

        def zb_start(j, carry):
            zblock(j).start()
            return carry

        def zb_wait(j, carry):
            zblock(j).wait()
            return carry

        lax.fori_loop(na_ref[0], n_blocks, zb_start, 0)
        lax.fori_loop(na_ref[0], n_blocks, zb_wait, 0)

        for e in range(N_EXPERTS):
            cnt = cnt_ref[e]
            base = pst_ref[e] + cnt
            npad = (EXPERT_BLOCK - cnt % EXPERT_BLOCK) % EXPERT_BLOCK

            def zfill(r, carry):
                _row_copy(zrow_ref, 0, xb_ref, base + r, zsem).start()
                return carry

            def zwait(r, carry):
                _row_copy(zrow_ref, 0, xb_ref, base, zsem).wait()
                return carry

            lax.fori_loop(0, npad, zfill, 0)
            lax.fori_loop(0, npad, zwait, 0)

    src = hf_ref.at[0]

    def issue(t, carry):
        for k in range(TOP_K):
            dst = pst_ref[e_ref[0, 0, TOP_K * t + k]] + rk_ref[0, 0, TOP_K * t + k]
            _row_copy(src, t, xb_ref, dst, sem.at[k]).start()
        return carry

    lax.fori_loop(0, TM, issue, 0, unroll=4)

    def drain(t, carry):
        for k in range(TOP_K):
            _row_copy(src, 0, xb_ref, 0, sem.at[k]).wait()
        return carry

    lax.fori_loop(0, TM, drain, 0, unroll=16)


def _dispatch(pstart, counts, n_act, e_flat, rk_flat, hf, n_buf):
    B, Tq, D = hf.shape
    nt = Tq // TM
    grid_spec = pltpu.PrefetchScalarGridSpec(
        num_scalar_prefetch=3,
        grid=(B, nt),
        in_specs=[
            pl.BlockSpec((1, 1, TM * TOP_K), lambda b, i, *_: (b * nt + i, 0, 0), memory_space=pltpu.SMEM),
            pl.BlockSpec((1, 1, TM * TOP_K), lambda b, i, *_: (b * nt + i, 0, 0), memory_space=pltpu.SMEM),
            pl.BlockSpec((1, TM, D), lambda b, i, *_: (b, i, 0)),
        ],
        out_specs=pl.BlockSpec(memory_space=pl.ANY),
        scratch_shapes=[pltpu.VMEM((EXPERT_BLOCK, D), F32), pltpu.SemaphoreType.DMA((TOP_K,)), pltpu.SemaphoreType.DMA],
    )
    return pl.pallas_call(
        functools.partial(_disp_kernel, n_blocks=n_buf // EXPERT_BLOCK),
        grid_spec=grid_spec,
        out_shape=jax.ShapeDtypeStruct((n_buf, D), F32),
        compiler_params=_cp(("arbitrary", "arbitrary")),
        name="dispatch",
    )(pstart, counts, n_act, e_flat, rk_flat, hf)


def _exp_kernel(be_ref, bx_ref, na_ref, x_ref, w1a_ref, w1b_ref, b1a_ref, b1b_ref, w2_ref, b2_ref, o_ref):
    j = pl.program_id(0)

    @pl.when(j < na_ref[0])
    def _():
        x = x_ref[...].astype(BF16)
        ug = jnp.dot(x, w1a_ref[0], preferred_element_type=F32) + b1a_ref[0]
        ul = jnp.dot(x, w1b_ref[0], preferred_element_type=F32) + b1b_ref[0]
        xg = jnp.minimum(ug, SWIGLU_LIMIT)
        xl = jnp.clip(ul, -SWIGLU_LIMIT, SWIGLU_LIMIT)
        act = xg * jax.nn.sigmoid(SWIGLU_ALPHA * xg) * (xl + 1.0)
        o_ref[...] = jnp.dot(act.astype(BF16), w2_ref[0], preferred_element_type=F32) + b2_ref[0]

    @pl.when(j >= na_ref[0])
    def _():
        o_ref[...] = jnp.zeros_like(o_ref)


def _experts(blk_e, blk_x, n_act, xb, lw):
    n_buf, D = xb.shape
    nb = n_buf // EXPERT_BLOCK
    De = lw["w1a"].shape[2]
    wmap = lambda j, be, bx, na: (be[j], 0, 0)
    grid_spec = pltpu.PrefetchScalarGridSpec(
        num_scalar_prefetch=3,
        grid=(nb,),
        in_specs=[
            pl.BlockSpec((EXPERT_BLOCK, D), lambda j, be, bx, na: (bx[j], 0)),
            pl.BlockSpec((1, D, De), wmap),
            pl.BlockSpec((1, D, De), wmap),
            pl.BlockSpec((1, 1, De), wmap),
            pl.BlockSpec((1, 1, De), wmap),
            pl.BlockSpec((1, De, D), wmap),
            pl.BlockSpec((1, 1, D), wmap),
        ],
        out_specs=pl.BlockSpec((EXPERT_BLOCK, D), lambda j, be, bx, na: (j, 0)),
    )
    return pl.pallas_call(
        _exp_kernel,
        grid_spec=grid_spec,
        out_shape=jax.ShapeDtypeStruct((n_buf, D), F32),
        compiler_params=_cp(("arbitrary",)),
        name="experts",
    )(blk_e, blk_x, n_act, xb, lw["w1a"], lw["w1b"], lw["b1a"], lw["b1b"], lw["w2"], lw["b2"])


def _comb_kernel(pst_ref, e_ref, rk_ref, gt_ref, xn_ref, mod_ref, yb_ref, o_ref, buf_ref, sem):
    def issue(t, carry):
        for k in range(TOP_K):
            srow = pst_ref[e_ref[0, 0, TOP_K * t + k]] + rk_ref[0, 0, TOP_K * t + k]
            _row_copy(yb_ref, srow, buf_ref.at[k], t, sem.at[k]).start()
        return carry

    lax.fori_loop(0, TM, issue, 0, unroll=4)

    def drain(t, carry):
        for k in range(TOP_K):
            _row_copy(yb_ref, 0, buf_ref.at[k], 0, sem.at[k]).wait()
        return carry

    lax.fori_loop(0, TM, drain, 0, unroll=16)

    gt = gt_ref[0]
    y = gt[:, 0:1] * buf_ref[0]
    for k in range(1, TOP_K):
        y = y + gt[:, k:k + 1] * buf_ref[k]
    g2 = mod_ref[0, 0][5:6]
    o_ref[0] = xn_ref[0] + g2 * y


def _combine(pstart, e_flat, rk_flat, gates, xn, modsel, yb, t_off):
    B, Tq, D = xn.shape
    nt = Tq // TM
    grid_spec = pltpu.PrefetchScalarGridSpec(
        num_scalar_prefetch=1,
        grid=(B, nt),
        in_specs=[
            pl.BlockSpec((1, 1, TM * TOP_K), lambda b, i, *_: (b * nt + i, 0, 0), memory_space=pltpu.SMEM),
            pl.BlockSpec((1, 1, TM * TOP_K), lambda b, i, *_: (b * nt + i, 0, 0), memory_space=pltpu.SMEM),
            pl.BlockSpec((1, TM, LANES), lambda b, i, *_: (b, i, 0)),
            pl.BlockSpec((1, TM, D), lambda b, i, *_: (b, i, 0)),
            pl.BlockSpec((1, 1, 6, D), lambda b, i, *_: (b, jnp.minimum(i + t_off, 1), 0, 0)),
            pl.BlockSpec(memory_space=pl.ANY),
        ],
        out_specs=pl.BlockSpec((1, TM, D), lambda b, i, *_: (b, i, 0)),
        scratch_shapes=[pltpu.VMEM((TOP_K, TM, D), F32), pltpu.SemaphoreType.DMA((TOP_K,))],
    )
    return pl.pallas_call(
        _comb_kernel,
        grid_spec=grid_spec,
        out_shape=jax.ShapeDtypeStruct((B, Tq, D), F32),
        compiler_params=_cp(("arbitrary", "arbitrary")),
        name="combine",
    )(pstart, e_flat, rk_flat, gates, xn, modsel, yb)


def _take_cols(w, cols):
    cols = np.asarray(cols)
    out = jnp.take(w, jnp.asarray(np.maximum(cols, 0)), axis=-1)
    return jnp.where(jnp.asarray(cols >= 0), out, 0.0)


def _in_cols():
    pi = _PI
    cols = list(range(0, 256))
    cols += [256 + i for i in range(64)] + [256 + pi[i] for i in range(64)]
    for g in range(SWA_KV_HEADS):
        base = 320 + 64 * g
        cols += [base + i for i in range(64)] + [base + pi[i] for i in range(64)]
    for g in range(SWA_KV_HEADS):
        base = 448 + 64 * g
        cols += [base + i for i in range(64)] + [-1] * 64
    cols += list(range(576, 960))
    for h in range(SWA_HEADS):
        base = 960 + 64 * h
        cols += [base + i for i in range(64)] + [base + pi[i] for i in range(64)]
    cols += list(range(1216, 1728))
    return cols


def _layer_weights(l, a):
    pi = _PI
    lw = {}
    lw["n1"] = a["norm1_g"][l][None, :]
    lw["n2"] = a["norm2_g"][l][None, :]
    lw["w_in"] = _take_cols(a["w_in"][l], _in_cols()).astype(BF16)
    lw["kvg"] = a["mla_kv_norm"][l][None, :]
    lw["qg"] = a["mla_q_norm"][l][None, :]
    uq_cols = []
    for h in range(MLA_HEADS):
        base = MLA_QK * h
        uq_cols += [base + i for i in range(128)] + [base + 128 + i for i in range(64)] + [base + 128 + pi[i] for i in range(64)]
    lw["w_uq"] = _take_cols(a["mla_w_uq"][l], uq_cols).astype(BF16)
    ukv_cols = [256 * h + i for h in range(MLA_HEADS) for i in range(128)]
    ukv_cols += [256 * h + 128 + i for h in range(MLA_HEADS) for i in range(128)]
    lw["w_ukv"] = _take_cols(a["mla_w_ukv"][l], ukv_cols).astype(BF16)
    gq = a["mla_q_head_norm"][l]
    lw["gq"] = (jnp.concatenate([gq[:128], gq[128:], gq[128:][pi]]) * (MLA_QK ** -0.5))[None, :]
    gk = a["mla_k_head_norm"][l]
    lw["gkn"] = gk[:128][None, :]
    lw["gkp"] = jnp.concatenate([gk[128:], gk[128:][pi]])[None, :]
    sq = a["swa_q_norm"][l]
    lw["gsq"] = (jnp.concatenate([sq, sq[pi]]) * (SWA_HEAD_DIM ** -0.5))[None, :]
    sk = a["swa_k_norm"][l]
    lw["gsk"] = jnp.concatenate([sk, sk[pi]])[None, :]
    lw["conv_w"] = a["conv_w"][l]
    lw["conv_b"] = a["conv_b"][l][None, :]
    lw["conv_g"] = a["conv_ln_g"][l][None, :]
    lw["conv_bb"] = a["conv_ln_b"][l][None, :]
    lw["sink"] = a["swa_sink"][l]
    wo = a["w_out"][l]
    lw["w_o1"] = wo[0:512].astype(BF16)
    lw["w_o2"] = wo[512:768].astype(BF16)
    o3 = wo[768:1024].reshape(SWA_HEADS, SWA_HEAD_DIM, -1)
    lw["w_o3"] = jnp.concatenate([o3, jnp.zeros_like(o3)], axis=1).reshape(SWA_HEADS * 128, -1).astype(BF16)
    wr = jnp.pad(a["router_w"][l], ((0, 0), (0, LANES - N_EXPERTS)))
    lw["wr_hi"] = wr.astype(BF16)
    lw["wr_lo"] = (wr - lw["wr_hi"].astype(F32)).astype(BF16)
    lw["br"] = jnp.pad(a["router_b"][l], (0, LANES - N_EXPERTS), constant_values=NEG)[None, :]
    w1 = a["exp_w1"][l]
    lw["w1a"] = w1[:, :, 0::2].astype(BF16)
    lw["w1b"] = w1[:, :, 1::2].astype(BF16)
    b1 = a["exp_b1"][l]
    lw["b1a"] = b1[:, None, 0::2]
    lw["b1b"] = b1[:, None, 1::2]
    lw["w2"] = a["exp_w2"][l].astype(BF16)
    lw["b2"] = a["exp_b2"][l][:, None, :]
    return lw


def _rope_tables(n_ctx, n_lat):
    q = MLA_ROPE // 4
    n = jnp.arange(n_lat, dtype=I32)
    row = (n // GRID_W).astype(F32)
    col = (n % GRID_W).astype(F32)
    inv = ROPE_BASE ** (-jnp.arange(q, dtype=F32) / q)
    ang_r = row[:, None] * inv
    ang_c = col[:, None] * inv
    cos = jnp.concatenate([jnp.cos(ang_r), jnp.cos(ang_r), jnp.cos(ang_c), jnp.cos(ang_c)], axis=1)
    sin = jnp.concatenate([-jnp.sin(ang_r), jnp.sin(ang_r), -jnp.sin(ang_c), jnp.sin(ang_c)], axis=1)
    cos = jnp.concatenate([jnp.ones((n_ctx, 64), F32), cos], axis=0)
    sin = jnp.concatenate([jnp.zeros((n_ctx, 64), F32), sin], axis=0)
    z = jnp.zeros_like(cos)
    return jnp.concatenate([cos, z], axis=1), jnp.concatenate([sin, z], axis=1)


def _routing_tables(cnt_f, n_blocks):
    counts = cnt_f[0, :N_EXPERTS].astype(I32)
    padded = (counts + EXPERT_BLOCK - 1) // EXPERT_BLOCK * EXPERT_BLOCK
    pend = jnp.cumsum(padded)
    pstart = pend - padded
    n_act = pend[-1] // EXPERT_BLOCK
    blk = jnp.minimum(jnp.arange(n_blocks, dtype=I32), n_act - 1)
    blk_e = jnp.minimum(jnp.searchsorted(pend, blk * EXPERT_BLOCK, side="right"), N_EXPERTS - 1).astype(I32)
    return counts, pstart.astype(I32), blk_e, blk.astype(I32), n_act.reshape(1).astype(I32)


def kernel(x, c, ctx, c_ctx, norm1_g, norm2_g, w_ada, b_ada, w_in, mla_q_norm, mla_kv_norm, mla_w_uq, mla_w_ukv, mla_q_head_norm, mla_k_head_norm, conv_w, conv_b, conv_ln_g, conv_ln_b, swa_q_norm, swa_k_norm, swa_sink, w_out, router_w, router_b, exp_w1, exp_b1, exp_w2, exp_b2):
    a = dict(norm1_g=norm1_g, norm2_g=norm2_g, w_in=w_in, mla_q_norm=mla_q_norm, mla_kv_norm=mla_kv_norm,
             mla_w_uq=mla_w_uq, mla_w_ukv=mla_w_ukv, mla_q_head_norm=mla_q_head_norm, mla_k_head_norm=mla_k_head_norm,
             conv_w=conv_w, conv_b=conv_b, conv_ln_g=conv_ln_g, conv_ln_b=conv_ln_b, swa_q_norm=swa_q_norm,
             swa_k_norm=swa_k_norm, swa_sink=swa_sink, w_out=w_out, router_w=router_w, router_b=router_b,
             exp_w1=exp_w1, exp_b1=exp_b1, exp_w2=exp_w2, exp_b2=exp_b2)
    B, S, D = x.shape
    n_ctx = ctx.shape[1]
    depth = w_ada.shape[0]
    assert n_ctx == TM and S % TM == 0 and B + 1 <= 16
    T = n_ctx + S

    s_in = jnp.zeros((16, D), F32).at[:B].set(c).at[B].set(c_ctx)
    mods = _ada(s_in, w_ada, b_ada)
    cos_t, sin_t = _rope_tables(n_ctx, S)
    tri = jnp.tril(jnp.ones((TM, TM), F32), -1).astype(BF16)

    xu = jnp.concatenate([ctx, x], axis=1)
    for l in range(depth):
        last = l == depth - 1
        t_off = 1 if last else 0
        lw = _layer_weights(l, a)
        m = mods[l].reshape(16, 6, D)
        modsel = jnp.stack([jnp.broadcast_to(m[B], (B, 6, D)), m[:B]], axis=1)

        qm, km, vm, qs, ks, vs, u = _prep(xu, modsel, lw, cos_t, sin_t, 0)
        om = _mla(qm, km, vm, t_off)
        oc = _conv(u, lw, t_off)
        osw = _swa(lw["sink"], qs, ks, vs, t_off)
        xn, hf, e_o, rk_o, gt_o, cnt = _outproj(om, oc, osw, xu, modsel, lw, tri, t_off)

        n_tok = B * (T - t_off * TM)
        nk = n_tok * TOP_K
        n_buf = -(-nk // EXPERT_BLOCK) * EXPERT_BLOCK + N_EXPERTS * EXPERT_BLOCK
        counts, pstart, blk_e, blk_x, n_act = _routing_tables(cnt, n_buf // EXPERT_BLOCK)
        e_flat = e_o[:, :, :TOP_K].reshape(-1, 1, TM * TOP_K)
        rk_flat = rk_o[:, :, :TOP_K].reshape(-1, 1, TM * TOP_K)
        xb = _dispatch(pstart, counts, n_act, e_flat, rk_flat, hf, n_buf)
        yb = _experts(blk_e, blk_x, n_act, xb, lw)
        xu = _combine(pstart, e_flat, rk_flat, gt_o, xn, modsel, yb, t_off)
    return xu
```

```python
import functools

import numpy as np
import jax
import jax.numpy as jnp
from jax import lax
from jax.experimental import pallas as pl
from jax.experimental.pallas import tpu as pltpu

F32 = jnp.float32
BF16 = jnp.bfloat16
I32 = jnp.int32

GRID_W = 64
ROPE_BASE = 10000.0
EPS = 1e-6
MLA_HEADS = 4
MLA_NOPE = 128
MLA_ROPE = 64
MLA_V = 128
MLA_QK = MLA_NOPE + MLA_ROPE
MLA_Q_RANK = 384
MLA_KV_RANK = 256
CONV_CH = 256
CONV_WIDTH = 31
SWA_HEADS = 4
SWA_KV_HEADS = 2
SWA_HEAD_DIM = 64
SWA_WINDOW = 128
N_EXPERTS = 32
TOP_K = 4
SWIGLU_LIMIT = 7.0
SWIGLU_ALPHA = 1.702
EXPERT_BLOCK = 256

LANES = 128
TM = 256
CONV_HALO = 16
VMEM_LIMIT = 48 * 1024 * 1024
EXPERT_VMEM_LIMIT = 56 * 1024 * 1024
NEG = -1e30

_PI = np.array([i + 16 if (i % 32) < 16 else i - 16 for i in range(64)])


def _cp(sem):
    return pltpu.CompilerParams(dimension_semantics=sem, vmem_limit_bytes=VMEM_LIMIT)


def _full(shape):
    n = len(shape)
    return pl.BlockSpec(shape, lambda *a, _n=n: (0,) * _n)


def _split(x):
    hi = x.astype(BF16)
    lo = (x - hi.astype(F32)).astype(BF16)
    return hi, lo


def _dot3(a, b):
    ah, al = _split(a)
    bh, bl = _split(b)
    d = functools.partial(jnp.dot, preferred_element_type=F32)
    return d(ah, bh) + d(ah, bl) + d(al, bh)


def _ada_kernel(s_ref, w_ref, b_ref, o_ref):
    s = s_ref[...]
    s = s * jax.nn.sigmoid(s)
    o_ref[0] = _dot3(s, w_ref[0]) + b_ref[0]


def _ada(s_in, w_ada, b_ada):
    L, D, N = w_ada.shape
    tn = 1536
    return pl.pallas_call(
        _ada_kernel,
        grid=(L, N // tn),
        in_specs=[
            pl.BlockSpec((16, D), lambda l, j: (0, 0)),
            pl.BlockSpec((1, D, tn), lambda l, j: (l, 0, j)),
            pl.BlockSpec((1, 1, tn), lambda l, j: (l, 0, j)),
        ],
        out_specs=pl.BlockSpec((1, 16, tn), lambda l, j: (l, 0, j)),
        out_shape=jax.ShapeDtypeStruct((L, 16, N), F32),
        compiler_params=_cp(("arbitrary", "arbitrary")),
        name="ada",
    )(s_in, w_ada, b_ada.reshape(L, 1, N))


def _rope(x, c, s):
    return x * c + pltpu.roll(x, 64, axis=1) * s


def _prep_kernel(x_ref, mod_ref, n1_ref, win_ref, kvg_ref, wukv_ref, qg_ref, wuq_ref, gq_ref, gkn_ref, gkp_ref,
                 gsq_ref, gsk_ref, cos_ref, sin_ref, qm_ref, km_ref, vm_ref, qs_ref, ks_ref, vs_ref, u_ref):
    x = x_ref[0]
    mod = mod_ref[0, 0]
    sh, sc = mod[0:1], mod[1:2]
    y = x * lax.rsqrt(jnp.mean(x * x, axis=-1, keepdims=True) + EPS) * n1_ref[...]
    h = y * (1.0 + sc) + sh
    p = jnp.dot(h.astype(BF16), win_ref[...], preferred_element_type=F32)
    c = cos_ref[...]
    s = sin_ref[...]
    lane = lax.broadcasted_iota(I32, (TM, LANES), 1)
    lo = lane < 64

    def ss_lo(v):
        return jnp.sum(jnp.where(lo, v * v, 0.0), axis=-1, keepdims=True)

    ckv = p[:, 0:256]
    ckvn = ckv * lax.rsqrt(jnp.mean(ckv * ckv, axis=-1, keepdims=True) + EPS) * kvg_ref[...]
    kv = jnp.dot(ckvn.astype(BF16), wukv_ref[...], preferred_element_type=F32)
    kpe = p[:, 256:384]
    ss_pe = ss_lo(kpe)
    kpe_rot = _rope(kpe * gkp_ref[...], c, s)
    for hh in range(MLA_HEADS):
        kn = kv[:, 128 * hh:128 * hh + 128]
        r = lax.rsqrt((jnp.sum(kn * kn, axis=-1, keepdims=True) + ss_pe) * (1.0 / MLA_QK) + EPS)
        km_ref[0, hh, :, 0:128] = (kn * r * gkn_ref[...]).astype(BF16)
        km_ref[0, hh, :, 128:256] = (kpe_rot * r).astype(BF16)
        vm_ref[0, hh] = kv[:, 512 + 128 * hh:640 + 128 * hh].astype(BF16)

    cq = p[:, 896:1280]
    cqn = cq * lax.rsqrt(jnp.mean(cq * cq, axis=-1, keepdims=True) + EPS) * qg_ref[...]
    q = jnp.dot(cqn.astype(BF16), wuq_ref[...], preferred_element_type=F32)
    gq = gq_ref[...]
    for hh in range(MLA_HEADS):
        qn = q[:, 256 * hh:256 * hh + 128]
        qp = q[:, 256 * hh + 128:256 * hh + 256]
        r = lax.rsqrt((jnp.sum(qn * qn, axis=-1, keepdims=True) + ss_lo(qp)) * (1.0 / MLA_QK) + EPS)
        qm_ref[0, hh, :, 0:128] = (qn * r * gq[:, 0:128]).astype(BF16)
        qm_ref[0, hh, :, 128:256] = _rope(qp * r * gq[:, 128:256], c, s).astype(BF16)

    for g in range(SWA_KV_HEADS):
        xk = p[:, 384 + 128 * g:512 + 128 * g]
        r = lax.rsqrt(ss_lo(xk) * (1.0 / SWA_HEAD_DIM) + EPS)
        ks_ref[0, g] = _rope(xk * r * gsk_ref[...], c, s).astype(BF16)
        vs_ref[0, g] = p[:, 640 + 128 * g:768 + 128 * g].astype(BF16)
    for hh in range(SWA_HEADS):
        xq = p[:, 1280 + 128 * hh:1408 + 128 * hh]
        r = lax.rsqrt(ss_lo(xq) * (1.0 / SWA_HEAD_DIM) + EPS)
        qs_ref[0, hh] = _rope(xq * r * gsq_ref[...], c, s).astype(BF16)

    u_ref[0] = p[:, 1792:2048] * jax.nn.sigmoid(p[:, 2048:2304])


def _prep(xu, modsel, lw, cos_t, sin_t, t_off):
    B, T, D = xu.shape
    nt = T // TM - t_off
    ncol = lw["w_in"].shape[1]
    row = lambda b, i: (b, i + t_off, 0)
    head = lambda b, i: (b, 0, i + t_off, 0)
    in_specs = [
        pl.BlockSpec((1, TM, D), row),
        pl.BlockSpec((1, 1, 6, D), lambda b, i: (b, jnp.minimum(i + t_off, 1), 0, 0)),
        _full((1, D)),
        _full((D, ncol)),
        _full((1, MLA_KV_RANK)),
        _full((MLA_KV_RANK, 1024)),
        _full((1, MLA_Q_RANK)),
        _full((MLA_Q_RANK, 1024)),
        _full((1, 256)),
        _full((1, 128)),
        _full((1, 128)),
        _full((1, 128)),
        _full((1, 128)),
        pl.BlockSpec((TM, LANES), lambda b, i: (i + t_off, 0)),
        pl.BlockSpec((TM, LANES), lambda b, i: (i + t_off, 0)),
    ]
    out_shape = [
        jax.ShapeDtypeStruct((B, MLA_HEADS, T, 256), BF16),
        jax.ShapeDtypeStruct((B, MLA_HEADS, T, 256), BF16),
        jax.ShapeDtypeStruct((B, MLA_HEADS, T, 128), BF16),
        jax.ShapeDtypeStruct((B, SWA_HEADS, T, 128), BF16),
        jax.ShapeDtypeStruct((B, SWA_KV_HEADS, T, 128), BF16),
        jax.ShapeDtypeStruct((B, SWA_KV_HEADS, T, 128), BF16),
        jax.ShapeDtypeStruct((B, T, CONV_CH), F32),
    ]
    out_specs = [
        pl.BlockSpec((1, MLA_HEADS, TM, 256), head),
        pl.BlockSpec((1, MLA_HEADS, TM, 256), head),
        pl.BlockSpec((1, MLA_HEADS, TM, 128), head),
        pl.BlockSpec((1, SWA_HEADS, TM, 128), head),
        pl.BlockSpec((1, SWA_KV_HEADS, TM, 128), head),
        pl.BlockSpec((1, SWA_KV_HEADS, TM, 128), head),
        pl.BlockSpec((1, TM, CONV_CH), row),
    ]
    return pl.pallas_call(
        _prep_kernel,
        grid=(B, nt),
        in_specs=in_specs,
        out_specs=out_specs,
        out_shape=out_shape,
        compiler_params=_cp(("arbitrary", "arbitrary")),
        name="prep",
    )(xu, modsel, lw["n1"], lw["w_in"], lw["kvg"], lw["w_ukv"], lw["qg"], lw["w_uq"], lw["gq"], lw["gkn"], lw["gkp"],
      lw["gsq"], lw["gsk"], cos_t, sin_t)


def _mla_kernel(q_ref, k_ref, v_ref, o_ref, *, q_off, n_keys):
    qi = pl.program_id(1) + q_off

    def attend(nk):
        for h in range(MLA_HEADS):
            q = q_ref[0, h]
            k = k_ref[0, h, 0:nk, :]
            v = v_ref[0, h, 0:nk, :]
            s = lax.dot_general(q, k, (((1,), (1,)), ((), ())), preferred_element_type=F32)
            m = jnp.max(s, axis=-1, keepdims=True)
            p = jnp.exp(s - m)
            l = jnp.sum(p, axis=-1, keepdims=True)
            o = jnp.dot(p.astype(BF16), v, preferred_element_type=F32) / l
            o_ref[0, :, 128 * h:128 * h + 128] = o.astype(BF16)

    if q_off == 0:
        @pl.when(qi == 0)
        def _():
            attend(TM)

        @pl.when(qi > 0)
        def _():
            attend(n_keys)
    else:
        attend(n_keys)


def _mla(qm, km, vm, t_off):
    B, H, T, _ = qm.shape
    nt = T // TM - t_off
    return pl.pallas_call(
        functools.partial(_mla_kernel, q_off=t_off, n_keys=T),
        grid=(B, nt),
        in_specs=[
            pl.BlockSpec((1, H, TM, 256), lambda b, i: (b, 0, i + t_off, 0)),
            pl.BlockSpec((1, H, T, 256), lambda b, i: (b, 0, 0, 0)),
            pl.BlockSpec((1, H, T, 128), lambda b, i: (b, 0, 0, 0)),
        ],
        out_specs=pl.BlockSpec((1, TM, H * MLA_V), lambda b, i: (b, i, 0)),
        out_shape=jax.ShapeDtypeStruct((B, nt * TM, H * MLA_V), BF16),
        compiler_params=_cp(("arbitrary", "arbitrary")),
        name="mla_attn",
    )(qm, km, vm)


SWA_SPAN = TM + 2 * SWA_WINDOW


def _swa_kernel(sink_ref, q_ref, k_ref, v_ref, o_ref, *, q_off, n_rows):
    qi = pl.program_id(1) + q_off
    nt_dims = (((1,), (1,)), ((), ()))

    def latent():
        start = jnp.clip(qi * TM - SWA_WINDOW, 0, n_rows - SWA_SPAN)
        start = pl.multiple_of(start, SWA_WINDOW)
        qpos = qi * TM + lax.broadcasted_iota(I32, (TM, SWA_SPAN), 0)
        kpos = start + lax.broadcasted_iota(I32, (TM, SWA_SPAN), 1)
        valid = (jnp.abs(qpos - kpos) <= SWA_WINDOW) & (kpos >= TM)
        for h in range(SWA_HEADS):
            g = h // (SWA_HEADS // SWA_KV_HEADS)
            q = q_ref[0, h]
            kl = k_ref[0, g, pl.ds(start, SWA_SPAN), :]
            vl = v_ref[0, g, pl.ds(start, SWA_SPAN), :]
            kc = k_ref[0, g, 0:TM, :]
            vc = v_ref[0, g, 0:TM, :]
            sl = lax.dot_general(q, kl, nt_dims, preferred_element_type=F32)
            sl = jnp.where(valid, sl, NEG)
            scx = lax.dot_general(q, kc, nt_dims, preferred_element_type=F32)
            sink = sink_ref[h]
            m = jnp.maximum(jnp.maximum(jnp.max(sl, axis=-1, keepdims=True), jnp.max(scx, axis=-1, keepdims=True)), sink)
            pl_ = jnp.exp(sl - m)
            pc = jnp.exp(scx - m)
            l = jnp.sum(pl_, axis=-1, keepdims=True) + jnp.sum(pc, axis=-1, keepdims=True) + jnp.exp(sink - m)
            o = (jnp.dot(pl_.astype(BF16), vl, preferred_element_type=F32)
                 + jnp.dot(pc.astype(BF16), vc, preferred_element_type=F32)) / l
            o_ref[0, :, 128 * h:128 * h + 128] = o.astype(BF16)

    def context():
        for h in range(SWA_HEADS):
            g = h // (SWA_HEADS // SWA_KV_HEADS)
            q = q_ref[0, h]
            kc = k_ref[0, g, 0:TM, :]
            vc = v_ref[0, g, 0:TM, :]
            scx = lax.dot_general(q, kc, nt_dims, preferred_element_type=F32)
            sink = sink_ref[h]
            m = jnp.maximum(jnp.max(scx, axis=-1, keepdims=True), sink)
            pc = jnp.exp(scx - m)
            l = jnp.sum(pc, axis=-1, keepdims=True) + jnp.exp(sink - m)
            o = jnp.dot(pc.astype(BF16), vc, preferred_element_type=F32) / l
            o_ref[0, :, 128 * h:128 * h + 128] = o.astype(BF16)

    if q_off == 0:
        pl.when(qi == 0)(context)
        pl.when(qi > 0)(latent)
    else:
        latent()


def _swa(sink, qs, ks, vs, t_off):
    B, H, T, _ = qs.shape
    G = ks.shape[1]
    nt = T // TM - t_off
    return pl.pallas_call(
        functools.partial(_swa_kernel, q_off=t_off, n_rows=T),
        grid=(B, nt),
        in_specs=[
            pl.BlockSpec(memory_space=pltpu.SMEM),
            pl.BlockSpec((1, H, TM, 128), lambda b, i: (b, 0, i + t_off, 0)),
            pl.BlockSpec((1, G, T, 128), lambda b, i: (b, 0, 0, 0)),
            pl.BlockSpec((1, G, T, 128), lambda b, i: (b, 0, 0, 0)),
        ],
        out_specs=pl.BlockSpec((1, TM, H * 128), lambda b, i: (b, i, 0)),
        out_shape=jax.ShapeDtypeStruct((B, nt * TM, H * 128), BF16),
        compiler_params=_cp(("arbitrary", "arbitrary")),
        name="swa_attn",
    )(sink, qs, ks, vs)


def _conv_kernel(u_ref, w_ref, b_ref, g_ref, bb_ref, o_ref, buf_ref, *, t_off, n_rows):
    i = pl.program_id(1) + t_off
    nt_all = n_rows // TM
    start = pl.multiple_of(i * TM, TM)
    ps = pl.multiple_of(jnp.maximum(start - CONV_HALO, 0), 8)
    ns = pl.multiple_of(jnp.minimum(start + TM, n_rows - CONV_HALO), 8)
    keep_prev = jnp.where(i <= 1, 0.0, 1.0)
    keep_next = jnp.where((i == 0) | (i == nt_all - 1), 0.0, 1.0)
    buf_ref[0:CONV_HALO, :] = u_ref[0, pl.ds(ps, CONV_HALO), :] * keep_prev
    buf_ref[CONV_HALO:CONV_HALO + TM, :] = u_ref[0, pl.ds(start, TM), :]
    buf_ref[CONV_HALO + TM:2 * CONV_HALO + TM, :] = u_ref[0, pl.ds(ns, CONV_HALO), :] * keep_next
    off = CONV_HALO - CONV_WIDTH // 2
    acc = jnp.zeros((TM, CONV_CH), F32)
    for j in range(CONV_WIDTH):
        acc = acc + buf_ref[off + j:off + j + TM, :] * w_ref[j:j + 1, :]
    y = acc + b_ref[...]
    mu = jnp.mean(y, axis=-1, keepdims=True)
    d = y - mu
    var = jnp.mean(d * d, axis=-1, keepdims=True)
    z = d * lax.rsqrt(var + EPS) * g_ref[...] + bb_ref[...]
    o_ref[0] = (z * jax.nn.sigmoid(z)).astype(BF16)


def _conv(u, lw, t_off):
    B, T, C = u.shape
    nt = T // TM - t_off
    return pl.pallas_call(
        functools.partial(_conv_kernel, t_off=t_off, n_rows=T),
        grid=(B, nt),
        in_specs=[
            pl.BlockSpec((1, T, C), lambda b, i: (b, 0, 0)),
            _full((CONV_WIDTH, C)),
            _full((1, C)),
            _full((1, C)),
            _full((1, C)),
        ],
        out_specs=pl.BlockSpec((1, TM, C), lambda b, i: (b, i, 0)),
        out_shape=jax.ShapeDtypeStruct((B, nt * TM, C), BF16),
        scratch_shapes=[pltpu.VMEM((TM + 2 * CONV_HALO, C), F32)],
        compiler_params=_cp(("arbitrary", "arbitrary")),
        name="conv",
    )(u, lw["conv_w"], lw["conv_b"], lw["conv_g"], lw["conv_bb"])


def _out_kernel(om_ref, oc_ref, os_ref, x_ref, mod_ref, w1_ref, w2_ref, w3_ref, n2_ref, wrh_ref, wrl_ref, br_ref,
                tri_ref, xn_ref, hf_ref, e_ref, rk_ref, gt_ref, cnt_ref, run_ref):
    first = (pl.program_id(0) == 0) & (pl.program_id(1) == 0)

    @pl.when(first)
    def _():
        run_ref[...] = jnp.zeros_like(run_ref)

    d = functools.partial(jnp.dot, preferred_element_type=F32)
    mix = d(om_ref[0], w1_ref[...]) + d(oc_ref[0], w2_ref[...]) + d(os_ref[0], w3_ref[...])
    mod = mod_ref[0, 0]
    g1, sh2, sc2 = mod[2:3], mod[3:4], mod[4:5]
    xn = x_ref[0] + g1 * mix
    xn_ref[0] = xn
    hf = xn * lax.rsqrt(jnp.mean(xn * xn, axis=-1, keepdims=True) + EPS) * n2_ref[...]
    hf = hf * (1.0 + sc2) + sh2
    hf_ref[0] = hf

    hi, lo = _split(hf)
    logits = d(hi, wrh_ref[...]) + d(hi, wrl_ref[...]) + d(lo, wrh_ref[...]) + br_ref[...]
    lane = lax.broadcasted_iota(I32, (TM, LANES), 1)
    l = logits
    ohs, vals, idxs = [], [], []
    for _ in range(TOP_K):
        m = jnp.max(l, axis=-1, keepdims=True)
        idx = jnp.min(jnp.where(l == m, lane, LANES), axis=-1, keepdims=True)
        oh = lane == idx
        ohs.append(oh)
        vals.append(m)
        idxs.append(idx)
        l = jnp.where(oh, -jnp.inf, l)
    ex = [jnp.exp(v - vals[0]) for v in vals]
    den = ex[0] + ex[1] + ex[2] + ex[3]
    gates = [e / den for e in ex]

    oa = jnp.zeros((TM, LANES), F32)
    for oh in ohs:
        oa = oa + oh.astype(F32)
    tot = d(tri_ref[...], oa.astype(BF16)) + run_ref[0:1, :]
    rks = [jnp.sum(jnp.where(oh, tot, 0.0), axis=-1, keepdims=True).astype(I32) for oh in ohs]
    run_ref[...] = run_ref[...] + jnp.sum(oa, axis=0, keepdims=True)
    cnt_ref[...] = run_ref[...]

    e_out = jnp.zeros((TM, LANES), I32)
    r_out = jnp.zeros((TM, LANES), I32)
    g_out = jnp.zeros((TM, LANES), F32)
    for k in range(TOP_K):
        e_out = jnp.where(lane == k, idxs[k], e_out)
        r_out = jnp.where(lane == k, rks[k], r_out)
        g_out = jnp.where(lane == k, gates[k], g_out)
    e_ref[0] = e_out
    rk_ref[0] = r_out
    gt_ref[0] = g_out


def _outproj(om, oc, osw, xu, modsel, lw, tri, t_off):
    B, T, D = xu.shape
    nt = T // TM - t_off
    row = lambda b, i: (b, i, 0)
    Tq = nt * TM
    in_specs = [
        pl.BlockSpec((1, TM, 512), row),
        pl.BlockSpec((1, TM, 256), row),
        pl.BlockSpec((1, TM, 512), row),
        pl.BlockSpec((1, TM, D), lambda b, i: (b, i + t_off, 0)),
        pl.BlockSpec((1, 1, 6, D), lambda b, i: (b, jnp.minimum(i + t_off, 1), 0, 0)),
        _full((512, D)),
        _full((256, D)),
        _full((512, D)),
        _full((1, D)),
        _full((D, LANES)),
        _full((D, LANES)),
        _full((1, LANES)),
        _full((TM, TM)),
    ]
    out_shape = [
        jax.ShapeDtypeStruct((B, Tq, D), F32),
        jax.ShapeDtypeStruct((B, Tq, D), F32),
        jax.ShapeDtypeStruct((B, Tq, LANES), I32),
        jax.ShapeDtypeStruct((B, Tq, LANES), I32),
        jax.ShapeDtypeStruct((B, Tq, LANES), F32),
        jax.ShapeDtypeStruct((8, LANES), F32),
    ]
    out_specs = [
        pl.BlockSpec((1, TM, D), row),
        pl.BlockSpec((1, TM, D), row),
        pl.BlockSpec((1, TM, LANES), row),
        pl.BlockSpec((1, TM, LANES), row),
        pl.BlockSpec((1, TM, LANES), row),
        pl.BlockSpec((8, LANES), lambda b, i: (0, 0)),
    ]
    return pl.pallas_call(
        _out_kernel,
        grid=(B, nt),
        in_specs=in_specs,
        out_specs=out_specs,
        out_shape=out_shape,
        scratch_shapes=[pltpu.VMEM((8, LANES), F32)],
        compiler_params=_cp(("arbitrary", "arbitrary")),
        name="outproj_router",
    )(om, oc, osw, xu, modsel, lw["w_o1"], lw["w_o2"], lw["w_o3"], lw["n2"], lw["wr_hi"], lw["wr_lo"], lw["br"], tri)


def _row_copy(src_ref, src_row, dst_ref, dst_row, sem):
    return pltpu.make_async_copy(src_ref.at[pl.ds(src_row, 1), :], dst_ref.at[pl.ds(dst_row, 1), :], sem)


def _disp_kernel(pst_ref, cnt_ref, na_ref, e_ref, rk_ref, hf_ref, xb_ref, zrow_ref, sem, zsem, *, n_blocks):
    first = (pl.program_id(0) == 0) & (pl.program_id(1) == 0)

    @pl.when(first)
    def _():
        zrow_ref[...] = jnp.zeros_like(zrow_ref)

        def zblock(j):
            return pltpu.make_async_copy(zrow_ref, xb_ref.at[pl.ds(pl.multiple_of(j * EXPERT_BLOCK, EXPERT_BLOCK), EXPERT_BLOCK), :], zsem)

        def zb_start(j, carry):
            zblock(j).start()
            return carry

        def zb_wait(j, carry):
            zblock(j).wait()
            return carry

        lax.fori_loop(na_ref[0], n_blocks, zb_start, 0)
        lax.fori_loop(na_ref[0], n_blocks, zb_wait, 0)

        for e in range(N_EXPERTS):
            cnt = cnt_ref[e]
            base = pst_ref[e] + cnt
            npad = (EXPERT_BLOCK - cnt % EXPERT_BLOCK) % EXPERT_BLOCK

            def zfill(r, carry):
                _row_copy(zrow_ref, 0, xb_ref, base + r, zsem).start()
                return carry

            def zwait(r, carry):
                _row_copy(zrow_ref, 0, xb_ref, base, zsem).wait()
                return carry

            lax.fori_loop(0, npad, zfill, 0)
            lax.fori_loop(0, npad, zwait, 0)

    src = hf_ref.at[0]

    def issue(t, carry):
        for k in range(TOP_K):
            dst = pst_ref[e_ref[0, 0, TOP_K * t + k]] + rk_ref[0, 0, TOP_K * t + k]
            _row_copy(src, t, xb_ref, dst, sem.at[k]).start()
        return carry

    lax.fori_loop(0, TM, issue, 0, unroll=4)

    def drain(t, carry):
        for k in range(TOP_K):
            _row_copy(src, 0, xb_ref, 0, sem.at[k]).wait()
        return carry

    lax.fori_loop(0, TM, drain, 0, unroll=16)


def _dispatch(pstart, counts, n_act, e_flat, rk_flat, hf, n_buf):
    B, Tq, D = hf.shape
    nt = Tq // TM
    grid_spec = pltpu.PrefetchScalarGridSpec(
        num_scalar_prefetch=3,
        grid=(B, nt),
        in_specs=[
            pl.BlockSpec((1, 1, TM * TOP_K), lambda b, i, *_: (b * nt + i, 0, 0), memory_space=pltpu.SMEM),
            pl.BlockSpec((1, 1, TM * TOP_K), lambda b, i, *_: (b * nt + i, 0, 0), memory_space=pltpu.SMEM),
            pl.BlockSpec((1, TM, D), lambda b, i, *_: (b, i, 0)),
        ],
        out_specs=pl.BlockSpec(memory_space=pl.ANY),
        scratch_shapes=[pltpu.VMEM((EXPERT_BLOCK, D), F32), pltpu.SemaphoreType.DMA((TOP_K,)), pltpu.SemaphoreType.DMA],
    )
    return pl.pallas_call(
        functools.partial(_disp_kernel, n_blocks=n_buf // EXPERT_BLOCK),
        grid_spec=grid_spec,
        out_shape=jax.ShapeDtypeStruct((n_buf, D), F32),
        compiler_params=_cp(("arbitrary", "arbitrary")),
        name="dispatch",
    )(pstart, counts, n_act, e_flat, rk_flat, hf)


W_ROWS = 256


def _exp_kernel(be_ref, bx_ref, na_ref, x_ref, w1_ref, b1a_ref, b1b_ref, w2_ref, b2_ref, o_ref, w1a_s, w1b_s, w2_s):
    j = pl.program_id(0)
    active = j < na_ref[0]
    fresh = (j == 0) | (be_ref[j] != be_ref[jnp.maximum(j - 1, 0)])

    @pl.when(active & fresh)
    def _():
        lane = lax.broadcasted_iota(I32, (W_ROWS, LANES), 1)
        lo = lane < 64
        idx = jnp.where(lo, 2 * lane, 2 * (lane - 64) + 1)

        def rows(i, carry):
            r0 = pl.multiple_of(i * W_ROWS, W_ROWS)
            for c in range(w1_ref.shape[2] // (2 * LANES)):
                a = w1_ref[0, pl.ds(r0, W_ROWS), 2 * LANES * c:2 * LANES * c + LANES]
                b = w1_ref[0, pl.ds(r0, W_ROWS), 2 * LANES * c + LANES:2 * LANES * (c + 1)]
                pa = jnp.take_along_axis(a, idx, axis=1)
                pb = jnp.take_along_axis(b, idx, axis=1)
                ev = jnp.where(lo, pa, pltpu.roll(pb, 64, axis=1))
                od = jnp.where(lo, pltpu.roll(pa, 64, axis=1), pb)
                w1a_s[pl.ds(r0, W_ROWS), LANES * c:LANES * (c + 1)] = ev.astype(BF16)
                w1b_s[pl.ds(r0, W_ROWS), LANES * c:LANES * (c + 1)] = od.astype(BF16)
            w2_s[pl.ds(r0, W_ROWS), :] = w2_ref[0, pl.ds(r0, W_ROWS), :].astype(BF16)
            return carry

        lax.fori_loop(0, w1_ref.shape[1] // W_ROWS, rows, 0)

    @pl.when(active)
    def _():
        x = x_ref[...].astype(BF16)
        ug = jnp.dot(x, w1a_s[...], preferred_element_type=F32) + b1a_ref[0]
        ul = jnp.dot(x, w1b_s[...], preferred_element_type=F32) + b1b_ref[0]
        xg = jnp.minimum(ug, SWIGLU_LIMIT)
        xl = jnp.clip(ul, -SWIGLU_LIMIT, SWIGLU_LIMIT)
        act = xg * jax.nn.sigmoid(SWIGLU_ALPHA * xg) * (xl + 1.0)
        o_ref[...] = jnp.dot(act.astype(BF16), w2_s[...], preferred_element_type=F32) + b2_ref[0]

    @pl.when(jnp.logical_not(active))
    def _():
        o_ref[...] = jnp.zeros_like(o_ref)


def _experts(blk_e, blk_x, n_act, xb, lw):
    n_buf, D = xb.shape
    nb = n_buf // EXPERT_BLOCK
    De = lw["w2"].shape[1]
    assert De == D
    wmap = lambda j, be, bx, na: (be[j], 0, 0)
    grid_spec = pltpu.PrefetchScalarGridSpec(
        num_scalar_prefetch=3,
        grid=(nb,),
        in_specs=[
            pl.BlockSpec((EXPERT_BLOCK, D), lambda j, be, bx, na: (bx[j], 0)),
            pl.BlockSpec((1, D, 2 * De), wmap),
            pl.BlockSpec((1, 1, De), wmap),
            pl.BlockSpec((1, 1, De), wmap),
            pl.BlockSpec((1, De, D), wmap),
            pl.BlockSpec((1, 1, D), wmap),
        ],
        out_specs=pl.BlockSpec((EXPERT_BLOCK, D), lambda j, be, bx, na: (j, 0)),
        scratch_shapes=[pltpu.VMEM((D, De), BF16), pltpu.VMEM((D, De), BF16), pltpu.VMEM((De, D), BF16)],
    )
    return pl.pallas_call(
        _exp_kernel,
        grid_spec=grid_spec,
        out_shape=jax.ShapeDtypeStruct((n_buf, D), F32),
        compiler_params=pltpu.CompilerParams(dimension_semantics=("arbitrary",), vmem_limit_bytes=EXPERT_VMEM_LIMIT),
        name="experts",
    )(blk_e, blk_x, n_act, xb, lw["w1"], lw["b1a"], lw["b1b"], lw["w2"], lw["b2"])


def _comb_kernel(pst_ref, e_ref, rk_ref, gt_ref, xn_ref, mod_ref, yb_ref, o_ref, buf_ref, sem):
    def issue(t, carry):
        for k in range(TOP_K):
            srow = pst_ref[e_ref[0, 0, TOP_K * t + k]] + rk_ref[0, 0, TOP_K * t + k]
            _row_copy(yb_ref, srow, buf_ref.at[k], t, sem.at[k]).start()
        return carry

    lax.fori_loop(0, TM, issue, 0, unroll=4)

    def drain(t, carry):
        for k in range(TOP_K):
            _row_copy(yb_ref, 0, buf_ref.at[k], 0, sem.at[k]).wait()
        return carry

    lax.fori_loop(0, TM, drain, 0, unroll=16)

    gt = gt_ref[0]
    y = gt[:, 0:1] * buf_ref[0]
    for k in range(1, TOP_K):
        y = y + gt[:, k:k + 1] * buf_ref[k]
    g2 = mod_ref[0, 0][5:6]
    o_ref[0] = xn_ref[0] + g2 * y


def _combine(pstart, e_flat, rk_flat, gates, xn, modsel, yb, t_off):
    B, Tq, D = xn.shape
    nt = Tq // TM
    grid_spec = pltpu.PrefetchScalarGridSpec(
        num_scalar_prefetch=1,
        grid=(B, nt),
        in_specs=[
            pl.BlockSpec((1, 1, TM * TOP_K), lambda b, i, *_: (b * nt + i, 0, 0), memory_space=pltpu.SMEM),
            pl.BlockSpec((1, 1, TM * TOP_K), lambda b, i, *_: (b * nt + i, 0, 0), memory_space=pltpu.SMEM),
            pl.BlockSpec((1, TM, LANES), lambda b, i, *_: (b, i, 0)),
            pl.BlockSpec((1, TM, D), lambda b, i, *_: (b, i, 0)),
            pl.BlockSpec((1, 1, 6, D), lambda b, i, *_: (b, jnp.minimum(i + t_off, 1), 0, 0)),
            pl.BlockSpec(memory_space=pl.ANY),
        ],
        out_specs=pl.BlockSpec((1, TM, D), lambda b, i, *_: (b, i, 0)),
        scratch_shapes=[pltpu.VMEM((TOP_K, TM, D), F32), pltpu.SemaphoreType.DMA((TOP_K,))],
    )
    return pl.pallas_call(
        _comb_kernel,
        grid_spec=grid_spec,
        out_shape=jax.ShapeDtypeStruct((B, Tq, D), F32),
        compiler_params=_cp(("arbitrary", "arbitrary")),
        name="combine",
    )(pstart, e_flat, rk_flat, gates, xn, modsel, yb)


def _take_cols(w, cols):
    cols = np.asarray(cols)
    out = jnp.take(w, jnp.asarray(np.maximum(cols, 0)), axis=-1)
    return jnp.where(jnp.asarray(cols >= 0), out, 0.0)


def _in_cols():
    pi = _PI
    cols = list(range(0, 256))
    cols += [256 + i for i in range(64)] + [256 + pi[i] for i in range(64)]
    for g in range(SWA_KV_HEADS):
        base = 320 + 64 * g
        cols += [base + i for i in range(64)] + [base + pi[i] for i in range(64)]
    for g in range(SWA_KV_HEADS):
        base = 448 + 64 * g
        cols += [base + i for i in range(64)] + [-1] * 64
    cols += list(range(576, 960))
    for h in range(SWA_HEADS):
        base = 960 + 64 * h
        cols += [base + i for i in range(64)] + [base + pi[i] for i in range(64)]
    cols += list(range(1216, 1728))
    return cols


def _layer_weights(l, a):
    pi = _PI
    lw = {}
    lw["n1"] = a["norm1_g"][l][None, :]
    lw["n2"] = a["norm2_g"][l][None, :]
    lw["w_in"] = _take_cols(a["w_in"][l], _in_cols()).astype(BF16)
    lw["kvg"] = a["mla_kv_norm"][l][None, :]
    lw["qg"] = a["mla_q_norm"][l][None, :]
    uq_cols = []
    for h in range(MLA_HEADS):
        base = MLA_QK * h
        uq_cols += [base + i for i in range(128)] + [base + 128 + i for i in range(64)] + [base + 128 + pi[i] for i in range(64)]
    lw["w_uq"] = _take_cols(a["mla_w_uq"][l], uq_cols).astype(BF16)
    ukv_cols = [256 * h + i for h in range(MLA_HEADS) for i in range(128)]
    ukv_cols += [256 * h + 128 + i for h in range(MLA_HEADS) for i in range(128)]
    lw["w_ukv"] = _take_cols(a["mla_w_ukv"][l], ukv_cols).astype(BF16)
    gq = a["mla_q_head_norm"][l]
    lw["gq"] = (jnp.concatenate([gq[:128], gq[128:], gq[128:][pi]]) * (MLA_QK ** -0.5))[None, :]
    gk = a["mla_k_head_norm"][l]
    lw["gkn"] = gk[:128][None, :]
    lw["gkp"] = jnp.concatenate([gk[128:], gk[128:][pi]])[None, :]
    sq = a["swa_q_norm"][l]
    lw["gsq"] = (jnp.concatenate([sq, sq[pi]]) * (SWA_HEAD_DIM ** -0.5))[None, :]
    sk = a["swa_k_norm"][l]
    lw["gsk"] = jnp.concatenate([sk, sk[pi]])[None, :]
    lw["conv_w"] = a["conv_w"][l]
    lw["conv_b"] = a["conv_b"][l][None, :]
    lw["conv_g"] = a["conv_ln_g"][l][None, :]
    lw["conv_bb"] = a["conv_ln_b"][l][None, :]
    lw["sink"] = a["swa_sink"][l]
    wo = a["w_out"][l]
    lw["w_o1"] = wo[0:512].astype(BF16)
    lw["w_o2"] = wo[512:768].astype(BF16)
    o3 = wo[768:1024].reshape(SWA_HEADS, SWA_HEAD_DIM, -1)
    lw["w_o3"] = jnp.concatenate([o3, jnp.zeros_like(o3)], axis=1).reshape(SWA_HEADS * 128, -1).astype(BF16)
    wr = jnp.pad(a["router_w"][l], ((0, 0), (0, LANES - N_EXPERTS)))
    lw["wr_hi"] = wr.astype(BF16)
    lw["wr_lo"] = (wr - lw["wr_hi"].astype(F32)).astype(BF16)
    lw["br"] = jnp.pad(a["router_b"][l], (0, LANES - N_EXPERTS), constant_values=NEG)[None, :]
    lw["w1"] = a["exp_w1"][l]
    b1 = a["exp_b1"][l]
    lw["b1a"] = b1[:, None, 0::2]
    lw["b1b"] = b1[:, None, 1::2]
    lw["w2"] = a["exp_w2"][l]
    lw["b2"] = a["exp_b2"][l][:, None, :]
    return lw


def _rope_tables(n_ctx, n_lat):
    q = MLA_ROPE // 4
    n = jnp.arange(n_lat, dtype=I32)
    row = (n // GRID_W).astype(F32)
    col = (n % GRID_W).astype(F32)
    inv = ROPE_BASE ** (-jnp.arange(q, dtype=F32) / q)
    ang_r = row[:, None] * inv
    ang_c = col[:, None] * inv
    cos = jnp.concatenate([jnp.cos(ang_r), jnp.cos(ang_r), jnp.cos(ang_c), jnp.cos(ang_c)], axis=1)
    sin = jnp.concatenate([-jnp.sin(ang_r), jnp.sin(ang_r), -jnp.sin(ang_c), jnp.sin(ang_c)], axis=1)
    cos = jnp.concatenate([jnp.ones((n_ctx, 64), F32), cos], axis=0)
    sin = jnp.concatenate([jnp.zeros((n_ctx, 64), F32), sin], axis=0)
    z = jnp.zeros_like(cos)
    return jnp.concatenate([cos, z], axis=1), jnp.concatenate([sin, z], axis=1)


def _routing_tables(cnt_f, n_blocks):
    counts = cnt_f[0, :N_EXPERTS].astype(I32)
    padded = (counts + EXPERT_BLOCK - 1) // EXPERT_BLOCK * EXPERT_BLOCK
    pend = jnp.cumsum(padded)
    pstart = pend - padded
    n_act = pend[-1] // EXPERT_BLOCK
    blk = jnp.minimum(jnp.arange(n_blocks, dtype=I32), n_act - 1)
    blk_e = jnp.sum((pend[None, :] <= (blk * EXPERT_BLOCK)[:, None]).astype(I32), axis=1)
    blk_e = jnp.minimum(blk_e, N_EXPERTS - 1)
    return counts, pstart.astype(I32), blk_e, blk.astype(I32), n_act.reshape(1).astype(I32)


def kernel(x, c, ctx, c_ctx, norm1_g, norm2_g, w_ada, b_ada, w_in, mla_q_norm, mla_kv_norm, mla_w_uq, mla_w_ukv, mla_q_head_norm, mla_k_head_norm, conv_w, conv_b, conv_ln_g, conv_ln_b, swa_q_norm, swa_k_norm, swa_sink, w_out, router_w, router_b, exp_w1, exp_b1, exp_w2, exp_b2):
    a = dict(norm1_g=norm1_g, norm2_g=norm2_g, w_in=w_in, mla_q_norm=mla_q_norm, mla_kv_norm=mla_kv_norm,
             mla_w_uq=mla_w_uq, mla_w_ukv=mla_w_ukv, mla_q_head_norm=mla_q_head_norm, mla_k_head_norm=mla_k_head_norm,
             conv_w=conv_w, conv_b=conv_b, conv_ln_g=conv_ln_g, conv_ln_b=conv_ln_b, swa_q_norm=swa_q_norm,
             swa_k_norm=swa_k_norm, swa_sink=swa_sink, w_out=w_out, router_w=router_w, router_b=router_b,
             exp_w1=exp_w1, exp_b1=exp_b1, exp_w2=exp_w2, exp_b2=exp_b2)
    B, S, D = x.shape
    n_ctx = ctx.shape[1]
    depth = w_ada.shape[0]
    assert n_ctx == TM and S % TM == 0 and B + 1 <= 16
    T = n_ctx + S

    s_in = jnp.zeros((16, D), F32).at[:B].set(c).at[B].set(c_ctx)
    mods = _ada(s_in, w_ada, b_ada)
    cos_t, sin_t = _rope_tables(n_ctx, S)
    tri = jnp.tril(jnp.ones((TM, TM), F32), -1).astype(BF16)

    xu = jnp.concatenate([ctx, x], axis=1)
    for l in range(depth):
        last = l == depth - 1
        t_off = 1 if last else 0
        lw = _layer_weights(l, a)
        m = mods[l].reshape(16, 6, D)
        modsel = jnp.stack([jnp.broadcast_to(m[B], (B, 6, D)), m[:B]], axis=1)

        qm, km, vm, qs, ks, vs, u = _prep(xu, modsel, lw, cos_t, sin_t, 0)
        om = _mla(qm, km, vm, t_off)
        oc = _conv(u, lw, t_off)
        osw = _swa(lw["sink"], qs, ks, vs, t_off)
        xn, hf, e_o, rk_o, gt_o, cnt = _outproj(om, oc, osw, xu, modsel, lw, tri, t_off)

        n_tok = B * (T - t_off * TM)
        nk = n_tok * TOP_K
        n_buf = -(-nk // EXPERT_BLOCK) * EXPERT_BLOCK + N_EXPERTS * EXPERT_BLOCK
        counts, pstart, blk_e, blk_x, n_act = _routing_tables(cnt, n_buf // EXPERT_BLOCK)
        e_flat = e_o[:, :, :TOP_K].reshape(-1, 1, TM * TOP_K)
        rk_flat = rk_o[:, :, :TOP_K].reshape(-1, 1, TM * TOP_K)
        xb = _dispatch(pstart, counts, n_act, e_flat, rk_flat, hf, n_buf)
        yb = _experts(blk_e, blk_x, n_act, xb, lw)
        xu = _combine(pstart, e_flat, rk_flat, gt_o, xn, modsel, yb, t_off)
    return xu
```

```python
import functools

import numpy as np
import jax
import jax.numpy as jnp
from jax import lax
from jax.experimental import pallas as pl
from jax.experimental.pallas import tpu as pltpu

F32 = jnp.float32
BF16 = jnp.bfloat16
I32 = jnp.int32

GRID_W = 64
ROPE_BASE = 10000.0
EPS = 1e-6
MLA_HEADS = 4
MLA_NOPE = 128
MLA_ROPE = 64
MLA_V = 128
MLA_QK = MLA_NOPE + MLA_ROPE
MLA_Q_RANK = 384
MLA_KV_RANK = 256
CONV_CH = 256
CONV_WIDTH = 31
SWA_HEADS = 4
SWA_KV_HEADS = 2
SWA_HEAD_DIM = 64
SWA_WINDOW = 128
N_EXPERTS = 32
TOP_K = 4
SWIGLU_LIMIT = 7.0
SWIGLU_ALPHA = 1.702
EXPERT_BLOCK = 256

LANES = 128
TM = 256
CONV_HALO = 16
VMEM_LIMIT = 48 * 1024 * 1024
EXPERT_VMEM_LIMIT = 56 * 1024 * 1024
NEG = -1e30

_PI = np.array([i + 16 if (i % 32) < 16 else i - 16 for i in range(64)])


def _cp(sem):
    return pltpu.CompilerParams(dimension_semantics=sem, vmem_limit_bytes=VMEM_LIMIT)


def _full(shape):
    n = len(shape)
    return pl.BlockSpec(shape, lambda *a, _n=n: (0,) * _n)


def _split(x):
    hi = x.astype(BF16)
    lo = (x - hi.astype(F32)).astype(BF16)
    return hi, lo


def _dot3(a, b):
    ah, al = _split(a)
    bh, bl = _split(b)
    d = functools.partial(jnp.dot, preferred_element_type=F32)
    return d(ah, bh) + d(ah, bl) + d(al, bh)


def _ada_kernel(s_ref, w_ref, b_ref, o_ref):
    s = s_ref[...]
    s = s * jax.nn.sigmoid(s)
    o_ref[0] = _dot3(s, w_ref[0]) + b_ref[0]


def _ada(s_in, w_ada, b_ada):
    L, D, N = w_ada.shape
    tn = 1536
    return pl.pallas_call(
        _ada_kernel,
        grid=(L, N // tn),
        in_specs=[
            pl.BlockSpec((16, D), lambda l, j: (0, 0)),
            pl.BlockSpec((1, D, tn), lambda l, j: (l, 0, j)),
            pl.BlockSpec((1, 1, tn), lambda l, j: (l, 0, j)),
        ],
        out_specs=pl.BlockSpec((1, 16, tn), lambda l, j: (l, 0, j)),
        out_shape=jax.ShapeDtypeStruct((L, 16, N), F32),
        compiler_params=_cp(("arbitrary", "arbitrary")),
        name="ada",
    )(s_in, w_ada, b_ada.reshape(L, 1, N))


def _rope(x, c, s):
    return x * c + pltpu.roll(x, 64, axis=1) * s


def _prep_kernel(x_ref, mod_ref, n1_ref, win_ref, kvg_ref, wukv_ref, qg_ref, wuq_ref, gq_ref, gkn_ref, gkp_ref,
                 gsq_ref, gsk_ref, cos_ref, sin_ref, qm_ref, km_ref, vm_ref, qs_ref, ks_ref, vs_ref, u_ref):
    x = x_ref[0]
    mod = mod_ref[0, 0]
    sh, sc = mod[0:1], mod[1:2]
    y = x * lax.rsqrt(jnp.mean(x * x, axis=-1, keepdims=True) + EPS) * n1_ref[...]
    h = y * (1.0 + sc) + sh
    p = jnp.dot(h.astype(BF16), win_ref[...], preferred_element_type=F32)
    c = cos_ref[...]
    s = sin_ref[...]
    lane = lax.broadcasted_iota(I32, (TM, LANES), 1)
    lo = lane < 64

    def ss_lo(v):
        return jnp.sum(jnp.where(lo, v * v, 0.0), axis=-1, keepdims=True)

    ckv = p[:, 0:256]
    ckvn = ckv * lax.rsqrt(jnp.mean(ckv * ckv, axis=-1, keepdims=True) + EPS) * kvg_ref[...]
    kv = jnp.dot(ckvn.astype(BF16), wukv_ref[...], preferred_element_type=F32)
    kpe = p[:, 256:384]
    ss_pe = ss_lo(kpe)
    kpe_rot = _rope(kpe * gkp_ref[...], c, s)
    for hh in range(MLA_HEADS):
        kn = kv[:, 128 * hh:128 * hh + 128]
        r = lax.rsqrt((jnp.sum(kn * kn, axis=-1, keepdims=True) + ss_pe) * (1.0 / MLA_QK) + EPS)
        km_ref[0, hh, :, 0:128] = (kn * r * gkn_ref[...]).astype(BF16)
        km_ref[0, hh, :, 128:256] = (kpe_rot * r).astype(BF16)
        vm_ref[0, hh] = kv[:, 512 + 128 * hh:640 + 128 * hh].astype(BF16)

    cq = p[:, 896:1280]
    cqn = cq * lax.rsqrt(jnp.mean(cq * cq, axis=-1, keepdims=True) + EPS) * qg_ref[...]
    q = jnp.dot(cqn.astype(BF16), wuq_ref[...], preferred_element_type=F32)
    gq = gq_ref[...]
    for hh in range(MLA_HEADS):
        qn = q[:, 256 * hh:256 * hh + 128]
        qp = q[:, 256 * hh + 128:256 * hh + 256]
        r = lax.rsqrt((jnp.sum(qn * qn, axis=-1, keepdims=True) + ss_lo(qp)) * (1.0 / MLA_QK) + EPS)
        qm_ref[0, hh, :, 0:128] = (qn * r * gq[:, 0:128]).astype(BF16)
        qm_ref[0, hh, :, 128:256] = _rope(qp * r * gq[:, 128:256], c, s).astype(BF16)

    for g in range(SWA_KV_HEADS):
        xk = p[:, 384 + 128 * g:512 + 128 * g]
        r = lax.rsqrt(ss_lo(xk) * (1.0 / SWA_HEAD_DIM) + EPS)
        ks_ref[0, g] = _rope(xk * r * gsk_ref[...], c, s).astype(BF16)
        vs_ref[0, g] = p[:, 640 + 128 * g:768 + 128 * g].astype(BF16)
    for hh in range(SWA_HEADS):
        xq = p[:, 1280 + 128 * hh:1408 + 128 * hh]
        r = lax.rsqrt(ss_lo(xq) * (1.0 / SWA_HEAD_DIM) + EPS)
        qs_ref[0, hh] = _rope(xq * r * gsq_ref[...], c, s).astype(BF16)

    u_ref[0] = p[:, 1792:2048] * jax.nn.sigmoid(p[:, 2048:2304])


def _prep(xu, modsel, lw, cos_t, sin_t, t_off):
    B, T, D = xu.shape
    nt = T // TM - t_off
    ncol = lw["w_in"].shape[1]
    row = lambda b, i: (b, i + t_off, 0)
    head = lambda b, i: (b, 0, i + t_off, 0)
    in_specs = [
        pl.BlockSpec((1, TM, D), row),
        pl.BlockSpec((1, 1, 6, D), lambda b, i: (b, jnp.minimum(i + t_off, 1), 0, 0)),
        _full((1, D)),
        _full((D, ncol)),
        _full((1, MLA_KV_RANK)),
        _full((MLA_KV_RANK, 1024)),
        _full((1, MLA_Q_RANK)),
        _full((MLA_Q_RANK, 1024)),
        _full((1, 256)),
        _full((1, 128)),
        _full((1, 128)),
        _full((1, 128)),
        _full((1, 128)),
        pl.BlockSpec((TM, LANES), lambda b, i: (i + t_off, 0)),
        pl.BlockSpec((TM, LANES), lambda b, i: (i + t_off, 0)),
    ]
    out_shape = [
        jax.ShapeDtypeStruct((B, MLA_HEADS, T, 256), BF16),
        jax.ShapeDtypeStruct((B, MLA_HEADS, T, 256), BF16),
        jax.ShapeDtypeStruct((B, MLA_HEADS, T, 128), BF16),
        jax.ShapeDtypeStruct((B, SWA_HEADS, T, 128), BF16),
        jax.ShapeDtypeStruct((B, SWA_KV_HEADS, T, 128), BF16),
        jax.ShapeDtypeStruct((B, SWA_KV_HEADS, T, 128), BF16),
        jax.ShapeDtypeStruct((B, T, CONV_CH), F32),
    ]
    out_specs = [
        pl.BlockSpec((1, MLA_HEADS, TM, 256), head),
        pl.BlockSpec((1, MLA_HEADS, TM, 256), head),
        pl.BlockSpec((1, MLA_HEADS, TM, 128), head),
        pl.BlockSpec((1, SWA_HEADS, TM, 128), head),
        pl.BlockSpec((1, SWA_KV_HEADS, TM, 128), head),
        pl.BlockSpec((1, SWA_KV_HEADS, TM, 128), head),
        pl.BlockSpec((1, TM, CONV_CH), row),
    ]
    return pl.pallas_call(
        _prep_kernel,
        grid=(B, nt),
        in_specs=in_specs,
        out_specs=out_specs,
        out_shape=out_shape,
        compiler_params=_cp(("arbitrary", "arbitrary")),
        name="prep",
    )(xu, modsel, lw["n1"], lw["w_in"], lw["kvg"], lw["w_ukv"], lw["qg"], lw["w_uq"], lw["gq"], lw["gkn"], lw["gkp"],
      lw["gsq"], lw["gsk"], cos_t, sin_t)


def _mla_kernel(q_ref, k_ref, v_ref, o_ref, *, q_off, n_keys):
    qi = pl.program_id(1) + q_off

    def attend(nk):
        for h in range(MLA_HEADS):
            q = q_ref[0, h]
            k = k_ref[0, h, 0:nk, :]
            v = v_ref[0, h, 0:nk, :]
            s = lax.dot_general(q, k, (((1,), (1,)), ((), ())), preferred_element_type=F32)
            m = jnp.max(s, axis=-1, keepdims=True)
            p = jnp.exp(s - m)
            l = jnp.sum(p, axis=-1, keepdims=True)
            o = jnp.dot(p.astype(BF16), v, preferred_element_type=F32) / l
            o_ref[0, :, 128 * h:128 * h + 128] = o.astype(BF16)

    if q_off == 0:
        @pl.when(qi == 0)
        def _():
            attend(TM)

        @pl.when(qi > 0)
        def _():
            attend(n_keys)
    else:
        attend(n_keys)


def _mla(qm, km, vm, t_off):
    B, H, T, _ = qm.shape
    nt = T // TM - t_off
    return pl.pallas_call(
        functools.partial(_mla_kernel, q_off=t_off, n_keys=T),
        grid=(B, nt),
        in_specs=[
            pl.BlockSpec((1, H, TM, 256), lambda b, i: (b, 0, i + t_off, 0)),
            pl.BlockSpec((1, H, T, 256), lambda b, i: (b, 0, 0, 0)),
            pl.BlockSpec((1, H, T, 128), lambda b, i: (b, 0, 0, 0)),
        ],
        out_specs=pl.BlockSpec((1, TM, H * MLA_V), lambda b, i: (b, i, 0)),
        out_shape=jax.ShapeDtypeStruct((B, nt * TM, H * MLA_V), BF16),
        compiler_params=_cp(("arbitrary", "arbitrary")),
        name="mla_attn",
    )(qm, km, vm)


SWA_SPAN = TM + 2 * SWA_WINDOW


def _swa_kernel(sink_ref, q_ref, k_ref, v_ref, o_ref, *, q_off, n_rows):
    qi = pl.program_id(1) + q_off
    nt_dims = (((1,), (1,)), ((), ()))

    def latent():
        start = jnp.clip(qi * TM - SWA_WINDOW, 0, n_rows - SWA_SPAN)
        start = pl.multiple_of(start, SWA_WINDOW)
        qpos = qi * TM + lax.broadcasted_iota(I32, (TM, SWA_SPAN), 0)
        kpos = start + lax.broadcasted_iota(I32, (TM, SWA_SPAN), 1)
        valid = (jnp.abs(qpos - kpos) <= SWA_WINDOW) & (kpos >= TM)
        for h in range(SWA_HEADS):
            g = h // (SWA_HEADS // SWA_KV_HEADS)
            q = q_ref[0, h]
            kl = k_ref[0, g, pl.ds(start, SWA_SPAN), :]
            vl = v_ref[0, g, pl.ds(start, SWA_SPAN), :]
            kc = k_ref[0, g, 0:TM, :]
            vc = v_ref[0, g, 0:TM, :]
            sl = lax.dot_general(q, kl, nt_dims, preferred_element_type=F32)
            sl = jnp.where(valid, sl, NEG)
            scx = lax.dot_general(q, kc, nt_dims, preferred_element_type=F32)
            sink = sink_ref[h]
            m = jnp.maximum(jnp.maximum(jnp.max(sl, axis=-1, keepdims=True), jnp.max(scx, axis=-1, keepdims=True)), sink)
            pl_ = jnp.exp(sl - m)
            pc = jnp.exp(scx - m)
            l = jnp.sum(pl_, axis=-1, keepdims=True) + jnp.sum(pc, axis=-1, keepdims=True) + jnp.exp(sink - m)
            o = (jnp.dot(pl_.astype(BF16), vl, preferred_element_type=F32)
                 + jnp.dot(pc.astype(BF16), vc, preferred_element_type=F32)) / l
            o_ref[0, :, 128 * h:128 * h + 128] = o.astype(BF16)

    def context():
        for h in range(SWA_HEADS):
            g = h // (SWA_HEADS // SWA_KV_HEADS)
            q = q_ref[0, h]
            kc = k_ref[0, g, 0:TM, :]
            vc = v_ref[0, g, 0:TM, :]
            scx = lax.dot_general(q, kc, nt_dims, preferred_element_type=F32)
            sink = sink_ref[h]
            m = jnp.maximum(jnp.max(scx, axis=-1, keepdims=True), sink)
            pc = jnp.exp(scx - m)
            l = jnp.sum(pc, axis=-1, keepdims=True) + jnp.exp(sink - m)
            o = jnp.dot(pc.astype(BF16), vc, preferred_element_type=F32) / l
            o_ref[0, :, 128 * h:128 * h + 128] = o.astype(BF16)

    if q_off == 0:
        pl.when(qi == 0)(context)
        pl.when(qi > 0)(latent)
    else:
        latent()


def _swa(sink, qs, ks, vs, t_off):
    B, H, T, _ = qs.shape
    G = ks.shape[1]
    nt = T // TM - t_off
    return pl.pallas_call(
        functools.partial(_swa_kernel, q_off=t_off, n_rows=T),
        grid=(B, nt),
        in_specs=[
            pl.BlockSpec(memory_space=pltpu.SMEM),
            pl.BlockSpec((1, H, TM, 128), lambda b, i: (b, 0, i + t_off, 0)),
            pl.BlockSpec((1, G, T, 128), lambda b, i: (b, 0, 0, 0)),
            pl.BlockSpec((1, G, T, 128), lambda b, i: (b, 0, 0, 0)),
        ],
        out_specs=pl.BlockSpec((1, TM, H * 128), lambda b, i: (b, i, 0)),
        out_shape=jax.ShapeDtypeStruct((B, nt * TM, H * 128), BF16),
        compiler_params=_cp(("arbitrary", "arbitrary")),
        name="swa_attn",
    )(sink, qs, ks, vs)


def _conv_kernel(u_ref, w_ref, b_ref, g_ref, bb_ref, o_ref, buf_ref, *, t_off, n_rows):
    i = pl.program_id(1) + t_off
    nt_all = n_rows // TM
    start = pl.multiple_of(i * TM, TM)
    ps = pl.multiple_of(jnp.maximum(start - CONV_HALO, 0), 8)
    ns = pl.multiple_of(jnp.minimum(start + TM, n_rows - CONV_HALO), 8)
    keep_prev = jnp.where(i <= 1, 0.0, 1.0)
    keep_next = jnp.where((i == 0) | (i == nt_all - 1), 0.0, 1.0)
    buf_ref[0:CONV_HALO, :] = u_ref[0, pl.ds(ps, CONV_HALO), :] * keep_prev
    buf_ref[CONV_HALO:CONV_HALO + TM, :] = u_ref[0, pl.ds(start, TM), :]
    buf_ref[CONV_HALO + TM:2 * CONV_HALO + TM, :] = u_ref[0, pl.ds(ns, CONV_HALO), :] * keep_next
    off = CONV_HALO - CONV_WIDTH // 2
    acc = jnp.zeros((TM, CONV_CH), F32)
    for j in range(CONV_WIDTH):
        acc = acc + buf_ref[off + j:off + j + TM, :] * w_ref[j:j + 1, :]
    y = acc + b_ref[...]
    mu = jnp.mean(y, axis=-1, keepdims=True)
    d = y - mu
    var = jnp.mean(d * d, axis=-1, keepdims=True)
    z = d * lax.rsqrt(var + EPS) * g_ref[...] + bb_ref[...]
    o_ref[0] = (z * jax.nn.sigmoid(z)).astype(BF16)


def _conv(u, lw, t_off):
    B, T, C = u.shape
    nt = T // TM - t_off
    return pl.pallas_call(
        functools.partial(_conv_kernel, t_off=t_off, n_rows=T),
        grid=(B, nt),
        in_specs=[
            pl.BlockSpec((1, T, C), lambda b, i: (b, 0, 0)),
            _full((CONV_WIDTH, C)),
            _full((1, C)),
            _full((1, C)),
            _full((1, C)),
        ],
        out_specs=pl.BlockSpec((1, TM, C), lambda b, i: (b, i, 0)),
        out_shape=jax.ShapeDtypeStruct((B, nt * TM, C), BF16),
        scratch_shapes=[pltpu.VMEM((TM + 2 * CONV_HALO, C), F32)],
        compiler_params=_cp(("arbitrary", "arbitrary")),
        name="conv",
    )(u, lw["conv_w"], lw["conv_b"], lw["conv_g"], lw["conv_bb"])


def _out_kernel(om_ref, oc_ref, os_ref, x_ref, mod_ref, w1_ref, w2_ref, w3_ref, n2_ref, wrh_ref, wrl_ref, br_ref,
                tri_ref, upper_ref, xn_ref, hf_ref, pos_ref, gt_ref, meta_ref, cnt_ref, run_ref):
    first = (pl.program_id(0) == 0) & (pl.program_id(1) == 0)

    @pl.when(first)
    def _():
        run_ref[...] = jnp.zeros_like(run_ref)

    d = functools.partial(jnp.dot, preferred_element_type=F32)
    mix = d(om_ref[0], w1_ref[...]) + d(oc_ref[0], w2_ref[...]) + d(os_ref[0], w3_ref[...])
    mod = mod_ref[0, 0]
    g1, sh2, sc2 = mod[2:3], mod[3:4], mod[4:5]
    xn = x_ref[0] + g1 * mix
    xn_ref[0] = xn
    hf = xn * lax.rsqrt(jnp.mean(xn * xn, axis=-1, keepdims=True) + EPS) * n2_ref[...]
    hf = hf * (1.0 + sc2) + sh2
    hf_ref[0] = hf.astype(BF16)

    hi, lo = _split(hf)
    logits = d(hi, wrh_ref[...]) + d(hi, wrl_ref[...]) + d(lo, wrh_ref[...]) + br_ref[...]
    lane = lax.broadcasted_iota(I32, (TM, LANES), 1)
    l = logits
    ohs, vals = [], []
    for _ in range(TOP_K):
        m = jnp.max(l, axis=-1, keepdims=True)
        idx = jnp.min(jnp.where(l == m, lane, LANES), axis=-1, keepdims=True)
        oh = lane == idx
        ohs.append(oh)
        vals.append(m)
        l = jnp.where(oh, -jnp.inf, l)
    ex = [jnp.exp(v - vals[0]) for v in vals]
    den = ex[0] + ex[1] + ex[2] + ex[3]
    gates = [e / den for e in ex]

    oa = jnp.zeros((TM, LANES), F32)
    for oh in ohs:
        oa = oa + oh.astype(F32)
    hist = jnp.sum(oa, axis=0, keepdims=True)
    slot_rows = jnp.floor((hist + (RUN_CHUNK - 1)) * (1.0 / RUN_CHUNK)) * RUN_CHUNK
    slot_off = d(jnp.broadcast_to(slot_rows, (8, LANES)).astype(BF16), upper_ref[...])
    where_ = d(tri_ref[...], oa.astype(BF16)) + slot_off[0:1, :]
    poss = [jnp.sum(jnp.where(oh, where_, 0.0), axis=-1, keepdims=True).astype(I32) for oh in ohs]

    p_out = jnp.zeros((TM, LANES), I32)
    g_out = jnp.zeros((TM, LANES), F32)
    for k in range(TOP_K):
        p_out = jnp.where(lane == k, poss[k], p_out)
        g_out = jnp.where(lane == k, gates[k], g_out)
    pos_ref[0] = p_out
    gt_ref[0] = g_out

    srow = lax.broadcasted_iota(I32, (8, LANES), 0)
    meta = jnp.where(srow == 0, hist, jnp.where(srow == 1, run_ref[...], jnp.where(srow == 2, slot_off, 0.0)))
    meta_ref[0] = meta.astype(I32)
    run_ref[...] = run_ref[...] + hist
    cnt_ref[...] = run_ref[...]


def _outproj(om, oc, osw, xu, modsel, lw, tri, upper, t_off):
    B, T, D = xu.shape
    nt = T // TM - t_off
    row = lambda b, i: (b, i, 0)
    Tq = nt * TM
    in_specs = [
        pl.BlockSpec((1, TM, 512), row),
        pl.BlockSpec((1, TM, 256), row),
        pl.BlockSpec((1, TM, 512), row),
        pl.BlockSpec((1, TM, D), lambda b, i: (b, i + t_off, 0)),
        pl.BlockSpec((1, 1, 6, D), lambda b, i: (b, jnp.minimum(i + t_off, 1), 0, 0)),
        _full((512, D)),
        _full((256, D)),
        _full((512, D)),
        _full((1, D)),
        _full((D, LANES)),
        _full((D, LANES)),
        _full((1, LANES)),
        _full((TM, TM)),
        _full((LANES, LANES)),
    ]
    out_shape = [
        jax.ShapeDtypeStruct((B, Tq, D), F32),
        jax.ShapeDtypeStruct((B, Tq, D), BF16),
        jax.ShapeDtypeStruct((B, Tq, LANES), I32),
        jax.ShapeDtypeStruct((B, Tq, LANES), F32),
        jax.ShapeDtypeStruct((B * nt, 8, LANES), I32),
        jax.ShapeDtypeStruct((8, LANES), F32),
    ]
    out_specs = [
        pl.BlockSpec((1, TM, D), row),
        pl.BlockSpec((1, TM, D), row),
        pl.BlockSpec((1, TM, LANES), row),
        pl.BlockSpec((1, TM, LANES), row),
        pl.BlockSpec((1, 8, LANES), lambda b, i: (b * nt + i, 0, 0)),
        pl.BlockSpec((8, LANES), lambda b, i: (0, 0)),
    ]
    return pl.pallas_call(
        _out_kernel,
        grid=(B, nt),
        in_specs=in_specs,
        out_specs=out_specs,
        out_shape=out_shape,
        scratch_shapes=[pltpu.VMEM((8, LANES), F32)],
        compiler_params=_cp(("arbitrary", "arbitrary")),
        name="outproj_router",
    )(om, oc, osw, xu, modsel, lw["w_o1"], lw["w_o2"], lw["w_o3"], lw["n2"], lw["wr_hi"], lw["wr_lo"], lw["br"], tri, upper)


SUB = 8
RUN_CHUNK = 16
SLOT_ROWS = TM * TOP_K + N_EXPERTS * RUN_CHUNK


def _rows(ref, row0, nrows):
    start = row0 * SUB if isinstance(row0, int) else pl.multiple_of(row0 * SUB, SUB)
    return ref.at[pl.ds(start, nrows * SUB), :]


def _to_tiles(ref, val, nrows):
    for j in range(SUB):
        ref[pl.ds(j, nrows, stride=SUB), :] = val[:, LANES * j:LANES * (j + 1)]


def _from_tiles(ref, nrows):
    return jnp.concatenate([ref[pl.ds(j, nrows, stride=SUB), :] for j in range(SUB)], axis=1)


def _slot_matrix(pos, weights):
    col = lax.broadcasted_iota(I32, (TM, SLOT_ROWS), 1)
    m = jnp.zeros((TM, SLOT_ROWS), F32)
    for k in range(TOP_K):
        m = jnp.where(col == pos[:, k:k + 1], weights[k], m)
    return m.astype(BF16)


def _run_copies(meta_ref, pst_ref, buf_ref, hbm_ref, sem, to_hbm):
    total = jnp.int32(0)
    for e in range(N_EXPERTS):
        n = meta_ref[0, 0, e]
        nch = (n + (RUN_CHUNK - 1)) // RUN_CHUNK
        seg0 = pst_ref[e] + meta_ref[0, 1, e]
        slot0 = meta_ref[0, 2, e]

        def chunk(c, carry):
            a = _rows(buf_ref, slot0 + c * RUN_CHUNK, RUN_CHUNK)
            b = _rows(hbm_ref, seg0 + c * RUN_CHUNK, RUN_CHUNK)
            (pltpu.make_async_copy(a, b, sem) if to_hbm else pltpu.make_async_copy(b, a, sem)).start()
            return carry

        lax.fori_loop(0, nch, chunk, 0)
        total = total + nch
    return total


def _run_waits(total, buf_ref, hbm_ref, sem, to_hbm):
    a = _rows(buf_ref, 0, RUN_CHUNK)
    b = _rows(hbm_ref, 0, RUN_CHUNK)

    def one(c, carry):
        (pltpu.make_async_copy(a, b, sem) if to_hbm else pltpu.make_async_copy(b, a, sem)).wait()
        return carry

    lax.fori_loop(0, total, one, 0)


def _disp_kernel(pst_ref, cnt_ref, pad_ref, na_ref, meta_ref, hf_ref, pos_ref, xb_ref, srt_ref, zero_ref, sem, zsem, *,
                 n_blocks):
    first = (pl.program_id(0) == 0) & (pl.program_id(1) == 0)

    @pl.when(first)
    def _():
        zero_ref[...] = jnp.zeros_like(zero_ref)

        def zblock(j):
            return pltpu.make_async_copy(zero_ref, _rows(xb_ref, j * EXPERT_BLOCK, EXPERT_BLOCK), zsem)

        def zb_start(j, carry):
            zblock(j).start()
            return carry

        def zb_wait(j, carry):
            zblock(j).wait()
            return carry

        lax.fori_loop(na_ref[0], n_blocks, zb_start, 0)
        lax.fori_loop(na_ref[0], n_blocks, zb_wait, 0)

        for e in range(N_EXPERTS):
            lo = pst_ref[e] + cnt_ref[e]
            nrow = pad_ref[e] - cnt_ref[e]
            nz = nrow // RUN_CHUNK
            lo1 = lo + nz * RUN_CHUNK
            n1 = nrow - nz * RUN_CHUNK

            def zchunk(c):
                return pltpu.make_async_copy(_rows(zero_ref, 0, RUN_CHUNK), _rows(xb_ref, lo + c * RUN_CHUNK, RUN_CHUNK), zsem)

            def zrow(r):
                return pltpu.make_async_copy(_rows(zero_ref, 0, 1), _rows(xb_ref, lo1 + r, 1), zsem)

            def zc_start(c, carry):
                zchunk(c).start()
                return carry

            def zc_wait(c, carry):
                zchunk(c).wait()
                return carry

            def zr_start(r, carry):
                zrow(r).start()
                return carry

            def zr_wait(r, carry):
                zrow(r).wait()
                return carry

            lax.fori_loop(0, nz, zc_start, 0)
            lax.fori_loop(0, n1, zr_start, 0)
            lax.fori_loop(0, nz, zc_wait, 0)
            lax.fori_loop(0, n1, zr_wait, 0)

    ones = [1.0] * TOP_K
    q = _slot_matrix(pos_ref[0], ones)
    srt = lax.dot_general(q, hf_ref[0], (((0,), (0,)), ((), ())), preferred_element_type=F32)
    _to_tiles(srt_ref, srt, SLOT_ROWS)
    total = _run_copies(meta_ref, pst_ref, srt_ref, xb_ref, sem, True)
    _run_waits(total, srt_ref, xb_ref, sem, True)


def _dispatch(pstart, counts, padded, n_act, meta, hf, pos, n_buf):
    B, Tq, D = hf.shape
    nt = Tq // TM
    assert D == SUB * LANES
    grid_spec = pltpu.PrefetchScalarGridSpec(
        num_scalar_prefetch=4,
        grid=(B, nt),
        in_specs=[
            pl.BlockSpec((1, 8, LANES), lambda b, i, *_: (b * nt + i, 0, 0), memory_space=pltpu.SMEM),
            pl.BlockSpec((1, TM, D), lambda b, i, *_: (b, i, 0)),
            pl.BlockSpec((1, TM, LANES), lambda b, i, *_: (b, i, 0)),
        ],
        out_specs=pl.BlockSpec(memory_space=pl.ANY),
        scratch_shapes=[pltpu.VMEM((SLOT_ROWS * SUB, LANES), F32), pltpu.VMEM((EXPERT_BLOCK * SUB, LANES), F32),
                        pltpu.SemaphoreType.DMA, pltpu.SemaphoreType.DMA],
    )
    return pl.pallas_call(
        functools.partial(_disp_kernel, n_blocks=n_buf // EXPERT_BLOCK),
        grid_spec=grid_spec,
        out_shape=jax.ShapeDtypeStruct((n_buf * SUB, LANES), F32),
        compiler_params=_cp(("arbitrary", "arbitrary")),
        name="dispatch",
    )(pstart, counts, padded, n_act, meta, hf, pos)


W_ROWS = 256


def _exp_kernel(be_ref, bx_ref, na_ref, x_ref, w1_ref, b1a_ref, b1b_ref, w2_ref, b2_ref, o_ref, w1a_s, w1b_s, w2_s):
    j = pl.program_id(0)
    active = j < na_ref[0]
    fresh = (j == 0) | (be_ref[j] != be_ref[jnp.maximum(j - 1, 0)])

    @pl.when(active & fresh)
    def _():
        lane = lax.broadcasted_iota(I32, (W_ROWS, LANES), 1)
        lo = lane < 64
        idx = jnp.where(lo, 2 * lane, 2 * (lane - 64) + 1)

        def rows(i, carry):
            r0 = pl.multiple_of(i * W_ROWS, W_ROWS)
            for c in range(w1_ref.shape[2] // (2 * LANES)):
                a = w1_ref[0, pl.ds(r0, W_ROWS), 2 * LANES * c:2 * LANES * c + LANES]
                b = w1_ref[0, pl.ds(r0, W_ROWS), 2 * LANES * c + LANES:2 * LANES * (c + 1)]
                pa = jnp.take_along_axis(a, idx, axis=1)
                pb = jnp.take_along_axis(b, idx, axis=1)
                ev = jnp.where(lo, pa, pltpu.roll(pb, 64, axis=1))
                od = jnp.where(lo, pltpu.roll(pa, 64, axis=1), pb)
                w1a_s[pl.ds(r0, W_ROWS), LANES * c:LANES * (c + 1)] = ev.astype(BF16)
                w1b_s[pl.ds(r0, W_ROWS), LANES * c:LANES * (c + 1)] = od.astype(BF16)
            w2_s[pl.ds(r0, W_ROWS), :] = w2_ref[0, pl.ds(r0, W_ROWS), :].astype(BF16)
            return carry

        lax.fori_loop(0, w1_ref.shape[1] // W_ROWS, rows, 0)

    @pl.when(active)
    def _():
        x = _from_tiles(x_ref, EXPERT_BLOCK).astype(BF16)
        ug = jnp.dot(x, w1a_s[...], preferred_element_type=F32) + b1a_ref[0]
        ul = jnp.dot(x, w1b_s[...], preferred_element_type=F32) + b1b_ref[0]
        xg = jnp.minimum(ug, SWIGLU_LIMIT)
        xl = jnp.clip(ul, -SWIGLU_LIMIT, SWIGLU_LIMIT)
        act = xg * jax.nn.sigmoid(SWIGLU_ALPHA * xg) * (xl + 1.0)
        y = jnp.dot(act.astype(BF16), w2_s[...], preferred_element_type=F32) + b2_ref[0]
        _to_tiles(o_ref, y, EXPERT_BLOCK)

    @pl.when(jnp.logical_not(active))
    def _():
        o_ref[...] = jnp.zeros_like(o_ref)


def _experts(blk_e, blk_x, n_act, xb, lw):
    nb = xb.shape[0] // (EXPERT_BLOCK * SUB)
    De, D = lw["w2"].shape[1:]
    assert De == D
    wmap = lambda j, be, bx, na: (be[j], 0, 0)
    grid_spec = pltpu.PrefetchScalarGridSpec(
        num_scalar_prefetch=3,
        grid=(nb,),
        in_specs=[
            pl.BlockSpec((EXPERT_BLOCK * SUB, LANES), lambda j, be, bx, na: (bx[j], 0)),
            pl.BlockSpec((1, D, 2 * De), wmap),
            pl.BlockSpec((1, 1, De), wmap),
            pl.BlockSpec((1, 1, De), wmap),
            pl.BlockSpec((1, De, D), wmap),
            pl.BlockSpec((1, 1, D), wmap),
        ],
        out_specs=pl.BlockSpec((EXPERT_BLOCK * SUB, LANES), lambda j, be, bx, na: (j, 0)),
        scratch_shapes=[pltpu.VMEM((D, De), BF16), pltpu.VMEM((D, De), BF16), pltpu.VMEM((De, D), BF16)],
    )
    return pl.pallas_call(
        _exp_kernel,
        grid_spec=grid_spec,
        out_shape=jax.ShapeDtypeStruct(xb.shape, F32),
        compiler_params=pltpu.CompilerParams(dimension_semantics=("arbitrary",), vmem_limit_bytes=EXPERT_VMEM_LIMIT),
        name="experts",
    )(blk_e, blk_x, n_act, xb, lw["w1"], lw["b1a"], lw["b1b"], lw["w2"], lw["b2"])


def _comb_kernel(pst_ref, meta_ref, pos_ref, gt_ref, xn_ref, mod_ref, yb_ref, o_ref, buf_ref, sem):
    first = (pl.program_id(0) == 0) & (pl.program_id(1) == 0)

    @pl.when(first)
    def _():
        buf_ref[...] = jnp.zeros_like(buf_ref)

    total = _run_copies(meta_ref, pst_ref, buf_ref, yb_ref, sem, False)
    gt = gt_ref[0]
    g = _slot_matrix(pos_ref[0], [gt[:, k:k + 1] for k in range(TOP_K)])
    _run_waits(total, buf_ref, yb_ref, sem, False)
    rows = _from_tiles(buf_ref, SLOT_ROWS).astype(BF16)
    y = jnp.dot(g, rows, preferred_element_type=F32)
    g2 = mod_ref[0, 0][5:6]
    o_ref[0] = xn_ref[0] + g2 * y


def _combine(pstart, meta, pos, gates, xn, modsel, yb, t_off):
    B, Tq, D = xn.shape
    nt = Tq // TM
    grid_spec = pltpu.PrefetchScalarGridSpec(
        num_scalar_prefetch=1,
        grid=(B, nt),
        in_specs=[
            pl.BlockSpec((1, 8, LANES), lambda b, i, *_: (b * nt + i, 0, 0), memory_space=pltpu.SMEM),
            pl.BlockSpec((1, TM, LANES), lambda b, i, *_: (b, i, 0)),
            pl.BlockSpec((1, TM, LANES), lambda b, i, *_: (b, i, 0)),
            pl.BlockSpec((1, TM, D), lambda b, i, *_: (b, i, 0)),
            pl.BlockSpec((1, 1, 6, D), lambda b, i, *_: (b, jnp.minimum(i + t_off, 1), 0, 0)),
            pl.BlockSpec(memory_space=pl.ANY),
        ],
        out_specs=pl.BlockSpec((1, TM, D), lambda b, i, *_: (b, i, 0)),
        scratch_shapes=[pltpu.VMEM((SLOT_ROWS * SUB, LANES), F32), pltpu.SemaphoreType.DMA],
    )
    return pl.pallas_call(
        _comb_kernel,
        grid_spec=grid_spec,
        out_shape=jax.ShapeDtypeStruct((B, Tq, D), F32),
        compiler_params=_cp(("arbitrary", "arbitrary")),
        name="combine",
    )(pstart, meta, pos, gates, xn, modsel, yb)


def _take_cols(w, cols):
    cols = np.asarray(cols)
    out = jnp.take(w, jnp.asarray(np.maximum(cols, 0)), axis=-1)
    return jnp.where(jnp.asarray(cols >= 0), out, 0.0)


def _in_cols():
    pi = _PI
    cols = list(range(0, 256))
    cols += [256 + i for i in range(64)] + [256 + pi[i] for i in range(64)]
    for g in range(SWA_KV_HEADS):
        base = 320 + 64 * g
        cols += [base + i for i in range(64)] + [base + pi[i] for i in range(64)]
    for g in range(SWA_KV_HEADS):
        base = 448 + 64 * g
        cols += [base + i for i in range(64)] + [-1] * 64
    cols += list(range(576, 960))
    for h in range(SWA_HEADS):
        base = 960 + 64 * h
        cols += [base + i for i in range(64)] + [base + pi[i] for i in range(64)]
    cols += list(range(1216, 1728))
    return cols


def _layer_weights(l, a):
    pi = _PI
    lw = {}
    lw["n1"] = a["norm1_g"][l][None, :]
    lw["n2"] = a["norm2_g"][l][None, :]
    lw["w_in"] = _take_cols(a["w_in"][l], _in_cols()).astype(BF16)
    lw["kvg"] = a["mla_kv_norm"][l][None, :]
    lw["qg"] = a["mla_q_norm"][l][None, :]
    uq_cols = []
    for h in range(MLA_HEADS):
        base = MLA_QK * h
        uq_cols += [base + i for i in range(128)] + [base + 128 + i for i in range(64)] + [base + 128 + pi[i] for i in range(64)]
    lw["w_uq"] = _take_cols(a["mla_w_uq"][l], uq_cols).astype(BF16)
    ukv_cols = [256 * h + i for h in range(MLA_HEADS) for i in range(128)]
    ukv_cols += [256 * h + 128 + i for h in range(MLA_HEADS) for i in range(128)]
    lw["w_ukv"] = _take_cols(a["mla_w_ukv"][l], ukv_cols).astype(BF16)
    gq = a["mla_q_head_norm"][l]
    lw["gq"] = (jnp.concatenate([gq[:128], gq[128:], gq[128:][pi]]) * (MLA_QK ** -0.5))[None, :]
    gk = a["mla_k_head_norm"][l]
    lw["gkn"] = gk[:128][None, :]
    lw["gkp"] = jnp.concatenate([gk[128:], gk[128:][pi]])[None, :]
    sq = a["swa_q_norm"][l]
    lw["gsq"] = (jnp.concatenate([sq, sq[pi]]) * (SWA_HEAD_DIM ** -0.5))[None, :]
    sk = a["swa_k_norm"][l]
    lw["gsk"] = jnp.concatenate([sk, sk[pi]])[None, :]
    lw["conv_w"] = a["conv_w"][l]
    lw["conv_b"] = a["conv_b"][l][None, :]
    lw["conv_g"] = a["conv_ln_g"][l][None, :]
    lw["conv_bb"] = a["conv_ln_b"][l][None, :]
    lw["sink"] = a["swa_sink"][l]
    wo = a["w_out"][l]
    lw["w_o1"] = wo[0:512].astype(BF16)
    lw["w_o2"] = wo[512:768].astype(BF16)
    o3 = wo[768:1024].reshape(SWA_HEADS, SWA_HEAD_DIM, -1)
    lw["w_o3"] = jnp.concatenate([o3, jnp.zeros_like(o3)], axis=1).reshape(SWA_HEADS * 128, -1).astype(BF16)
    wr = jnp.pad(a["router_w"][l], ((0, 0), (0, LANES - N_EXPERTS)))
    lw["wr_hi"] = wr.astype(BF16)
    lw["wr_lo"] = (wr - lw["wr_hi"].astype(F32)).astype(BF16)
    lw["br"] = jnp.pad(a["router_b"][l], (0, LANES - N_EXPERTS), constant_values=NEG)[None, :]
    lw["w1"] = a["exp_w1"][l]
    b1 = a["exp_b1"][l]
    lw["b1a"] = b1[:, None, 0::2]
    lw["b1b"] = b1[:, None, 1::2]
    lw["w2"] = a["exp_w2"][l]
    lw["b2"] = a["exp_b2"][l][:, None, :]
    return lw


def _rope_tables(n_ctx, n_lat):
    q = MLA_ROPE // 4
    n = jnp.arange(n_lat, dtype=I32)
    row = (n // GRID_W).astype(F32)
    col = (n % GRID_W).astype(F32)
    inv = ROPE_BASE ** (-jnp.arange(q, dtype=F32) / q)
    ang_r = row[:, None] * inv
    ang_c = col[:, None] * inv
    cos = jnp.concatenate([jnp.cos(ang_r), jnp.cos(ang_r), jnp.cos(ang_c), jnp.cos(ang_c)], axis=1)
    sin = jnp.concatenate([-jnp.sin(ang_r), jnp.sin(ang_r), -jnp.sin(ang_c), jnp.sin(ang_c)], axis=1)
    cos = jnp.concatenate([jnp.ones((n_ctx, 64), F32), cos], axis=0)
    sin = jnp.concatenate([jnp.zeros((n_ctx, 64), F32), sin], axis=0)
    z = jnp.zeros_like(cos)
    return jnp.concatenate([cos, z], axis=1), jnp.concatenate([sin, z], axis=1)


def _routing_tables(cnt_f, n_blocks):
    counts = cnt_f[0, :N_EXPERTS].astype(I32)
    padded = (counts + (RUN_CHUNK - 1) + EXPERT_BLOCK - 1) // EXPERT_BLOCK * EXPERT_BLOCK
    padded = jnp.where(counts > 0, padded, 0)
    pend = jnp.cumsum(padded)
    pstart = pend - padded
    n_act = pend[-1] // EXPERT_BLOCK
    blk = jnp.minimum(jnp.arange(n_blocks, dtype=I32), n_act - 1)
    blk_e = jnp.sum((pend[None, :] <= (blk * EXPERT_BLOCK)[:, None]).astype(I32), axis=1)
    blk_e = jnp.minimum(blk_e, N_EXPERTS - 1)
    return counts, padded.astype(I32), pstart.astype(I32), blk_e, blk.astype(I32), n_act.reshape(1).astype(I32)


def kernel(x, c, ctx, c_ctx, norm1_g, norm2_g, w_ada, b_ada, w_in, mla_q_norm, mla_kv_norm, mla_w_uq, mla_w_ukv, mla_q_head_norm, mla_k_head_norm, conv_w, conv_b, conv_ln_g, conv_ln_b, swa_q_norm, swa_k_norm, swa_sink, w_out, router_w, router_b, exp_w1, exp_b1, exp_w2, exp_b2):
    a = dict(norm1_g=norm1_g, norm2_g=norm2_g, w_in=w_in, mla_q_norm=mla_q_norm, mla_kv_norm=mla_kv_norm,
             mla_w_uq=mla_w_uq, mla_w_ukv=mla_w_ukv, mla_q_head_norm=mla_q_head_norm, mla_k_head_norm=mla_k_head_norm,
             conv_w=conv_w, conv_b=conv_b, conv_ln_g=conv_ln_g, conv_ln_b=conv_ln_b, swa_q_norm=swa_q_norm,
             swa_k_norm=swa_k_norm, swa_sink=swa_sink, w_out=w_out, router_w=router_w, router_b=router_b,
             exp_w1=exp_w1, exp_b1=exp_b1, exp_w2=exp_w2, exp_b2=exp_b2)
    B, S, D = x.shape
    n_ctx = ctx.shape[1]
    depth = w_ada.shape[0]
    assert n_ctx == TM and S % TM == 0 and B + 1 <= 16
    T = n_ctx + S

    s_in = jnp.zeros((16, D), F32).at[:B].set(c).at[B].set(c_ctx)
    mods = _ada(s_in, w_ada, b_ada)
    cos_t, sin_t = _rope_tables(n_ctx, S)
    tri = jnp.tril(jnp.ones((TM, TM), F32), -1).astype(BF16)
    upper = jnp.triu(jnp.ones((LANES, LANES), F32), 1).astype(BF16)

    xu = jnp.concatenate([ctx, x], axis=1)
    for l in range(depth):
        last = l == depth - 1
        t_off = 1 if last else 0
        lw = _layer_weights(l, a)
        m = mods[l].reshape(16, 6, D)
        modsel = jnp.stack([jnp.broadcast_to(m[B], (B, 6, D)), m[:B]], axis=1)

        qm, km, vm, qs, ks, vs, u = _prep(xu, modsel, lw, cos_t, sin_t, 0)
        om = _mla(qm, km, vm, t_off)
        oc = _conv(u, lw, t_off)
        osw = _swa(lw["sink"], qs, ks, vs, t_off)
        xn, hf, pos, gt_o, meta, cnt = _outproj(om, oc, osw, xu, modsel, lw, tri, upper, t_off)

        n_tok = B * (T - t_off * TM)
        nk = n_tok * TOP_K
        n_buf = -(-(nk + N_EXPERTS * (RUN_CHUNK - 1 + EXPERT_BLOCK - 1)) // EXPERT_BLOCK) * EXPERT_BLOCK
        counts, padded, pstart, blk_e, blk_x, n_act = _routing_tables(cnt, n_buf // EXPERT_BLOCK)
        xb = _dispatch(pstart, counts, padded, n_act, meta, hf, pos, n_buf)
        yb = _experts(blk_e, blk_x, n_act, xb, lw)
        xu = _combine(pstart, meta, pos, gt_o, xn, modsel, yb, t_off)
    return xu
```

```python
import functools

import numpy as np
import jax
import jax.numpy as jnp
from jax import lax
from jax.experimental import pallas as pl
from jax.experimental.pallas import tpu as pltpu

F32 = jnp.float32
BF16 = jnp.bfloat16
I32 = jnp.int32

GRID_W = 64
ROPE_BASE = 10000.0
EPS = 1e-6
MLA_HEADS = 4
MLA_NOPE = 128
MLA_ROPE = 64
MLA_V = 128
MLA_QK = MLA_NOPE + MLA_ROPE
MLA_Q_RANK = 384
MLA_KV_RANK = 256
CONV_CH = 256
CONV_WIDTH = 31
SWA_HEADS = 4
SWA_KV_HEADS = 2
SWA_HEAD_DIM = 64
SWA_WINDOW = 128
N_EXPERTS = 32
TOP_K = 4
SWIGLU_LIMIT = 7.0
SWIGLU_ALPHA = 1.702
EXPERT_BLOCK = 256

LANES = 128
TM = 256
CONV_HALO = 16
VMEM_LIMIT = 48 * 1024 * 1024
EXPERT_VMEM_LIMIT = 56 * 1024 * 1024
NEG = -1e30

_PI = np.array([i + 16 if (i % 32) < 16 else i - 16 for i in range(64)])


def _cp(sem):
    return pltpu.CompilerParams(dimension_semantics=sem, vmem_limit_bytes=VMEM_LIMIT)


def _full(shape):
    n = len(shape)
    return pl.BlockSpec(shape, lambda *a, _n=n: (0,) * _n)


def _split(x):
    hi = x.astype(BF16)
    lo = (x - hi.astype(F32)).astype(BF16)
    return hi, lo


def _dot3(a, b):
    ah, al = _split(a)
    bh, bl = _split(b)
    d = functools.partial(jnp.dot, preferred_element_type=F32)
    return d(ah, bh) + d(ah, bl) + d(al, bh)


def _ada_kernel(s_ref, w_ref, b_ref, o_ref):
    s = s_ref[...]
    s = s * jax.nn.sigmoid(s)
    o_ref[0] = _dot3(s, w_ref[0]) + b_ref[0]


def _ada(s_in, w_ada, b_ada):
    L, D, N = w_ada.shape
    tn = 1536
    return pl.pallas_call(
        _ada_kernel,
        grid=(L, N // tn),
        in_specs=[
            pl.BlockSpec((16, D), lambda l, j: (0, 0)),
            pl.BlockSpec((1, D, tn), lambda l, j: (l, 0, j)),
            pl.BlockSpec((1, 1, tn), lambda l, j: (l, 0, j)),
        ],
        out_specs=pl.BlockSpec((1, 16, tn), lambda l, j: (l, 0, j)),
        out_shape=jax.ShapeDtypeStruct((L, 16, N), F32),
        compiler_params=_cp(("arbitrary", "arbitrary")),
        name="ada",
    )(s_in, w_ada, b_ada.reshape(L, 1, N))


def _rope(x, c, s):
    return x * c + pltpu.roll(x, 64, axis=1) * s


def _prep_kernel(x_ref, mod_ref, n1_ref, win_ref, kvg_ref, wukv_ref, qg_ref, wuq_ref, gq_ref, gkn_ref, gkp_ref,
                 gsq_ref, gsk_ref, cos_ref, sin_ref, qm_ref, km_ref, vm_ref, qs_ref, ks_ref, vs_ref, u_ref):
    x = x_ref[0]
    mod = mod_ref[0, 0]
    sh, sc = mod[0:1], mod[1:2]
    y = x * lax.rsqrt(jnp.mean(x * x, axis=-1, keepdims=True) + EPS) * n1_ref[...]
    h = y * (1.0 + sc) + sh
    p = jnp.dot(h.astype(BF16), win_ref[...], preferred_element_type=F32)
    c = cos_ref[...]
    s = sin_ref[...]
    lane = lax.broadcasted_iota(I32, (TM, LANES), 1)
    lo = lane < 64

    def ss_lo(v):
        return jnp.sum(jnp.where(lo, v * v, 0.0), axis=-1, keepdims=True)

    ckv = p[:, 0:256]
    ckvn = ckv * lax.rsqrt(jnp.mean(ckv * ckv, axis=-1, keepdims=True) + EPS) * kvg_ref[...]
    kv = jnp.dot(ckvn.astype(BF16), wukv_ref[...], preferred_element_type=F32)
    kpe = p[:, 256:384]
    ss_pe = ss_lo(kpe)
    kpe_rot = _rope(kpe * gkp_ref[...], c, s)
    for hh in range(MLA_HEADS):
        kn = kv[:, 128 * hh:128 * hh + 128]
        r = lax.rsqrt((jnp.sum(kn * kn, axis=-1, keepdims=True) + ss_pe) * (1.0 / MLA_QK) + EPS)
        km_ref[0, hh, :, 0:128] = (kn * r * gkn_ref[...]).astype(BF16)
        km_ref[0, hh, :, 128:256] = (kpe_rot * r).astype(BF16)
        vm_ref[0, hh] = kv[:, 512 + 128 * hh:640 + 128 * hh].astype(BF16)

    cq = p[:, 896:1280]
    cqn = cq * lax.rsqrt(jnp.mean(cq * cq, axis=-1, keepdims=True) + EPS) * qg_ref[...]
    q = jnp.dot(cqn.astype(BF16), wuq_ref[...], preferred_element_type=F32)
    gq = gq_ref[...]
    for hh in range(MLA_HEADS):
        qn = q[:, 256 * hh:256 * hh + 128]
        qp = q[:, 256 * hh + 128:256 * hh + 256]
        r = lax.rsqrt((jnp.sum(qn * qn, axis=-1, keepdims=True) + ss_lo(qp)) * (1.0 / MLA_QK) + EPS)
        qm_ref[0, hh, :, 0:128] = (qn * r * gq[:, 0:128]).astype(BF16)
        qm_ref[0, hh, :, 128:256] = _rope(qp * r * gq[:, 128:256], c, s).astype(BF16)

    for g in range(SWA_KV_HEADS):
        xk = p[:, 384 + 128 * g:512 + 128 * g]
        r = lax.rsqrt(ss_lo(xk) * (1.0 / SWA_HEAD_DIM) + EPS)
        ks_ref[0, g] = _rope(xk * r * gsk_ref[...], c, s).astype(BF16)
        vs_ref[0, g] = p[:, 640 + 128 * g:768 + 128 * g].astype(BF16)
    for hh in range(SWA_HEADS):
        xq = p[:, 1280 + 128 * hh:1408 + 128 * hh]
        r = lax.rsqrt(ss_lo(xq) * (1.0 / SWA_HEAD_DIM) + EPS)
        qs_ref[0, hh] = _rope(xq * r * gsq_ref[...], c, s).astype(BF16)

    u_ref[0] = p[:, 1792:2048] * jax.nn.sigmoid(p[:, 2048:2304])


def _prep(xu, modsel, lw, cos_t, sin_t, t_off):
    B, T, D = xu.shape
    nt = T // TM - t_off
    ncol = lw["w_in"].shape[1]
    row = lambda b, i: (b, i + t_off, 0)
    head = lambda b, i: (b, 0, i + t_off, 0)
    in_specs = [
        pl.BlockSpec((1, TM, D), row),
        pl.BlockSpec((1, 1, 6, D), lambda b, i: (b, jnp.minimum(i + t_off, 1), 0, 0)),
        _full((1, D)),
        _full((D, ncol)),
        _full((1, MLA_KV_RANK)),
        _full((MLA_KV_RANK, 1024)),
        _full((1, MLA_Q_RANK)),
        _full((MLA_Q_RANK, 1024)),
        _full((1, 256)),
        _full((1, 128)),
        _full((1, 128)),
        _full((1, 128)),
        _full((1, 128)),
        pl.BlockSpec((TM, LANES), lambda b, i: (i + t_off, 0)),
        pl.BlockSpec((TM, LANES), lambda b, i: (i + t_off, 0)),
    ]
    out_shape = [
        jax.ShapeDtypeStruct((B, MLA_HEADS, T, 256), BF16),
        jax.ShapeDtypeStruct((B, MLA_HEADS, T, 256), BF16),
        jax.ShapeDtypeStruct((B, MLA_HEADS, T, 128), BF16),
        jax.ShapeDtypeStruct((B, SWA_HEADS, T, 128), BF16),
        jax.ShapeDtypeStruct((B, SWA_KV_HEADS, T, 128), BF16),
        jax.ShapeDtypeStruct((B, SWA_KV_HEADS, T, 128), BF16),
        jax.ShapeDtypeStruct((B, T, CONV_CH), F32),
    ]
    out_specs = [
        pl.BlockSpec((1, MLA_HEADS, TM, 256), head),
        pl.BlockSpec((1, MLA_HEADS, TM, 256), head),
        pl.BlockSpec((1, MLA_HEADS, TM, 128), head),
        pl.BlockSpec((1, SWA_HEADS, TM, 128), head),
        pl.BlockSpec((1, SWA_KV_HEADS, TM, 128), head),
        pl.BlockSpec((1, SWA_KV_HEADS, TM, 128), head),
        pl.BlockSpec((1, TM, CONV_CH), row),
    ]
    return pl.pallas_call(
        _prep_kernel,
        grid=(B, nt),
        in_specs=in_specs,
        out_specs=out_specs,
        out_shape=out_shape,
        compiler_params=_cp(("arbitrary", "arbitrary")),
        name="prep",
    )(xu, modsel, lw["n1"], lw["w_in"], lw["kvg"], lw["w_ukv"], lw["qg"], lw["w_uq"], lw["gq"], lw["gkn"], lw["gkp"],
      lw["gsq"], lw["gsk"], cos_t, sin_t)


def _mla_kernel(q_ref, k_ref, v_ref, o_ref, *, q_off, n_keys):
    qi = pl.program_id(1) + q_off

    def attend(nk):
        for h in range(MLA_HEADS):
            q = q_ref[0, h]
            k = k_ref[0, h, 0:nk, :]
            v = v_ref[0, h, 0:nk, :]
            s = lax.dot_general(q, k, (((1,), (1,)), ((), ())), preferred_element_type=F32)
            m = jnp.max(s, axis=-1, keepdims=True)
            p = jnp.exp(s - m)
            l = jnp.sum(p, axis=-1, keepdims=True)
            o = jnp.dot(p.astype(BF16), v, preferred_element_type=F32) / l
            o_ref[0, :, 128 * h:128 * h + 128] = o.astype(BF16)

    if q_off == 0:
        @pl.when(qi == 0)
        def _():
            attend(TM)

        @pl.when(qi > 0)
        def _():
            attend(n_keys)
    else:
        attend(n_keys)


def _mla(qm, km, vm, t_off):
    B, H, T, _ = qm.shape
    nt = T // TM - t_off
    return pl.pallas_call(
        functools.partial(_mla_kernel, q_off=t_off, n_keys=T),
        grid=(B, nt),
        in_specs=[
            pl.BlockSpec((1, H, TM, 256), lambda b, i: (b, 0, i + t_off, 0)),
            pl.BlockSpec((1, H, T, 256), lambda b, i: (b, 0, 0, 0)),
            pl.BlockSpec((1, H, T, 128), lambda b, i: (b, 0, 0, 0)),
        ],
        out_specs=pl.BlockSpec((1, TM, H * MLA_V), lambda b, i: (b, i, 0)),
        out_shape=jax.ShapeDtypeStruct((B, nt * TM, H * MLA_V), BF16),
        compiler_params=_cp(("arbitrary", "arbitrary")),
        name="mla_attn",
    )(qm, km, vm)


SWA_SPAN = TM + 2 * SWA_WINDOW


def _swa_kernel(sink_ref, q_ref, k_ref, v_ref, o_ref, *, q_off, n_rows):
    qi = pl.program_id(1) + q_off
    nt_dims = (((1,), (1,)), ((), ()))

    def latent():
        start = jnp.clip(qi * TM - SWA_WINDOW, 0, n_rows - SWA_SPAN)
        start = pl.multiple_of(start, SWA_WINDOW)
        qpos = qi * TM + lax.broadcasted_iota(I32, (TM, SWA_SPAN), 0)
        kpos = start + lax.broadcasted_iota(I32, (TM, SWA_SPAN), 1)
        valid = (jnp.abs(qpos - kpos) <= SWA_WINDOW) & (kpos >= TM)
        for h in range(SWA_HEADS):
            g = h // (SWA_HEADS // SWA_KV_HEADS)
            q = q_ref[0, h]
            kl = k_ref[0, g, pl.ds(start, SWA_SPAN), :]
            vl = v_ref[0, g, pl.ds(start, SWA_SPAN), :]
            kc = k_ref[0, g, 0:TM, :]
            vc = v_ref[0, g, 0:TM, :]
            sl = lax.dot_general(q, kl, nt_dims, preferred_element_type=F32)
            sl = jnp.where(valid, sl, NEG)
            scx = lax.dot_general(q, kc, nt_dims, preferred_element_type=F32)
            sink = sink_ref[h]
            m = jnp.maximum(jnp.maximum(jnp.max(sl, axis=-1, keepdims=True), jnp.max(scx, axis=-1, keepdims=True)), sink)
            pl_ = jnp.exp(sl - m)
            pc = jnp.exp(scx - m)
            l = jnp.sum(pl_, axis=-1, keepdims=True) + jnp.sum(pc, axis=-1, keepdims=True) + jnp.exp(sink - m)
            o = (jnp.dot(pl_.astype(BF16), vl, preferred_element_type=F32)
                 + jnp.dot(pc.astype(BF16), vc, preferred_element_type=F32)) / l
            o_ref[0, :, 128 * h:128 * h + 128] = o.astype(BF16)

    def context():
        for h in range(SWA_HEADS):
            g = h // (SWA_HEADS // SWA_KV_HEADS)
            q = q_ref[0, h]
            kc = k_ref[0, g, 0:TM, :]
            vc = v_ref[0, g, 0:TM, :]
            scx = lax.dot_general(q, kc, nt_dims, preferred_element_type=F32)
            sink = sink_ref[h]
            m = jnp.maximum(jnp.max(scx, axis=-1, keepdims=True), sink)
            pc = jnp.exp(scx - m)
            l = jnp.sum(pc, axis=-1, keepdims=True) + jnp.exp(sink - m)
            o = jnp.dot(pc.astype(BF16), vc, preferred_element_type=F32) / l
            o_ref[0, :, 128 * h:128 * h + 128] = o.astype(BF16)

    if q_off == 0:
        pl.when(qi == 0)(context)
        pl.when(qi > 0)(latent)
    else:
        latent()


def _swa(sink, qs, ks, vs, t_off):
    B, H, T, _ = qs.shape
    G = ks.shape[1]
    nt = T // TM - t_off
    return pl.pallas_call(
        functools.partial(_swa_kernel, q_off=t_off, n_rows=T),
        grid=(B, nt),
        in_specs=[
            pl.BlockSpec(memory_space=pltpu.SMEM),
            pl.BlockSpec((1, H, TM, 128), lambda b, i: (b, 0, i + t_off, 0)),
            pl.BlockSpec((1, G, T, 128), lambda b, i: (b, 0, 0, 0)),
            pl.BlockSpec((1, G, T, 128), lambda b, i: (b, 0, 0, 0)),
        ],
        out_specs=pl.BlockSpec((1, TM, H * 128), lambda b, i: (b, i, 0)),
        out_shape=jax.ShapeDtypeStruct((B, nt * TM, H * 128), BF16),
        compiler_params=_cp(("arbitrary", "arbitrary")),
        name="swa_attn",
    )(sink, qs, ks, vs)


def _conv_kernel(u_ref, w_ref, b_ref, g_ref, bb_ref, o_ref, buf_ref, *, t_off, n_rows):
    i = pl.program_id(1) + t_off
    nt_all = n_rows // TM
    start = pl.multiple_of(i * TM, TM)
    ps = pl.multiple_of(jnp.maximum(start - CONV_HALO, 0), 8)
    ns = pl.multiple_of(jnp.minimum(start + TM, n_rows - CONV_HALO), 8)
    keep_prev = jnp.where(i <= 1, 0.0, 1.0)
    keep_next = jnp.where((i == 0) | (i == nt_all - 1), 0.0, 1.0)
    buf_ref[0:CONV_HALO, :] = u_ref[0, pl.ds(ps, CONV_HALO), :] * keep_prev
    buf_ref[CONV_HALO:CONV_HALO + TM, :] = u_ref[0, pl.ds(start, TM), :]
    buf_ref[CONV_HALO + TM:2 * CONV_HALO + TM, :] = u_ref[0, pl.ds(ns, CONV_HALO), :] * keep_next
    off = CONV_HALO - CONV_WIDTH // 2
    acc = jnp.zeros((TM, CONV_CH), F32)
    for j in range(CONV_WIDTH):
        acc = acc + buf_ref[off + j:off + j + TM, :] * w_ref[j:j + 1, :]
    y = acc + b_ref[...]
    mu = jnp.mean(y, axis=-1, keepdims=True)
    d = y - mu
    var = jnp.mean(d * d, axis=-1, keepdims=True)
    z = d * lax.rsqrt(var + EPS) * g_ref[...] + bb_ref[...]
    o_ref[0] = (z * jax.nn.sigmoid(z)).astype(BF16)


def _conv(u, lw, t_off):
    B, T, C = u.shape
    nt = T // TM - t_off
    return pl.pallas_call(
        functools.partial(_conv_kernel, t_off=t_off, n_rows=T),
        grid=(B, nt),
        in_specs=[
            pl.BlockSpec((1, T, C), lambda b, i: (b, 0, 0)),
            _full((CONV_WIDTH, C)),
            _full((1, C)),
            _full((1, C)),
            _full((1, C)),
        ],
        out_specs=pl.BlockSpec((1, TM, C), lambda b, i: (b, i, 0)),
        out_shape=jax.ShapeDtypeStruct((B, nt * TM, C), BF16),
        scratch_shapes=[pltpu.VMEM((TM + 2 * CONV_HALO, C), F32)],
        compiler_params=_cp(("arbitrary", "arbitrary")),
        name="conv",
    )(u, lw["conv_w"], lw["conv_b"], lw["conv_g"], lw["conv_bb"])


def _out_kernel(om_ref, oc_ref, os_ref, x_ref, mod_ref, w1_ref, w2_ref, w3_ref, n2_ref, wrh_ref, wrl_ref, br_ref,
                tri_ref, upper_ref, xn_ref, hf_ref, pos_ref, gt_ref, meta_ref, cnt_ref, run_ref):
    first = (pl.program_id(0) == 0) & (pl.program_id(1) == 0)

    @pl.when(first)
    def _():
        run_ref[...] = jnp.zeros_like(run_ref)

    d = functools.partial(jnp.dot, preferred_element_type=F32)
    mix = d(om_ref[0], w1_ref[...]) + d(oc_ref[0], w2_ref[...]) + d(os_ref[0], w3_ref[...])
    mod = mod_ref[0, 0]
    g1, sh2, sc2 = mod[2:3], mod[3:4], mod[4:5]
    xn = x_ref[0] + g1 * mix
    xn_ref[0] = xn
    hf = xn * lax.rsqrt(jnp.mean(xn * xn, axis=-1, keepdims=True) + EPS) * n2_ref[...]
    hf = hf * (1.0 + sc2) + sh2
    hf_ref[0] = hf.astype(BF16)

    hi, lo = _split(hf)
    logits = d(hi, wrh_ref[...]) + d(hi, wrl_ref[...]) + d(lo, wrh_ref[...]) + br_ref[...]
    lane = lax.broadcasted_iota(I32, (TM, LANES), 1)
    l = logits
    ohs, vals = [], []
    for _ in range(TOP_K):
        m = jnp.max(l, axis=-1, keepdims=True)
        idx = jnp.min(jnp.where(l == m, lane, LANES), axis=-1, keepdims=True)
        oh = lane == idx
        ohs.append(oh)
        vals.append(m)
        l = jnp.where(oh, -jnp.inf, l)
    ex = [jnp.exp(v - vals[0]) for v in vals]
    den = ex[0] + ex[1] + ex[2] + ex[3]
    gates = [e / den for e in ex]

    oa = jnp.zeros((TM, LANES), F32)
    for oh in ohs:
        oa = oa + oh.astype(F32)
    hist = jnp.sum(oa, axis=0, keepdims=True)
    slot_rows = jnp.floor((hist + (RUN_CHUNK - 1)) * (1.0 / RUN_CHUNK)) * RUN_CHUNK
    slot_off = d(jnp.broadcast_to(slot_rows, (8, LANES)).astype(BF16), upper_ref[...])
    where_ = d(tri_ref[...], oa.astype(BF16)) + slot_off[0:1, :]
    poss = [jnp.sum(jnp.where(oh, where_, 0.0), axis=-1, keepdims=True).astype(I32) for oh in ohs]

    p_out = jnp.zeros((TM, LANES), I32)
    g_out = jnp.zeros((TM, LANES), F32)
    for k in range(TOP_K):
        p_out = jnp.where(lane == k, poss[k], p_out)
        g_out = jnp.where(lane == k, gates[k], g_out)
    pos_ref[0] = p_out
    gt_ref[0] = g_out

    srow = lax.broadcasted_iota(I32, (8, LANES), 0)
    meta = jnp.where(srow == 0, hist, jnp.where(srow == 1, run_ref[...], jnp.where(srow == 2, slot_off, 0.0)))
    meta_ref[0] = meta.astype(I32)
    run_ref[...] = run_ref[...] + hist
    cnt_ref[...] = run_ref[...]


def _outproj(om, oc, osw, xu, modsel, lw, tri, upper, t_off):
    B, T, D = xu.shape
    nt = T // TM - t_off
    row = lambda b, i: (b, i, 0)
    Tq = nt * TM
    in_specs = [
        pl.BlockSpec((1, TM, 512), row),
        pl.BlockSpec((1, TM, 256), row),
        pl.BlockSpec((1, TM, 512), row),
        pl.BlockSpec((1, TM, D), lambda b, i: (b, i + t_off, 0)),
        pl.BlockSpec((1, 1, 6, D), lambda b, i: (b, jnp.minimum(i + t_off, 1), 0, 0)),
        _full((512, D)),
        _full((256, D)),
        _full((512, D)),
        _full((1, D)),
        _full((D, LANES)),
        _full((D, LANES)),
        _full((1, LANES)),
        _full((TM, TM)),
        _full((LANES, LANES)),
    ]
    out_shape = [
        jax.ShapeDtypeStruct((B, Tq, D), F32),
        jax.ShapeDtypeStruct((B, Tq, D), BF16),
        jax.ShapeDtypeStruct((B, Tq, LANES), I32),
        jax.ShapeDtypeStruct((B, Tq, LANES), F32),
        jax.ShapeDtypeStruct((B * nt, 8, LANES), I32),
        jax.ShapeDtypeStruct((8, LANES), F32),
    ]
    out_specs = [
        pl.BlockSpec((1, TM, D), row),
        pl.BlockSpec((1, TM, D), row),
        pl.BlockSpec((1, TM, LANES), row),
        pl.BlockSpec((1, TM, LANES), row),
        pl.BlockSpec((1, 8, LANES), lambda b, i: (b * nt + i, 0, 0)),
        pl.BlockSpec((8, LANES), lambda b, i: (0, 0)),
    ]
    return pl.pallas_call(
        _out_kernel,
        grid=(B, nt),
        in_specs=in_specs,
        out_specs=out_specs,
        out_shape=out_shape,
        scratch_shapes=[pltpu.VMEM((8, LANES), F32)],
        compiler_params=_cp(("arbitrary", "arbitrary")),
        name="outproj_router",
    )(om, oc, osw, xu, modsel, lw["w_o1"], lw["w_o2"], lw["w_o3"], lw["n2"], lw["wr_hi"], lw["wr_lo"], lw["br"], tri, upper)


SUB = 8
RUN_CHUNK = 16
SLOT_ROWS = TM * TOP_K + N_EXPERTS * RUN_CHUNK


def _rows(ref, row0, nrows):
    start = row0 * SUB if isinstance(row0, int) else pl.multiple_of(row0 * SUB, SUB)
    return ref.at[pl.ds(start, nrows * SUB), :]


def _to_tiles(ref, val, nrows):
    for j in range(SUB):
        ref[pl.ds(j, nrows, stride=SUB), :] = val[:, LANES * j:LANES * (j + 1)]


def _from_tiles(ref, nrows):
    return jnp.concatenate([ref[pl.ds(j, nrows, stride=SUB), :] for j in range(SUB)], axis=1)


def _slot_matrix(pos, weights):
    col = lax.broadcasted_iota(I32, (TM, SLOT_ROWS), 1)
    m = jnp.zeros((TM, SLOT_ROWS), F32)
    for k in range(TOP_K):
        m = jnp.where(col == pos[:, k:k + 1], weights[k], m)
    return m.astype(BF16)


def _run_copies(meta_ref, pst_ref, buf_ref, hbm_ref, sem, to_hbm):
    total = jnp.int32(0)
    for e in range(N_EXPERTS):
        n = meta_ref[0, 0, e]
        nch = (n + (RUN_CHUNK - 1)) // RUN_CHUNK
        seg0 = pst_ref[e] + meta_ref[0, 1, e]
        slot0 = meta_ref[0, 2, e]

        def chunk(c, carry):
            a = _rows(buf_ref, slot0 + c * RUN_CHUNK, RUN_CHUNK)
            b = _rows(hbm_ref, seg0 + c * RUN_CHUNK, RUN_CHUNK)
            (pltpu.make_async_copy(a, b, sem) if to_hbm else pltpu.make_async_copy(b, a, sem)).start()
            return carry

        lax.fori_loop(0, nch, chunk, 0)
        total = total + nch
    return total


def _run_waits(total, buf_ref, hbm_ref, sem, to_hbm):
    a = _rows(buf_ref, 0, RUN_CHUNK)
    b = _rows(hbm_ref, 0, RUN_CHUNK)

    def one(c, carry):
        (pltpu.make_async_copy(a, b, sem) if to_hbm else pltpu.make_async_copy(b, a, sem)).wait()
        return carry

    lax.fori_loop(0, total, one, 0)


def _run_total(meta_ref):
    total = jnp.int32(0)
    for e in range(N_EXPERTS):
        total = total + (meta_ref[0, 0, e] + (RUN_CHUNK - 1)) // RUN_CHUNK
    return total


def _disp_kernel(pst_ref, cnt_ref, pad_ref, na_ref, meta_ref, hf_ref, pos_ref, xb_ref, srt_ref, zero_ref, tot_ref,
                 sem, zsem, *, n_blocks):
    step = pl.program_id(0) * pl.num_programs(1) + pl.program_id(1)
    n_steps = pl.num_programs(0) * pl.num_programs(1)
    slot = step % 2
    first = step == 0

    @pl.when(first)
    def _():
        zero_ref[...] = jnp.zeros_like(zero_ref)

        def zblock(j):
            return pltpu.make_async_copy(zero_ref, _rows(xb_ref, j * EXPERT_BLOCK, EXPERT_BLOCK), zsem)

        def zb_start(j, carry):
            zblock(j).start()
            return carry

        def zb_wait(j, carry):
            zblock(j).wait()
            return carry

        lax.fori_loop(na_ref[0], n_blocks, zb_start, 0)
        lax.fori_loop(na_ref[0], n_blocks, zb_wait, 0)

        for e in range(N_EXPERTS):
            lo = pst_ref[e] + cnt_ref[e]
            nrow = pad_ref[e] - cnt_ref[e]
            nz = nrow // RUN_CHUNK
            lo1 = lo + nz * RUN_CHUNK
            n1 = nrow - nz * RUN_CHUNK

            def zchunk(c):
                return pltpu.make_async_copy(_rows(zero_ref, 0, RUN_CHUNK), _rows(xb_ref, lo + c * RUN_CHUNK, RUN_CHUNK), zsem)

            def zrow(r):
                return pltpu.make_async_copy(_rows(zero_ref, 0, 1), _rows(xb_ref, lo1 + r, 1), zsem)

            def zc_start(c, carry):
                zchunk(c).start()
                return carry

            def zc_wait(c, carry):
                zchunk(c).wait()
                return carry

            def zr_start(r, carry):
                zrow(r).start()
                return carry

            def zr_wait(r, carry):
                zrow(r).wait()
                return carry

            lax.fori_loop(0, nz, zc_start, 0)
            lax.fori_loop(0, n1, zr_start, 0)
            lax.fori_loop(0, nz, zc_wait, 0)
            lax.fori_loop(0, n1, zr_wait, 0)

    ones = [1.0] * TOP_K
    q = _slot_matrix(pos_ref[0], ones)
    srt = lax.dot_general(q, hf_ref[0], (((0,), (0,)), ((), ())), preferred_element_type=F32)
    _to_tiles(srt_ref.at[slot], srt, SLOT_ROWS)

    @pl.when(step > 0)
    def _():
        _run_waits(tot_ref[1 - slot], srt_ref.at[1 - slot], xb_ref, sem.at[1 - slot], True)

    total = _run_copies(meta_ref, pst_ref, srt_ref.at[slot], xb_ref, sem.at[slot], True)
    tot_ref[slot] = total

    @pl.when(step == n_steps - 1)
    def _():
        _run_waits(total, srt_ref.at[slot], xb_ref, sem.at[slot], True)


def _dispatch(pstart, counts, padded, n_act, meta, hf, pos, n_buf):
    B, Tq, D = hf.shape
    nt = Tq // TM
    assert D == SUB * LANES
    grid_spec = pltpu.PrefetchScalarGridSpec(
        num_scalar_prefetch=4,
        grid=(B, nt),
        in_specs=[
            pl.BlockSpec((1, 8, LANES), lambda b, i, *_: (b * nt + i, 0, 0), memory_space=pltpu.SMEM),
            pl.BlockSpec((1, TM, D), lambda b, i, *_: (b, i, 0)),
            pl.BlockSpec((1, TM, LANES), lambda b, i, *_: (b, i, 0)),
        ],
        out_specs=pl.BlockSpec(memory_space=pl.ANY),
        scratch_shapes=[pltpu.VMEM((2, SLOT_ROWS * SUB, LANES), F32), pltpu.VMEM((EXPERT_BLOCK * SUB, LANES), F32),
                        pltpu.SMEM((2,), I32), pltpu.SemaphoreType.DMA((2,)), pltpu.SemaphoreType.DMA],
    )
    return pl.pallas_call(
        functools.partial(_disp_kernel, n_blocks=n_buf // EXPERT_BLOCK),
        grid_spec=grid_spec,
        out_shape=jax.ShapeDtypeStruct((n_buf * SUB, LANES), F32),
        compiler_params=_cp(("arbitrary", "arbitrary")),
        name="dispatch",
    )(pstart, counts, padded, n_act, meta, hf, pos)


W_ROWS = 256


def _exp_kernel(be_ref, bx_ref, na_ref, x_ref, w1_ref, b1a_ref, b1b_ref, w2_ref, b2_ref, o_ref, w1a_s, w1b_s, w2_s):
    j = pl.program_id(0)
    active = j < na_ref[0]
    fresh = (j == 0) | (be_ref[j] != be_ref[jnp.maximum(j - 1, 0)])

    @pl.when(active & fresh)
    def _():
        lane = lax.broadcasted_iota(I32, (W_ROWS, LANES), 1)
        lo = lane < 64
        idx = jnp.where(lo, 2 * lane, 2 * (lane - 64) + 1)

        def rows(i, carry):
            r0 = pl.multiple_of(i * W_ROWS, W_ROWS)
            for c in range(w1_ref.shape[2] // (2 * LANES)):
                a = w1_ref[0, pl.ds(r0, W_ROWS), 2 * LANES * c:2 * LANES * c + LANES]
                b = w1_ref[0, pl.ds(r0, W_ROWS), 2 * LANES * c + LANES:2 * LANES * (c + 1)]
                pa = jnp.take_along_axis(a, idx, axis=1)
                pb = jnp.take_along_axis(b, idx, axis=1)
                ev = jnp.where(lo, pa, pltpu.roll(pb, 64, axis=1))
                od = jnp.where(lo, pltpu.roll(pa, 64, axis=1), pb)
                w1a_s[pl.ds(r0, W_ROWS), LANES * c:LANES * (c + 1)] = ev.astype(BF16)
                w1b_s[pl.ds(r0, W_ROWS), LANES * c:LANES * (c + 1)] = od.astype(BF16)
            w2_s[pl.ds(r0, W_ROWS), :] = w2_ref[0, pl.ds(r0, W_ROWS), :].astype(BF16)
            return carry

        lax.fori_loop(0, w1_ref.shape[1] // W_ROWS, rows, 0)

    @pl.when(active)
    def _():
        x = _from_tiles(x_ref, EXPERT_BLOCK).astype(BF16)
        ug = jnp.dot(x, w1a_s[...], preferred_element_type=F32) + b1a_ref[0]
        ul = jnp.dot(x, w1b_s[...], preferred_element_type=F32) + b1b_ref[0]
        xg = jnp.minimum(ug, SWIGLU_LIMIT)
        xl = jnp.clip(ul, -SWIGLU_LIMIT, SWIGLU_LIMIT)
        act = xg * jax.nn.sigmoid(SWIGLU_ALPHA * xg) * (xl + 1.0)
        y = jnp.dot(act.astype(BF16), w2_s[...], preferred_element_type=F32) + b2_ref[0]
        _to_tiles(o_ref, y, EXPERT_BLOCK)

    @pl.when(jnp.logical_not(active))
    def _():
        o_ref[...] = jnp.zeros_like(o_ref)


def _experts(blk_e, blk_x, n_act, xb, lw):
    nb = xb.shape[0] // (EXPERT_BLOCK * SUB)
    De, D = lw["w2"].shape[1:]
    assert De == D
    wmap = lambda j, be, bx, na: (be[j], 0, 0)
    e_off = lw["e_off"]
    wmap_all = lambda j, be, bx, na: (be[j] + e_off, 0, 0)
    grid_spec = pltpu.PrefetchScalarGridSpec(
        num_scalar_prefetch=3,
        grid=(nb,),
        in_specs=[
            pl.BlockSpec((EXPERT_BLOCK * SUB, LANES), lambda j, be, bx, na: (bx[j], 0)),
            pl.BlockSpec((1, D, 2 * De), wmap_all),
            pl.BlockSpec((1, 1, De), wmap),
            pl.BlockSpec((1, 1, De), wmap),
            pl.BlockSpec((1, De, D), wmap_all),
            pl.BlockSpec((1, 1, D), wmap),
        ],
        out_specs=pl.BlockSpec((EXPERT_BLOCK * SUB, LANES), lambda j, be, bx, na: (j, 0)),
        scratch_shapes=[pltpu.VMEM((D, De), BF16), pltpu.VMEM((D, De), BF16), pltpu.VMEM((De, D), BF16)],
    )
    return pl.pallas_call(
        _exp_kernel,
        grid_spec=grid_spec,
        out_shape=jax.ShapeDtypeStruct(xb.shape, F32),
        compiler_params=pltpu.CompilerParams(dimension_semantics=("arbitrary",), vmem_limit_bytes=EXPERT_VMEM_LIMIT),
        name="experts",
    )(blk_e, blk_x, n_act, xb, lw["w1"], lw["b1a"], lw["b1b"], lw["w2"], lw["b2"])


def _comb_kernel(pst_ref, meta_ref, meta_next_ref, pos_ref, gt_ref, xn_ref, mod_ref, yb_ref, o_ref, buf_ref, sem):
    step = pl.program_id(0) * pl.num_programs(1) + pl.program_id(1)
    n_steps = pl.num_programs(0) * pl.num_programs(1)
    slot = step % 2

    @pl.when(step == 0)
    def _():
        buf_ref[...] = jnp.zeros_like(buf_ref)
        _run_copies(meta_ref, pst_ref, buf_ref.at[0], yb_ref, sem.at[0], False)

    @pl.when(step < n_steps - 1)
    def _():
        _run_copies(meta_next_ref, pst_ref, buf_ref.at[1 - slot], yb_ref, sem.at[1 - slot], False)

    gt = gt_ref[0]
    g = _slot_matrix(pos_ref[0], [gt[:, k:k + 1] for k in range(TOP_K)])
    _run_waits(_run_total(meta_ref), buf_ref.at[slot], yb_ref, sem.at[slot], False)
    rows = _from_tiles(buf_ref.at[slot], SLOT_ROWS).astype(BF16)
    y = jnp.dot(g, rows, preferred_element_type=F32)
    g2 = mod_ref[0, 0][5:6]
    o_ref[0] = xn_ref[0] + g2 * y


def _combine(pstart, meta, pos, gates, xn, modsel, yb, t_off):
    B, Tq, D = xn.shape
    nt = Tq // TM
    grid_spec = pltpu.PrefetchScalarGridSpec(
        num_scalar_prefetch=1,
        grid=(B, nt),
        in_specs=[
            pl.BlockSpec((1, 8, LANES), lambda b, i, *_: (b * nt + i, 0, 0), memory_space=pltpu.SMEM),
            pl.BlockSpec((1, 8, LANES), lambda b, i, *_: (jnp.minimum(b * nt + i + 1, B * nt - 1), 0, 0),
                         memory_space=pltpu.SMEM),
            pl.BlockSpec((1, TM, LANES), lambda b, i, *_: (b, i, 0)),
            pl.BlockSpec((1, TM, LANES), lambda b, i, *_: (b, i, 0)),
            pl.BlockSpec((1, TM, D), lambda b, i, *_: (b, i, 0)),
            pl.BlockSpec((1, 1, 6, D), lambda b, i, *_: (b, jnp.minimum(i + t_off, 1), 0, 0)),
            pl.BlockSpec(memory_space=pl.ANY),
        ],
        out_specs=pl.BlockSpec((1, TM, D), lambda b, i, *_: (b, i, 0)),
        scratch_shapes=[pltpu.VMEM((2, SLOT_ROWS * SUB, LANES), F32), pltpu.SemaphoreType.DMA((2,))],
    )
    return pl.pallas_call(
        _comb_kernel,
        grid_spec=grid_spec,
        out_shape=jax.ShapeDtypeStruct((B, Tq, D), F32),
        compiler_params=_cp(("arbitrary", "arbitrary")),
        name="combine",
    )(pstart, meta, meta, pos, gates, xn, modsel, yb)


def _take_cols(w, cols):
    cols = np.asarray(cols)
    out = jnp.take(w, jnp.asarray(np.maximum(cols, 0)), axis=-1)
    return jnp.where(jnp.asarray(cols >= 0), out, 0.0)


def _in_cols():
    pi = _PI
    cols = list(range(0, 256))
    cols += [256 + i for i in range(64)] + [256 + pi[i] for i in range(64)]
    for g in range(SWA_KV_HEADS):
        base = 320 + 64 * g
        cols += [base + i for i in range(64)] + [base + pi[i] for i in range(64)]
    for g in range(SWA_KV_HEADS):
        base = 448 + 64 * g
        cols += [base + i for i in range(64)] + [-1] * 64
    cols += list(range(576, 960))
    for h in range(SWA_HEADS):
        base = 960 + 64 * h
        cols += [base + i for i in range(64)] + [base + pi[i] for i in range(64)]
    cols += list(range(1216, 1728))
    return cols


def _layer_weights(l, a):
    pi = _PI
    lw = {}
    lw["n1"] = a["norm1_g"][l][None, :]
    lw["n2"] = a["norm2_g"][l][None, :]
    lw["w_in"] = _take_cols(a["w_in"][l], _in_cols()).astype(BF16)
    lw["kvg"] = a["mla_kv_norm"][l][None, :]
    lw["qg"] = a["mla_q_norm"][l][None, :]
    uq_cols = []
    for h in range(MLA_HEADS):
        base = MLA_QK * h
        uq_cols += [base + i for i in range(128)] + [base + 128 + i for i in range(64)] + [base + 128 + pi[i] for i in range(64)]
    lw["w_uq"] = _take_cols(a["mla_w_uq"][l], uq_cols).astype(BF16)
    ukv_cols = [256 * h + i for h in range(MLA_HEADS) for i in range(128)]
    ukv_cols += [256 * h + 128 + i for h in range(MLA_HEADS) for i in range(128)]
    lw["w_ukv"] = _take_cols(a["mla_w_ukv"][l], ukv_cols).astype(BF16)
    gq = a["mla_q_head_norm"][l]
    lw["gq"] = (jnp.concatenate([gq[:128], gq[128:], gq[128:][pi]]) * (MLA_QK ** -0.5))[None, :]
    gk = a["mla_k_head_norm"][l]
    lw["gkn"] = gk[:128][None, :]
    lw["gkp"] = jnp.concatenate([gk[128:], gk[128:][pi]])[None, :]
    sq = a["swa_q_norm"][l]
    lw["gsq"] = (jnp.concatenate([sq, sq[pi]]) * (SWA_HEAD_DIM ** -0.5))[None, :]
    sk = a["swa_k_norm"][l]
    lw["gsk"] = jnp.concatenate([sk, sk[pi]])[None, :]
    lw["conv_w"] = a["conv_w"][l]
    lw["conv_b"] = a["conv_b"][l][None, :]
    lw["conv_g"] = a["conv_ln_g"][l][None, :]
    lw["conv_bb"] = a["conv_ln_b"][l][None, :]
    lw["sink"] = a["swa_sink"][l]
    wo = a["w_out"][l]
    lw["w_o1"] = wo[0:512].astype(BF16)
    lw["w_o2"] = wo[512:768].astype(BF16)
    o3 = wo[768:1024].reshape(SWA_HEADS, SWA_HEAD_DIM, -1)
    lw["w_o3"] = jnp.concatenate([o3, jnp.zeros_like(o3)], axis=1).reshape(SWA_HEADS * 128, -1).astype(BF16)
    wr = jnp.pad(a["router_w"][l], ((0, 0), (0, LANES - N_EXPERTS)))
    lw["wr_hi"] = wr.astype(BF16)
    lw["wr_lo"] = (wr - lw["wr_hi"].astype(F32)).astype(BF16)
    lw["br"] = jnp.pad(a["router_b"][l], (0, LANES - N_EXPERTS), constant_values=NEG)[None, :]
    lw["w1"] = a["exp_w1"].reshape((-1,) + a["exp_w1"].shape[2:])
    lw["e_off"] = l * a["exp_w1"].shape[1]
    b1 = a["exp_b1"][l]
    lw["b1a"] = b1[:, None, 0::2]
    lw["b1b"] = b1[:, None, 1::2]
    lw["w2"] = a["exp_w2"].reshape((-1,) + a["exp_w2"].shape[2:])
    lw["b2"] = a["exp_b2"][l][:, None, :]
    return lw


def _rope_tables(n_ctx, n_lat):
    q = MLA_ROPE // 4
    n = jnp.arange(n_lat, dtype=I32)
    row = (n // GRID_W).astype(F32)
    col = (n % GRID_W).astype(F32)
    inv = ROPE_BASE ** (-jnp.arange(q, dtype=F32) / q)
    ang_r = row[:, None] * inv
    ang_c = col[:, None] * inv
    cos = jnp.concatenate([jnp.cos(ang_r), jnp.cos(ang_r), jnp.cos(ang_c), jnp.cos(ang_c)], axis=1)
    sin = jnp.concatenate([-jnp.sin(ang_r), jnp.sin(ang_r), -jnp.sin(ang_c), jnp.sin(ang_c)], axis=1)
    cos = jnp.concatenate([jnp.ones((n_ctx, 64), F32), cos], axis=0)
    sin = jnp.concatenate([jnp.zeros((n_ctx, 64), F32), sin], axis=0)
    z = jnp.zeros_like(cos)
    return jnp.concatenate([cos, z], axis=1), jnp.concatenate([sin, z], axis=1)


def _routing_tables(cnt_f, n_blocks):
    counts = cnt_f[0, :N_EXPERTS].astype(I32)
    padded = (counts + (RUN_CHUNK - 1) + EXPERT_BLOCK - 1) // EXPERT_BLOCK * EXPERT_BLOCK
    padded = jnp.where(counts > 0, padded, 0)
    pend = jnp.cumsum(padded)
    pstart = pend - padded
    n_act = pend[-1] // EXPERT_BLOCK
    blk = jnp.minimum(jnp.arange(n_blocks, dtype=I32), n_act - 1)
    blk_e = jnp.sum((pend[None, :] <= (blk * EXPERT_BLOCK)[:, None]).astype(I32), axis=1)
    blk_e = jnp.minimum(blk_e, N_EXPERTS - 1)
    return counts, padded.astype(I32), pstart.astype(I32), blk_e, blk.astype(I32), n_act.reshape(1).astype(I32)


def kernel(x, c, ctx, c_ctx, norm1_g, norm2_g, w_ada, b_ada, w_in, mla_q_norm, mla_kv_norm, mla_w_uq, mla_w_ukv, mla_q_head_norm, mla_k_head_norm, conv_w, conv_b, conv_ln_g, conv_ln_b, swa_q_norm, swa_k_norm, swa_sink, w_out, router_w, router_b, exp_w1, exp_b1, exp_w2, exp_b2):
    a = dict(norm1_g=norm1_g, norm2_g=norm2_g, w_in=w_in, mla_q_norm=mla_q_norm, mla_kv_norm=mla_kv_norm,
             mla_w_uq=mla_w_uq, mla_w_ukv=mla_w_ukv, mla_q_head_norm=mla_q_head_norm, mla_k_head_norm=mla_k_head_norm,
             conv_w=conv_w, conv_b=conv_b, conv_ln_g=conv_ln_g, conv_ln_b=conv_ln_b, swa_q_norm=swa_q_norm,
             swa_k_norm=swa_k_norm, swa_sink=swa_sink, w_out=w_out, router_w=router_w, router_b=router_b,
             exp_w1=exp_w1, exp_b1=exp_b1, exp_w2=exp_w2, exp_b2=exp_b2)
    B, S, D = x.shape
    n_ctx = ctx.shape[1]
    depth = w_ada.shape[0]
    assert n_ctx == TM and S % TM == 0 and B + 1 <= 16
    T = n_ctx + S

    s_in = jnp.zeros((16, D), F32).at[:B].set(c).at[B].set(c_ctx)
    mods = _ada(s_in, w_ada, b_ada)
    cos_t, sin_t = _rope_tables(n_ctx, S)
    tri = jnp.tril(jnp.ones((TM, TM), F32), -1).astype(BF16)
    upper = jnp.triu(jnp.ones((LANES, LANES), F32), 1).astype(BF16)

    xu = jnp.concatenate([ctx, x], axis=1)
    for l in range(depth):
        last = l == depth - 1
        t_off = 1 if last else 0
        lw = _layer_weights(l, a)
        m = mods[l].reshape(16, 6, D)
        modsel = jnp.stack([jnp.broadcast_to(m[B], (B, 6, D)), m[:B]], axis=1)

        qm, km, vm, qs, ks, vs, u = _prep(xu, modsel, lw, cos_t, sin_t, 0)
        om = _mla(qm, km, vm, t_off)
        oc = _conv(u, lw, t_off)
        osw = _swa(lw["sink"], qs, ks, vs, t_off)
        xn, hf, pos, gt_o, meta, cnt = _outproj(om, oc, osw, xu, modsel, lw, tri, upper, t_off)

        n_tok = B * (T - t_off * TM)
        nk = n_tok * TOP_K
        n_buf = -(-(nk + N_EXPERTS * (RUN_CHUNK - 1 + EXPERT_BLOCK - 1)) // EXPERT_BLOCK) * EXPERT_BLOCK
        counts, padded, pstart, blk_e, blk_x, n_act = _routing_tables(cnt, n_buf // EXPERT_BLOCK)
        xb = _dispatch(pstart, counts, padded, n_act, meta, hf, pos, n_buf)
        yb = _experts(blk_e, blk_x, n_act, xb, lw)
        xu = _combine(pstart, meta, pos, gt_o, xn, modsel, yb, t_off)
    return xu
```

```python
import functools

import numpy as np
import jax
import jax.numpy as jnp
from jax import lax
from jax.experimental import pallas as pl
from jax.experimental.pallas import tpu as pltpu

F32 = jnp.float32
BF16 = jnp.bfloat16
I32 = jnp.int32

GRID_W = 64
ROPE_BASE = 10000.0
EPS = 1e-6
MLA_HEADS = 4
MLA_NOPE = 128
MLA_ROPE = 64
MLA_V = 128
MLA_QK = MLA_NOPE + MLA_ROPE
MLA_Q_RANK = 384
MLA_KV_RANK = 256
CONV_CH = 256
CONV_WIDTH = 31
SWA_HEADS = 4
SWA_KV_HEADS = 2
SWA_HEAD_DIM = 64
SWA_WINDOW = 128
N_EXPERTS = 32
TOP_K = 4
SWIGLU_LIMIT = 7.0
SWIGLU_ALPHA = 1.702
EXPERT_BLOCK = 256

LANES = 128
TM = 256
CONV_HALO = 16
VMEM_LIMIT = 48 * 1024 * 1024
EXPERT_VMEM_LIMIT = 56 * 1024 * 1024
NEG = -1e30
LOG2E = 1.4426950408889634

_PI = np.array([i + 16 if (i % 32) < 16 else i - 16 for i in range(64)])


def _cp(sem):
    return pltpu.CompilerParams(dimension_semantics=sem, vmem_limit_bytes=VMEM_LIMIT)


def _full(shape):
    n = len(shape)
    return pl.BlockSpec(shape, lambda *a, _n=n: (0,) * _n)


def _split(x):
    hi = x.astype(BF16)
    lo = (x - hi.astype(F32)).astype(BF16)
    return hi, lo


def _dot3(a, b):
    ah, al = _split(a)
    bh, bl = _split(b)
    d = functools.partial(jnp.dot, preferred_element_type=F32)
    return d(ah, bh) + d(ah, bl) + d(al, bh)


def _ada_kernel(s_ref, w_ref, b_ref, o_ref):
    s = s_ref[...]
    s = s * jax.nn.sigmoid(s)
    o_ref[0] = _dot3(s, w_ref[0]) + b_ref[0]


def _ada(s_in, w_ada, b_ada):
    L, D, N = w_ada.shape
    tn = 1536
    return pl.pallas_call(
        _ada_kernel,
        grid=(L, N // tn),
        in_specs=[
            pl.BlockSpec((16, D), lambda l, j: (0, 0)),
            pl.BlockSpec((1, D, tn), lambda l, j: (l, 0, j)),
            pl.BlockSpec((1, 1, tn), lambda l, j: (l, 0, j)),
        ],
        out_specs=pl.BlockSpec((1, 16, tn), lambda l, j: (l, 0, j)),
        out_shape=jax.ShapeDtypeStruct((L, 16, N), F32),
        compiler_params=_cp(("arbitrary", "arbitrary")),
        name="ada",
    )(s_in, w_ada, b_ada.reshape(L, 1, N))


def _rope(x, c, s):
    return x * c + pltpu.roll(x, 64, axis=1) * s


def _prep_kernel(x_ref, mod_ref, n1_ref, win_ref, kvg_ref, wukv_ref, qg_ref, wuq_ref, gq_ref, gkn_ref, gkp_ref,
                 gsq_ref, gsk_ref, cos_ref, sin_ref, qm_ref, km_ref, vm_ref, qs_ref, ks_ref, vs_ref, u_ref):
    x = x_ref[0]
    mod = mod_ref[0, 0]
    sh, sc = mod[0:1], mod[1:2]
    y = x * lax.rsqrt(jnp.mean(x * x, axis=-1, keepdims=True) + EPS) * n1_ref[...]
    h = y * (1.0 + sc) + sh
    p = jnp.dot(h.astype(BF16), win_ref[...], preferred_element_type=F32)
    c = cos_ref[...]
    s = sin_ref[...]
    lane = lax.broadcasted_iota(I32, (TM, LANES), 1)
    lo = lane < 64

    def ss_lo(v):
        return jnp.sum(jnp.where(lo, v * v, 0.0), axis=-1, keepdims=True)

    ckv = p[:, 0:256]
    ckvn = ckv * lax.rsqrt(jnp.mean(ckv * ckv, axis=-1, keepdims=True) + EPS) * kvg_ref[...]
    kv = jnp.dot(ckvn.astype(BF16), wukv_ref[...], preferred_element_type=F32)
    kpe = p[:, 256:384]
    ss_pe = ss_lo(kpe)
    kpe_rot = _rope(kpe * gkp_ref[...], c, s)
    for hh in range(MLA_HEADS):
        kn = kv[:, 128 * hh:128 * hh + 128]
        r = lax.rsqrt((jnp.sum(kn * kn, axis=-1, keepdims=True) + ss_pe) * (1.0 / MLA_QK) + EPS)
        km_ref[0, hh, :, 0:128] = (kn * r * gkn_ref[...]).astype(BF16)
        km_ref[0, hh, :, 128:256] = (kpe_rot * r).astype(BF16)
        vm_ref[0, hh, :, 0:128] = kv[:, 512 + 128 * hh:640 + 128 * hh].astype(BF16)
        vm_ref[0, hh, :, 128:256] = (lane == 0).astype(BF16)

    cq = p[:, 896:1280]
    cqn = cq * lax.rsqrt(jnp.mean(cq * cq, axis=-1, keepdims=True) + EPS) * qg_ref[...]
    q = jnp.dot(cqn.astype(BF16), wuq_ref[...], preferred_element_type=F32)
    gq = gq_ref[...]
    for hh in range(MLA_HEADS):
        qn = q[:, 256 * hh:256 * hh + 128]
        qp = q[:, 256 * hh + 128:256 * hh + 256]
        r = lax.rsqrt((jnp.sum(qn * qn, axis=-1, keepdims=True) + ss_lo(qp)) * (1.0 / MLA_QK) + EPS)
        qm_ref[0, hh, :, 0:128] = (qn * r * gq[:, 0:128]).astype(BF16)
        qm_ref[0, hh, :, 128:256] = _rope(qp * r * gq[:, 128:256], c, s).astype(BF16)

    for g in range(SWA_KV_HEADS):
        xk = p[:, 384 + 128 * g:512 + 128 * g]
        r = lax.rsqrt(ss_lo(xk) * (1.0 / SWA_HEAD_DIM) + EPS)
        ks_ref[0, g] = _rope(xk * r * gsk_ref[...], c, s).astype(BF16)
        vs_ref[0, g] = p[:, 640 + 128 * g:768 + 128 * g].astype(BF16)
    for hh in range(SWA_HEADS):
        xq = p[:, 1280 + 128 * hh:1408 + 128 * hh]
        r = lax.rsqrt(ss_lo(xq) * (1.0 / SWA_HEAD_DIM) + EPS)
        qs_ref[0, hh] = _rope(xq * r * gsq_ref[...], c, s).astype(BF16)

    u_ref[0] = p[:, 1792:2048] * jax.nn.sigmoid(p[:, 2048:2304])


def _prep(xu, modsel, lw, cos_t, sin_t, t_off):
    B, T, D = xu.shape
    nt = T // TM - t_off
    ncol = lw["w_in"].shape[1]
    row = lambda b, i: (b, i + t_off, 0)
    head = lambda b, i: (b, 0, i + t_off, 0)
    in_specs = [
        pl.BlockSpec((1, TM, D), row),
        pl.BlockSpec((1, 1, 6, D), lambda b, i: (b, jnp.minimum(i + t_off, 1), 0, 0)),
        _full((1, D)),
        _full((D, ncol)),
        _full((1, MLA_KV_RANK)),
        _full((MLA_KV_RANK, 1024)),
        _full((1, MLA_Q_RANK)),
        _full((MLA_Q_RANK, 1024)),
        _full((1, 256)),
        _full((1, 128)),
        _full((1, 128)),
        _full((1, 128)),
        _full((1, 128)),
        pl.BlockSpec((TM, LANES), lambda b, i: (i + t_off, 0)),
        pl.BlockSpec((TM, LANES), lambda b, i: (i + t_off, 0)),
    ]
    out_shape = [
        jax.ShapeDtypeStruct((B, MLA_HEADS, T, 256), BF16),
        jax.ShapeDtypeStruct((B, MLA_HEADS, T, 256), BF16),
        jax.ShapeDtypeStruct((B, MLA_HEADS, T, 256), BF16),
        jax.ShapeDtypeStruct((B, SWA_HEADS, T, 128), BF16),
        jax.ShapeDtypeStruct((B, SWA_KV_HEADS, T, 128), BF16),
        jax.ShapeDtypeStruct((B, SWA_KV_HEADS, T, 128), BF16),
        jax.ShapeDtypeStruct((B, T, CONV_CH), F32),
    ]
    out_specs = [
        pl.BlockSpec((1, MLA_HEADS, TM, 256), head),
        pl.BlockSpec((1, MLA_HEADS, TM, 256), head),
        pl.BlockSpec((1, MLA_HEADS, TM, 256), head),
        pl.BlockSpec((1, SWA_HEADS, TM, 128), head),
        pl.BlockSpec((1, SWA_KV_HEADS, TM, 128), head),
        pl.BlockSpec((1, SWA_KV_HEADS, TM, 128), head),
        pl.BlockSpec((1, TM, CONV_CH), row),
    ]
    return pl.pallas_call(
        _prep_kernel,
        grid=(B, nt),
        in_specs=in_specs,
        out_specs=out_specs,
        out_shape=out_shape,
        compiler_params=_cp(("arbitrary", "arbitrary")),
        name="prep",
    )(xu, modsel, lw["n1"], lw["w_in"], lw["kvg"], lw["w_ukv"], lw["qg"], lw["w_uq"], lw["gq"], lw["gkn"], lw["gkp"],
      lw["gsq"], lw["gsk"], cos_t, sin_t)


def _mla_kernel(q_ref, k_ref, v_ref, o_ref, *, q_off, n_keys):
    qi = pl.program_id(1) + q_off

    def attend(nk):
        for h in range(MLA_HEADS):
            q = q_ref[0, h]
            k = k_ref[0, h, 0:nk, :]
            v = v_ref[0, h, 0:nk, :]
            s = lax.dot_general(q, k, (((1,), (1,)), ((), ())), preferred_element_type=F32)
            m = jnp.max(s, axis=-1, keepdims=True)
            p = jnp.exp2(s - m)
            ol = jnp.dot(p.astype(BF16), v, preferred_element_type=F32)
            o = ol[:, 0:MLA_V] / ol[:, MLA_V:MLA_V + 1]
            o_ref[0, :, 128 * h:128 * h + 128] = o.astype(BF16)

    if q_off == 0:
        @pl.when(qi == 0)
        def _():
            attend(TM)

        @pl.when(qi > 0)
        def _():
            attend(n_keys)
    else:
        attend(n_keys)


def _mla(qm, km, vm, t_off):
    B, H, T, _ = qm.shape
    nt = T // TM - t_off
    return pl.pallas_call(
        functools.partial(_mla_kernel, q_off=t_off, n_keys=T),
        grid=(B, nt),
        in_specs=[
            pl.BlockSpec((1, H, TM, 256), lambda b, i: (b, 0, i + t_off, 0)),
            pl.BlockSpec((1, H, T, 256), lambda b, i: (b, 0, 0, 0)),
            pl.BlockSpec((1, H, T, 256), lambda b, i: (b, 0, 0, 0)),
        ],
        out_specs=pl.BlockSpec((1, TM, H * MLA_V), lambda b, i: (b, i, 0)),
        out_shape=jax.ShapeDtypeStruct((B, nt * TM, H * MLA_V), BF16),
        compiler_params=_cp(("arbitrary", "arbitrary")),
        name="mla_attn",
    )(qm, km, vm)


SWA_SPAN = TM + 2 * SWA_WINDOW


def _swa_kernel(sink_ref, q_ref, k_ref, v_ref, o_ref, *, q_off, n_rows):
    qi = pl.program_id(1) + q_off
    nt_dims = (((1,), (1,)), ((), ()))

    def latent():
        start = jnp.clip(qi * TM - SWA_WINDOW, 0, n_rows - SWA_SPAN)
        start = pl.multiple_of(start, SWA_WINDOW)
        qpos = qi * TM + lax.broadcasted_iota(I32, (TM, SWA_SPAN), 0)
        kpos = start + lax.broadcasted_iota(I32, (TM, SWA_SPAN), 1)
        valid = (jnp.abs(qpos - kpos) <= SWA_WINDOW) & (kpos >= TM)
        for h in range(SWA_HEADS):
            g = h // (SWA_HEADS // SWA_KV_HEADS)
            q = q_ref[0, h]
            kl = k_ref[0, g, pl.ds(start, SWA_SPAN), :]
            vl = v_ref[0, g, pl.ds(start, SWA_SPAN), :]
            kc = k_ref[0, g, 0:TM, :]
            vc = v_ref[0, g, 0:TM, :]
            sl = lax.dot_general(q, kl, nt_dims, preferred_element_type=F32)
            sl = jnp.where(valid, sl, NEG)
            scx = lax.dot_general(q, kc, nt_dims, preferred_element_type=F32)
            sink = sink_ref[h]
            m = jnp.maximum(jnp.maximum(jnp.max(sl, axis=-1, keepdims=True), jnp.max(scx, axis=-1, keepdims=True)), sink)
            pl_ = jnp.exp(sl - m)
            pc = jnp.exp(scx - m)
            l = jnp.sum(pl_, axis=-1, keepdims=True) + jnp.sum(pc, axis=-1, keepdims=True) + jnp.exp(sink - m)
            o = (jnp.dot(pl_.astype(BF16), vl, preferred_element_type=F32)
                 + jnp.dot(pc.astype(BF16), vc, preferred_element_type=F32)) / l
            o_ref[0, :, 128 * h:128 * h + 128] = o.astype(BF16)

    def context():
        for h in range(SWA_HEADS):
            g = h // (SWA_HEADS // SWA_KV_HEADS)
            q = q_ref[0, h]
            kc = k_ref[0, g, 0:TM, :]
            vc = v_ref[0, g, 0:TM, :]
            scx = lax.dot_general(q, kc, nt_dims, preferred_element_type=F32)
            sink = sink_ref[h]
            m = jnp.maximum(jnp.max(scx, axis=-1, keepdims=True), sink)
            pc = jnp.exp(scx - m)
            l = jnp.sum(pc, axis=-1, keepdims=True) + jnp.exp(sink - m)
            o = jnp.dot(pc.astype(BF16), vc, preferred_element_type=F32) / l
            o_ref[0, :, 128 * h:128 * h + 128] = o.astype(BF16)

    if q_off == 0:
        pl.when(qi == 0)(context)
        pl.when(qi > 0)(latent)
    else:
        latent()


def _swa(sink, qs, ks, vs, t_off):
    B, H, T, _ = qs.shape
    G = ks.shape[1]
    nt = T // TM - t_off
    return pl.pallas_call(
        functools.partial(_swa_kernel, q_off=t_off, n_rows=T),
        grid=(B, nt),
        in_specs=[
            pl.BlockSpec(memory_space=pltpu.SMEM),
            pl.BlockSpec((1, H, TM, 128), lambda b, i: (b, 0, i + t_off, 0)),
            pl.BlockSpec((1, G, T, 128), lambda b, i: (b, 0, 0, 0)),
            pl.BlockSpec((1, G, T, 128), lambda b, i: (b, 0, 0, 0)),
        ],
        out_specs=pl.BlockSpec((1, TM, H * 128), lambda b, i: (b, i, 0)),
        out_shape=jax.ShapeDtypeStruct((B, nt * TM, H * 128), BF16),
        compiler_params=_cp(("arbitrary", "arbitrary")),
        name="swa_attn",
    )(sink, qs, ks, vs)


def _conv_kernel(u_ref, w_ref, b_ref, g_ref, bb_ref, o_ref, buf_ref, *, t_off, n_rows):
    i = pl.program_id(1) + t_off
    nt_all = n_rows // TM
    start = pl.multiple_of(i * TM, TM)
    ps = pl.multiple_of(jnp.maximum(start - CONV_HALO, 0), 8)
    ns = pl.multiple_of(jnp.minimum(start + TM, n_rows - CONV_HALO), 8)
    keep_prev = jnp.where(i <= 1, 0.0, 1.0)
    keep_next = jnp.where((i == 0) | (i == nt_all - 1), 0.0, 1.0)
    buf_ref[0:CONV_HALO, :] = u_ref[0, pl.ds(ps, CONV_HALO), :] * keep_prev
    buf_ref[CONV_HALO:CONV_HALO + TM, :] = u_ref[0, pl.ds(start, TM), :]
    buf_ref[CONV_HALO + TM:2 * CONV_HALO + TM, :] = u_ref[0, pl.ds(ns, CONV_HALO), :] * keep_next
    off = CONV_HALO - CONV_WIDTH // 2
    acc = jnp.zeros((TM, CONV_CH), F32)
    for j in range(CONV_WIDTH):
        acc = acc + buf_ref[off + j:off + j + TM, :] * w_ref[j:j + 1, :]
    y = acc + b_ref[...]
    mu = jnp.mean(y, axis=-1, keepdims=True)
    d = y - mu
    var = jnp.mean(d * d, axis=-1, keepdims=True)
    z = d * lax.rsqrt(var + EPS) * g_ref[...] + bb_ref[...]
    o_ref[0] = (z * jax.nn.sigmoid(z)).astype(BF16)


def _conv(u, lw, t_off):
    B, T, C = u.shape
    nt = T // TM - t_off
    return pl.pallas_call(
        functools.partial(_conv_kernel, t_off=t_off, n_rows=T),
        grid=(B, nt),
        in_specs=[
            pl.BlockSpec((1, T, C), lambda b, i: (b, 0, 0)),
            _full((CONV_WIDTH, C)),
            _full((1, C)),
            _full((1, C)),
            _full((1, C)),
        ],
        out_specs=pl.BlockSpec((1, TM, C), lambda b, i: (b, i, 0)),
        out_shape=jax.ShapeDtypeStruct((B, nt * TM, C), BF16),
        scratch_shapes=[pltpu.VMEM((TM + 2 * CONV_HALO, C), F32)],
        compiler_params=_cp(("arbitrary", "arbitrary")),
        name="conv",
    )(u, lw["conv_w"], lw["conv_b"], lw["conv_g"], lw["conv_bb"])


def _out_kernel(om_ref, oc_ref, os_ref, x_ref, mod_ref, w1_ref, w2_ref, w3_ref, n2_ref, wrh_ref, wrl_ref, br_ref,
                tri_ref, upper_ref, xn_ref, hf_ref, pos_ref, gt_ref, meta_ref, cnt_ref, run_ref):
    first = (pl.program_id(0) == 0) & (pl.program_id(1) == 0)

    @pl.when(first)
    def _():
        run_ref[...] = jnp.zeros_like(run_ref)

    d = functools.partial(jnp.dot, preferred_element_type=F32)
    mix = d(om_ref[0], w1_ref[...]) + d(oc_ref[0], w2_ref[...]) + d(os_ref[0], w3_ref[...])
    mod = mod_ref[0, 0]
    g1, sh2, sc2 = mod[2:3], mod[3:4], mod[4:5]
    xn = x_ref[0] + g1 * mix
    xn_ref[0] = xn
    hf = xn * lax.rsqrt(jnp.mean(xn * xn, axis=-1, keepdims=True) + EPS) * n2_ref[...]
    hf = hf * (1.0 + sc2) + sh2
    hf_ref[0] = hf.astype(BF16)

    hi, lo = _split(hf)
    logits = d(hi, wrh_ref[...]) + d(hi, wrl_ref[...]) + d(lo, wrh_ref[...]) + br_ref[...]
    lane = lax.broadcasted_iota(I32, (TM, LANES), 1)
    l = logits
    ohs, vals = [], []
    for _ in range(TOP_K):
        m = jnp.max(l, axis=-1, keepdims=True)
        idx = jnp.min(jnp.where(l == m, lane, LANES), axis=-1, keepdims=True)
        oh = lane == idx
        ohs.append(oh)
        vals.append(m)
        l = jnp.where(oh, -jnp.inf, l)
    ex = [jnp.exp(v - vals[0]) for v in vals]
    den = ex[0] + ex[1] + ex[2] + ex[3]
    gates = [e / den for e in ex]

    oa = jnp.zeros((TM, LANES), F32)
    for oh in ohs:
        oa = oa + oh.astype(F32)
    hist = jnp.sum(oa, axis=0, keepdims=True)
    slot_rows = jnp.floor((hist + (RUN_CHUNK - 1)) * (1.0 / RUN_CHUNK)) * RUN_CHUNK
    slot_off = d(jnp.broadcast_to(slot_rows, (8, LANES)).astype(BF16), upper_ref[...])
    where_ = d(tri_ref[...], oa.astype(BF16)) + slot_off[0:1, :]
    poss = [jnp.sum(jnp.where(oh, where_, 0.0), axis=-1, keepdims=True).astype(I32) for oh in ohs]

    p_out = jnp.zeros((TM, LANES), I32)
    g_out = jnp.zeros((TM, LANES), F32)
    for k in range(TOP_K):
        p_out = jnp.where(lane == k, poss[k], p_out)
        g_out = jnp.where(lane == k, gates[k], g_out)
    pos_ref[0] = p_out
    gt_ref[0] = g_out

    srow = lax.broadcasted_iota(I32, (8, LANES), 0)
    meta = jnp.where(srow == 0, hist, jnp.where(srow == 1, run_ref[...], jnp.where(srow == 2, slot_off, 0.0)))
    meta_ref[0] = meta.astype(I32)
    run_ref[...] = run_ref[...] + hist
    cnt_ref[...] = run_ref[...]


def _outproj(om, oc, osw, xu, modsel, lw, tri, upper, t_off):
    B, T, D = xu.shape
    nt = T // TM - t_off
    row = lambda b, i: (b, i, 0)
    Tq = nt * TM
    in_specs = [
        pl.BlockSpec((1, TM, 512), row),
        pl.BlockSpec((1, TM, 256), row),
        pl.BlockSpec((1, TM, 512), row),
        pl.BlockSpec((1, TM, D), lambda b, i: (b, i + t_off, 0)),
        pl.BlockSpec((1, 1, 6, D), lambda b, i: (b, jnp.minimum(i + t_off, 1), 0, 0)),
        _full((512, D)),
        _full((256, D)),
        _full((512, D)),
        _full((1, D)),
        _full((D, LANES)),
        _full((D, LANES)),
        _full((1, LANES)),
        _full((TM, TM)),
        _full((LANES, LANES)),
    ]
    out_shape = [
        jax.ShapeDtypeStruct((B, Tq, D), F32),
        jax.ShapeDtypeStruct((B, Tq, D), BF16),
        jax.ShapeDtypeStruct((B, Tq, LANES), I32),
        jax.ShapeDtypeStruct((B, Tq, LANES), F32),
        jax.ShapeDtypeStruct((B * nt, 8, LANES), I32),
        jax.ShapeDtypeStruct((8, LANES), F32),
    ]
    out_specs = [
        pl.BlockSpec((1, TM, D), row),
        pl.BlockSpec((1, TM, D), row),
        pl.BlockSpec((1, TM, LANES), row),
        pl.BlockSpec((1, TM, LANES), row),
        pl.BlockSpec((1, 8, LANES), lambda b, i: (b * nt + i, 0, 0)),
        pl.BlockSpec((8, LANES), lambda b, i: (0, 0)),
    ]
    return pl.pallas_call(
        _out_kernel,
        grid=(B, nt),
        in_specs=in_specs,
        out_specs=out_specs,
        out_shape=out_shape,
        scratch_shapes=[pltpu.VMEM((8, LANES), F32)],
        compiler_params=_cp(("arbitrary", "arbitrary")),
        name="outproj_router",
    )(om, oc, osw, xu, modsel, lw["w_o1"], lw["w_o2"], lw["w_o3"], lw["n2"], lw["wr_hi"], lw["wr_lo"], lw["br"], tri, upper)


SUB = 8
RUN_CHUNK = 16
SLOT_ROWS = TM * TOP_K + N_EXPERTS * RUN_CHUNK


def _rows(ref, row0, nrows):
    start = row0 * SUB if isinstance(row0, int) else pl.multiple_of(row0 * SUB, SUB)
    return ref.at[pl.ds(start, nrows * SUB), :]


def _to_tiles(ref, val, nrows):
    for j in range(SUB):
        ref[pl.ds(j, nrows, stride=SUB), :] = val[:, LANES * j:LANES * (j + 1)]


def _from_tiles(ref, nrows):
    return jnp.concatenate([ref[pl.ds(j, nrows, stride=SUB), :] for j in range(SUB)], axis=1)


def _slot_matrix(pos, weights):
    col = lax.broadcasted_iota(I32, (TM, SLOT_ROWS), 1)
    m = jnp.zeros((TM, SLOT_ROWS), F32)
    for k in range(TOP_K):
        m = jnp.where(col == pos[:, k:k + 1], weights[k], m)
    return m.astype(BF16)


def _run_copies(meta_ref, pst_ref, buf_ref, hbm_ref, sem, to_hbm):
    total = jnp.int32(0)
    for e in range(N_EXPERTS):
        n = meta_ref[0, 0, e]
        nch = (n + (RUN_CHUNK - 1)) // RUN_CHUNK
        seg0 = pst_ref[e] + meta_ref[0, 1, e]
        slot0 = meta_ref[0, 2, e]

        def chunk(c, carry):
            a = _rows(buf_ref, slot0 + c * RUN_CHUNK, RUN_CHUNK)
            b = _rows(hbm_ref, seg0 + c * RUN_CHUNK, RUN_CHUNK)
            (pltpu.make_async_copy(a, b, sem) if to_hbm else pltpu.make_async_copy(b, a, sem)).start()
            return carry

        lax.fori_loop(0, nch, chunk, 0)
        total = total + nch
    return total


def _run_waits(total, buf_ref, hbm_ref, sem, to_hbm):
    a = _rows(buf_ref, 0, RUN_CHUNK)
    b = _rows(hbm_ref, 0, RUN_CHUNK)

    def one(c, carry):
        (pltpu.make_async_copy(a, b, sem) if to_hbm else pltpu.make_async_copy(b, a, sem)).wait()
        return carry

    lax.fori_loop(0, total, one, 0)


def _run_total(meta_ref):
    total = jnp.int32(0)
    for e in range(N_EXPERTS):
        total = total + (meta_ref[0, 0, e] + (RUN_CHUNK - 1)) // RUN_CHUNK
    return total


def _disp_kernel(pst_ref, cnt_ref, pad_ref, na_ref, meta_ref, hf_ref, pos_ref, xb_ref, srt_ref, zero_ref, tot_ref,
                 sem, zsem, *, n_blocks):
    step = pl.program_id(0) * pl.num_programs(1) + pl.program_id(1)
    n_steps = pl.num_programs(0) * pl.num_programs(1)
    slot = step % 2
    first = step == 0

    @pl.when(first)
    def _():
        zero_ref[...] = jnp.zeros_like(zero_ref)

        def zblock(j):
            return pltpu.make_async_copy(zero_ref, _rows(xb_ref, j * EXPERT_BLOCK, EXPERT_BLOCK), zsem)

        def zb_start(j, carry):
            zblock(j).start()
            return carry

        def zb_wait(j, carry):
            zblock(j).wait()
            return carry

        lax.fori_loop(na_ref[0], n_blocks, zb_start, 0)
        lax.fori_loop(na_ref[0], n_blocks, zb_wait, 0)

        for e in range(N_EXPERTS):
            lo = pst_ref[e] + cnt_ref[e]
            nrow = pad_ref[e] - cnt_ref[e]
            nz = nrow // RUN_CHUNK
            lo1 = lo + nz * RUN_CHUNK
            n1 = nrow - nz * RUN_CHUNK

            def zchunk(c):
                return pltpu.make_async_copy(_rows(zero_ref, 0, RUN_CHUNK), _rows(xb_ref, lo + c * RUN_CHUNK, RUN_CHUNK), zsem)

            def zrow(r):
                return pltpu.make_async_copy(_rows(zero_ref, 0, 1), _rows(xb_ref, lo1 + r, 1), zsem)

            def zc_start(c, carry):
                zchunk(c).start()
                return carry

            def zc_wait(c, carry):
                zchunk(c).wait()
                return carry

            def zr_start(r, carry):
                zrow(r).start()
                return carry

            def zr_wait(r, carry):
                zrow(r).wait()
                return carry

            lax.fori_loop(0, nz, zc_start, 0)
            lax.fori_loop(0, n1, zr_start, 0)
            lax.fori_loop(0, nz, zc_wait, 0)
            lax.fori_loop(0, n1, zr_wait, 0)

    ones = [1.0] * TOP_K
    q = _slot_matrix(pos_ref[0], ones)
    srt = lax.dot_general(q, hf_ref[0], (((0,), (0,)), ((), ())), preferred_element_type=F32)
    _to_tiles(srt_ref.at[slot], srt, SLOT_ROWS)

    @pl.when(step > 0)
    def _():
        _run_waits(tot_ref[1 - slot], srt_ref.at[1 - slot], xb_ref, sem.at[1 - slot], True)

    total = _run_copies(meta_ref, pst_ref, srt_ref.at[slot], xb_ref, sem.at[slot], True)
    tot_ref[slot] = total

    @pl.when(step == n_steps - 1)
    def _():
        _run_waits(total, srt_ref.at[slot], xb_ref, sem.at[slot], True)


def _dispatch(pstart, counts, padded, n_act, meta, hf, pos, n_buf):
    B, Tq, D = hf.shape
    nt = Tq // TM
    assert D == SUB * LANES
    grid_spec = pltpu.PrefetchScalarGridSpec(
        num_scalar_prefetch=4,
        grid=(B, nt),
        in_specs=[
            pl.BlockSpec((1, 8, LANES), lambda b, i, *_: (b * nt + i, 0, 0), memory_space=pltpu.SMEM),
            pl.BlockSpec((1, TM, D), lambda b, i, *_: (b, i, 0)),
            pl.BlockSpec((1, TM, LANES), lambda b, i, *_: (b, i, 0)),
        ],
        out_specs=pl.BlockSpec(memory_space=pl.ANY),
        scratch_shapes=[pltpu.VMEM((2, SLOT_ROWS * SUB, LANES), F32), pltpu.VMEM((EXPERT_BLOCK * SUB, LANES), F32),
                        pltpu.SMEM((2,), I32), pltpu.SemaphoreType.DMA((2,)), pltpu.SemaphoreType.DMA],
    )
    return pl.pallas_call(
        functools.partial(_disp_kernel, n_blocks=n_buf // EXPERT_BLOCK),
        grid_spec=grid_spec,
        out_shape=jax.ShapeDtypeStruct((n_buf * SUB, LANES), F32),
        compiler_params=_cp(("arbitrary", "arbitrary")),
        name="dispatch",
    )(pstart, counts, padded, n_act, meta, hf, pos)


W_ROWS = 256


def _exp_kernel(be_ref, bx_ref, na_ref, nx_ref, ps_ref, pr_ref, npv_ref, x_ref, b1a_ref, b1b_ref, b2_ref, w1_hbm, w2_hbm,
                o_ref, st1, st2, w1a_s, w1b_s, w2_s, sem, *, e_off):
    j = pl.program_id(0)
    active = j < na_ref[0]
    p = pr_ref[j]
    k = ps_ref[j]
    nxt = nx_ref[j]
    n_slices = st1.shape[0] // W_ROWS

    def copies(e):
        return (pltpu.make_async_copy(w1_hbm.at[e], st1, sem.at[0]), pltpu.make_async_copy(w2_hbm.at[e], st2, sem.at[1]))

    def fetch(e):
        for cp in copies(e):
            cp.start()

    def fetch_wait():
        for cp in copies(0):
            cp.wait()

    def convert(slot, s):
        r0 = pl.multiple_of(s * W_ROWS, W_ROWS)
        lane = lax.broadcasted_iota(I32, (W_ROWS, LANES), 1)
        lo = lane < 64
        idx = jnp.where(lo, 2 * lane, 2 * (lane - 64) + 1)
        for c in range(st1.shape[1] // (2 * LANES)):
            a = st1[pl.ds(r0, W_ROWS), 2 * LANES * c:2 * LANES * c + LANES]
            b = st1[pl.ds(r0, W_ROWS), 2 * LANES * c + LANES:2 * LANES * (c + 1)]
            pa = jnp.take_along_axis(a, idx, axis=1)
            pb = jnp.take_along_axis(b, idx, axis=1)
            ev = jnp.where(lo, pa, pltpu.roll(pb, 64, axis=1))
            od = jnp.where(lo, pltpu.roll(pa, 64, axis=1), pb)
            w1a_s[slot, pl.ds(r0, W_ROWS), LANES * c:LANES * (c + 1)] = ev.astype(BF16)
            w1b_s[slot, pl.ds(r0, W_ROWS), LANES * c:LANES * (c + 1)] = od.astype(BF16)
        w2_s[slot, pl.ds(r0, W_ROWS), :] = st2[pl.ds(r0, W_ROWS), :].astype(BF16)

    def convert_range(slot, s0):
        def one(s, carry):
            convert(slot, s)
            return carry

        lax.fori_loop(s0, n_slices, one, 0)

    @pl.when(j == 0)
    def _():
        fetch(be_ref[0] + e_off)
        fetch_wait()
        convert_range(0, 0)

    @pl.when(active & (k == 0) & (j > 0))
    def _():
        @pl.when(npv_ref[j] < 2)
        def _():
            fetch_wait()

        convert_range(p, jnp.clip(npv_ref[j] - 1, 0, n_slices))

    @pl.when(active & (k == 0) & (nxt >= 0))
    def _():
        fetch(nxt + e_off)

    conv = active & (nxt >= 0) & (k >= 1) & (k <= n_slices)

    @pl.when(conv & (k == 1))
    def _():
        fetch_wait()

    def ffn(with_convert):
        if with_convert:
            convert(1 - p, k - 1)
        x = _from_tiles(x_ref, EXPERT_BLOCK).astype(BF16)
        ug = jnp.dot(x, w1a_s[p], preferred_element_type=F32) + b1a_ref[0]
        ul = jnp.dot(x, w1b_s[p], preferred_element_type=F32) + b1b_ref[0]
        xg = jnp.minimum(ug, SWIGLU_LIMIT)
        xl = jnp.clip(ul, -SWIGLU_LIMIT, SWIGLU_LIMIT)
        act = xg * jax.nn.sigmoid(SWIGLU_ALPHA * xg) * (xl + 1.0)
        y = jnp.dot(act.astype(BF16), w2_s[p], preferred_element_type=F32) + b2_ref[0]
        _to_tiles(o_ref, y, EXPERT_BLOCK)

    pl.when(conv)(functools.partial(ffn, True))
    pl.when(active & jnp.logical_not(conv))(functools.partial(ffn, False))

    @pl.when(jnp.logical_not(active))
    def _():
        o_ref[...] = jnp.zeros_like(o_ref)


def _experts(sched, xb, lw):
    nb = xb.shape[0] // (EXPERT_BLOCK * SUB)
    De, D = lw["w2"].shape[1:]
    assert De == D and D % W_ROWS == 0
    wmap = lambda j, be, *_: (be[j], 0, 0)
    grid_spec = pltpu.PrefetchScalarGridSpec(
        num_scalar_prefetch=len(sched),
        grid=(nb,),
        in_specs=[
            pl.BlockSpec((EXPERT_BLOCK * SUB, LANES), lambda j, be, bx, *_: (bx[j], 0)),
            pl.BlockSpec((1, 1, De), wmap),
            pl.BlockSpec((1, 1, De), wmap),
            pl.BlockSpec((1, 1, D), wmap),
            pl.BlockSpec(memory_space=pl.ANY),
            pl.BlockSpec(memory_space=pl.ANY),
        ],
        out_specs=pl.BlockSpec((EXPERT_BLOCK * SUB, LANES), lambda j, *_: (j, 0)),
        scratch_shapes=[pltpu.VMEM((D, 2 * De), F32), pltpu.VMEM((De, D), F32),
                        pltpu.VMEM((2, D, De), BF16), pltpu.VMEM((2, D, De), BF16), pltpu.VMEM((2, De, D), BF16),
                        pltpu.SemaphoreType.DMA((2,))],
    )
    return pl.pallas_call(
        functools.partial(_exp_kernel, e_off=lw["e_off"]),
        grid_spec=grid_spec,
        out_shape=jax.ShapeDtypeStruct(xb.shape, F32),
        compiler_params=pltpu.CompilerParams(dimension_semantics=("arbitrary",), vmem_limit_bytes=EXPERT_VMEM_LIMIT),
        name="experts",
    )(*sched, xb, lw["b1a"], lw["b1b"], lw["b2"], lw["w1"], lw["w2"])


def _comb_kernel(pst_ref, meta_ref, meta_next_ref, pos_ref, gt_ref, xn_ref, mod_ref, yb_ref, o_ref, buf_ref, sem):
    step = pl.program_id(0) * pl.num_programs(1) + pl.program_id(1)
    n_steps = pl.num_programs(0) * pl.num_programs(1)
    slot = step % 2

    @pl.when(step == 0)
    def _():
        buf_ref[...] = jnp.zeros_like(buf_ref)
        _run_copies(meta_ref, pst_ref, buf_ref.at[0], yb_ref, sem.at[0], False)

    @pl.when(step < n_steps - 1)
    def _():
        _run_copies(meta_next_ref, pst_ref, buf_ref.at[1 - slot], yb_ref, sem.at[1 - slot], False)

    gt = gt_ref[0]
    g = _slot_matrix(pos_ref[0], [gt[:, k:k + 1] for k in range(TOP_K)])
    _run_waits(_run_total(meta_ref), buf_ref.at[slot], yb_ref, sem.at[slot], False)
    rows = _from_tiles(buf_ref.at[slot], SLOT_ROWS).astype(BF16)
    y = jnp.dot(g, rows, preferred_element_type=F32)
    g2 = mod_ref[0, 0][5:6]
    o_ref[0] = xn_ref[0] + g2 * y


def _combine(pstart, meta, pos, gates, xn, modsel, yb, t_off):
    B, Tq, D = xn.shape
    nt = Tq // TM
    grid_spec = pltpu.PrefetchScalarGridSpec(
        num_scalar_prefetch=1,
        grid=(B, nt),
        in_specs=[
            pl.BlockSpec((1, 8, LANES), lambda b, i, *_: (b * nt + i, 0, 0), memory_space=pltpu.SMEM),
            pl.BlockSpec((1, 8, LANES), lambda b, i, *_: (jnp.minimum(b * nt + i + 1, B * nt - 1), 0, 0),
                         memory_space=pltpu.SMEM),
            pl.BlockSpec((1, TM, LANES), lambda b, i, *_: (b, i, 0)),
            pl.BlockSpec((1, TM, LANES), lambda b, i, *_: (b, i, 0)),
            pl.BlockSpec((1, TM, D), lambda b, i, *_: (b, i, 0)),
            pl.BlockSpec((1, 1, 6, D), lambda b, i, *_: (b, jnp.minimum(i + t_off, 1), 0, 0)),
            pl.BlockSpec(memory_space=pl.ANY),
        ],
        out_specs=pl.BlockSpec((1, TM, D), lambda b, i, *_: (b, i, 0)),
        scratch_shapes=[pltpu.VMEM((2, SLOT_ROWS * SUB, LANES), F32), pltpu.SemaphoreType.DMA((2,))],
    )
    return pl.pallas_call(
        _comb_kernel,
        grid_spec=grid_spec,
        out_shape=jax.ShapeDtypeStruct((B, Tq, D), F32),
        compiler_params=_cp(("arbitrary", "arbitrary")),
        name="combine",
    )(pstart, meta, meta, pos, gates, xn, modsel, yb)


def _take_cols(w, cols):
    cols = np.asarray(cols)
    out = jnp.take(w, jnp.asarray(np.maximum(cols, 0)), axis=-1)
    return jnp.where(jnp.asarray(cols >= 0), out, 0.0)


def _in_cols():
    pi = _PI
    cols = list(range(0, 256))
    cols += [256 + i for i in range(64)] + [256 + pi[i] for i in range(64)]
    for g in range(SWA_KV_HEADS):
        base = 320 + 64 * g
        cols += [base + i for i in range(64)] + [base + pi[i] for i in range(64)]
    for g in range(SWA_KV_HEADS):
        base = 448 + 64 * g
        cols += [base + i for i in range(64)] + [-1] * 64
    cols += list(range(576, 960))
    for h in range(SWA_HEADS):
        base = 960 + 64 * h
        cols += [base + i for i in range(64)] + [base + pi[i] for i in range(64)]
    cols += list(range(1216, 1728))
    return cols


def _layer_weights(l, a):
    pi = _PI
    lw = {}
    lw["n1"] = a["norm1_g"][l][None, :]
    lw["n2"] = a["norm2_g"][l][None, :]
    lw["w_in"] = _take_cols(a["w_in"][l], _in_cols()).astype(BF16)
    lw["kvg"] = a["mla_kv_norm"][l][None, :]
    lw["qg"] = a["mla_q_norm"][l][None, :]
    uq_cols = []
    for h in range(MLA_HEADS):
        base = MLA_QK * h
        uq_cols += [base + i for i in range(128)] + [base + 128 + i for i in range(64)] + [base + 128 + pi[i] for i in range(64)]
    lw["w_uq"] = _take_cols(a["mla_w_uq"][l], uq_cols).astype(BF16)
    ukv_cols = [256 * h + i for h in range(MLA_HEADS) for i in range(128)]
    ukv_cols += [256 * h + 128 + i for h in range(MLA_HEADS) for i in range(128)]
    lw["w_ukv"] = _take_cols(a["mla_w_ukv"][l], ukv_cols).astype(BF16)
    gq = a["mla_q_head_norm"][l]
    lw["gq"] = (jnp.concatenate([gq[:128], gq[128:], gq[128:][pi]]) * (MLA_QK ** -0.5 * LOG2E))[None, :]
    gk = a["mla_k_head_norm"][l]
    lw["gkn"] = gk[:128][None, :]
    lw["gkp"] = jnp.concatenate([gk[128:], gk[128:][pi]])[None, :]
    sq = a["swa_q_norm"][l]
    lw["gsq"] = (jnp.concatenate([sq, sq[pi]]) * (SWA_HEAD_DIM ** -0.5))[None, :]
    sk = a["swa_k_norm"][l]
    lw["gsk"] = jnp.concatenate([sk, sk[pi]])[None, :]
    lw["conv_w"] = a["conv_w"][l]
    lw["conv_b"] = a["conv_b"][l][None, :]
    lw["conv_g"] = a["conv_ln_g"][l][None, :]
    lw["conv_bb"] = a["conv_ln_b"][l][None, :]
    lw["sink"] = a["swa_sink"][l]
    wo = a["w_out"][l]
    lw["w_o1"] = wo[0:512].astype(BF16)
    lw["w_o2"] = wo[512:768].astype(BF16)
    o3 = wo[768:1024].reshape(SWA_HEADS, SWA_HEAD_DIM, -1)
    lw["w_o3"] = jnp.concatenate([o3, jnp.zeros_like(o3)], axis=1).reshape(SWA_HEADS * 128, -1).astype(BF16)
    wr = jnp.pad(a["router_w"][l], ((0, 0), (0, LANES - N_EXPERTS)))
    lw["wr_hi"] = wr.astype(BF16)
    lw["wr_lo"] = (wr - lw["wr_hi"].astype(F32)).astype(BF16)
    lw["br"] = jnp.pad(a["router_b"][l], (0, LANES - N_EXPERTS), constant_values=NEG)[None, :]
    lw["w1"] = a["exp_w1"].reshape((-1,) + a["exp_w1"].shape[2:])
    lw["e_off"] = l * a["exp_w1"].shape[1]
    b1 = a["exp_b1"][l]
    lw["b1a"] = b1[:, None, 0::2]
    lw["b1b"] = b1[:, None, 1::2]
    lw["w2"] = a["exp_w2"].reshape((-1,) + a["exp_w2"].shape[2:])
    lw["b2"] = a["exp_b2"][l][:, None, :]
    return lw


def _rope_tables(n_ctx, n_lat):
    q = MLA_ROPE // 4
    n = jnp.arange(n_lat, dtype=I32)
    row = (n // GRID_W).astype(F32)
    col = (n % GRID_W).astype(F32)
    inv = ROPE_BASE ** (-jnp.arange(q, dtype=F32) / q)
    ang_r = row[:, None] * inv
    ang_c = col[:, None] * inv
    cos = jnp.concatenate([jnp.cos(ang_r), jnp.cos(ang_r), jnp.cos(ang_c), jnp.cos(ang_c)], axis=1)
    sin = jnp.concatenate([-jnp.sin(ang_r), jnp.sin(ang_r), -jnp.sin(ang_c), jnp.sin(ang_c)], axis=1)
    cos = jnp.concatenate([jnp.ones((n_ctx, 64), F32), cos], axis=0)
    sin = jnp.concatenate([jnp.zeros((n_ctx, 64), F32), sin], axis=0)
    z = jnp.zeros_like(cos)
    return jnp.concatenate([cos, z], axis=1), jnp.concatenate([sin, z], axis=1)


def _routing_tables(cnt_f, n_blocks):
    counts = cnt_f[0, :N_EXPERTS].astype(I32)
    padded = (counts + (RUN_CHUNK - 1) + EXPERT_BLOCK - 1) // EXPERT_BLOCK * EXPERT_BLOCK
    padded = jnp.where(counts > 0, padded, 0)
    pend = jnp.cumsum(padded)
    pstart = pend - padded
    n_act = pend[-1] // EXPERT_BLOCK
    blk = jnp.minimum(jnp.arange(n_blocks, dtype=I32), n_act - 1)
    blk_e = jnp.sum((pend[None, :] <= (blk * EXPERT_BLOCK)[:, None]).astype(I32), axis=1)
    blk_e = jnp.minimum(blk_e, N_EXPERTS - 1)
    nbk = padded // EXPERT_BLOCK
    has = nbk > 0
    ids = jnp.arange(N_EXPERTS, dtype=I32)
    later = has[None, :] & (ids[None, :] > ids[:, None])
    nxt_e = jnp.min(jnp.where(later, ids[None, :], N_EXPERTS), axis=1)
    nxt_e = jnp.where(nxt_e == N_EXPERTS, -1, nxt_e)
    earlier = has[None, :] & (ids[None, :] < ids[:, None])
    prv_e = jnp.max(jnp.where(earlier, ids[None, :], -1), axis=1)
    nb_prev_e = jnp.where(prv_e >= 0, nbk[jnp.maximum(prv_e, 0)], 0)
    set_e = (jnp.cumsum(has.astype(I32)) - 1) % 2
    pos_b = blk - (pstart // EXPERT_BLOCK)[blk_e]
    sched = (blk_e, blk, n_act.reshape(1), nxt_e[blk_e], pos_b, set_e[blk_e], nb_prev_e[blk_e])
    sched = tuple(s.astype(I32) for s in sched)
    return counts, padded.astype(I32), pstart.astype(I32), sched


def kernel(x, c, ctx, c_ctx, norm1_g, norm2_g, w_ada, b_ada, w_in, mla_q_norm, mla_kv_norm, mla_w_uq, mla_w_ukv, mla_q_head_norm, mla_k_head_norm, conv_w, conv_b, conv_ln_g, conv_ln_b, swa_q_norm, swa_k_norm, swa_sink, w_out, router_w, router_b, exp_w1, exp_b1, exp_w2, exp_b2):
    a = dict(norm1_g=norm1_g, norm2_g=norm2_g, w_in=w_in, mla_q_norm=mla_q_norm, mla_kv_norm=mla_kv_norm,
             mla_w_uq=mla_w_uq, mla_w_ukv=mla_w_ukv, mla_q_head_norm=mla_q_head_norm, mla_k_head_norm=mla_k_head_norm,
             conv_w=conv_w, conv_b=conv_b, conv_ln_g=conv_ln_g, conv_ln_b=conv_ln_b, swa_q_norm=swa_q_norm,
             swa_k_norm=swa_k_norm, swa_sink=swa_sink, w_out=w_out, router_w=router_w, router_b=router_b,
             exp_w1=exp_w1, exp_b1=exp_b1, exp_w2=exp_w2, exp_b2=exp_b2)
    B, S, D = x.shape
    n_ctx = ctx.shape[1]
    depth = w_ada.shape[0]
    assert n_ctx == TM and S % TM == 0 and B + 1 <= 16
    T = n_ctx + S

    s_in = jnp.zeros((16, D), F32).at[:B].set(c).at[B].set(c_ctx)
    mods = _ada(s_in, w_ada, b_ada)
    cos_t, sin_t = _rope_tables(n_ctx, S)
    tri = jnp.tril(jnp.ones((TM, TM), F32), -1).astype(BF16)
    upper = jnp.triu(jnp.ones((LANES, LANES), F32), 1).astype(BF16)

    xu = jnp.concatenate([ctx, x], axis=1)
    for l in range(depth):
        last = l == depth - 1
        t_off = 1 if last else 0
        lw = _layer_weights(l, a)
        m = mods[l].reshape(16, 6, D)
        modsel = jnp.stack([jnp.broadcast_to(m[B], (B, 6, D)), m[:B]], axis=1)

        qm, km, vm, qs, ks, vs, u = _prep(xu, modsel, lw, cos_t, sin_t, 0)
        om = _mla(qm, km, vm, t_off)
        oc = _conv(u, lw, t_off)
        osw = _swa(lw["sink"], qs, ks, vs, t_off)
        xn, hf, pos, gt_o, meta, cnt = _outproj(om, oc, osw, xu, modsel, lw, tri, upper, t_off)

        n_tok = B * (T - t_off * TM)
        nk = n_tok * TOP_K
        n_buf = -(-(nk + N_EXPERTS * (RUN_CHUNK - 1 + EXPERT_BLOCK - 1)) // EXPERT_BLOCK) * EXPERT_BLOCK
        counts, padded, pstart, sched = _routing_tables(cnt, n_buf // EXPERT_BLOCK)
        xb = _dispatch(pstart, counts, padded, sched[2], meta, hf, pos, n_buf)
        yb = _experts(sched, xb, lw)
        xu = _combine(pstart, meta, pos, gt_o, xn, modsel, yb, t_off)
    return xu
```

```python
import functools

import numpy as np
import jax
import jax.numpy as jnp
from jax import lax
from jax.experimental import pallas as pl
from jax.experimental.pallas import tpu as pltpu

F32 = jnp.float32
BF16 = jnp.bfloat16
I32 = jnp.int32

GRID_W = 64
ROPE_BASE = 10000.0
EPS = 1e-6
MLA_HEADS = 4
MLA_NOPE = 128
MLA_ROPE = 64
MLA_V = 128
MLA_QK = MLA_NOPE + MLA_ROPE
MLA_Q_RANK = 384
MLA_KV_RANK = 256
CONV_CH = 256
CONV_WIDTH = 31
SWA_HEADS = 4
SWA_KV_HEADS = 2
SWA_HEAD_DIM = 64
SWA_WINDOW = 128
N_EXPERTS = 32
TOP_K = 4
SWIGLU_LIMIT = 7.0
SWIGLU_ALPHA = 1.702
EXPERT_BLOCK = 256

LANES = 128
TM = 256
CONV_HALO = 16
VMEM_LIMIT = 48 * 1024 * 1024
EXPERT_VMEM_LIMIT = 56 * 1024 * 1024
NEG = -1e30
LOG2E = 1.4426950408889634

_PI = np.array([i + 16 if (i % 32) < 16 else i - 16 for i in range(64)])


def _cp(sem):
    return pltpu.CompilerParams(dimension_semantics=sem, vmem_limit_bytes=VMEM_LIMIT)


def _full(shape):
    n = len(shape)
    return pl.BlockSpec(shape, lambda *a, _n=n: (0,) * _n)


def _split(x):
    hi = x.astype(BF16)
    lo = (x - hi.astype(F32)).astype(BF16)
    return hi, lo


def _dot3(a, b):
    ah, al = _split(a)
    bh, bl = _split(b)
    d = functools.partial(jnp.dot, preferred_element_type=F32)
    return d(ah, bh) + d(ah, bl) + d(al, bh)


def _ada_kernel(s_ref, w_ref, b_ref, o_ref):
    s = s_ref[...]
    s = s * jax.nn.sigmoid(s)
    o_ref[0] = _dot3(s, w_ref[0]) + b_ref[0]


def _ada(s_in, w_ada, b_ada):
    L, D, N = w_ada.shape
    tn = 1536
    return pl.pallas_call(
        _ada_kernel,
        grid=(L, N // tn),
        in_specs=[
            pl.BlockSpec((16, D), lambda l, j: (0, 0)),
            pl.BlockSpec((1, D, tn), lambda l, j: (l, 0, j)),
            pl.BlockSpec((1, 1, tn), lambda l, j: (l, 0, j)),
        ],
        out_specs=pl.BlockSpec((1, 16, tn), lambda l, j: (l, 0, j)),
        out_shape=jax.ShapeDtypeStruct((L, 16, N), F32),
        compiler_params=_cp(("arbitrary", "arbitrary")),
        name="ada",
    )(s_in, w_ada, b_ada.reshape(L, 1, N))


def _rope(x, c, s):
    return x * c + pltpu.roll(x, 64, axis=1) * s


def _prep_kernel(x_ref, mod_ref, n1_ref, win_ref, kvg_ref, wukv_ref, qg_ref, wuq_ref, gq_ref, gkn_ref, gkp_ref,
                 gsq_ref, gsk_ref, cos_ref, sin_ref, qm_ref, km_ref, vm_ref, qs_ref, ks_ref, vs_ref, u_ref):
    x = x_ref[0]
    mod = mod_ref[0, 0]
    sh, sc = mod[0:1], mod[1:2]
    y = x * lax.rsqrt(jnp.mean(x * x, axis=-1, keepdims=True) + EPS) * n1_ref[...]
    h = y * (1.0 + sc) + sh
    p = jnp.dot(h.astype(BF16), win_ref[...], preferred_element_type=F32)
    c = cos_ref[...]
    s = sin_ref[...]
    lane = lax.broadcasted_iota(I32, (TM, LANES), 1)
    lo = lane < 64

    def ss_lo(v):
        return jnp.sum(jnp.where(lo, v * v, 0.0), axis=-1, keepdims=True)

    ckv = p[:, 0:256]
    ckvn = ckv * lax.rsqrt(jnp.mean(ckv * ckv, axis=-1, keepdims=True) + EPS) * kvg_ref[...]
    kv = jnp.dot(ckvn.astype(BF16), wukv_ref[...], preferred_element_type=F32)
    kpe = p[:, 256:384]
    ss_pe = ss_lo(kpe)
    kpe_rot = _rope(kpe * gkp_ref[...], c, s)
    for hh in range(MLA_HEADS):
        kn = kv[:, 128 * hh:128 * hh + 128]
        r = lax.rsqrt((jnp.sum(kn * kn, axis=-1, keepdims=True) + ss_pe) * (1.0 / MLA_QK) + EPS)
        km_ref[0, hh, :, 0:128] = (kn * r * gkn_ref[...]).astype(BF16)
        km_ref[0, hh, :, 128:256] = (kpe_rot * r).astype(BF16)
        vm_ref[0, hh, :, 0:128] = kv[:, 512 + 128 * hh:640 + 128 * hh].astype(BF16)
        vm_ref[0, hh, :, 128:256] = (lane == 0).astype(BF16)

    cq = p[:, 896:1280]
    cqn = cq * lax.rsqrt(jnp.mean(cq * cq, axis=-1, keepdims=True) + EPS) * qg_ref[...]
    q = jnp.dot(cqn.astype(BF16), wuq_ref[...], preferred_element_type=F32)
    gq = gq_ref[...]
    for hh in range(MLA_HEADS):
        qn = q[:, 256 * hh:256 * hh + 128]
        qp = q[:, 256 * hh + 128:256 * hh + 256]
        r = lax.rsqrt((jnp.sum(qn * qn, axis=-1, keepdims=True) + ss_lo(qp)) * (1.0 / MLA_QK) + EPS)
        qm_ref[0, hh, :, 0:128] = (qn * r * gq[:, 0:128]).astype(BF16)
        qm_ref[0, hh, :, 128:256] = _rope(qp * r * gq[:, 128:256], c, s).astype(BF16)

    for g in range(SWA_KV_HEADS):
        xk = p[:, 384 + 128 * g:512 + 128 * g]
        r = lax.rsqrt(ss_lo(xk) * (1.0 / SWA_HEAD_DIM) + EPS)
        ks_ref[0, g] = _rope(xk * r * gsk_ref[...], c, s).astype(BF16)
        vs_ref[0, g] = p[:, 640 + 128 * g:768 + 128 * g].astype(BF16)
    for hh in range(SWA_HEADS):
        xq = p[:, 1280 + 128 * hh:1408 + 128 * hh]
        r = lax.rsqrt(ss_lo(xq) * (1.0 / SWA_HEAD_DIM) + EPS)
        qs_ref[0, hh] = _rope(xq * r * gsq_ref[...], c, s).astype(BF16)

    u_ref[0] = p[:, 1792:2048] * jax.nn.sigmoid(p[:, 2048:2304])


def _prep(xu, modsel, lw, cos_t, sin_t, t_off):
    B, T, D = xu.shape
    nt = T // TM - t_off
    ncol = lw["w_in"].shape[1]
    row = lambda b, i: (b, i + t_off, 0)
    head = lambda b, i: (b, 0, i + t_off, 0)
    in_specs = [
        pl.BlockSpec((1, TM, D), row),
        pl.BlockSpec((1, 1, 6, D), lambda b, i: (b, jnp.minimum(i + t_off, 1), 0, 0)),
        _full((1, D)),
        _full((D, ncol)),
        _full((1, MLA_KV_RANK)),
        _full((MLA_KV_RANK, 1024)),
        _full((1, MLA_Q_RANK)),
        _full((MLA_Q_RANK, 1024)),
        _full((1, 256)),
        _full((1, 128)),
        _full((1, 128)),
        _full((1, 128)),
        _full((1, 128)),
        pl.BlockSpec((TM, LANES), lambda b, i: (i + t_off, 0)),
        pl.BlockSpec((TM, LANES), lambda b, i: (i + t_off, 0)),
    ]
    out_shape = [
        jax.ShapeDtypeStruct((B, MLA_HEADS, T, 256), BF16),
        jax.ShapeDtypeStruct((B, MLA_HEADS, T, 256), BF16),
        jax.ShapeDtypeStruct((B, MLA_HEADS, T, 256), BF16),
        jax.ShapeDtypeStruct((B, SWA_HEADS, T, 128), BF16),
        jax.ShapeDtypeStruct((B, SWA_KV_HEADS, T, 128), BF16),
        jax.ShapeDtypeStruct((B, SWA_KV_HEADS, T, 128), BF16),
        jax.ShapeDtypeStruct((B, T, CONV_CH), F32),
    ]
    out_specs = [
        pl.BlockSpec((1, MLA_HEADS, TM, 256), head),
        pl.BlockSpec((1, MLA_HEADS, TM, 256), head),
        pl.BlockSpec((1, MLA_HEADS, TM, 256), head),
        pl.BlockSpec((1, SWA_HEADS, TM, 128), head),
        pl.BlockSpec((1, SWA_KV_HEADS, TM, 128), head),
        pl.BlockSpec((1, SWA_KV_HEADS, TM, 128), head),
        pl.BlockSpec((1, TM, CONV_CH), row),
    ]
    return pl.pallas_call(
        _prep_kernel,
        grid=(B, nt),
        in_specs=in_specs,
        out_specs=out_specs,
        out_shape=out_shape,
        compiler_params=_cp(("arbitrary", "arbitrary")),
        name="prep",
    )(xu, modsel, lw["n1"], lw["w_in"], lw["kvg"], lw["w_ukv"], lw["qg"], lw["w_uq"], lw["gq"], lw["gkn"], lw["gkp"],
      lw["gsq"], lw["gsk"], cos_t, sin_t)


def _mla_kernel(q_ref, k_ref, v_ref, o_ref, *, q_off, n_keys):
    qi = pl.program_id(1) + q_off

    def attend(nk):
        for h in range(MLA_HEADS):
            q = q_ref[0, h]
            k = k_ref[0, h, 0:nk, :]
            v = v_ref[0, h, 0:nk, :]
            s = lax.dot_general(q, k, (((1,), (1,)), ((), ())), preferred_element_type=F32)
            m = jnp.max(s, axis=-1, keepdims=True)
            p = jnp.exp2(s - m)
            ol = jnp.dot(p.astype(BF16), v, preferred_element_type=F32)
            o = ol[:, 0:MLA_V] / ol[:, MLA_V:MLA_V + 1]
            o_ref[0, :, 128 * h:128 * h + 128] = o.astype(BF16)

    if q_off == 0:
        @pl.when(qi == 0)
        def _():
            attend(TM)

        @pl.when(qi > 0)
        def _():
            attend(n_keys)
    else:
        attend(n_keys)


def _mla(qm, km, vm, t_off):
    B, H, T, _ = qm.shape
    nt = T // TM - t_off
    return pl.pallas_call(
        functools.partial(_mla_kernel, q_off=t_off, n_keys=T),
        grid=(B, nt),
        in_specs=[
            pl.BlockSpec((1, H, TM, 256), lambda b, i: (b, 0, i + t_off, 0)),
            pl.BlockSpec((1, H, T, 256), lambda b, i: (b, 0, 0, 0)),
            pl.BlockSpec((1, H, T, 256), lambda b, i: (b, 0, 0, 0)),
        ],
        out_specs=pl.BlockSpec((1, TM, H * MLA_V), lambda b, i: (b, i, 0)),
        out_shape=jax.ShapeDtypeStruct((B, nt * TM, H * MLA_V), BF16),
        compiler_params=_cp(("arbitrary", "arbitrary")),
        name="mla_attn",
    )(qm, km, vm)


SWA_SPAN = TM + 2 * SWA_WINDOW


def _swa_kernel(sink_ref, q_ref, k_ref, v_ref, o_ref, *, q_off, n_rows):
    qi = pl.program_id(1) + q_off
    nt_dims = (((1,), (1,)), ((), ()))

    def latent():
        start = jnp.clip(qi * TM - SWA_WINDOW, 0, n_rows - SWA_SPAN)
        start = pl.multiple_of(start, SWA_WINDOW)
        qpos = qi * TM + lax.broadcasted_iota(I32, (TM, SWA_SPAN), 0)
        kpos = start + lax.broadcasted_iota(I32, (TM, SWA_SPAN), 1)
        valid = (jnp.abs(qpos - kpos) <= SWA_WINDOW) & (kpos >= TM)
        for h in range(SWA_HEADS):
            g = h // (SWA_HEADS // SWA_KV_HEADS)
            q = q_ref[0, h]
            kl = k_ref[0, g, pl.ds(start, SWA_SPAN), :]
            vl = v_ref[0, g, pl.ds(start, SWA_SPAN), :]
            kc = k_ref[0, g, 0:TM, :]
            vc = v_ref[0, g, 0:TM, :]
            sl = lax.dot_general(q, kl, nt_dims, preferred_element_type=F32)
            sl = jnp.where(valid, sl, NEG)
            scx = lax.dot_general(q, kc, nt_dims, preferred_element_type=F32)
            sink = sink_ref[h]
            m = jnp.maximum(jnp.maximum(jnp.max(sl, axis=-1, keepdims=True), jnp.max(scx, axis=-1, keepdims=True)), sink)
            pl_ = jnp.exp(sl - m)
            pc = jnp.exp(scx - m)
            l = jnp.sum(pl_, axis=-1, keepdims=True) + jnp.sum(pc, axis=-1, keepdims=True) + jnp.exp(sink - m)
            o = (jnp.dot(pl_.astype(BF16), vl, preferred_element_type=F32)
                 + jnp.dot(pc.astype(BF16), vc, preferred_element_type=F32)) / l
            o_ref[0, :, 128 * h:128 * h + 128] = o.astype(BF16)

    def context():
        for h in range(SWA_HEADS):
            g = h // (SWA_HEADS // SWA_KV_HEADS)
            q = q_ref[0, h]
            kc = k_ref[0, g, 0:TM, :]
            vc = v_ref[0, g, 0:TM, :]
            scx = lax.dot_general(q, kc, nt_dims, preferred_element_type=F32)
            sink = sink_ref[h]
            m = jnp.maximum(jnp.max(scx, axis=-1, keepdims=True), sink)
            pc = jnp.exp(scx - m)
            l = jnp.sum(pc, axis=-1, keepdims=True) + jnp.exp(sink - m)
            o = jnp.dot(pc.astype(BF16), vc, preferred_element_type=F32) / l
            o_ref[0, :, 128 * h:128 * h + 128] = o.astype(BF16)

    if q_off == 0:
        pl.when(qi == 0)(context)
        pl.when(qi > 0)(latent)
    else:
        latent()


def _swa(sink, qs, ks, vs, t_off):
    B, H, T, _ = qs.shape
    G = ks.shape[1]
    nt = T // TM - t_off
    return pl.pallas_call(
        functools.partial(_swa_kernel, q_off=t_off, n_rows=T),
        grid=(B, nt),
        in_specs=[
            pl.BlockSpec(memory_space=pltpu.SMEM),
            pl.BlockSpec((1, H, TM, 128), lambda b, i: (b, 0, i + t_off, 0)),
            pl.BlockSpec((1, G, T, 128), lambda b, i: (b, 0, 0, 0)),
            pl.BlockSpec((1, G, T, 128), lambda b, i: (b, 0, 0, 0)),
        ],
        out_specs=pl.BlockSpec((1, TM, H * 128), lambda b, i: (b, i, 0)),
        out_shape=jax.ShapeDtypeStruct((B, nt * TM, H * 128), BF16),
        compiler_params=_cp(("arbitrary", "arbitrary")),
        name="swa_attn",
    )(sink, qs, ks, vs)


def _conv_kernel(u_ref, w_ref, b_ref, g_ref, bb_ref, o_ref, buf_ref, *, t_off, n_rows):
    i = pl.program_id(1) + t_off
    nt_all = n_rows // TM
    start = pl.multiple_of(i * TM, TM)
    ps = pl.multiple_of(jnp.maximum(start - CONV_HALO, 0), 8)
    ns = pl.multiple_of(jnp.minimum(start + TM, n_rows - CONV_HALO), 8)
    keep_prev = jnp.where(i <= 1, 0.0, 1.0)
    keep_next = jnp.where((i == 0) | (i == nt_all - 1), 0.0, 1.0)
    buf_ref[0:CONV_HALO, :] = u_ref[0, pl.ds(ps, CONV_HALO), :] * keep_prev
    buf_ref[CONV_HALO:CONV_HALO + TM, :] = u_ref[0, pl.ds(start, TM), :]
    buf_ref[CONV_HALO + TM:2 * CONV_HALO + TM, :] = u_ref[0, pl.ds(ns, CONV_HALO), :] * keep_next
    off = CONV_HALO - CONV_WIDTH // 2
    acc = jnp.zeros((TM, CONV_CH), F32)
    for j in range(CONV_WIDTH):
        acc = acc + buf_ref[off + j:off + j + TM, :] * w_ref[j:j + 1, :]
    y = acc + b_ref[...]
    mu = jnp.mean(y, axis=-1, keepdims=True)
    d = y - mu
    var = jnp.mean(d * d, axis=-1, keepdims=True)
    z = d * lax.rsqrt(var + EPS) * g_ref[...] + bb_ref[...]
    o_ref[0] = (z * jax.nn.sigmoid(z)).astype(BF16)


def _conv(u, lw, t_off):
    B, T, C = u.shape
    nt = T // TM - t_off
    return pl.pallas_call(
        functools.partial(_conv_kernel, t_off=t_off, n_rows=T),
        grid=(B, nt),
        in_specs=[
            pl.BlockSpec((1, T, C), lambda b, i: (b, 0, 0)),
            _full((CONV_WIDTH, C)),
            _full((1, C)),
            _full((1, C)),
            _full((1, C)),
        ],
        out_specs=pl.BlockSpec((1, TM, C), lambda b, i: (b, i, 0)),
        out_shape=jax.ShapeDtypeStruct((B, nt * TM, C), BF16),
        scratch_shapes=[pltpu.VMEM((TM + 2 * CONV_HALO, C), F32)],
        compiler_params=_cp(("arbitrary", "arbitrary")),
        name="conv",
    )(u, lw["conv_w"], lw["conv_b"], lw["conv_g"], lw["conv_bb"])


def _out_kernel(om_ref, oc_ref, os_ref, x_ref, mod_ref, w1_ref, w2_ref, w3_ref, n2_ref, wrh_ref, wrl_ref, br_ref,
                tri_ref, upper_ref, xn_ref, hf_ref, pos_ref, gt_ref, meta_ref, cnt_ref, run_ref):
    first = (pl.program_id(0) == 0) & (pl.program_id(1) == 0)

    @pl.when(first)
    def _():
        run_ref[...] = jnp.zeros_like(run_ref)

    d = functools.partial(jnp.dot, preferred_element_type=F32)
    mix = d(om_ref[0], w1_ref[...]) + d(oc_ref[0], w2_ref[...]) + d(os_ref[0], w3_ref[...])
    mod = mod_ref[0, 0]
    g1, sh2, sc2 = mod[2:3], mod[3:4], mod[4:5]
    xn = x_ref[0] + g1 * mix
    xn_ref[0] = xn
    hf = xn * lax.rsqrt(jnp.mean(xn * xn, axis=-1, keepdims=True) + EPS) * n2_ref[...]
    hf = hf * (1.0 + sc2) + sh2
    hf_ref[0] = hf.astype(BF16)

    hi, lo = _split(hf)
    logits = d(hi, wrh_ref[...]) + d(hi, wrl_ref[...]) + d(lo, wrh_ref[...]) + br_ref[...]
    lane = lax.broadcasted_iota(I32, (TM, LANES), 1)
    l = logits
    ohs, vals = [], []
    for _ in range(TOP_K):
        m = jnp.max(l, axis=-1, keepdims=True)
        idx = jnp.min(jnp.where(l == m, lane, LANES), axis=-1, keepdims=True)
        oh = lane == idx
        ohs.append(oh)
        vals.append(m)
        l = jnp.where(oh, -jnp.inf, l)
    ex = [jnp.exp(v - vals[0]) for v in vals]
    den = ex[0] + ex[1] + ex[2] + ex[3]
    gates = [e / den for e in ex]

    oa = jnp.zeros((TM, LANES), F32)
    for oh in ohs:
        oa = oa + oh.astype(F32)
    hist = jnp.sum(oa, axis=0, keepdims=True)
    slot_rows = jnp.floor((hist + (RUN_CHUNK - 1)) * (1.0 / RUN_CHUNK)) * RUN_CHUNK
    slot_off = d(jnp.broadcast_to(slot_rows, (8, LANES)).astype(BF16), upper_ref[...])
    where_ = d(tri_ref[...], oa.astype(BF16)) + slot_off[0:1, :]
    poss = [jnp.sum(jnp.where(oh, where_, 0.0), axis=-1, keepdims=True).astype(I32) for oh in ohs]

    p_out = jnp.zeros((TM, LANES), I32)
    g_out = jnp.zeros((TM, LANES), F32)
    for k in range(TOP_K):
        p_out = jnp.where(lane == k, poss[k], p_out)
        g_out = jnp.where(lane == k, gates[k], g_out)
    pos_ref[0] = p_out
    gt_ref[0] = g_out

    srow = lax.broadcasted_iota(I32, (8, LANES), 0)
    meta = jnp.where(srow == 0, hist, jnp.where(srow == 1, run_ref[...], jnp.where(srow == 2, slot_off, 0.0)))
    meta_ref[0] = meta.astype(I32)
    run_ref[...] = run_ref[...] + hist
    cnt_ref[...] = run_ref[...]


def _outproj(om, oc, osw, xu, modsel, lw, tri, upper, t_off):
    B, T, D = xu.shape
    nt = T // TM - t_off
    row = lambda b, i: (b, i, 0)
    Tq = nt * TM
    in_specs = [
        pl.BlockSpec((1, TM, 512), row),
        pl.BlockSpec((1, TM, 256), row),
        pl.BlockSpec((1, TM, 512), row),
        pl.BlockSpec((1, TM, D), lambda b, i: (b, i + t_off, 0)),
        pl.BlockSpec((1, 1, 6, D), lambda b, i: (b, jnp.minimum(i + t_off, 1), 0, 0)),
        _full((512, D)),
        _full((256, D)),
        _full((512, D)),
        _full((1, D)),
        _full((D, LANES)),
        _full((D, LANES)),
        _full((1, LANES)),
        _full((TM, TM)),
        _full((LANES, LANES)),
    ]
    out_shape = [
        jax.ShapeDtypeStruct((B, Tq, D), F32),
        jax.ShapeDtypeStruct((B, Tq, D), BF16),
        jax.ShapeDtypeStruct((B, Tq, LANES), I32),
        jax.ShapeDtypeStruct((B, Tq, LANES), F32),
        jax.ShapeDtypeStruct((B * nt, 8, LANES), I32),
        jax.ShapeDtypeStruct((8, LANES), F32),
    ]
    out_specs = [
        pl.BlockSpec((1, TM, D), row),
        pl.BlockSpec((1, TM, D), row),
        pl.BlockSpec((1, TM, LANES), row),
        pl.BlockSpec((1, TM, LANES), row),
        pl.BlockSpec((1, 8, LANES), lambda b, i: (b * nt + i, 0, 0)),
        pl.BlockSpec((8, LANES), lambda b, i: (0, 0)),
    ]
    return pl.pallas_call(
        _out_kernel,
        grid=(B, nt),
        in_specs=in_specs,
        out_specs=out_specs,
        out_shape=out_shape,
        scratch_shapes=[pltpu.VMEM((8, LANES), F32)],
        compiler_params=_cp(("arbitrary", "arbitrary")),
        name="outproj_router",
    )(om, oc, osw, xu, modsel, lw["w_o1"], lw["w_o2"], lw["w_o3"], lw["n2"], lw["wr_hi"], lw["wr_lo"], lw["br"], tri, upper)


SUB = 8
RUN_CHUNK = 16
SLOT_ROWS = TM * TOP_K + N_EXPERTS * RUN_CHUNK


def _rows(ref, row0, nrows):
    start = row0 * SUB if isinstance(row0, int) else pl.multiple_of(row0 * SUB, SUB)
    return ref.at[pl.ds(start, nrows * SUB), :]


def _to_tiles(ref, val, nrows):
    for j in range(SUB):
        ref[pl.ds(j, nrows, stride=SUB), :] = val[:, LANES * j:LANES * (j + 1)]


def _from_tiles(ref, nrows):
    return jnp.concatenate([ref[pl.ds(j, nrows, stride=SUB), :] for j in range(SUB)], axis=1)


def _slot_matrix(pos, weights):
    col = lax.broadcasted_iota(I32, (TM, SLOT_ROWS), 1)
    m = jnp.zeros((TM, SLOT_ROWS), F32)
    for k in range(TOP_K):
        m = jnp.where(col == pos[:, k:k + 1], weights[k], m)
    return m.astype(BF16)


def _run_copies(meta_ref, pst_ref, buf_ref, hbm_ref, sem, to_hbm):
    total = jnp.int32(0)
    for e in range(N_EXPERTS):
        n = meta_ref[0, 0, e]
        nch = (n + (RUN_CHUNK - 1)) // RUN_CHUNK
        seg0 = pst_ref[e] + meta_ref[0, 1, e]
        slot0 = meta_ref[0, 2, e]

        def chunk(c, carry):
            a = _rows(buf_ref, slot0 + c * RUN_CHUNK, RUN_CHUNK)
            b = _rows(hbm_ref, seg0 + c * RUN_CHUNK, RUN_CHUNK)
            (pltpu.make_async_copy(a, b, sem) if to_hbm else pltpu.make_async_copy(b, a, sem)).start(priority=e % 2)
            return carry

        lax.fori_loop(0, nch, chunk, 0)
        total = total + nch
    return total


def _run_waits(total, buf_ref, hbm_ref, sem, to_hbm):
    a = _rows(buf_ref, 0, RUN_CHUNK)
    b = _rows(hbm_ref, 0, RUN_CHUNK)

    def one(c, carry):
        (pltpu.make_async_copy(a, b, sem) if to_hbm else pltpu.make_async_copy(b, a, sem)).wait()
        return carry

    lax.fori_loop(0, total, one, 0)


def _run_total(meta_ref):
    total = jnp.int32(0)
    for e in range(N_EXPERTS):
        total = total + (meta_ref[0, 0, e] + (RUN_CHUNK - 1)) // RUN_CHUNK
    return total


def _disp_kernel(pst_ref, cnt_ref, pad_ref, na_ref, meta_ref, hf_ref, pos_ref, xb_ref, srt_ref, zero_ref, tot_ref,
                 sem, zsem, *, n_blocks):
    step = pl.program_id(0) * pl.num_programs(1) + pl.program_id(1)
    n_steps = pl.num_programs(0) * pl.num_programs(1)
    slot = step % 2
    first = step == 0

    @pl.when(first)
    def _():
        zero_ref[...] = jnp.zeros_like(zero_ref)

        def zblock(j):
            return pltpu.make_async_copy(zero_ref, _rows(xb_ref, j * EXPERT_BLOCK, EXPERT_BLOCK), zsem)

        def zb_start(j, carry):
            zblock(j).start()
            return carry

        def zb_wait(j, carry):
            zblock(j).wait()
            return carry

        lax.fori_loop(na_ref[0], n_blocks, zb_start, 0)
        lax.fori_loop(na_ref[0], n_blocks, zb_wait, 0)

        for e in range(N_EXPERTS):
            lo = pst_ref[e] + cnt_ref[e]
            nrow = pad_ref[e] - cnt_ref[e]
            nz = nrow // RUN_CHUNK
            lo1 = lo + nz * RUN_CHUNK
            n1 = nrow - nz * RUN_CHUNK

            def zchunk(c):
                return pltpu.make_async_copy(_rows(zero_ref, 0, RUN_CHUNK), _rows(xb_ref, lo + c * RUN_CHUNK, RUN_CHUNK), zsem)

            def zrow(r):
                return pltpu.make_async_copy(_rows(zero_ref, 0, 1), _rows(xb_ref, lo1 + r, 1), zsem)

            def zc_start(c, carry):
                zchunk(c).start()
                return carry

            def zc_wait(c, carry):
                zchunk(c).wait()
                return carry

            def zr_start(r, carry):
                zrow(r).start()
                return carry

            def zr_wait(r, carry):
                zrow(r).wait()
                return carry

            lax.fori_loop(0, nz, zc_start, 0)
            lax.fori_loop(0, n1, zr_start, 0)
            lax.fori_loop(0, nz, zc_wait, 0)
            lax.fori_loop(0, n1, zr_wait, 0)

    ones = [1.0] * TOP_K
    q = _slot_matrix(pos_ref[0], ones)
    srt = lax.dot_general(q, hf_ref[0], (((0,), (0,)), ((), ())), preferred_element_type=F32)
    _to_tiles(srt_ref.at[slot], srt, SLOT_ROWS)

    @pl.when(step > 0)
    def _():
        _run_waits(tot_ref[1 - slot], srt_ref.at[1 - slot], xb_ref, sem.at[1 - slot], True)

    total = _run_copies(meta_ref, pst_ref, srt_ref.at[slot], xb_ref, sem.at[slot], True)
    tot_ref[slot] = total

    @pl.when(step == n_steps - 1)
    def _():
        _run_waits(total, srt_ref.at[slot], xb_ref, sem.at[slot], True)


def _dispatch(pstart, counts, padded, n_act, meta, hf, pos, n_buf):
    B, Tq, D = hf.shape
    nt = Tq // TM
    assert D == SUB * LANES
    grid_spec = pltpu.PrefetchScalarGridSpec(
        num_scalar_prefetch=4,
        grid=(B, nt),
        in_specs=[
            pl.BlockSpec((1, 8, LANES), lambda b, i, *_: (b * nt + i, 0, 0), memory_space=pltpu.SMEM),
            pl.BlockSpec((1, TM, D), lambda b, i, *_: (b, i, 0)),
            pl.BlockSpec((1, TM, LANES), lambda b, i, *_: (b, i, 0)),
        ],
        out_specs=pl.BlockSpec(memory_space=pl.ANY),
        scratch_shapes=[pltpu.VMEM((2, SLOT_ROWS * SUB, LANES), F32), pltpu.VMEM((EXPERT_BLOCK * SUB, LANES), F32),
                        pltpu.SMEM((2,), I32), pltpu.SemaphoreType.DMA((2,)), pltpu.SemaphoreType.DMA],
    )
    return pl.pallas_call(
        functools.partial(_disp_kernel, n_blocks=n_buf // EXPERT_BLOCK),
        grid_spec=grid_spec,
        out_shape=jax.ShapeDtypeStruct((n_buf * SUB, LANES), F32),
        compiler_params=_cp(("arbitrary", "arbitrary")),
        name="dispatch",
    )(pstart, counts, padded, n_act, meta, hf, pos)


W_ROWS = 256


def _exp_kernel(be_ref, bx_ref, na_ref, nx_ref, ps_ref, pr_ref, npv_ref, x_ref, b1a_ref, b1b_ref, b2_ref, w1_hbm, w2_hbm,
                o_ref, st1, st2, w1a_s, w1b_s, w2_s, sem, *, e_off):
    j = pl.program_id(0)
    active = j < na_ref[0]
    e_cur = be_ref[j]
    p = pr_ref[e_cur]
    k = bx_ref[j] - ps_ref[e_cur]
    nxt = nx_ref[e_cur]
    n_prev = npv_ref[e_cur]
    n_slices = st1.shape[0] // W_ROWS

    def copies(e):
        return (pltpu.make_async_copy(w1_hbm.at[e], st1, sem.at[0]), pltpu.make_async_copy(w2_hbm.at[e], st2, sem.at[1]))

    def fetch(e):
        for cp in copies(e):
            cp.start(priority=1)

    def fetch_wait():
        for cp in copies(0):
            cp.wait()

    def convert(slot, s):
        r0 = pl.multiple_of(s * W_ROWS, W_ROWS)
        lane = lax.broadcasted_iota(I32, (W_ROWS, LANES), 1)
        lo = lane < 64
        idx = jnp.where(lo, 2 * lane, 2 * (lane - 64) + 1)
        for c in range(st1.shape[1] // (2 * LANES)):
            a = st1[pl.ds(r0, W_ROWS), 2 * LANES * c:2 * LANES * c + LANES]
            b = st1[pl.ds(r0, W_ROWS), 2 * LANES * c + LANES:2 * LANES * (c + 1)]
            pa = jnp.take_along_axis(a, idx, axis=1)
            pb = jnp.take_along_axis(b, idx, axis=1)
            ev = jnp.where(lo, pa, pltpu.roll(pb, 64, axis=1))
            od = jnp.where(lo, pltpu.roll(pa, 64, axis=1), pb)
            w1a_s[slot, pl.ds(r0, W_ROWS), LANES * c:LANES * (c + 1)] = ev.astype(BF16)
            w1b_s[slot, pl.ds(r0, W_ROWS), LANES * c:LANES * (c + 1)] = od.astype(BF16)
        w2_s[slot, pl.ds(r0, W_ROWS), :] = st2[pl.ds(r0, W_ROWS), :].astype(BF16)

    def convert_range(slot, s0):
        def one(s, carry):
            convert(slot, s)
            return carry

        lax.fori_loop(s0, n_slices, one, 0)

    @pl.when(j == 0)
    def _():
        fetch(be_ref[0] + e_off)
        fetch_wait()
        convert_range(0, 0)

    @pl.when(active & (k == 0) & (j > 0))
    def _():
        @pl.when(n_prev < 2)
        def _():
            fetch_wait()

        convert_range(p, jnp.clip(n_prev - 1, 0, n_slices))

    @pl.when(active & (k == 0) & (nxt >= 0))
    def _():
        fetch(nxt + e_off)

    conv = active & (nxt >= 0) & (k >= 1) & (k <= n_slices)

    @pl.when(conv & (k == 1))
    def _():
        fetch_wait()

    def ffn(with_convert):
        if with_convert:
            convert(1 - p, k - 1)
        x = _from_tiles(x_ref, EXPERT_BLOCK).astype(BF16)
        ug = jnp.dot(x, w1a_s[p], preferred_element_type=F32) + b1a_ref[0]
        ul = jnp.dot(x, w1b_s[p], preferred_element_type=F32) + b1b_ref[0]
        xg = jnp.minimum(ug, SWIGLU_LIMIT)
        xl = jnp.clip(ul, -SWIGLU_LIMIT, SWIGLU_LIMIT)
        act = xg * jax.nn.sigmoid(SWIGLU_ALPHA * xg) * (xl + 1.0)
        y = jnp.dot(act.astype(BF16), w2_s[p], preferred_element_type=F32) + b2_ref[0]
        _to_tiles(o_ref, y, EXPERT_BLOCK)

    pl.when(conv)(functools.partial(ffn, True))
    pl.when(active & jnp.logical_not(conv))(functools.partial(ffn, False))

    @pl.when(jnp.logical_not(active))
    def _():
        o_ref[...] = jnp.zeros_like(o_ref)


def _experts(sched, xb, lw):
    nb = xb.shape[0] // (EXPERT_BLOCK * SUB)
    De, D = lw["w2"].shape[1:]
    assert De == D and D % W_ROWS == 0
    wmap = lambda j, be, *_: (be[j], 0, 0)
    grid_spec = pltpu.PrefetchScalarGridSpec(
        num_scalar_prefetch=len(sched),
        grid=(nb,),
        in_specs=[
            pl.BlockSpec((EXPERT_BLOCK * SUB, LANES), lambda j, be, bx, *_: (bx[j], 0)),
            pl.BlockSpec((1, 1, De), wmap),
            pl.BlockSpec((1, 1, De), wmap),
            pl.BlockSpec((1, 1, D), wmap),
            pl.BlockSpec(memory_space=pl.ANY),
            pl.BlockSpec(memory_space=pl.ANY),
        ],
        out_specs=pl.BlockSpec((EXPERT_BLOCK * SUB, LANES), lambda j, *_: (j, 0)),
        scratch_shapes=[pltpu.VMEM((D, 2 * De), F32), pltpu.VMEM((De, D), F32),
                        pltpu.VMEM((2, D, De), BF16), pltpu.VMEM((2, D, De), BF16), pltpu.VMEM((2, De, D), BF16),
                        pltpu.SemaphoreType.DMA((2,))],
    )
    return pl.pallas_call(
        functools.partial(_exp_kernel, e_off=lw["e_off"]),
        grid_spec=grid_spec,
        out_shape=jax.ShapeDtypeStruct(xb.shape, F32),
        compiler_params=pltpu.CompilerParams(dimension_semantics=("arbitrary",), vmem_limit_bytes=EXPERT_VMEM_LIMIT),
        name="experts",
    )(*sched, xb, lw["b1a"], lw["b1b"], lw["b2"], lw["w1"], lw["w2"])


def _comb_kernel(pst_ref, meta_ref, meta_next_ref, pos_ref, gt_ref, xn_ref, mod_ref, yb_ref, o_ref, buf_ref, sem):
    step = pl.program_id(0) * pl.num_programs(1) + pl.program_id(1)
    n_steps = pl.num_programs(0) * pl.num_programs(1)
    slot = step % 2

    @pl.when(step == 0)
    def _():
        buf_ref[...] = jnp.zeros_like(buf_ref)
        _run_copies(meta_ref, pst_ref, buf_ref.at[0], yb_ref, sem.at[0], False)

    @pl.when(step < n_steps - 1)
    def _():
        _run_copies(meta_next_ref, pst_ref, buf_ref.at[1 - slot], yb_ref, sem.at[1 - slot], False)

    gt = gt_ref[0]
    g = _slot_matrix(pos_ref[0], [gt[:, k:k + 1] for k in range(TOP_K)])
    _run_waits(_run_total(meta_ref), buf_ref.at[slot], yb_ref, sem.at[slot], False)
    rows = _from_tiles(buf_ref.at[slot], SLOT_ROWS).astype(BF16)
    y = jnp.dot(g, rows, preferred_element_type=F32)
    g2 = mod_ref[0, 0][5:6]
    o_ref[0] = xn_ref[0] + g2 * y


def _combine(pstart, meta, pos, gates, xn, modsel, yb, t_off):
    B, Tq, D = xn.shape
    nt = Tq // TM
    grid_spec = pltpu.PrefetchScalarGridSpec(
        num_scalar_prefetch=1,
        grid=(B, nt),
        in_specs=[
            pl.BlockSpec((1, 8, LANES), lambda b, i, *_: (b * nt + i, 0, 0), memory_space=pltpu.SMEM),
            pl.BlockSpec((1, 8, LANES), lambda b, i, *_: (jnp.minimum(b * nt + i + 1, B * nt - 1), 0, 0),
                         memory_space=pltpu.SMEM),
            pl.BlockSpec((1, TM, LANES), lambda b, i, *_: (b, i, 0)),
            pl.BlockSpec((1, TM, LANES), lambda b, i, *_: (b, i, 0)),
            pl.BlockSpec((1, TM, D), lambda b, i, *_: (b, i, 0)),
            pl.BlockSpec((1, 1, 6, D), lambda b, i, *_: (b, jnp.minimum(i + t_off, 1), 0, 0)),
            pl.BlockSpec(memory_space=pl.ANY),
        ],
        out_specs=pl.BlockSpec((1, TM, D), lambda b, i, *_: (b, i, 0)),
        scratch_shapes=[pltpu.VMEM((2, SLOT_ROWS * SUB, LANES), F32), pltpu.SemaphoreType.DMA((2,))],
    )
    return pl.pallas_call(
        _comb_kernel,
        grid_spec=grid_spec,
        out_shape=jax.ShapeDtypeStruct((B, Tq, D), F32),
        compiler_params=_cp(("arbitrary", "arbitrary")),
        name="combine",
    )(pstart, meta, meta, pos, gates, xn, modsel, yb)


def _take_cols(w, cols):
    cols = np.asarray(cols)
    out = jnp.take(w, jnp.asarray(np.maximum(cols, 0)), axis=-1)
    return jnp.where(jnp.asarray(cols >= 0), out, 0.0)


def _in_cols():
    pi = _PI
    cols = list(range(0, 256))
    cols += [256 + i for i in range(64)] + [256 + pi[i] for i in range(64)]
    for g in range(SWA_KV_HEADS):
        base = 320 + 64 * g
        cols += [base + i for i in range(64)] + [base + pi[i] for i in range(64)]
    for g in range(SWA_KV_HEADS):
        base = 448 + 64 * g
        cols += [base + i for i in range(64)] + [-1] * 64
    cols += list(range(576, 960))
    for h in range(SWA_HEADS):
        base = 960 + 64 * h
        cols += [base + i for i in range(64)] + [base + pi[i] for i in range(64)]
    cols += list(range(1216, 1728))
    return cols


def _layer_weights(l, a):
    pi = _PI
    lw = {}
    lw["n1"] = a["norm1_g"][l][None, :]
    lw["n2"] = a["norm2_g"][l][None, :]
    lw["w_in"] = _take_cols(a["w_in"][l], _in_cols()).astype(BF16)
    lw["kvg"] = a["mla_kv_norm"][l][None, :]
    lw["qg"] = a["mla_q_norm"][l][None, :]
    uq_cols = []
    for h in range(MLA_HEADS):
        base = MLA_QK * h
        uq_cols += [base + i for i in range(128)] + [base + 128 + i for i in range(64)] + [base + 128 + pi[i] for i in range(64)]
    lw["w_uq"] = _take_cols(a["mla_w_uq"][l], uq_cols).astype(BF16)
    ukv_cols = [256 * h + i for h in range(MLA_HEADS) for i in range(128)]
    ukv_cols += [256 * h + 128 + i for h in range(MLA_HEADS) for i in range(128)]
    lw["w_ukv"] = _take_cols(a["mla_w_ukv"][l], ukv_cols).astype(BF16)
    gq = a["mla_q_head_norm"][l]
    lw["gq"] = (jnp.concatenate([gq[:128], gq[128:], gq[128:][pi]]) * (MLA_QK ** -0.5 * LOG2E))[None, :]
    gk = a["mla_k_head_norm"][l]
    lw["gkn"] = gk[:128][None, :]
    lw["gkp"] = jnp.concatenate([gk[128:], gk[128:][pi]])[None, :]
    sq = a["swa_q_norm"][l]
    lw["gsq"] = (jnp.concatenate([sq, sq[pi]]) * (SWA_HEAD_DIM ** -0.5))[None, :]
    sk = a["swa_k_norm"][l]
    lw["gsk"] = jnp.concatenate([sk, sk[pi]])[None, :]
    lw["conv_w"] = a["conv_w"][l]
    lw["conv_b"] = a["conv_b"][l][None, :]
    lw["conv_g"] = a["conv_ln_g"][l][None, :]
    lw["conv_bb"] = a["conv_ln_b"][l][None, :]
    lw["sink"] = a["swa_sink"][l]
    wo = a["w_out"][l]
    lw["w_o1"] = wo[0:512].astype(BF16)
    lw["w_o2"] = wo[512:768].astype(BF16)
    o3 = wo[768:1024].reshape(SWA_HEADS, SWA_HEAD_DIM, -1)
    lw["w_o3"] = jnp.concatenate([o3, jnp.zeros_like(o3)], axis=1).reshape(SWA_HEADS * 128, -1).astype(BF16)
    wr = jnp.pad(a["router_w"][l], ((0, 0), (0, LANES - N_EXPERTS)))
    lw["wr_hi"] = wr.astype(BF16)
    lw["wr_lo"] = (wr - lw["wr_hi"].astype(F32)).astype(BF16)
    lw["br"] = jnp.pad(a["router_b"][l], (0, LANES - N_EXPERTS), constant_values=NEG)[None, :]
    lw["w1"] = a["exp_w1"].reshape((-1,) + a["exp_w1"].shape[2:])
    lw["e_off"] = l * a["exp_w1"].shape[1]
    b1 = a["exp_b1"][l]
    lw["b1a"] = b1[:, None, 0::2]
    lw["b1b"] = b1[:, None, 1::2]
    lw["w2"] = a["exp_w2"].reshape((-1,) + a["exp_w2"].shape[2:])
    lw["b2"] = a["exp_b2"][l][:, None, :]
    return lw


def _rope_tables(n_ctx, n_lat):
    q = MLA_ROPE // 4
    n = jnp.arange(n_lat, dtype=I32)
    row = (n // GRID_W).astype(F32)
    col = (n % GRID_W).astype(F32)
    inv = ROPE_BASE ** (-jnp.arange(q, dtype=F32) / q)
    ang_r = row[:, None] * inv
    ang_c = col[:, None] * inv
    cos = jnp.concatenate([jnp.cos(ang_r), jnp.cos(ang_r), jnp.cos(ang_c), jnp.cos(ang_c)], axis=1)
    sin = jnp.concatenate([-jnp.sin(ang_r), jnp.sin(ang_r), -jnp.sin(ang_c), jnp.sin(ang_c)], axis=1)
    cos = jnp.concatenate([jnp.ones((n_ctx, 64), F32), cos], axis=0)
    sin = jnp.concatenate([jnp.zeros((n_ctx, 64), F32), sin], axis=0)
    z = jnp.zeros_like(cos)
    return jnp.concatenate([cos, z], axis=1), jnp.concatenate([sin, z], axis=1)


def _routing_tables(cnt_f, n_blocks):
    counts = cnt_f[0, :N_EXPERTS].astype(I32)
    padded = (counts + (RUN_CHUNK - 1) + EXPERT_BLOCK - 1) // EXPERT_BLOCK * EXPERT_BLOCK
    padded = jnp.where(counts > 0, padded, 0)
    pend = jnp.cumsum(padded)
    pstart = pend - padded
    n_act = pend[-1] // EXPERT_BLOCK
    blk = jnp.minimum(jnp.arange(n_blocks, dtype=I32), n_act - 1)
    blk_e = jnp.sum((pend[None, :] <= (blk * EXPERT_BLOCK)[:, None]).astype(I32), axis=1)
    blk_e = jnp.minimum(blk_e, N_EXPERTS - 1)
    nbk = padded // EXPERT_BLOCK
    has = nbk > 0
    ids = jnp.arange(N_EXPERTS, dtype=I32)
    later = has[None, :] & (ids[None, :] > ids[:, None])
    nxt_e = jnp.min(jnp.where(later, ids[None, :], N_EXPERTS), axis=1)
    nxt_e = jnp.where(nxt_e == N_EXPERTS, -1, nxt_e)
    earlier = has[None, :] & (ids[None, :] < ids[:, None])
    prv_e = jnp.max(jnp.where(earlier, ids[None, :], -1), axis=1)
    nb_prev_e = jnp.sum(jnp.where(ids[None, :] == prv_e[:, None], nbk[None, :], 0), axis=1)
    set_e = (jnp.cumsum(has.astype(I32)) - 1) % 2
    sched = (blk_e, blk, n_act.reshape(1), nxt_e, pstart // EXPERT_BLOCK, set_e, nb_prev_e)
    sched = tuple(s.astype(I32) for s in sched)
    return counts, padded.astype(I32), pstart.astype(I32), sched


def kernel(x, c, ctx, c_ctx, norm1_g, norm2_g, w_ada, b_ada, w_in, mla_q_norm, mla_kv_norm, mla_w_uq, mla_w_ukv, mla_q_head_norm, mla_k_head_norm, conv_w, conv_b, conv_ln_g, conv_ln_b, swa_q_norm, swa_k_norm, swa_sink, w_out, router_w, router_b, exp_w1, exp_b1, exp_w2, exp_b2):
    a = dict(norm1_g=norm1_g, norm2_g=norm2_g, w_in=w_in, mla_q_norm=mla_q_norm, mla_kv_norm=mla_kv_norm,
             mla_w_uq=mla_w_uq, mla_w_ukv=mla_w_ukv, mla_q_head_norm=mla_q_head_norm, mla_k_head_norm=mla_k_head_norm,
             conv_w=conv_w, conv_b=conv_b, conv_ln_g=conv_ln_g, conv_ln_b=conv_ln_b, swa_q_norm=swa_q_norm,
             swa_k_norm=swa_k_norm, swa_sink=swa_sink, w_out=w_out, router_w=router_w, router_b=router_b,
             exp_w1=exp_w1, exp_b1=exp_b1, exp_w2=exp_w2, exp_b2=exp_b2)
    B, S, D = x.shape
    n_ctx = ctx.shape[1]
    depth = w_ada.shape[0]
    assert n_ctx == TM and S % TM == 0 and B + 1 <= 16
    T = n_ctx + S

    s_in = jnp.zeros((16, D), F32).at[:B].set(c).at[B].set(c_ctx)
    mods = _ada(s_in, w_ada, b_ada)
    cos_t, sin_t = _rope_tables(n_ctx, S)
    tri = jnp.tril(jnp.ones((TM, TM), F32), -1).astype(BF16)
    upper = jnp.triu(jnp.ones((LANES, LANES), F32), 1).astype(BF16)

    xu = jnp.concatenate([ctx, x], axis=1)
    for l in range(depth):
        last = l == depth - 1
        t_off = 1 if last else 0
        lw = _layer_weights(l, a)
        m = mods[l].reshape(16, 6, D)
        modsel = jnp.stack([jnp.broadcast_to(m[B], (B, 6, D)), m[:B]], axis=1)

        qm, km, vm, qs, ks, vs, u = _prep(xu, modsel, lw, cos_t, sin_t, 0)
        om = _mla(qm, km, vm, t_off)
        oc = _conv(u, lw, t_off)
        osw = _swa(lw["sink"], qs, ks, vs, t_off)
        xn, hf, pos, gt_o, meta, cnt = _outproj(om, oc, osw, xu, modsel, lw, tri, upper, t_off)

        n_tok = B * (T - t_off * TM)
        nk = n_tok * TOP_K
        n_buf = -(-(nk + N_EXPERTS * (RUN_CHUNK - 1 + EXPERT_BLOCK - 1)) // EXPERT_BLOCK) * EXPERT_BLOCK
        counts, padded, pstart, sched = _routing_tables(cnt, n_buf // EXPERT_BLOCK)
        xb = _dispatch(pstart, counts, padded, sched[2], meta, hf, pos, n_buf)
        yb = _experts(sched, xb, lw)
        xu = _combine(pstart, meta, pos, gt_o, xn, modsel, yb, t_off)
    return xu
```

```python
import functools

import numpy as np
import jax
import jax.numpy as jnp
from jax import lax
from jax.experimental import pallas as pl
from jax.experimental.pallas import tpu as pltpu

F32 = jnp.float32
BF16 = jnp.bfloat16
I32 = jnp.int32

GRID_W = 64
ROPE_BASE = 10000.0
EPS = 1e-6
MLA_HEADS = 4
MLA_NOPE = 128
MLA_ROPE = 64
MLA_V = 128
MLA_QK = MLA_NOPE + MLA_ROPE
MLA_Q_RANK = 384
MLA_KV_RANK = 256
CONV_CH = 256
CONV_WIDTH = 31
SWA_HEADS = 4
SWA_KV_HEADS = 2
SWA_HEAD_DIM = 64
SWA_WINDOW = 128
N_EXPERTS = 32
TOP_K = 4
SWIGLU_LIMIT = 7.0
SWIGLU_ALPHA = 1.702
EXPERT_BLOCK = 512

LANES = 128
TM = 256
CONV_HALO = 16
VMEM_LIMIT = 48 * 1024 * 1024
EXPERT_VMEM_LIMIT = 56 * 1024 * 1024
NEG = -1e30
LOG2E = 1.4426950408889634

_PI = np.array([i + 16 if (i % 32) < 16 else i - 16 for i in range(64)])


def _cp(sem):
    return pltpu.CompilerParams(dimension_semantics=sem, vmem_limit_bytes=VMEM_LIMIT)


def _full(shape):
    n = len(shape)
    return pl.BlockSpec(shape, lambda *a, _n=n: (0,) * _n)


def _split(x):
    hi = x.astype(BF16)
    lo = (x - hi.astype(F32)).astype(BF16)
    return hi, lo


def _dot3(a, b):
    ah, al = _split(a)
    bh, bl = _split(b)
    d = functools.partial(jnp.dot, preferred_element_type=F32)
    return d(ah, bh) + d(ah, bl) + d(al, bh)


def _ada_kernel(s_ref, w_ref, b_ref, o_ref):
    s = s_ref[...]
    s = s * jax.nn.sigmoid(s)
    o_ref[0] = _dot3(s, w_ref[0]) + b_ref[0]


def _ada(s_in, w_ada, b_ada):
    L, D, N = w_ada.shape
    tn = 1536
    return pl.pallas_call(
        _ada_kernel,
        grid=(L, N // tn),
        in_specs=[
            pl.BlockSpec((16, D), lambda l, j: (0, 0)),
            pl.BlockSpec((1, D, tn), lambda l, j: (l, 0, j)),
            pl.BlockSpec((1, 1, tn), lambda l, j: (l, 0, j)),
        ],
        out_specs=pl.BlockSpec((1, 16, tn), lambda l, j: (l, 0, j)),
        out_shape=jax.ShapeDtypeStruct((L, 16, N), F32),
        compiler_params=_cp(("arbitrary", "arbitrary")),
        name="ada",
    )(s_in, w_ada, b_ada.reshape(L, 1, N))


def _rope(x, c, s):
    return x * c + pltpu.roll(x, 64, axis=1) * s


def _prep_kernel(x_ref, mod_ref, n1_ref, win_ref, kvg_ref, wukv_ref, qg_ref, wuq_ref, gq_ref, gkn_ref, gkp_ref,
                 gsq_ref, gsk_ref, cos_ref, sin_ref, qm_ref, km_ref, vm_ref, qs_ref, ks_ref, vs_ref, u_ref):
    x = x_ref[0]
    mod = mod_ref[0, 0]
    sh, sc = mod[0:1], mod[1:2]
    y = x * lax.rsqrt(jnp.mean(x * x, axis=-1, keepdims=True) + EPS) * n1_ref[...]
    h = y * (1.0 + sc) + sh
    p = jnp.dot(h.astype(BF16), win_ref[...], preferred_element_type=F32)
    c = cos_ref[...]
    s = sin_ref[...]
    lane = lax.broadcasted_iota(I32, (TM, LANES), 1)
    lo = lane < 64

    def ss_lo(v):
        return jnp.sum(jnp.where(lo, v * v, 0.0), axis=-1, keepdims=True)

    ckv = p[:, 0:256]
    ckvn = ckv * lax.rsqrt(jnp.mean(ckv * ckv, axis=-1, keepdims=True) + EPS) * kvg_ref[...]
    kv = jnp.dot(ckvn.astype(BF16), wukv_ref[...], preferred_element_type=F32)
    kpe = p[:, 256:384]
    ss_pe = ss_lo(kpe)
    kpe_rot = _rope(kpe * gkp_ref[...], c, s)
    for hh in range(MLA_HEADS):
        kn = kv[:, 128 * hh:128 * hh + 128]
        r = lax.rsqrt((jnp.sum(kn * kn, axis=-1, keepdims=True) + ss_pe) * (1.0 / MLA_QK) + EPS)
        km_ref[0, hh, :, 0:128] = (kn * r * gkn_ref[...]).astype(BF16)
        km_ref[0, hh, :, 128:256] = (kpe_rot * r).astype(BF16)
        vm_ref[0, hh, :, 0:128] = kv[:, 512 + 128 * hh:640 + 128 * hh].astype(BF16)
        vm_ref[0, hh, :, 128:256] = (lane == 0).astype(BF16)

    cq = p[:, 896:1280]
    cqn = cq * lax.rsqrt(jnp.mean(cq * cq, axis=-1, keepdims=True) + EPS) * qg_ref[...]
    q = jnp.dot(cqn.astype(BF16), wuq_ref[...], preferred_element_type=F32)
    gq = gq_ref[...]
    for hh in range(MLA_HEADS):
        qn = q[:, 256 * hh:256 * hh + 128]
        qp = q[:, 256 * hh + 128:256 * hh + 256]
        r = lax.rsqrt((jnp.sum(qn * qn, axis=-1, keepdims=True) + ss_lo(qp)) * (1.0 / MLA_QK) + EPS)
        qm_ref[0, hh, :, 0:128] = (qn * r * gq[:, 0:128]).astype(BF16)
        qm_ref[0, hh, :, 128:256] = _rope(qp * r * gq[:, 128:256], c, s).astype(BF16)

    for g in range(SWA_KV_HEADS):
        xk = p[:, 384 + 128 * g:512 + 128 * g]
        r = lax.rsqrt(ss_lo(xk) * (1.0 / SWA_HEAD_DIM) + EPS)
        ks_ref[0, g] = _rope(xk * r * gsk_ref[...], c, s).astype(BF16)
        vs_ref[0, g] = p[:, 640 + 128 * g:768 + 128 * g].astype(BF16)
    for hh in range(SWA_HEADS):
        xq = p[:, 1280 + 128 * hh:1408 + 128 * hh]
        r = lax.rsqrt(ss_lo(xq) * (1.0 / SWA_HEAD_DIM) + EPS)
        qs_ref[0, hh] = _rope(xq * r * gsq_ref[...], c, s).astype(BF16)

    u_ref[0] = p[:, 1792:2048] * jax.nn.sigmoid(p[:, 2048:2304])


def _prep(xu, modsel, lw, cos_t, sin_t, t_off):
    B, T, D = xu.shape
    nt = T // TM - t_off
    ncol = lw["w_in"].shape[1]
    row = lambda b, i: (b, i + t_off, 0)
    head = lambda b, i: (b, 0, i + t_off, 0)
    in_specs = [
        pl.BlockSpec((1, TM, D), row),
        pl.BlockSpec((1, 1, 6, D), lambda b, i: (b, jnp.minimum(i + t_off, 1), 0, 0)),
        _full((1, D)),
        _full((D, ncol)),
        _full((1, MLA_KV_RANK)),
        _full((MLA_KV_RANK, 1024)),
        _full((1, MLA_Q_RANK)),
        _full((MLA_Q_RANK, 1024)),
        _full((1, 256)),
        _full((1, 128)),
        _full((1, 128)),
        _full((1, 128)),
        _full((1, 128)),
        pl.BlockSpec((TM, LANES), lambda b, i: (i + t_off, 0)),
        pl.BlockSpec((TM, LANES), lambda b, i: (i + t_off, 0)),
    ]
    out_shape = [
        jax.ShapeDtypeStruct((B, MLA_HEADS, T, 256), BF16),
        jax.ShapeDtypeStruct((B, MLA_HEADS, T, 256), BF16),
        jax.ShapeDtypeStruct((B, MLA_HEADS, T, 256), BF16),
        jax.ShapeDtypeStruct((B, SWA_HEADS, T, 128), BF16),
        jax.ShapeDtypeStruct((B, SWA_KV_HEADS, T, 128), BF16),
        jax.ShapeDtypeStruct((B, SWA_KV_HEADS, T, 128), BF16),
        jax.ShapeDtypeStruct((B, T, CONV_CH), F32),
    ]
    out_specs = [
        pl.BlockSpec((1, MLA_HEADS, TM, 256), head),
        pl.BlockSpec((1, MLA_HEADS, TM, 256), head),
        pl.BlockSpec((1, MLA_HEADS, TM, 256), head),
        pl.BlockSpec((1, SWA_HEADS, TM, 128), head),
        pl.BlockSpec((1, SWA_KV_HEADS, TM, 128), head),
        pl.BlockSpec((1, SWA_KV_HEADS, TM, 128), head),
        pl.BlockSpec((1, TM, CONV_CH), row),
    ]
    return pl.pallas_call(
        _prep_kernel,
        grid=(B, nt),
        in_specs=in_specs,
        out_specs=out_specs,
        out_shape=out_shape,
        compiler_params=_cp(("arbitrary", "arbitrary")),
        name="prep",
    )(xu, modsel, lw["n1"], lw["w_in"], lw["kvg"], lw["w_ukv"], lw["qg"], lw["w_uq"], lw["gq"], lw["gkn"], lw["gkp"],
      lw["gsq"], lw["gsk"], cos_t, sin_t)


KEY_CHUNK = 256


def _mla_kernel(q_ref, k_ref, v_ref, o_ref, s_ref, p_ref, *, q_off, n_keys):
    qi = pl.program_id(1) + q_off

    def attend(nk):
        for h in range(MLA_HEADS):
            q = q_ref[0, h]
            macc = jnp.full((TM, LANES), -jnp.inf, F32)
            for c in range(nk // KEY_CHUNK):
                k = k_ref[0, h, c * KEY_CHUNK:(c + 1) * KEY_CHUNK, :]
                s = lax.dot_general(q, k, (((1,), (1,)), ((), ())), preferred_element_type=F32)
                s_ref[h, :, c * KEY_CHUNK:(c + 1) * KEY_CHUNK] = s
                for j in range(KEY_CHUNK // LANES):
                    macc = jnp.maximum(macc, s[:, j * LANES:(j + 1) * LANES])
            m = jnp.max(macc, axis=-1, keepdims=True)
            for c in range(nk // KEY_CHUNK):
                p = jnp.exp2(s_ref[h, :, c * KEY_CHUNK:(c + 1) * KEY_CHUNK] - m)
                p_ref[h, :, c * KEY_CHUNK:(c + 1) * KEY_CHUNK] = p.astype(BF16)
            ol = jnp.dot(p_ref[h, :, 0:nk], v_ref[0, h, 0:nk, :], preferred_element_type=F32)
            o = ol[:, 0:MLA_V] / ol[:, MLA_V:MLA_V + 1]
            o_ref[0, :, 128 * h:128 * h + 128] = o.astype(BF16)

    if q_off == 0:
        @pl.when(qi == 0)
        def _():
            attend(TM)

        @pl.when(qi > 0)
        def _():
            attend(n_keys)
    else:
        attend(n_keys)


def _mla(qm, km, vm, t_off):
    B, H, T, _ = qm.shape
    nt = T // TM - t_off
    return pl.pallas_call(
        functools.partial(_mla_kernel, q_off=t_off, n_keys=T),
        grid=(B, nt),
        in_specs=[
            pl.BlockSpec((1, H, TM, 256), lambda b, i: (b, 0, i + t_off, 0)),
            pl.BlockSpec((1, H, T, 256), lambda b, i: (b, 0, 0, 0)),
            pl.BlockSpec((1, H, T, 256), lambda b, i: (b, 0, 0, 0)),
        ],
        out_specs=pl.BlockSpec((1, TM, H * MLA_V), lambda b, i: (b, i, 0)),
        out_shape=jax.ShapeDtypeStruct((B, nt * TM, H * MLA_V), BF16),
        scratch_shapes=[pltpu.VMEM((H, TM, T), F32), pltpu.VMEM((H, TM, T), BF16)],
        compiler_params=_cp(("arbitrary", "arbitrary")),
        name="mla_attn",
    )(qm, km, vm)


SWA_SPAN = TM + 2 * SWA_WINDOW


def _swa_kernel(sink_ref, q_ref, k_ref, v_ref, o_ref, *, q_off, n_rows):
    qi = pl.program_id(1) + q_off
    nt_dims = (((1,), (1,)), ((), ()))

    def latent():
        start = jnp.clip(qi * TM - SWA_WINDOW, 0, n_rows - SWA_SPAN)
        start = pl.multiple_of(start, SWA_WINDOW)
        qpos = qi * TM + lax.broadcasted_iota(I32, (TM, SWA_SPAN), 0)
        kpos = start + lax.broadcasted_iota(I32, (TM, SWA_SPAN), 1)
        valid = (jnp.abs(qpos - kpos) <= SWA_WINDOW) & (kpos >= TM)
        for h in range(SWA_HEADS):
            g = h // (SWA_HEADS // SWA_KV_HEADS)
            q = q_ref[0, h]
            kl = k_ref[0, g, pl.ds(start, SWA_SPAN), :]
            vl = v_ref[0, g, pl.ds(start, SWA_SPAN), :]
            kc = k_ref[0, g, 0:TM, :]
            vc = v_ref[0, g, 0:TM, :]
            sl = lax.dot_general(q, kl, nt_dims, preferred_element_type=F32)
            sl = jnp.where(valid, sl, NEG)
            scx = lax.dot_general(q, kc, nt_dims, preferred_element_type=F32)
            sink = sink_ref[h]
            m = jnp.maximum(jnp.maximum(jnp.max(sl, axis=-1, keepdims=True), jnp.max(scx, axis=-1, keepdims=True)), sink)
            pl_ = jnp.exp(sl - m)
            pc = jnp.exp(scx - m)
            l = jnp.sum(pl_, axis=-1, keepdims=True) + jnp.sum(pc, axis=-1, keepdims=True) + jnp.exp(sink - m)
            o = (jnp.dot(pl_.astype(BF16), vl, preferred_element_type=F32)
                 + jnp.dot(pc.astype(BF16), vc, preferred_element_type=F32)) / l
            o_ref[0, :, 128 * h:128 * h + 128] = o.astype(BF16)

    def context():
        for h in range(SWA_HEADS):
            g = h // (SWA_HEADS // SWA_KV_HEADS)
            q = q_ref[0, h]
            kc = k_ref[0, g, 0:TM, :]
            vc = v_ref[0, g, 0:TM, :]
            scx = lax.dot_general(q, kc, nt_dims, preferred_element_type=F32)
            sink = sink_ref[h]
            m = jnp.maximum(jnp.max(scx, axis=-1, keepdims=True), sink)
            pc = jnp.exp(scx - m)
            l = jnp.sum(pc, axis=-1, keepdims=True) + jnp.exp(sink - m)
            o = jnp.dot(pc.astype(BF16), vc, preferred_element_type=F32) / l
            o_ref[0, :, 128 * h:128 * h + 128] = o.astype(BF16)

    if q_off == 0:
        pl.when(qi == 0)(context)
        pl.when(qi > 0)(latent)
    else:
        latent()


def _swa(sink, qs, ks, vs, t_off):
    B, H, T, _ = qs.shape
    G = ks.shape[1]
    nt = T // TM - t_off
    return pl.pallas_call(
        functools.partial(_swa_kernel, q_off=t_off, n_rows=T),
        grid=(B, nt),
        in_specs=[
            pl.BlockSpec(memory_space=pltpu.SMEM),
            pl.BlockSpec((1, H, TM, 128), lambda b, i: (b, 0, i + t_off, 0)),
            pl.BlockSpec((1, G, T, 128), lambda b, i: (b, 0, 0, 0)),
            pl.BlockSpec((1, G, T, 128), lambda b, i: (b, 0, 0, 0)),
        ],
        out_specs=pl.BlockSpec((1, TM, H * 128), lambda b, i: (b, i, 0)),
        out_shape=jax.ShapeDtypeStruct((B, nt * TM, H * 128), BF16),
        compiler_params=_cp(("arbitrary", "arbitrary")),
        name="swa_attn",
    )(sink, qs, ks, vs)


def _conv_kernel(u_ref, w_ref, b_ref, g_ref, bb_ref, o_ref, buf_ref, *, t_off, n_rows):
    i = pl.program_id(1) + t_off
    nt_all = n_rows // TM
    start = pl.multiple_of(i * TM, TM)
    ps = pl.multiple_of(jnp.maximum(start - CONV_HALO, 0), 8)
    ns = pl.multiple_of(jnp.minimum(start + TM, n_rows - CONV_HALO), 8)
    keep_prev = jnp.where(i <= 1, 0.0, 1.0)
    keep_next = jnp.where((i == 0) | (i == nt_all - 1), 0.0, 1.0)
    buf_ref[0:CONV_HALO, :] = u_ref[0, pl.ds(ps, CONV_HALO), :] * keep_prev
    buf_ref[CONV_HALO:CONV_HALO + TM, :] = u_ref[0, pl.ds(start, TM), :]
    buf_ref[CONV_HALO + TM:2 * CONV_HALO + TM, :] = u_ref[0, pl.ds(ns, CONV_HALO), :] * keep_next
    off = CONV_HALO - CONV_WIDTH // 2
    acc = jnp.zeros((TM, CONV_CH), F32)
    for j in range(CONV_WIDTH):
        acc = acc + buf_ref[off + j:off + j + TM, :] * w_ref[j:j + 1, :]
    y = acc + b_ref[...]
    mu = jnp.mean(y, axis=-1, keepdims=True)
    d = y - mu
    var = jnp.mean(d * d, axis=-1, keepdims=True)
    z = d * lax.rsqrt(var + EPS) * g_ref[...] + bb_ref[...]
    o_ref[0] = (z * jax.nn.sigmoid(z)).astype(BF16)


def _conv(u, lw, t_off):
    B, T, C = u.shape
    nt = T // TM - t_off
    return pl.pallas_call(
        functools.partial(_conv_kernel, t_off=t_off, n_rows=T),
        grid=(B, nt),
        in_specs=[
            pl.BlockSpec((1, T, C), lambda b, i: (b, 0, 0)),
            _full((CONV_WIDTH, C)),
            _full((1, C)),
            _full((1, C)),
            _full((1, C)),
        ],
        out_specs=pl.BlockSpec((1, TM, C), lambda b, i: (b, i, 0)),
        out_shape=jax.ShapeDtypeStruct((B, nt * TM, C), BF16),
        scratch_shapes=[pltpu.VMEM((TM + 2 * CONV_HALO, C), F32)],
        compiler_params=_cp(("arbitrary", "arbitrary")),
        name="conv",
    )(u, lw["conv_w"], lw["conv_b"], lw["conv_g"], lw["conv_bb"])


def _out_kernel(om_ref, oc_ref, os_ref, x_ref, mod_ref, w1_ref, w2_ref, w3_ref, n2_ref, wrh_ref, wrl_ref, br_ref,
                tri_ref, upper_ref, xn_ref, hf_ref, pos_ref, gt_ref, meta_ref, cnt_ref, run_ref):
    first = (pl.program_id(0) == 0) & (pl.program_id(1) == 0)

    @pl.when(first)
    def _():
        run_ref[...] = jnp.zeros_like(run_ref)

    d = functools.partial(jnp.dot, preferred_element_type=F32)
    mix = d(om_ref[0], w1_ref[...]) + d(oc_ref[0], w2_ref[...]) + d(os_ref[0], w3_ref[...])
    mod = mod_ref[0, 0]
    g1, sh2, sc2 = mod[2:3], mod[3:4], mod[4:5]
    xn = x_ref[0] + g1 * mix
    xn_ref[0] = xn
    hf = xn * lax.rsqrt(jnp.mean(xn * xn, axis=-1, keepdims=True) + EPS) * n2_ref[...]
    hf = hf * (1.0 + sc2) + sh2
    hf_ref[0] = hf.astype(BF16)

    hi, lo = _split(hf)
    logits = d(hi, wrh_ref[...]) + d(hi, wrl_ref[...]) + d(lo, wrh_ref[...]) + br_ref[...]
    lane = lax.broadcasted_iota(I32, (TM, LANES), 1)
    l = logits
    ohs, vals = [], []
    for _ in range(TOP_K):
        m = jnp.max(l, axis=-1, keepdims=True)
        idx = jnp.min(jnp.where(l == m, lane, LANES), axis=-1, keepdims=True)
        oh = lane == idx
        ohs.append(oh)
        vals.append(m)
        l = jnp.where(oh, -jnp.inf, l)
    ex = [jnp.exp(v - vals[0]) for v in vals]
    den = ex[0] + ex[1] + ex[2] + ex[3]
    gates = [e / den for e in ex]

    oa = jnp.zeros((TM, LANES), F32)
    for oh in ohs:
        oa = oa + oh.astype(F32)
    hist = jnp.sum(oa, axis=0, keepdims=True)
    slot_rows = jnp.floor((hist + (RUN_CHUNK - 1)) * (1.0 / RUN_CHUNK)) * RUN_CHUNK
    slot_off = d(jnp.broadcast_to(slot_rows, (8, LANES)).astype(BF16), upper_ref[...])
    where_ = d(tri_ref[...], oa.astype(BF16)) + slot_off[0:1, :]
    poss = [jnp.sum(jnp.where(oh, where_, 0.0), axis=-1, keepdims=True).astype(I32) for oh in ohs]

    p_out = jnp.zeros((TM, LANES), I32)
    g_out = jnp.zeros((TM, LANES), F32)
    for k in range(TOP_K):
        p_out = jnp.where(lane == k, poss[k], p_out)
        g_out = jnp.where(lane == k, gates[k], g_out)
    pos_ref[0] = p_out
    gt_ref[0] = g_out

    srow = lax.broadcasted_iota(I32, (8, LANES), 0)
    meta = jnp.where(srow == 0, hist, jnp.where(srow == 1, run_ref[...], jnp.where(srow == 2, slot_off, 0.0)))
    meta_ref[0] = meta.astype(I32)
    run_ref[...] = run_ref[...] + jnp.floor((hist + (ROW_ALIGN - 1)) * (1.0 / ROW_ALIGN)) * ROW_ALIGN
    cnt_ref[...] = run_ref[...]


def _outproj(om, oc, osw, xu, modsel, lw, tri, upper, t_off):
    B, T, D = xu.shape
    nt = T // TM - t_off
    row = lambda b, i: (b, i, 0)
    Tq = nt * TM
    in_specs = [
        pl.BlockSpec((1, TM, 512), row),
        pl.BlockSpec((1, TM, 256), row),
        pl.BlockSpec((1, TM, 512), row),
        pl.BlockSpec((1, TM, D), lambda b, i: (b, i + t_off, 0)),
        pl.BlockSpec((1, 1, 6, D), lambda b, i: (b, jnp.minimum(i + t_off, 1), 0, 0)),
        _full((512, D)),
        _full((256, D)),
        _full((512, D)),
        _full((1, D)),
        _full((D, LANES)),
        _full((D, LANES)),
        _full((1, LANES)),
        _full((TM, TM)),
        _full((LANES, LANES)),
    ]
    out_shape = [
        jax.ShapeDtypeStruct((B, Tq, D), F32),
        jax.ShapeDtypeStruct((B, Tq, D), BF16),
        jax.ShapeDtypeStruct((B, Tq, LANES), I32),
        jax.ShapeDtypeStruct((B, Tq, LANES), F32),
        jax.ShapeDtypeStruct((B * nt, 8, LANES), I32),
        jax.ShapeDtypeStruct((8, LANES), F32),
    ]
    out_specs = [
        pl.BlockSpec((1, TM, D), row),
        pl.BlockSpec((1, TM, D), row),
        pl.BlockSpec((1, TM, LANES), row),
        pl.BlockSpec((1, TM, LANES), row),
        pl.BlockSpec((1, 8, LANES), lambda b, i: (b * nt + i, 0, 0)),
        pl.BlockSpec((8, LANES), lambda b, i: (0, 0)),
    ]
    return pl.pallas_call(
        _out_kernel,
        grid=(B, nt),
        in_specs=in_specs,
        out_specs=out_specs,
        out_shape=out_shape,
        scratch_shapes=[pltpu.VMEM((8, LANES), F32)],
        compiler_params=_cp(("arbitrary", "arbitrary")),
        name="outproj_router",
    )(om, oc, osw, xu, modsel, lw["w_o1"], lw["w_o2"], lw["w_o3"], lw["n2"], lw["wr_hi"], lw["wr_lo"], lw["br"], tri, upper)


SUB = 8
TILE_SUBLANES = 8
ROW_ALIGN = TILE_SUBLANES // SUB
RUN_CHUNK = 16
SLOT_ROWS = TM * TOP_K + N_EXPERTS * RUN_CHUNK


def _rows(ref, row0, nrows):
    start = row0 * SUB if isinstance(row0, int) else pl.multiple_of(row0 * SUB, TILE_SUBLANES)
    return ref.at[pl.ds(start, nrows * SUB), :]


def _to_tiles(ref, val, nrows):
    for j in range(SUB):
        ref[pl.ds(j, nrows, stride=SUB), :] = val[:, LANES * j:LANES * (j + 1)]


def _from_tiles(ref, nrows):
    return jnp.concatenate([ref[pl.ds(j, nrows, stride=SUB), :] for j in range(SUB)], axis=1)


def _slot_matrix(pos, weights):
    col = lax.broadcasted_iota(I32, (TM, SLOT_ROWS), 1)
    m = jnp.zeros((TM, SLOT_ROWS), F32)
    for k in range(TOP_K):
        m = jnp.where(col == pos[:, k:k + 1], weights[k], m)
    return m.astype(BF16)


def _run_copies(meta_ref, pst_ref, buf_ref, hbm_ref, sem, to_hbm):
    total = jnp.int32(0)
    for e in range(N_EXPERTS):
        n = meta_ref[0, 0, e]
        nch = (n + (RUN_CHUNK - 1)) // RUN_CHUNK
        seg0 = pst_ref[e] + meta_ref[0, 1, e]
        slot0 = meta_ref[0, 2, e]

        def chunk(c, carry):
            a = _rows(buf_ref, slot0 + c * RUN_CHUNK, RUN_CHUNK)
            b = _rows(hbm_ref, seg0 + c * RUN_CHUNK, RUN_CHUNK)
            (pltpu.make_async_copy(a, b, sem) if to_hbm else pltpu.make_async_copy(b, a, sem)).start(priority=e % 2)
            return carry

        lax.fori_loop(0, nch, chunk, 0)
        total = total + nch
    return total


def _run_waits(total, buf_ref, hbm_ref, sem, to_hbm):
    a = _rows(buf_ref, 0, RUN_CHUNK)
    b = _rows(hbm_ref, 0, RUN_CHUNK)

    def one(c, carry):
        (pltpu.make_async_copy(a, b, sem) if to_hbm else pltpu.make_async_copy(b, a, sem)).wait()
        return carry

    lax.fori_loop(0, total, one, 0)


def _run_total(meta_ref):
    total = jnp.int32(0)
    for e in range(N_EXPERTS):
        total = total + (meta_ref[0, 0, e] + (RUN_CHUNK - 1)) // RUN_CHUNK
    return total


def _disp_kernel(pst_ref, cnt_ref, pad_ref, na_ref, meta_ref, hf_ref, pos_ref, xb_ref, srt_ref, zero_ref, tot_ref,
                 sem, zsem, *, n_blocks):
    step = pl.program_id(0) * pl.num_programs(1) + pl.program_id(1)
    n_steps = pl.num_programs(0) * pl.num_programs(1)
    slot = step % 2
    first = step == 0

    @pl.when(first)
    def _():
        zero_ref[...] = jnp.zeros_like(zero_ref)

        def zblock(j):
            return pltpu.make_async_copy(zero_ref, _rows(xb_ref, j * EXPERT_BLOCK, EXPERT_BLOCK), zsem)

        def zb_start(j, carry):
            zblock(j).start()
            return carry

        def zb_wait(j, carry):
            zblock(j).wait()
            return carry

        lax.fori_loop(na_ref[0], n_blocks, zb_start, 0)
        lax.fori_loop(na_ref[0], n_blocks, zb_wait, 0)

        for e in range(N_EXPERTS):
            lo = pst_ref[e] + cnt_ref[e]
            nrow = pad_ref[e] - cnt_ref[e]
            nz = nrow // RUN_CHUNK
            lo1 = lo + nz * RUN_CHUNK
            n1 = (nrow - nz * RUN_CHUNK) // ROW_ALIGN

            def zchunk(c):
                return pltpu.make_async_copy(_rows(zero_ref, 0, RUN_CHUNK), _rows(xb_ref, lo + c * RUN_CHUNK, RUN_CHUNK), zsem)

            def zrow(r):
                return pltpu.make_async_copy(_rows(zero_ref, 0, ROW_ALIGN), _rows(xb_ref, lo1 + r * ROW_ALIGN, ROW_ALIGN), zsem)

            def zc_start(c, carry):
                zchunk(c).start()
                return carry

            def zc_wait(c, carry):
                zchunk(c).wait()
                return carry

            def zr_start(r, carry):
                zrow(r).start()
                return carry

            def zr_wait(r, carry):
                zrow(r).wait()
                return carry

            lax.fori_loop(0, nz, zc_start, 0)
            lax.fori_loop(0, n1, zr_start, 0)
            lax.fori_loop(0, nz, zc_wait, 0)
            lax.fori_loop(0, n1, zr_wait, 0)

    ones = [1.0] * TOP_K
    q = _slot_matrix(pos_ref[0], ones)
    srt = lax.dot_general(q, hf_ref[0], (((0,), (0,)), ((), ())), preferred_element_type=F32)
    _to_tiles(srt_ref.at[slot], srt, SLOT_ROWS)

    @pl.when(step > 0)
    def _():
        _run_waits(tot_ref[1 - slot], srt_ref.at[1 - slot], xb_ref, sem.at[1 - slot], True)

    total = _run_copies(meta_ref, pst_ref, srt_ref.at[slot], xb_ref, sem.at[slot], True)
    tot_ref[slot] = total

    @pl.when(step == n_steps - 1)
    def _():
        _run_waits(total, srt_ref.at[slot], xb_ref, sem.at[slot], True)


def _dispatch(pstart, counts, padded, n_act, meta, hf, pos, n_buf):
    B, Tq, D = hf.shape
    nt = Tq // TM
    assert D == SUB * LANES
    grid_spec = pltpu.PrefetchScalarGridSpec(
        num_scalar_prefetch=4,
        grid=(B, nt),
        in_specs=[
            pl.BlockSpec((1, 8, LANES), lambda b, i, *_: (b * nt + i, 0, 0), memory_space=pltpu.SMEM),
            pl.BlockSpec((1, TM, D), lambda b, i, *_: (b, i, 0)),
            pl.BlockSpec((1, TM, LANES), lambda b, i, *_: (b, i, 0)),
        ],
        out_specs=pl.BlockSpec(memory_space=pl.ANY),
        scratch_shapes=[pltpu.VMEM((2, SLOT_ROWS * SUB, LANES), F32), pltpu.VMEM((EXPERT_BLOCK * SUB, LANES), F32),
                        pltpu.SMEM((2,), I32), pltpu.SemaphoreType.DMA((2,)), pltpu.SemaphoreType.DMA],
    )
    return pl.pallas_call(
        functools.partial(_disp_kernel, n_blocks=n_buf // EXPERT_BLOCK),
        grid_spec=grid_spec,
        out_shape=jax.ShapeDtypeStruct((n_buf * SUB, LANES), F32),
        compiler_params=_cp(("arbitrary", "arbitrary")),
        name="dispatch",
    )(pstart, counts, padded, n_act, meta, hf, pos)


W_ROWS = 512
W_DELAY = 2


def _exp_kernel(be_ref, bx_ref, na_ref, nx_ref, ps_ref, pr_ref, npv_ref, x_ref, b1a_ref, b1b_ref, b2_ref, w1_hbm, w2_hbm,
                o_ref, st1, st2, w1a_s, w1b_s, w2_s, sem, *, e_off):
    j = pl.program_id(0)
    active = j < na_ref[0]
    e_cur = be_ref[j]
    p = pr_ref[e_cur]
    k = bx_ref[j] - ps_ref[e_cur]
    nxt = nx_ref[e_cur]
    n_prev = npv_ref[e_cur]
    n_slices = st1.shape[0] // W_ROWS

    def copies(e):
        return (pltpu.make_async_copy(w1_hbm.at[e], st1, sem.at[0]), pltpu.make_async_copy(w2_hbm.at[e], st2, sem.at[1]))

    def fetch(e):
        for cp in copies(e):
            cp.start(priority=1)

    def fetch_wait():
        for cp in copies(0):
            cp.wait()

    def convert(slot, s):
        r0 = pl.multiple_of(s * W_ROWS, W_ROWS)
        lane = lax.broadcasted_iota(I32, (W_ROWS, LANES), 1)
        lo = lane < 64
        idx = jnp.where(lo, 2 * lane, 2 * (lane - 64) + 1)
        for c in range(st1.shape[1] // (2 * LANES)):
            a = st1[pl.ds(r0, W_ROWS), 2 * LANES * c:2 * LANES * c + LANES]
            b = st1[pl.ds(r0, W_ROWS), 2 * LANES * c + LANES:2 * LANES * (c + 1)]
            pa = jnp.take_along_axis(a, idx, axis=1)
            pb = jnp.take_along_axis(b, idx, axis=1)
            ev = jnp.where(lo, pa, pltpu.roll(pb, 64, axis=1))
            od = jnp.where(lo, pltpu.roll(pa, 64, axis=1), pb)
            w1a_s[slot, pl.ds(r0, W_ROWS), LANES * c:LANES * (c + 1)] = ev.astype(BF16)
            w1b_s[slot, pl.ds(r0, W_ROWS), LANES * c:LANES * (c + 1)] = od.astype(BF16)
        w2_s[slot, pl.ds(r0, W_ROWS), :] = st2[pl.ds(r0, W_ROWS), :].astype(BF16)

    def convert_range(slot, s0):
        def one(s, carry):
            convert(slot, s)
            return carry

        lax.fori_loop(s0, n_slices, one, 0)

    @pl.when(j == 0)
    def _():
        fetch(be_ref[0] + e_off)
        fetch_wait()
        convert_range(0, 0)

    @pl.when(active & (k == 0) & (j > 0))
    def _():
        @pl.when(n_prev <= W_DELAY)
        def _():
            fetch_wait()

        convert_range(p, jnp.clip(n_prev - W_DELAY, 0, n_slices))

    @pl.when(active & (k == 0) & (nxt >= 0))
    def _():
        fetch(nxt + e_off)

    conv = active & (nxt >= 0) & (k >= W_DELAY) & (k < W_DELAY + n_slices)

    @pl.when(conv & (k == W_DELAY))
    def _():
        fetch_wait()

    def ffn(with_convert):
        if with_convert:
            convert(1 - p, k - W_DELAY)
        x = _from_tiles(x_ref, EXPERT_BLOCK).astype(BF16)
        ug = jnp.dot(x, w1a_s[p], preferred_element_type=F32) + b1a_ref[0]
        ul = jnp.dot(x, w1b_s[p], preferred_element_type=F32) + b1b_ref[0]
        xg = jnp.minimum(ug, SWIGLU_LIMIT)
        xl = jnp.clip(ul, -SWIGLU_LIMIT, SWIGLU_LIMIT)
        act = xg * jax.nn.sigmoid(SWIGLU_ALPHA * xg) * (xl + 1.0)
        y = jnp.dot(act.astype(BF16), w2_s[p], preferred_element_type=F32) + b2_ref[0]
        _to_tiles(o_ref, y, EXPERT_BLOCK)

    pl.when(conv)(functools.partial(ffn, True))
    pl.when(active & jnp.logical_not(conv))(functools.partial(ffn, False))

    @pl.when(jnp.logical_not(active))
    def _():
        o_ref[...] = jnp.zeros_like(o_ref)


def _experts(sched, xb, lw):
    nb = xb.shape[0] // (EXPERT_BLOCK * SUB)
    De, D = lw["w2"].shape[1:]
    assert De == D and D % W_ROWS == 0
    wmap = lambda j, be, *_: (be[j], 0, 0)
    grid_spec = pltpu.PrefetchScalarGridSpec(
        num_scalar_prefetch=len(sched),
        grid=(nb,),
        in_specs=[
            pl.BlockSpec((EXPERT_BLOCK * SUB, LANES), lambda j, be, bx, *_: (bx[j], 0)),
            pl.BlockSpec((1, 1, De), wmap),
            pl.BlockSpec((1, 1, De), wmap),
            pl.BlockSpec((1, 1, D), wmap),
            pl.BlockSpec(memory_space=pl.ANY),
            pl.BlockSpec(memory_space=pl.ANY),
        ],
        out_specs=pl.BlockSpec((EXPERT_BLOCK * SUB, LANES), lambda j, *_: (j, 0)),
        scratch_shapes=[pltpu.VMEM((D, 2 * De), F32), pltpu.VMEM((De, D), F32),
                        pltpu.VMEM((2, D, De), BF16), pltpu.VMEM((2, D, De), BF16), pltpu.VMEM((2, De, D), BF16),
                        pltpu.SemaphoreType.DMA((2,))],
    )
    return pl.pallas_call(
        functools.partial(_exp_kernel, e_off=lw["e_off"]),
        grid_spec=grid_spec,
        out_shape=jax.ShapeDtypeStruct(xb.shape, F32),
        compiler_params=pltpu.CompilerParams(dimension_semantics=("arbitrary",), vmem_limit_bytes=EXPERT_VMEM_LIMIT),
        name="experts",
    )(*sched, xb, lw["b1a"], lw["b1b"], lw["b2"], lw["w1"], lw["w2"])


def _comb_kernel(pst_ref, meta_ref, meta_next_ref, pos_ref, gt_ref, xn_ref, mod_ref, yb_ref, o_ref, buf_ref, sem):
    step = pl.program_id(0) * pl.num_programs(1) + pl.program_id(1)
    n_steps = pl.num_programs(0) * pl.num_programs(1)
    slot = step % 2

    @pl.when(step == 0)
    def _():
        buf_ref[...] = jnp.zeros_like(buf_ref)
        _run_copies(meta_ref, pst_ref, buf_ref.at[0], yb_ref, sem.at[0], False)

    @pl.when(step < n_steps - 1)
    def _():
        _run_copies(meta_next_ref, pst_ref, buf_ref.at[1 - slot], yb_ref, sem.at[1 - slot], False)

    gt = gt_ref[0]
    g = _slot_matrix(pos_ref[0], [gt[:, k:k + 1] for k in range(TOP_K)])
    _run_waits(_run_total(meta_ref), buf_ref.at[slot], yb_ref, sem.at[slot], False)
    rows = _from_tiles(buf_ref.at[slot], SLOT_ROWS).astype(BF16)
    y = jnp.dot(g, rows, preferred_element_type=F32)
    g2 = mod_ref[0, 0][5:6]
    o_ref[0] = xn_ref[0] + g2 * y


def _combine(pstart, meta, pos, gates, xn, modsel, yb, t_off):
    B, Tq, D = xn.shape
    nt = Tq // TM
    grid_spec = pltpu.PrefetchScalarGridSpec(
        num_scalar_prefetch=1,
        grid=(B, nt),
        in_specs=[
            pl.BlockSpec((1, 8, LANES), lambda b, i, *_: (b * nt + i, 0, 0), memory_space=pltpu.SMEM),
            pl.BlockSpec((1, 8, LANES), lambda b, i, *_: (jnp.minimum(b * nt + i + 1, B * nt - 1), 0, 0),
                         memory_space=pltpu.SMEM),
            pl.BlockSpec((1, TM, LANES), lambda b, i, *_: (b, i, 0)),
            pl.BlockSpec((1, TM, LANES), lambda b, i, *_: (b, i, 0)),
            pl.BlockSpec((1, TM, D), lambda b, i, *_: (b, i, 0)),
            pl.BlockSpec((1, 1, 6, D), lambda b, i, *_: (b, jnp.minimum(i + t_off, 1), 0, 0)),
            pl.BlockSpec(memory_space=pl.ANY),
        ],
        out_specs=pl.BlockSpec((1, TM, D), lambda b, i, *_: (b, i, 0)),
        scratch_shapes=[pltpu.VMEM((2, SLOT_ROWS * SUB, LANES), F32), pltpu.SemaphoreType.DMA((2,))],
    )
    return pl.pallas_call(
        _comb_kernel,
        grid_spec=grid_spec,
        out_shape=jax.ShapeDtypeStruct((B, Tq, D), F32),
        compiler_params=_cp(("arbitrary", "arbitrary")),
        name="combine",
    )(pstart, meta, meta, pos, gates, xn, modsel, yb)


def _take_cols(w, cols):
    cols = np.asarray(cols)
    out = jnp.take(w, jnp.asarray(np.maximum(cols, 0)), axis=-1)
    return jnp.where(jnp.asarray(cols >= 0), out, 0.0)


def _in_cols():
    pi = _PI
    cols = list(range(0, 256))
    cols += [256 + i for i in range(64)] + [256 + pi[i] for i in range(64)]
    for g in range(SWA_KV_HEADS):
        base = 320 + 64 * g
        cols += [base + i for i in range(64)] + [base + pi[i] for i in range(64)]
    for g in range(SWA_KV_HEADS):
        base = 448 + 64 * g
        cols += [base + i for i in range(64)] + [-1] * 64
    cols += list(range(576, 960))
    for h in range(SWA_HEADS):
        base = 960 + 64 * h
        cols += [base + i for i in range(64)] + [base + pi[i] for i in range(64)]
    cols += list(range(1216, 1728))
    return cols


def _layer_weights(l, a):
    pi = _PI
    lw = {}
    lw["n1"] = a["norm1_g"][l][None, :]
    lw["n2"] = a["norm2_g"][l][None, :]
    lw["w_in"] = _take_cols(a["w_in"][l], _in_cols()).astype(BF16)
    lw["kvg"] = a["mla_kv_norm"][l][None, :]
    lw["qg"] = a["mla_q_norm"][l][None, :]
    uq_cols = []
    for h in range(MLA_HEADS):
        base = MLA_QK * h
        uq_cols += [base + i for i in range(128)] + [base + 128 + i for i in range(64)] + [base + 128 + pi[i] for i in range(64)]
    lw["w_uq"] = _take_cols(a["mla_w_uq"][l], uq_cols).astype(BF16)
    ukv_cols = [256 * h + i for h in range(MLA_HEADS) for i in range(128)]
    ukv_cols += [256 * h + 128 + i for h in range(MLA_HEADS) for i in range(128)]
    lw["w_ukv"] = _take_cols(a["mla_w_ukv"][l], ukv_cols).astype(BF16)
    gq = a["mla_q_head_norm"][l]
    lw["gq"] = (jnp.concatenate([gq[:128], gq[128:], gq[128:][pi]]) * (MLA_QK ** -0.5 * LOG2E))[None, :]
    gk = a["mla_k_head_norm"][l]
    lw["gkn"] = gk[:128][None, :]
    lw["gkp"] = jnp.concatenate([gk[128:], gk[128:][pi]])[None, :]
    sq = a["swa_q_norm"][l]
    lw["gsq"] = (jnp.concatenate([sq, sq[pi]]) * (SWA_HEAD_DIM ** -0.5))[None, :]
    sk = a["swa_k_norm"][l]
    lw["gsk"] = jnp.concatenate([sk, sk[pi]])[None, :]
    lw["conv_w"] = a["conv_w"][l]
    lw["conv_b"] = a["conv_b"][l][None, :]
    lw["conv_g"] = a["conv_ln_g"][l][None, :]
    lw["conv_bb"] = a["conv_ln_b"][l][None, :]
    lw["sink"] = a["swa_sink"][l]
    wo = a["w_out"][l]
    lw["w_o1"] = wo[0:512].astype(BF16)
    lw["w_o2"] = wo[512:768].astype(BF16)
    o3 = wo[768:1024].reshape(SWA_HEADS, SWA_HEAD_DIM, -1)
    lw["w_o3"] = jnp.concatenate([o3, jnp.zeros_like(o3)], axis=1).reshape(SWA_HEADS * 128, -1).astype(BF16)
    wr = jnp.pad(a["router_w"][l], ((0, 0), (0, LANES - N_EXPERTS)))
    lw["wr_hi"] = wr.astype(BF16)
    lw["wr_lo"] = (wr - lw["wr_hi"].astype(F32)).astype(BF16)
    lw["br"] = jnp.pad(a["router_b"][l], (0, LANES - N_EXPERTS), constant_values=NEG)[None, :]
    lw["w1"] = a["exp_w1"].reshape((-1,) + a["exp_w1"].shape[2:])
    lw["e_off"] = l * a["exp_w1"].shape[1]
    b1 = a["exp_b1"][l]
    lw["b1a"] = b1[:, None, 0::2]
    lw["b1b"] = b1[:, None, 1::2]
    lw["w2"] = a["exp_w2"].reshape((-1,) + a["exp_w2"].shape[2:])
    lw["b2"] = a["exp_b2"][l][:, None, :]
    return lw


def _rope_tables(n_ctx, n_lat):
    q = MLA_ROPE // 4
    n = jnp.arange(n_lat, dtype=I32)
    row = (n // GRID_W).astype(F32)
    col = (n % GRID_W).astype(F32)
    inv = ROPE_BASE ** (-jnp.arange(q, dtype=F32) / q)
    ang_r = row[:, None] * inv
    ang_c = col[:, None] * inv
    cos = jnp.concatenate([jnp.cos(ang_r), jnp.cos(ang_r), jnp.cos(ang_c), jnp.cos(ang_c)], axis=1)
    sin = jnp.concatenate([-jnp.sin(ang_r), jnp.sin(ang_r), -jnp.sin(ang_c), jnp.sin(ang_c)], axis=1)
    cos = jnp.concatenate([jnp.ones((n_ctx, 64), F32), cos], axis=0)
    sin = jnp.concatenate([jnp.zeros((n_ctx, 64), F32), sin], axis=0)
    z = jnp.zeros_like(cos)
    return jnp.concatenate([cos, z], axis=1), jnp.concatenate([sin, z], axis=1)


def _routing_tables(cnt_f, n_blocks):
    counts = cnt_f[0, :N_EXPERTS].astype(I32)
    padded = (counts + (RUN_CHUNK - 1) + EXPERT_BLOCK - 1) // EXPERT_BLOCK * EXPERT_BLOCK
    padded = jnp.where(counts > 0, padded, 0)
    pend = jnp.cumsum(padded)
    pstart = pend - padded
    n_act = pend[-1] // EXPERT_BLOCK
    blk = jnp.minimum(jnp.arange(n_blocks, dtype=I32), n_act - 1)
    blk_e = jnp.sum((pend[None, :] <= (blk * EXPERT_BLOCK)[:, None]).astype(I32), axis=1)
    blk_e = jnp.minimum(blk_e, N_EXPERTS - 1)
    nbk = padded // EXPERT_BLOCK
    has = nbk > 0
    ids = jnp.arange(N_EXPERTS, dtype=I32)
    later = has[None, :] & (ids[None, :] > ids[:, None])
    nxt_e = jnp.min(jnp.where(later, ids[None, :], N_EXPERTS), axis=1)
    nxt_e = jnp.where(nxt_e == N_EXPERTS, -1, nxt_e)
    earlier = has[None, :] & (ids[None, :] < ids[:, None])
    prv_e = jnp.max(jnp.where(earlier, ids[None, :], -1), axis=1)
    nb_prev_e = jnp.sum(jnp.where(ids[None, :] == prv_e[:, None], nbk[None, :], 0), axis=1)
    set_e = (jnp.cumsum(has.astype(I32)) - 1) % 2
    sched = (blk_e, blk, n_act.reshape(1), nxt_e, pstart // EXPERT_BLOCK, set_e, nb_prev_e)
    sched = tuple(s.astype(I32) for s in sched)
    return counts, padded.astype(I32), pstart.astype(I32), sched


def kernel(x, c, ctx, c_ctx, norm1_g, norm2_g, w_ada, b_ada, w_in, mla_q_norm, mla_kv_norm, mla_w_uq, mla_w_ukv, mla_q_head_norm, mla_k_head_norm, conv_w, conv_b, conv_ln_g, conv_ln_b, swa_q_norm, swa_k_norm, swa_sink, w_out, router_w, router_b, exp_w1, exp_b1, exp_w2, exp_b2):
    a = dict(norm1_g=norm1_g, norm2_g=norm2_g, w_in=w_in, mla_q_norm=mla_q_norm, mla_kv_norm=mla_kv_norm,
             mla_w_uq=mla_w_uq, mla_w_ukv=mla_w_ukv, mla_q_head_norm=mla_q_head_norm, mla_k_head_norm=mla_k_head_norm,
             conv_w=conv_w, conv_b=conv_b, conv_ln_g=conv_ln_g, conv_ln_b=conv_ln_b, swa_q_norm=swa_q_norm,
             swa_k_norm=swa_k_norm, swa_sink=swa_sink, w_out=w_out, router_w=router_w, router_b=router_b,
             exp_w1=exp_w1, exp_b1=exp_b1, exp_w2=exp_w2, exp_b2=exp_b2)
    B, S, D = x.shape
    n_ctx = ctx.shape[1]
    depth = w_ada.shape[0]
    assert n_ctx == TM and S % TM == 0 and B + 1 <= 16
    T = n_ctx + S

    s_in = jnp.zeros((16, D), F32).at[:B].set(c).at[B].set(c_ctx)
    mods = _ada(s_in, w_ada, b_ada)
    cos_t, sin_t = _rope_tables(n_ctx, S)
    tri = jnp.tril(jnp.ones((TM, TM), F32), -1).astype(BF16)
    upper = jnp.triu(jnp.ones((LANES, LANES), F32), 1).astype(BF16)

    xu = jnp.concatenate([ctx, x], axis=1)
    for l in range(depth):
        last = l == depth - 1
        t_off = 1 if last else 0
        lw = _layer_weights(l, a)
        m = mods[l].reshape(16, 6, D)
        modsel = jnp.stack([jnp.broadcast_to(m[B], (B, 6, D)), m[:B]], axis=1)

        qm, km, vm, qs, ks, vs, u = _prep(xu, modsel, lw, cos_t, sin_t, 0)
        om = _mla(qm, km, vm, t_off)
        oc = _conv(u, lw, t_off)
        osw = _swa(lw["sink"], qs, ks, vs, t_off)
        xn, hf, pos, gt_o, meta, cnt = _outproj(om, oc, osw, xu, modsel, lw, tri, upper, t_off)

        n_tok = B * (T - t_off * TM)
        nk = n_tok * TOP_K
        n_align = (n_tok // TM) * N_EXPERTS * (ROW_ALIGN - 1)
        n_buf = -(-(nk + n_align + N_EXPERTS * (RUN_CHUNK - 1 + EXPERT_BLOCK - 1)) // EXPERT_BLOCK) * EXPERT_BLOCK
        counts, padded, pstart, sched = _routing_tables(cnt, n_buf // EXPERT_BLOCK)
        xb = _dispatch(pstart, counts, padded, sched[2], meta, hf, pos, n_buf)
        yb = _experts(sched, xb, lw)
        xu = _combine(pstart, meta, pos, gt_o, xn, modsel, yb, t_off)
    return xu
```

```python
import functools

import numpy as np
import jax
import jax.numpy as jnp
from jax import lax
from jax.experimental import pallas as pl
from jax.experimental.pallas import tpu as pltpu

F32 = jnp.float32
BF16 = jnp.bfloat16
I32 = jnp.int32

GRID_W = 64
ROPE_BASE = 10000.0
EPS = 1e-6
MLA_HEADS = 4
MLA_NOPE = 128
MLA_ROPE = 64
MLA_V = 128
MLA_QK = MLA_NOPE + MLA_ROPE
MLA_Q_RANK = 384
MLA_KV_RANK = 256
CONV_CH = 256
CONV_WIDTH = 31
SWA_HEADS = 4
SWA_KV_HEADS = 2
SWA_HEAD_DIM = 64
SWA_WINDOW = 128
N_EXPERTS = 32
TOP_K = 4
SWIGLU_LIMIT = 7.0
SWIGLU_ALPHA = 1.702
EXPERT_BLOCK = 512

LANES = 128
TM = 256
CONV_HALO = 16
VMEM_LIMIT = 48 * 1024 * 1024
EXPERT_VMEM_LIMIT = 56 * 1024 * 1024
NEG = -1e30
LOG2E = 1.4426950408889634

_PI = np.array([i + 16 if (i % 32) < 16 else i - 16 for i in range(64)])


def _cp(sem):
    return pltpu.CompilerParams(dimension_semantics=sem, vmem_limit_bytes=VMEM_LIMIT)


def _full(shape):
    n = len(shape)
    return pl.BlockSpec(shape, lambda *a, _n=n: (0,) * _n)


def _split(x):
    hi = x.astype(BF16)
    lo = (x - hi.astype(F32)).astype(BF16)
    return hi, lo


def _dot3(a, b):
    ah, al = _split(a)
    bh, bl = _split(b)
    d = functools.partial(jnp.dot, preferred_element_type=F32)
    return d(ah, bh) + d(ah, bl) + d(al, bh)


def _ada_kernel(s_ref, w_ref, b_ref, o_ref):
    s = s_ref[...]
    s = s * jax.nn.sigmoid(s)
    o_ref[0] = _dot3(s, w_ref[0]) + b_ref[0]


def _ada(s_in, w_ada, b_ada):
    L, D, N = w_ada.shape
    tn = 1536
    return pl.pallas_call(
        _ada_kernel,
        grid=(L, N // tn),
        in_specs=[
            pl.BlockSpec((16, D), lambda l, j: (0, 0)),
            pl.BlockSpec((1, D, tn), lambda l, j: (l, 0, j)),
            pl.BlockSpec((1, 1, tn), lambda l, j: (l, 0, j)),
        ],
        out_specs=pl.BlockSpec((1, 16, tn), lambda l, j: (l, 0, j)),
        out_shape=jax.ShapeDtypeStruct((L, 16, N), F32),
        compiler_params=_cp(("arbitrary", "arbitrary")),
        name="ada",
    )(s_in, w_ada, b_ada.reshape(L, 1, N))


def _rope(x, c, s):
    return x * c + pltpu.roll(x, 64, axis=1) * s


def _stream_specs(D, t_off, shift):
    return [pl.BlockSpec((1, TM, D), lambda b, i, *_: (b, 0, 0)),
            pl.BlockSpec((1, TM, D), lambda b, i, *_: (b, jnp.maximum(i + t_off - shift, 0), 0))]


def _stream_tile(first_ref, rest_ref, t_off):
    return jnp.where(pl.program_id(1) + t_off == 0, first_ref[0], rest_ref[0])


def _prep_kernel(xa_ref, xb_ref, mod_ref, n1_ref, win_ref, kvg_ref, wukv_ref, qg_ref, wuq_ref, gq_ref, gkn_ref, gkp_ref,
                 gsq_ref, gsk_ref, cos_ref, sin_ref, qm_ref, km_ref, vm_ref, qs_ref, ks_ref, vs_ref, u_ref, *, t_off):
    x = _stream_tile(xa_ref, xb_ref, t_off)
    mod = mod_ref[0, 0]
    sh, sc = mod[0:1], mod[1:2]
    y = x * lax.rsqrt(jnp.mean(x * x, axis=-1, keepdims=True) + EPS) * n1_ref[...]
    h = y * (1.0 + sc) + sh
    p = jnp.dot(h.astype(BF16), win_ref[...], preferred_element_type=F32)
    c = cos_ref[...]
    s = sin_ref[...]
    lane = lax.broadcasted_iota(I32, (TM, LANES), 1)

    def ss_lo(v):
        return 0.5 * jnp.sum(v * v, axis=-1, keepdims=True)

    ckv = p[:, 0:256]
    ckvn = ckv * lax.rsqrt(jnp.mean(ckv * ckv, axis=-1, keepdims=True) + EPS) * kvg_ref[...]
    kv = jnp.dot(ckvn.astype(BF16), wukv_ref[...], preferred_element_type=F32)
    kpe = p[:, 256:384]
    ss_pe = ss_lo(kpe)
    kpe_rot = _rope(kpe * gkp_ref[...], c, s)
    for hh in range(MLA_HEADS):
        kn = kv[:, 128 * hh:128 * hh + 128]
        r = lax.rsqrt((jnp.sum(kn * kn, axis=-1, keepdims=True) + ss_pe) * (1.0 / MLA_QK) + EPS)
        km_ref[0, hh, :, 0:128] = (kn * r * gkn_ref[...]).astype(BF16)
        km_ref[0, hh, :, 128:256] = (kpe_rot * r).astype(BF16)
        vm_ref[0, hh, :, 0:128] = kv[:, 512 + 128 * hh:640 + 128 * hh].astype(BF16)
        vm_ref[0, hh, :, 128:256] = (lane == 0).astype(BF16)

    cq = p[:, 896:1280]
    cqn = cq * lax.rsqrt(jnp.mean(cq * cq, axis=-1, keepdims=True) + EPS) * qg_ref[...]
    q = jnp.dot(cqn.astype(BF16), wuq_ref[...], preferred_element_type=F32)
    gq = gq_ref[...]
    for hh in range(MLA_HEADS):
        qn = q[:, 256 * hh:256 * hh + 128]
        qp = q[:, 256 * hh + 128:256 * hh + 256]
        r = lax.rsqrt((jnp.sum(qn * qn, axis=-1, keepdims=True) + ss_lo(qp)) * (1.0 / MLA_QK) + EPS)
        qm_ref[0, hh, :, 0:128] = (qn * r * gq[:, 0:128]).astype(BF16)
        qm_ref[0, hh, :, 128:256] = _rope(qp * r * gq[:, 128:256], c, s).astype(BF16)

    for g in range(SWA_KV_HEADS):
        xk = p[:, 384 + 128 * g:512 + 128 * g]
        r = lax.rsqrt(ss_lo(xk) * (1.0 / SWA_HEAD_DIM) + EPS)
        ks_ref[0, g] = _rope(xk * r * gsk_ref[...], c, s).astype(BF16)
        vs_ref[0, g] = p[:, 640 + 128 * g:768 + 128 * g].astype(BF16)
    for hh in range(SWA_HEADS):
        xq = p[:, 1280 + 128 * hh:1408 + 128 * hh]
        r = lax.rsqrt(ss_lo(xq) * (1.0 / SWA_HEAD_DIM) + EPS)
        qs_ref[0, hh] = _rope(xq * r * gsq_ref[...], c, s).astype(BF16)

    u_ref[0] = p[:, 1792:2048] * jax.nn.sigmoid(p[:, 2048:2304])


def _prep(first, rest, shift, modsel, lw, cos_t, sin_t, t_off):
    B, _, D = first.shape
    T = rest.shape[1] + shift * TM
    nt = T // TM - t_off
    ncol = lw["w_in"].shape[1]
    row = lambda b, i: (b, i + t_off, 0)
    head = lambda b, i: (b, 0, i + t_off, 0)
    in_specs = _stream_specs(D, t_off, shift) + [
        pl.BlockSpec((1, 1, 6, D), lambda b, i: (b, jnp.minimum(i + t_off, 1), 0, 0)),
        _full((1, D)),
        _full((D, ncol)),
        _full((1, MLA_KV_RANK)),
        _full((MLA_KV_RANK, 1024)),
        _full((1, MLA_Q_RANK)),
        _full((MLA_Q_RANK, 1024)),
        _full((1, 256)),
        _full((1, 128)),
        _full((1, 128)),
        _full((1, 128)),
        _full((1, 128)),
        pl.BlockSpec((TM, LANES), lambda b, i: (i + t_off, 0)),
        pl.BlockSpec((TM, LANES), lambda b, i: (i + t_off, 0)),
    ]
    out_shape = [
        jax.ShapeDtypeStruct((B, MLA_HEADS, T, 256), BF16),
        jax.ShapeDtypeStruct((B, MLA_HEADS, T, 256), BF16),
        jax.ShapeDtypeStruct((B, MLA_HEADS, T, 256), BF16),
        jax.ShapeDtypeStruct((B, SWA_HEADS, T, 128), BF16),
        jax.ShapeDtypeStruct((B, SWA_KV_HEADS, T, 128), BF16),
        jax.ShapeDtypeStruct((B, SWA_KV_HEADS, T, 128), BF16),
        jax.ShapeDtypeStruct((B, T, CONV_CH), F32),
    ]
    out_specs = [
        pl.BlockSpec((1, MLA_HEADS, TM, 256), head),
        pl.BlockSpec((1, MLA_HEADS, TM, 256), head),
        pl.BlockSpec((1, MLA_HEADS, TM, 256), head),
        pl.BlockSpec((1, SWA_HEADS, TM, 128), head),
        pl.BlockSpec((1, SWA_KV_HEADS, TM, 128), head),
        pl.BlockSpec((1, SWA_KV_HEADS, TM, 128), head),
        pl.BlockSpec((1, TM, CONV_CH), row),
    ]
    return pl.pallas_call(
        functools.partial(_prep_kernel, t_off=t_off),
        grid=(B, nt),
        in_specs=in_specs,
        out_specs=out_specs,
        out_shape=out_shape,
        compiler_params=_cp(("arbitrary", "arbitrary")),
        name="prep",
    )(first, rest, modsel, lw["n1"], lw["w_in"], lw["kvg"], lw["w_ukv"], lw["qg"], lw["w_uq"], lw["gq"], lw["gkn"], lw["gkp"],
      lw["gsq"], lw["gsk"], cos_t, sin_t)


KEY_CHUNK = 256


def _mla_kernel(q_ref, k_ref, v_ref, o_ref, s_ref, p_ref, *, q_off, n_keys):
    qi = pl.program_id(1) + q_off

    def attend(nk):
        for h in range(MLA_HEADS):
            q = q_ref[0, h]
            macc = jnp.full((TM, LANES), -jnp.inf, F32)
            for c in range(nk // KEY_CHUNK):
                k = k_ref[0, h, c * KEY_CHUNK:(c + 1) * KEY_CHUNK, :]
                s = lax.dot_general(q, k, (((1,), (1,)), ((), ())), preferred_element_type=F32)
                s_ref[h, :, c * KEY_CHUNK:(c + 1) * KEY_CHUNK] = s
                for j in range(KEY_CHUNK // LANES):
                    macc = jnp.maximum(macc, s[:, j * LANES:(j + 1) * LANES])
            m = jnp.max(macc, axis=-1, keepdims=True)
            for c in range(nk // KEY_CHUNK):
                p = jnp.exp2(s_ref[h, :, c * KEY_CHUNK:(c + 1) * KEY_CHUNK] - m)
                p_ref[h, :, c * KEY_CHUNK:(c + 1) * KEY_CHUNK] = p.astype(BF16)
            ol = jnp.dot(p_ref[h, :, 0:nk], v_ref[0, h, 0:nk, :], preferred_element_type=F32)
            o = ol[:, 0:MLA_V] / ol[:, MLA_V:MLA_V + 1]
            o_ref[0, :, 128 * h:128 * h + 128] = o.astype(BF16)

    if q_off == 0:
        @pl.when(qi == 0)
        def _():
            attend(TM)

        @pl.when(qi > 0)
        def _():
            attend(n_keys)
    else:
        attend(n_keys)


def _mla(qm, km, vm, t_off):
    B, H, T, _ = qm.shape
    nt = T // TM - t_off
    return pl.pallas_call(
        functools.partial(_mla_kernel, q_off=t_off, n_keys=T),
        grid=(B, nt),
        in_specs=[
            pl.BlockSpec((1, H, TM, 256), lambda b, i: (b, 0, i + t_off, 0)),
            pl.BlockSpec((1, H, T, 256), lambda b, i: (b, 0, 0, 0)),
            pl.BlockSpec((1, H, T, 256), lambda b, i: (b, 0, 0, 0)),
        ],
        out_specs=pl.BlockSpec((1, TM, H * MLA_V), lambda b, i: (b, i, 0)),
        out_shape=jax.ShapeDtypeStruct((B, nt * TM, H * MLA_V), BF16),
        scratch_shapes=[pltpu.VMEM((H, TM, T), F32), pltpu.VMEM((H, TM, T), BF16)],
        compiler_params=_cp(("arbitrary", "arbitrary")),
        name="mla_attn",
    )(qm, km, vm)


SWA_SPAN = TM + 2 * SWA_WINDOW


def _swa_kernel(sink_ref, q_ref, k_ref, v_ref, o_ref, *, q_off, n_rows):
    qi = pl.program_id(1) + q_off
    nt_dims = (((1,), (1,)), ((), ()))

    def latent():
        start = jnp.clip(qi * TM - SWA_WINDOW, 0, n_rows - SWA_SPAN)
        start = pl.multiple_of(start, SWA_WINDOW)
        qpos = qi * TM + lax.broadcasted_iota(I32, (TM, SWA_SPAN), 0)
        kpos = start + lax.broadcasted_iota(I32, (TM, SWA_SPAN), 1)
        valid = (jnp.abs(qpos - kpos) <= SWA_WINDOW) & (kpos >= TM)
        for h in range(SWA_HEADS):
            g = h // (SWA_HEADS // SWA_KV_HEADS)
            q = q_ref[0, h]
            kl = k_ref[0, g, pl.ds(start, SWA_SPAN), :]
            vl = v_ref[0, g, pl.ds(start, SWA_SPAN), :]
            kc = k_ref[0, g, 0:TM, :]
            vc = v_ref[0, g, 0:TM, :]
            sl = lax.dot_general(q, kl, nt_dims, preferred_element_type=F32)
            sl = jnp.where(valid, sl, NEG)
            scx = lax.dot_general(q, kc, nt_dims, preferred_element_type=F32)
            sink = sink_ref[h]
            m = jnp.maximum(jnp.maximum(jnp.max(sl, axis=-1, keepdims=True), jnp.max(scx, axis=-1, keepdims=True)), sink)
            pl_ = jnp.exp(sl - m)
            pc = jnp.exp(scx - m)
            l = jnp.sum(pl_, axis=-1, keepdims=True) + jnp.sum(pc, axis=-1, keepdims=True) + jnp.exp(sink - m)
            o = (jnp.dot(pl_.astype(BF16), vl, preferred_element_type=F32)
                 + jnp.dot(pc.astype(BF16), vc, preferred_element_type=F32)) / l
            o_ref[0, :, 128 * h:128 * h + 128] = o.astype(BF16)

    def context():
        for h in range(SWA_HEADS):
            g = h // (SWA_HEADS // SWA_KV_HEADS)
            q = q_ref[0, h]
            kc = k_ref[0, g, 0:TM, :]
            vc = v_ref[0, g, 0:TM, :]
            scx = lax.dot_general(q, kc, nt_dims, preferred_element_type=F32)
            sink = sink_ref[h]
            m = jnp.maximum(jnp.max(scx, axis=-1, keepdims=True), sink)
            pc = jnp.exp(scx - m)
            l = jnp.sum(pc, axis=-1, keepdims=True) + jnp.exp(sink - m)
            o = jnp.dot(pc.astype(BF16), vc, preferred_element_type=F32) / l
            o_ref[0, :, 128 * h:128 * h + 128] = o.astype(BF16)

    if q_off == 0:
        pl.when(qi == 0)(context)
        pl.when(qi > 0)(latent)
    else:
        latent()


def _swa(sink, qs, ks, vs, t_off):
    B, H, T, _ = qs.shape
    G = ks.shape[1]
    nt = T // TM - t_off
    return pl.pallas_call(
        functools.partial(_swa_kernel, q_off=t_off, n_rows=T),
        grid=(B, nt),
        in_specs=[
            pl.BlockSpec(memory_space=pltpu.SMEM),
            pl.BlockSpec((1, H, TM, 128), lambda b, i: (b, 0, i + t_off, 0)),
            pl.BlockSpec((1, G, T, 128), lambda b, i: (b, 0, 0, 0)),
            pl.BlockSpec((1, G, T, 128), lambda b, i: (b, 0, 0, 0)),
        ],
        out_specs=pl.BlockSpec((1, TM, H * 128), lambda b, i: (b, i, 0)),
        out_shape=jax.ShapeDtypeStruct((B, nt * TM, H * 128), BF16),
        compiler_params=_cp(("arbitrary", "arbitrary")),
        name="swa_attn",
    )(sink, qs, ks, vs)


def _conv_kernel(u_ref, w_ref, b_ref, g_ref, bb_ref, o_ref, buf_ref, *, t_off, n_rows):
    i = pl.program_id(1) + t_off
    nt_all = n_rows // TM
    start = pl.multiple_of(i * TM, TM)
    ps = pl.multiple_of(jnp.maximum(start - CONV_HALO, 0), 8)
    ns = pl.multiple_of(jnp.minimum(start + TM, n_rows - CONV_HALO), 8)
    keep_prev = jnp.where(i <= 1, 0.0, 1.0)
    keep_next = jnp.where((i == 0) | (i == nt_all - 1), 0.0, 1.0)
    buf_ref[0:CONV_HALO, :] = u_ref[0, pl.ds(ps, CONV_HALO), :] * keep_prev
    buf_ref[CONV_HALO:CONV_HALO + TM, :] = u_ref[0, pl.ds(start, TM), :]
    buf_ref[CONV_HALO + TM:2 * CONV_HALO + TM, :] = u_ref[0, pl.ds(ns, CONV_HALO), :] * keep_next
    off = CONV_HALO - CONV_WIDTH // 2
    acc = jnp.zeros((TM, CONV_CH), F32)
    for j in range(CONV_WIDTH):
        acc = acc + buf_ref[off + j:off + j + TM, :] * w_ref[j:j + 1, :]
    y = acc + b_ref[...]
    mu = jnp.mean(y, axis=-1, keepdims=True)
    d = y - mu
    var = jnp.mean(d * d, axis=-1, keepdims=True)
    z = d * lax.rsqrt(var + EPS) * g_ref[...] + bb_ref[...]
    o_ref[0] = (z * jax.nn.sigmoid(z)).astype(BF16)


def _conv(u, lw, t_off):
    B, T, C = u.shape
    nt = T // TM - t_off
    return pl.pallas_call(
        functools.partial(_conv_kernel, t_off=t_off, n_rows=T),
        grid=(B, nt),
        in_specs=[
            pl.BlockSpec((1, T, C), lambda b, i: (b, 0, 0)),
            _full((CONV_WIDTH, C)),
            _full((1, C)),
            _full((1, C)),
            _full((1, C)),
        ],
        out_specs=pl.BlockSpec((1, TM, C), lambda b, i: (b, i, 0)),
        out_shape=jax.ShapeDtypeStruct((B, nt * TM, C), BF16),
        scratch_shapes=[pltpu.VMEM((TM + 2 * CONV_HALO, C), F32)],
        compiler_params=_cp(("arbitrary", "arbitrary")),
        name="conv",
    )(u, lw["conv_w"], lw["conv_b"], lw["conv_g"], lw["conv_bb"])


def _out_kernel(om_ref, oc_ref, os_ref, xa_ref, xb_ref, mod_ref, w1_ref, w2_ref, w3_ref, n2_ref, wrh_ref, wrl_ref, br_ref,
                tri_ref, upper_ref, xn_ref, hf_ref, pos_ref, post_ref, gt_ref, meta_ref, cnt_ref, run_ref, *, t_off):
    first = (pl.program_id(0) == 0) & (pl.program_id(1) == 0)

    @pl.when(first)
    def _():
        run_ref[...] = jnp.zeros_like(run_ref)

    d = functools.partial(jnp.dot, preferred_element_type=F32)
    mix = d(om_ref[0], w1_ref[...]) + d(oc_ref[0], w2_ref[...]) + d(os_ref[0], w3_ref[...])
    mod = mod_ref[0, 0]
    g1, sh2, sc2 = mod[2:3], mod[3:4], mod[4:5]
    xn = _stream_tile(xa_ref, xb_ref, t_off) + g1 * mix
    xn_ref[0] = xn
    hf = xn * lax.rsqrt(jnp.mean(xn * xn, axis=-1, keepdims=True) + EPS) * n2_ref[...]
    hf = hf * (1.0 + sc2) + sh2
    hf_ref[0] = hf.astype(BF16)

    hi, lo = _split(hf)
    logits = d(hi, wrh_ref[...]) + d(hi, wrl_ref[...]) + d(lo, wrh_ref[...]) + br_ref[...]
    lane = lax.broadcasted_iota(I32, (TM, LANES), 1)
    lane_f = lane.astype(F32)
    l = logits
    ohs, vals = [], []
    for _ in range(TOP_K):
        m = jnp.max(l, axis=-1, keepdims=True)
        idx = jnp.min(jnp.where(l == m, lane_f, float(LANES)), axis=-1, keepdims=True)
        oh = lane_f == idx
        ohs.append(oh)
        vals.append(m)
        l = jnp.where(oh, -jnp.inf, l)
    ex = [jnp.exp(v - vals[0]) for v in vals]
    den = ex[0] + ex[1] + ex[2] + ex[3]
    gates = [e / den for e in ex]

    oa = jnp.zeros((TM, LANES), F32)
    for oh in ohs:
        oa = oa + oh.astype(F32)
    hist = jnp.sum(oa, axis=0, keepdims=True)
    slot_rows = jnp.floor((hist + (RUN_CHUNK - 1)) * (1.0 / RUN_CHUNK)) * RUN_CHUNK
    slot_off = d(jnp.broadcast_to(slot_rows, (8, LANES)).astype(BF16), upper_ref[...])
    where_ = d(tri_ref[...], oa.astype(BF16)) + slot_off[0:1, :]
    poss = [jnp.sum(jnp.where(oh, where_, 0.0), axis=-1, keepdims=True).astype(I32) for oh in ohs]

    p_out = jnp.zeros((TM, LANES), I32)
    g_out = jnp.zeros((TM, LANES), F32)
    for k in range(TOP_K):
        p_out = jnp.where(lane == k, poss[k], p_out)
        g_out = jnp.where(lane == k, gates[k], g_out)
    pos_ref[0] = p_out
    post_ref[0] = jnp.transpose(p_out)[0:8, :]
    gt_ref[0] = g_out

    srow = lax.broadcasted_iota(I32, (8, LANES), 0)
    meta = jnp.where(srow == 0, hist, jnp.where(srow == 1, run_ref[...], jnp.where(srow == 2, slot_off, 0.0)))
    meta_ref[0] = meta.astype(I32)
    run_ref[...] = run_ref[...] + jnp.floor((hist + (ROW_ALIGN - 1)) * (1.0 / ROW_ALIGN)) * ROW_ALIGN
    cnt_ref[...] = run_ref[...]


def _outproj(om, oc, osw, first, rest, shift, modsel, lw, tri, upper, t_off):
    B, _, D = first.shape
    T = rest.shape[1] + shift * TM
    nt = T // TM - t_off
    row = lambda b, i: (b, i, 0)
    Tq = nt * TM
    in_specs = [
        pl.BlockSpec((1, TM, 512), row),
        pl.BlockSpec((1, TM, 256), row),
        pl.BlockSpec((1, TM, 512), row),
    ] + _stream_specs(D, t_off, shift) + [
        pl.BlockSpec((1, 1, 6, D), lambda b, i: (b, jnp.minimum(i + t_off, 1), 0, 0)),
        _full((512, D)),
        _full((256, D)),
        _full((512, D)),
        _full((1, D)),
        _full((D, LANES)),
        _full((D, LANES)),
        _full((1, LANES)),
        _full((TM, TM)),
        _full((LANES, LANES)),
    ]
    out_shape = [
        jax.ShapeDtypeStruct((B, Tq, D), F32),
        jax.ShapeDtypeStruct((B, Tq, D), BF16),
        jax.ShapeDtypeStruct((B, Tq, LANES), I32),
        jax.ShapeDtypeStruct((B * nt, 8, TM), I32),
        jax.ShapeDtypeStruct((B, Tq, LANES), F32),
        jax.ShapeDtypeStruct((B * nt, 8, LANES), I32),
        jax.ShapeDtypeStruct((8, LANES), F32),
    ]
    out_specs = [
        pl.BlockSpec((1, TM, D), row),
        pl.BlockSpec((1, TM, D), row),
        pl.BlockSpec((1, TM, LANES), row),
        pl.BlockSpec((1, 8, TM), lambda b, i: (b * nt + i, 0, 0)),
        pl.BlockSpec((1, TM, LANES), row),
        pl.BlockSpec((1, 8, LANES), lambda b, i: (b * nt + i, 0, 0)),
        pl.BlockSpec((8, LANES), lambda b, i: (0, 0)),
    ]
    return pl.pallas_call(
        functools.partial(_out_kernel, t_off=t_off),
        grid=(B, nt),
        in_specs=in_specs,
        out_specs=out_specs,
        out_shape=out_shape,
        scratch_shapes=[pltpu.VMEM((8, LANES), F32)],
        compiler_params=_cp(("arbitrary", "arbitrary")),
        name="outproj_router",
    )(om, oc, osw, first, rest, modsel, lw["w_o1"], lw["w_o2"], lw["w_o3"], lw["n2"], lw["wr_hi"], lw["wr_lo"], lw["br"], tri,
      upper)


SUB = 8
TILE_SUBLANES = 8
ROW_ALIGN = TILE_SUBLANES // SUB
RUN_CHUNK = 16
SLOT_ROWS = TM * TOP_K + N_EXPERTS * RUN_CHUNK


def _rows(ref, row0, nrows):
    start = row0 * SUB if isinstance(row0, int) else pl.multiple_of(row0 * SUB, TILE_SUBLANES)
    return ref.at[pl.ds(start, nrows * SUB), :]


def _to_tiles(ref, val, nrows):
    for j in range(SUB):
        ref[pl.ds(j, nrows, stride=SUB), :] = val[:, LANES * j:LANES * (j + 1)]


def _from_tiles(ref, nrows):
    return jnp.concatenate([ref[pl.ds(j, nrows, stride=SUB), :] for j in range(SUB)], axis=1)


def _slot_matrix(pos, weights):
    col = lax.broadcasted_iota(I32, (TM, SLOT_ROWS), 1)
    m = jnp.zeros((TM, SLOT_ROWS), F32)
    for k in range(TOP_K):
        m = jnp.where(col == pos[:, k:k + 1], weights[k], m)
    return m.astype(BF16)


def _run_copies(meta_ref, pst_ref, buf_ref, hbm_ref, sem, to_hbm):
    total = jnp.int32(0)
    for e in range(N_EXPERTS):
        n = meta_ref[0, 0, e]
        nch = (n + (RUN_CHUNK - 1)) // RUN_CHUNK
        seg0 = pst_ref[e] + meta_ref[0, 1, e]
        slot0 = meta_ref[0, 2, e]

        def chunk(c, carry):
            a = _rows(buf_ref, slot0 + c * RUN_CHUNK, RUN_CHUNK)
            b = _rows(hbm_ref, seg0 + c * RUN_CHUNK, RUN_CHUNK)
            (pltpu.make_async_copy(a, b, sem) if to_hbm else pltpu.make_async_copy(b, a, sem)).start(priority=e % 2)
            return carry

        lax.fori_loop(0, nch, chunk, 0)
        total = total + nch
    return total


def _run_waits(total, buf_ref, hbm_ref, sem, to_hbm):
    a = _rows(buf_ref, 0, RUN_CHUNK)
    b = _rows(hbm_ref, 0, RUN_CHUNK)

    def one(c, carry):
        (pltpu.make_async_copy(a, b, sem) if to_hbm else pltpu.make_async_copy(b, a, sem)).wait()
        return carry

    lax.fori_loop(0, total, one, 0)


def _run_total(meta_ref):
    total = jnp.int32(0)
    for e in range(N_EXPERTS):
        total = total + (meta_ref[0, 0, e] + (RUN_CHUNK - 1)) // RUN_CHUNK
    return total


def _disp_kernel(pst_ref, cnt_ref, pad_ref, na_ref, meta_ref, hf_ref, post_ref, xb_ref, srt_ref, zero_ref, tot_ref,
                 sem, zsem, *, n_blocks):
    step = pl.program_id(0) * pl.num_programs(1) + pl.program_id(1)
    n_steps = pl.num_programs(0) * pl.num_programs(1)
    slot = step % 2
    first = step == 0

    @pl.when(first)
    def _():
        zero_ref[...] = jnp.zeros_like(zero_ref)

        def zblock(j):
            return pltpu.make_async_copy(zero_ref, _rows(xb_ref, j * EXPERT_BLOCK, EXPERT_BLOCK), zsem)

        def zb_start(j, carry):
            zblock(j).start()
            return carry

        def zb_wait(j, carry):
            zblock(j).wait()
            return carry

        lax.fori_loop(na_ref[0], n_blocks, zb_start, 0)
        lax.fori_loop(na_ref[0], n_blocks, zb_wait, 0)

        for e in range(N_EXPERTS):
            lo = pst_ref[e] + cnt_ref[e]
            nrow = pad_ref[e] - cnt_ref[e]
            nz = nrow // RUN_CHUNK
            lo1 = lo + nz * RUN_CHUNK
            n1 = (nrow - nz * RUN_CHUNK) // ROW_ALIGN

            def zchunk(c):
                return pltpu.make_async_copy(_rows(zero_ref, 0, RUN_CHUNK), _rows(xb_ref, lo + c * RUN_CHUNK, RUN_CHUNK), zsem)

            def zrow(r):
                return pltpu.make_async_copy(_rows(zero_ref, 0, ROW_ALIGN), _rows(xb_ref, lo1 + r * ROW_ALIGN, ROW_ALIGN), zsem)

            def zc_start(c, carry):
                zchunk(c).start()
                return carry

            def zc_wait(c, carry):
                zchunk(c).wait()
                return carry

            def zr_start(r, carry):
                zrow(r).start()
                return carry

            def zr_wait(r, carry):
                zrow(r).wait()
                return carry

            lax.fori_loop(0, nz, zc_start, 0)
            lax.fori_loop(0, n1, zr_start, 0)
            lax.fori_loop(0, nz, zc_wait, 0)
            lax.fori_loop(0, n1, zr_wait, 0)

    srow = lax.broadcasted_iota(I32, (SLOT_ROWS, TM), 0)
    post = post_ref[0]
    perm = jnp.zeros((SLOT_ROWS, TM), F32)
    for k in range(TOP_K):
        perm = jnp.where(srow == post[k:k + 1, :], 1.0, perm)
    srt = jnp.dot(perm.astype(BF16), hf_ref[0], preferred_element_type=F32)
    _to_tiles(srt_ref.at[slot], srt, SLOT_ROWS)

    @pl.when(step > 0)
    def _():
        _run_waits(tot_ref[1 - slot], srt_ref.at[1 - slot], xb_ref, sem.at[1 - slot], True)

    total = _run_copies(meta_ref, pst_ref, srt_ref.at[slot], xb_ref, sem.at[slot], True)
    tot_ref[slot] = total

    @pl.when(step == n_steps - 1)
    def _():
        _run_waits(total, srt_ref.at[slot], xb_ref, sem.at[slot], True)


def _dispatch(pstart, counts, padded, n_act, meta, hf, pos, n_buf):
    B, Tq, D = hf.shape
    nt = Tq // TM
    assert D == SUB * LANES
    grid_spec = pltpu.PrefetchScalarGridSpec(
        num_scalar_prefetch=4,
        grid=(B, nt),
        in_specs=[
            pl.BlockSpec((1, 8, LANES), lambda b, i, *_: (b * nt + i, 0, 0), memory_space=pltpu.SMEM),
            pl.BlockSpec((1, TM, D), lambda b, i, *_: (b, i, 0)),
            pl.BlockSpec((1, 8, TM), lambda b, i, *_: (b * nt + i, 0, 0)),
        ],
        out_specs=pl.BlockSpec(memory_space=pl.ANY),
        scratch_shapes=[pltpu.VMEM((2, SLOT_ROWS * SUB, LANES), F32), pltpu.VMEM((EXPERT_BLOCK * SUB, LANES), F32),
                        pltpu.SMEM((2,), I32), pltpu.SemaphoreType.DMA((2,)), pltpu.SemaphoreType.DMA],
    )
    return pl.pallas_call(
        functools.partial(_disp_kernel, n_blocks=n_buf // EXPERT_BLOCK),
        grid_spec=grid_spec,
        out_shape=jax.ShapeDtypeStruct((n_buf * SUB, LANES), F32),
        compiler_params=_cp(("arbitrary", "arbitrary")),
        name="dispatch",
    )(pstart, counts, padded, n_act, meta, hf, pos)


W_ROWS = 512
W_DELAY = 2


def _exp_kernel(be_ref, bx_ref, na_ref, nx_ref, ps_ref, pr_ref, npv_ref, x_ref, b1a_ref, b1b_ref, b2_ref, w1_hbm, w2_hbm,
                o_ref, st1, st2, w1a_s, w1b_s, w2_s, sem, *, e_off):
    j = pl.program_id(0)
    active = j < na_ref[0]
    e_cur = be_ref[j]
    p = pr_ref[e_cur]
    k = bx_ref[j] - ps_ref[e_cur]
    nxt = nx_ref[e_cur]
    n_prev = npv_ref[e_cur]
    n_slices = st1.shape[0] // W_ROWS

    def copies(e):
        return (pltpu.make_async_copy(w1_hbm.at[e], st1, sem.at[0]), pltpu.make_async_copy(w2_hbm.at[e], st2, sem.at[1]))

    def fetch(e):
        for cp in copies(e):
            cp.start(priority=1)

    def fetch_wait():
        for cp in copies(0):
            cp.wait()

    def convert(slot, s):
        r0 = pl.multiple_of(s * W_ROWS, W_ROWS)
        lane = lax.broadcasted_iota(I32, (W_ROWS, LANES), 1)
        lo = lane < 64
        idx = jnp.where(lo, 2 * lane, 2 * (lane - 64) + 1)
        for c in range(st1.shape[1] // (2 * LANES)):
            a = st1[pl.ds(r0, W_ROWS), 2 * LANES * c:2 * LANES * c + LANES]
            b = st1[pl.ds(r0, W_ROWS), 2 * LANES * c + LANES:2 * LANES * (c + 1)]
            pa = jnp.take_along_axis(a, idx, axis=1)
            pb = jnp.take_along_axis(b, idx, axis=1)
            ev = jnp.where(lo, pa, pltpu.roll(pb, 64, axis=1))
            od = jnp.where(lo, pltpu.roll(pa, 64, axis=1), pb)
            w1a_s[slot, pl.ds(r0, W_ROWS), LANES * c:LANES * (c + 1)] = ev.astype(BF16)
            w1b_s[slot, pl.ds(r0, W_ROWS), LANES * c:LANES * (c + 1)] = od.astype(BF16)
        w2_s[slot, pl.ds(r0, W_ROWS), :] = st2[pl.ds(r0, W_ROWS), :].astype(BF16)

    def convert_range(slot, s0):
        def one(s, carry):
            convert(slot, s)
            return carry

        lax.fori_loop(s0, n_slices, one, 0)

    @pl.when(j == 0)
    def _():
        fetch(be_ref[0] + e_off)
        fetch_wait()
        convert_range(0, 0)

    @pl.when(active & (k == 0) & (j > 0))
    def _():
        @pl.when(n_prev <= W_DELAY)
        def _():
            fetch_wait()

        convert_range(p, jnp.clip(n_prev - W_DELAY, 0, n_slices))

    @pl.when(active & (k == 0) & (nxt >= 0))
    def _():
        fetch(nxt + e_off)

    conv = active & (nxt >= 0) & (k >= W_DELAY) & (k < W_DELAY + n_slices)

    @pl.when(conv & (k == W_DELAY))
    def _():
        fetch_wait()

    def ffn(with_convert):
        if with_convert:
            convert(1 - p, k - W_DELAY)
        x = _from_tiles(x_ref, EXPERT_BLOCK).astype(BF16)
        ug = jnp.dot(x, w1a_s[p], preferred_element_type=F32) + b1a_ref[0]
        ul = jnp.dot(x, w1b_s[p], preferred_element_type=F32) + b1b_ref[0]
        xg = jnp.minimum(ug, SWIGLU_LIMIT)
        xl = jnp.clip(ul, -SWIGLU_LIMIT, SWIGLU_LIMIT)
        act = xg * jax.nn.sigmoid(SWIGLU_ALPHA * xg) * (xl + 1.0)
        y = jnp.dot(act.astype(BF16), w2_s[p], preferred_element_type=F32) + b2_ref[0]
        _to_tiles(o_ref, y, EXPERT_BLOCK)

    pl.when(conv)(functools.partial(ffn, True))
    pl.when(active & jnp.logical_not(conv))(functools.partial(ffn, False))

    @pl.when(jnp.logical_not(active))
    def _():
        o_ref[...] = jnp.zeros_like(o_ref)


def _experts(sched, xb, lw):
    nb = xb.shape[0] // (EXPERT_BLOCK * SUB)
    De, D = lw["w2"].shape[1:]
    assert De == D and D % W_ROWS == 0
    wmap = lambda j, be, *_: (be[j], 0, 0)
    grid_spec = pltpu.PrefetchScalarGridSpec(
        num_scalar_prefetch=len(sched),
        grid=(nb,),
        in_specs=[
            pl.BlockSpec((EXPERT_BLOCK * SUB, LANES), lambda j, be, bx, *_: (bx[j], 0)),
            pl.BlockSpec((1, 1, De), wmap),
            pl.BlockSpec((1, 1, De), wmap),
            pl.BlockSpec((1, 1, D), wmap),
            pl.BlockSpec(memory_space=pl.ANY),
            pl.BlockSpec(memory_space=pl.ANY),
        ],
        out_specs=pl.BlockSpec((EXPERT_BLOCK * SUB, LANES), lambda j, *_: (j, 0)),
        scratch_shapes=[pltpu.VMEM((D, 2 * De), F32), pltpu.VMEM((De, D), F32),
                        pltpu.VMEM((2, D, De), BF16), pltpu.VMEM((2, D, De), BF16), pltpu.VMEM((2, De, D), BF16),
                        pltpu.SemaphoreType.DMA((2,))],
    )
    return pl.pallas_call(
        functools.partial(_exp_kernel, e_off=lw["e_off"]),
        grid_spec=grid_spec,
        out_shape=jax.ShapeDtypeStruct(xb.shape, F32),
        compiler_params=pltpu.CompilerParams(dimension_semantics=("arbitrary",), vmem_limit_bytes=EXPERT_VMEM_LIMIT),
        name="experts",
    )(*sched, xb, lw["b1a"], lw["b1b"], lw["b2"], lw["w1"], lw["w2"])


def _comb_kernel(pst_ref, meta_ref, meta_next_ref, pos_ref, gt_ref, xn_ref, mod_ref, yb_ref, o_ref, buf_ref, sem):
    step = pl.program_id(0) * pl.num_programs(1) + pl.program_id(1)
    n_steps = pl.num_programs(0) * pl.num_programs(1)
    slot = step % 2

    @pl.when(step == 0)
    def _():
        buf_ref[...] = jnp.zeros_like(buf_ref)
        _run_copies(meta_ref, pst_ref, buf_ref.at[0], yb_ref, sem.at[0], False)

    @pl.when(step < n_steps - 1)
    def _():
        _run_copies(meta_next_ref, pst_ref, buf_ref.at[1 - slot], yb_ref, sem.at[1 - slot], False)

    gt = gt_ref[0]
    g = _slot_matrix(pos_ref[0], [gt[:, k:k + 1] for k in range(TOP_K)])
    _run_waits(_run_total(meta_ref), buf_ref.at[slot], yb_ref, sem.at[slot], False)
    rows = _from_tiles(buf_ref.at[slot], SLOT_ROWS).astype(BF16)
    y = jnp.dot(g, rows, preferred_element_type=F32)
    g2 = mod_ref[0, 0][5:6]
    o_ref[0] = xn_ref[0] + g2 * y


def _combine(pstart, meta, pos, gates, xn, modsel, yb, t_off):
    B, Tq, D = xn.shape
    nt = Tq // TM
    grid_spec = pltpu.PrefetchScalarGridSpec(
        num_scalar_prefetch=1,
        grid=(B, nt),
        in_specs=[
            pl.BlockSpec((1, 8, LANES), lambda b, i, *_: (b * nt + i, 0, 0), memory_space=pltpu.SMEM),
            pl.BlockSpec((1, 8, LANES), lambda b, i, *_: (jnp.minimum(b * nt + i + 1, B * nt - 1), 0, 0),
                         memory_space=pltpu.SMEM),
            pl.BlockSpec((1, TM, LANES), lambda b, i, *_: (b, i, 0)),
            pl.BlockSpec((1, TM, LANES), lambda b, i, *_: (b, i, 0)),
            pl.BlockSpec((1, TM, D), lambda b, i, *_: (b, i, 0)),
            pl.BlockSpec((1, 1, 6, D), lambda b, i, *_: (b, jnp.minimum(i + t_off, 1), 0, 0)),
            pl.BlockSpec(memory_space=pl.ANY),
        ],
        out_specs=pl.BlockSpec((1, TM, D), lambda b, i, *_: (b, i, 0)),
        scratch_shapes=[pltpu.VMEM((2, SLOT_ROWS * SUB, LANES), F32), pltpu.SemaphoreType.DMA((2,))],
    )
    return pl.pallas_call(
        _comb_kernel,
        grid_spec=grid_spec,
        out_shape=jax.ShapeDtypeStruct((B, Tq, D), F32),
        compiler_params=_cp(("arbitrary", "arbitrary")),
        name="combine",
    )(pstart, meta, meta, pos, gates, xn, modsel, yb)


def _take_cols(w, cols):
    cols = np.asarray(cols)
    out = jnp.take(w, jnp.asarray(np.maximum(cols, 0)), axis=-1)
    return jnp.where(jnp.asarray(cols >= 0), out, 0.0)


def _in_cols():
    pi = _PI
    cols = list(range(0, 256))
    cols += [256 + i for i in range(64)] + [256 + pi[i] for i in range(64)]
    for g in range(SWA_KV_HEADS):
        base = 320 + 64 * g
        cols += [base + i for i in range(64)] + [base + pi[i] for i in range(64)]
    for g in range(SWA_KV_HEADS):
        base = 448 + 64 * g
        cols += [base + i for i in range(64)] + [-1] * 64
    cols += list(range(576, 960))
    for h in range(SWA_HEADS):
        base = 960 + 64 * h
        cols += [base + i for i in range(64)] + [base + pi[i] for i in range(64)]
    cols += list(range(1216, 1728))
    return cols


def _layer_weights(l, a):
    pi = _PI
    lw = {}
    lw["n1"] = a["norm1_g"][l][None, :]
    lw["n2"] = a["norm2_g"][l][None, :]
    lw["w_in"] = _take_cols(a["w_in"][l], _in_cols()).astype(BF16)
    lw["kvg"] = a["mla_kv_norm"][l][None, :]
    lw["qg"] = a["mla_q_norm"][l][None, :]
    uq_cols = []
    for h in range(MLA_HEADS):
        base = MLA_QK * h
        uq_cols += [base + i for i in range(128)] + [base + 128 + i for i in range(64)] + [base + 128 + pi[i] for i in range(64)]
    lw["w_uq"] = _take_cols(a["mla_w_uq"][l], uq_cols).astype(BF16)
    ukv_cols = [256 * h + i for h in range(MLA_HEADS) for i in range(128)]
    ukv_cols += [256 * h + 128 + i for h in range(MLA_HEADS) for i in range(128)]
    lw["w_ukv"] = _take_cols(a["mla_w_ukv"][l], ukv_cols).astype(BF16)
    gq = a["mla_q_head_norm"][l]
    lw["gq"] = (jnp.concatenate([gq[:128], gq[128:], gq[128:][pi]]) * (MLA_QK ** -0.5 * LOG2E))[None, :]
    gk = a["mla_k_head_norm"][l]
    lw["gkn"] = gk[:128][None, :]
    lw["gkp"] = jnp.concatenate([gk[128:], gk[128:][pi]])[None, :]
    sq = a["swa_q_norm"][l]
    lw["gsq"] = (jnp.concatenate([sq, sq[pi]]) * (SWA_HEAD_DIM ** -0.5))[None, :]
    sk = a["swa_k_norm"][l]
    lw["gsk"] = jnp.concatenate([sk, sk[pi]])[None, :]
    lw["conv_w"] = a["conv_w"][l]
    lw["conv_b"] = a["conv_b"][l][None, :]
    lw["conv_g"] = a["conv_ln_g"][l][None, :]
    lw["conv_bb"] = a["conv_ln_b"][l][None, :]
    lw["sink"] = a["swa_sink"][l]
    wo = a["w_out"][l]
    lw["w_o1"] = wo[0:512].astype(BF16)
    lw["w_o2"] = wo[512:768].astype(BF16)
    o3 = wo[768:1024].reshape(SWA_HEADS, SWA_HEAD_DIM, -1)
    lw["w_o3"] = jnp.concatenate([o3, jnp.zeros_like(o3)], axis=1).reshape(SWA_HEADS * 128, -1).astype(BF16)
    wr = jnp.pad(a["router_w"][l], ((0, 0), (0, LANES - N_EXPERTS)))
    lw["wr_hi"] = wr.astype(BF16)
    lw["wr_lo"] = (wr - lw["wr_hi"].astype(F32)).astype(BF16)
    lw["br"] = jnp.pad(a["router_b"][l], (0, LANES - N_EXPERTS), constant_values=NEG)[None, :]
    lw["w1"] = a["exp_w1"].reshape((-1,) + a["exp_w1"].shape[2:])
    lw["e_off"] = l * a["exp_w1"].shape[1]
    b1 = a["exp_b1"][l]
    lw["b1a"] = b1[:, None, 0::2]
    lw["b1b"] = b1[:, None, 1::2]
    lw["w2"] = a["exp_w2"].reshape((-1,) + a["exp_w2"].shape[2:])
    lw["b2"] = a["exp_b2"][l][:, None, :]
    return lw


def _rope_tables(n_ctx, n_lat):
    q = MLA_ROPE // 4
    n = jnp.arange(n_lat, dtype=I32)
    row = (n // GRID_W).astype(F32)
    col = (n % GRID_W).astype(F32)
    inv = ROPE_BASE ** (-jnp.arange(q, dtype=F32) / q)
    ang_r = row[:, None] * inv
    ang_c = col[:, None] * inv
    cos = jnp.concatenate([jnp.cos(ang_r), jnp.cos(ang_r), jnp.cos(ang_c), jnp.cos(ang_c)], axis=1)
    sin = jnp.concatenate([-jnp.sin(ang_r), jnp.sin(ang_r), -jnp.sin(ang_c), jnp.sin(ang_c)], axis=1)
    cos = jnp.concatenate([jnp.ones((n_ctx, 64), F32), cos], axis=0)
    sin = jnp.concatenate([jnp.zeros((n_ctx, 64), F32), sin], axis=0)
    z = jnp.zeros_like(cos)
    return jnp.concatenate([cos, z], axis=1), jnp.concatenate([sin, z], axis=1)


def _routing_tables(cnt_f, n_blocks):
    counts = cnt_f[0, :N_EXPERTS].astype(I32)
    padded = (counts + (RUN_CHUNK - 1) + EXPERT_BLOCK - 1) // EXPERT_BLOCK * EXPERT_BLOCK
    padded = jnp.where(counts > 0, padded, 0)
    pend = jnp.cumsum(padded)
    pstart = pend - padded
    n_act = pend[-1] // EXPERT_BLOCK
    blk = jnp.minimum(jnp.arange(n_blocks, dtype=I32), n_act - 1)
    blk_e = jnp.sum((pend[None, :] <= (blk * EXPERT_BLOCK)[:, None]).astype(I32), axis=1)
    blk_e = jnp.minimum(blk_e, N_EXPERTS - 1)
    nbk = padded // EXPERT_BLOCK
    has = nbk > 0
    ids = jnp.arange(N_EXPERTS, dtype=I32)
    later = has[None, :] & (ids[None, :] > ids[:, None])
    nxt_e = jnp.min(jnp.where(later, ids[None, :], N_EXPERTS), axis=1)
    nxt_e = jnp.where(nxt_e == N_EXPERTS, -1, nxt_e)
    earlier = has[None, :] & (ids[None, :] < ids[:, None])
    prv_e = jnp.max(jnp.where(earlier, ids[None, :], -1), axis=1)
    nb_prev_e = jnp.sum(jnp.where(ids[None, :] == prv_e[:, None], nbk[None, :], 0), axis=1)
    set_e = (jnp.cumsum(has.astype(I32)) - 1) % 2
    sched = (blk_e, blk, n_act.reshape(1), nxt_e, pstart // EXPERT_BLOCK, set_e, nb_prev_e)
    sched = tuple(s.astype(I32) for s in sched)
    return counts, padded.astype(I32), pstart.astype(I32), sched


def kernel(x, c, ctx, c_ctx, norm1_g, norm2_g, w_ada, b_ada, w_in, mla_q_norm, mla_kv_norm, mla_w_uq, mla_w_ukv, mla_q_head_norm, mla_k_head_norm, conv_w, conv_b, conv_ln_g, conv_ln_b, swa_q_norm, swa_k_norm, swa_sink, w_out, router_w, router_b, exp_w1, exp_b1, exp_w2, exp_b2):
    a = dict(norm1_g=norm1_g, norm2_g=norm2_g, w_in=w_in, mla_q_norm=mla_q_norm, mla_kv_norm=mla_kv_norm,
             mla_w_uq=mla_w_uq, mla_w_ukv=mla_w_ukv, mla_q_head_norm=mla_q_head_norm, mla_k_head_norm=mla_k_head_norm,
             conv_w=conv_w, conv_b=conv_b, conv_ln_g=conv_ln_g, conv_ln_b=conv_ln_b, swa_q_norm=swa_q_norm,
             swa_k_norm=swa_k_norm, swa_sink=swa_sink, w_out=w_out, router_w=router_w, router_b=router_b,
             exp_w1=exp_w1, exp_b1=exp_b1, exp_w2=exp_w2, exp_b2=exp_b2)
    B, S, D = x.shape
    n_ctx = ctx.shape[1]
    depth = w_ada.shape[0]
    assert n_ctx == TM and S % TM == 0 and B + 1 <= 16
    T = n_ctx + S

    s_in = jnp.zeros((16, D), F32).at[:B].set(c).at[B].set(c_ctx)
    mods = _ada(s_in, w_ada, b_ada)
    cos_t, sin_t = _rope_tables(n_ctx, S)
    tri = jnp.tril(jnp.ones((TM, TM), F32), -1).astype(BF16)
    upper = jnp.triu(jnp.ones((LANES, LANES), F32), 1).astype(BF16)

    first, rest, shift = ctx, x, 1
    for l in range(depth):
        last = l == depth - 1
        t_off = 1 if last else 0
        lw = _layer_weights(l, a)
        m = mods[l].reshape(16, 6, D)
        modsel = jnp.stack([jnp.broadcast_to(m[B], (B, 6, D)), m[:B]], axis=1)

        qm, km, vm, qs, ks, vs, u = _prep(first, rest, shift, modsel, lw, cos_t, sin_t, 0)
        om = _mla(qm, km, vm, t_off)
        oc = _conv(u, lw, t_off)
        osw = _swa(lw["sink"], qs, ks, vs, t_off)
        xn, hf, pos, pos_t, gt_o, meta, cnt = _outproj(om, oc, osw, first, rest, shift, modsel, lw, tri, upper, t_off)

        n_tok = B * (T - t_off * TM)
        nk = n_tok * TOP_K
        n_align = (n_tok // TM) * N_EXPERTS * (ROW_ALIGN - 1)
        n_buf = -(-(nk + n_align + N_EXPERTS * (RUN_CHUNK - 1 + EXPERT_BLOCK - 1)) // EXPERT_BLOCK) * EXPERT_BLOCK
        counts, padded, pstart, sched = _routing_tables(cnt, n_buf // EXPERT_BLOCK)
        xb = _dispatch(pstart, counts, padded, sched[2], meta, hf, pos_t, n_buf)
        yb = _experts(sched, xb, lw)
        xu = _combine(pstart, meta, pos, gt_o, xn, modsel, yb, t_off)
        first, rest, shift = xu, xu, 0
    return xu
```

```python
import functools

import numpy as np
import jax
import jax.numpy as jnp
from jax import lax
from jax.experimental import pallas as pl
from jax.experimental.pallas import tpu as pltpu

F32 = jnp.float32
BF16 = jnp.bfloat16
I32 = jnp.int32

GRID_W = 64
ROPE_BASE = 10000.0
EPS = 1e-6
MLA_HEADS = 4
MLA_NOPE = 128
MLA_ROPE = 64
MLA_V = 128
MLA_QK = MLA_NOPE + MLA_ROPE
MLA_Q_RANK = 384
MLA_KV_RANK = 256
CONV_CH = 256
CONV_WIDTH = 31
SWA_HEADS = 4
SWA_KV_HEADS = 2
SWA_HEAD_DIM = 64
SWA_WINDOW = 128
N_EXPERTS = 32
TOP_K = 4
SWIGLU_LIMIT = 7.0
SWIGLU_ALPHA = 1.702
EXPERT_BLOCK = 512

LANES = 128
TM = 256
CONV_HALO = 16
VMEM_LIMIT = 48 * 1024 * 1024
EXPERT_VMEM_LIMIT = 56 * 1024 * 1024
NEG = -1e30
LOG2E = 1.4426950408889634

_PI = np.array([i + 16 if (i % 32) < 16 else i - 16 for i in range(64)])


def _cp(sem):
    return pltpu.CompilerParams(dimension_semantics=sem, vmem_limit_bytes=VMEM_LIMIT)


def _full(shape):
    n = len(shape)
    return pl.BlockSpec(shape, lambda *a, _n=n: (0,) * _n)


def _split(x):
    hi = x.astype(BF16)
    lo = (x - hi.astype(F32)).astype(BF16)
    return hi, lo


def _dot3(a, b):
    ah, al = _split(a)
    bh, bl = _split(b)
    d = functools.partial(jnp.dot, preferred_element_type=F32)
    return d(ah, bh) + d(ah, bl) + d(al, bh)


def _ada_kernel(s_ref, w_ref, b_ref, o_ref):
    s = s_ref[...]
    s = s * jax.nn.sigmoid(s)
    o_ref[0] = _dot3(s, w_ref[0]) + b_ref[0]


def _ada(s_in, w_ada, b_ada):
    L, D, N = w_ada.shape
    tn = 1536
    return pl.pallas_call(
        _ada_kernel,
        grid=(L, N // tn),
        in_specs=[
            pl.BlockSpec((16, D), lambda l, j: (0, 0)),
            pl.BlockSpec((1, D, tn), lambda l, j: (l, 0, j)),
            pl.BlockSpec((1, 1, tn), lambda l, j: (l, 0, j)),
        ],
        out_specs=pl.BlockSpec((1, 16, tn), lambda l, j: (l, 0, j)),
        out_shape=jax.ShapeDtypeStruct((L, 16, N), F32),
        compiler_params=_cp(("arbitrary", "arbitrary")),
        name="ada",
    )(s_in, w_ada, b_ada.reshape(L, 1, N))


def _rope(x, c, s):
    return x * c + pltpu.roll(x, 64, axis=1) * s


def _stream_specs(D, t_off, shift):
    return [pl.BlockSpec((1, TM, D), lambda b, i, *_: (b, 0, 0)),
            pl.BlockSpec((1, TM, D), lambda b, i, *_: (b, jnp.maximum(i + t_off - shift, 0), 0))]


def _stream_tile(first_ref, rest_ref, t_off):
    return jnp.where(pl.program_id(1) + t_off == 0, first_ref[0], rest_ref[0])


def _prep_kernel(xa_ref, xb_ref, mod_ref, n1_ref, win_ref, kvg_ref, wukv_ref, qg_ref, wuq_ref, gq_ref, gkn_ref, gkp_ref,
                 gsq_ref, gsk_ref, cos_ref, sin_ref, qm_ref, km_ref, vm_ref, qs_ref, ks_ref, vs_ref, u_ref, *, t_off):
    x = _stream_tile(xa_ref, xb_ref, t_off)
    mod = mod_ref[0, 0]
    sh, sc = mod[0:1], mod[1:2]
    y = x * lax.rsqrt(jnp.mean(x * x, axis=-1, keepdims=True) + EPS) * n1_ref[...]
    h = y * (1.0 + sc) + sh
    p = jnp.dot(h.astype(BF16), win_ref[...], preferred_element_type=F32)
    c = cos_ref[...]
    s = sin_ref[...]
    lane = lax.broadcasted_iota(I32, (TM, LANES), 1)

    def ss_lo(v):
        return 0.5 * jnp.sum(v * v, axis=-1, keepdims=True)

    ckv = p[:, 0:256]
    ckvn = ckv * lax.rsqrt(jnp.mean(ckv * ckv, axis=-1, keepdims=True) + EPS) * kvg_ref[...]
    kv = jnp.dot(ckvn.astype(BF16), wukv_ref[...], preferred_element_type=F32)
    kpe = p[:, 256:384]
    ss_pe = ss_lo(kpe)
    kpe_rot = _rope(kpe * gkp_ref[...], c, s)
    for hh in range(MLA_HEADS):
        kn = kv[:, 128 * hh:128 * hh + 128]
        r = lax.rsqrt((jnp.sum(kn * kn, axis=-1, keepdims=True) + ss_pe) * (1.0 / MLA_QK) + EPS)
        km_ref[0, hh, :, 0:128] = (kn * r * gkn_ref[...]).astype(BF16)
        km_ref[0, hh, :, 128:256] = (kpe_rot * r).astype(BF16)
        vm_ref[0, hh, :, 0:128] = kv[:, 512 + 128 * hh:640 + 128 * hh].astype(BF16)
        vm_ref[0, hh, :, 128:256] = (lane == 0).astype(BF16)

    cq = p[:, 896:1280]
    cqn = cq * lax.rsqrt(jnp.mean(cq * cq, axis=-1, keepdims=True) + EPS) * qg_ref[...]
    q = jnp.dot(cqn.astype(BF16), wuq_ref[...], preferred_element_type=F32)
    gq = gq_ref[...]
    for hh in range(MLA_HEADS):
        qn = q[:, 256 * hh:256 * hh + 128]
        qp = q[:, 256 * hh + 128:256 * hh + 256]
        r = lax.rsqrt((jnp.sum(qn * qn, axis=-1, keepdims=True) + ss_lo(qp)) * (1.0 / MLA_QK) + EPS)
        qm_ref[0, hh, :, 0:128] = (qn * r * gq[:, 0:128]).astype(BF16)
        qm_ref[0, hh, :, 128:256] = _rope(qp * r * gq[:, 128:256], c, s).astype(BF16)

    for g in range(SWA_KV_HEADS):
        xk = p[:, 384 + 128 * g:512 + 128 * g]
        r = lax.rsqrt(ss_lo(xk) * (1.0 / SWA_HEAD_DIM) + EPS)
        ks_ref[0, g] = _rope(xk * r * gsk_ref[...], c, s).astype(BF16)
        vs_ref[0, g] = p[:, 640 + 128 * g:768 + 128 * g].astype(BF16)
    for hh in range(SWA_HEADS):
        xq = p[:, 1280 + 128 * hh:1408 + 128 * hh]
        r = lax.rsqrt(ss_lo(xq) * (1.0 / SWA_HEAD_DIM) + EPS)
        qs_ref[0, hh] = _rope(xq * r * gsq_ref[...], c, s).astype(BF16)

    u_ref[0] = p[:, 1792:2048] * jax.nn.sigmoid(p[:, 2048:2304])


def _prep(first, rest, shift, modsel, lw, cos_t, sin_t, t_off):
    B, _, D = first.shape
    T = rest.shape[1] + shift * TM
    nt = T // TM - t_off
    ncol = lw["w_in"].shape[1]
    row = lambda b, i: (b, i + t_off, 0)
    head = lambda b, i: (b, 0, i + t_off, 0)
    in_specs = _stream_specs(D, t_off, shift) + [
        pl.BlockSpec((1, 1, 6, D), lambda b, i: (b, jnp.minimum(i + t_off, 1), 0, 0)),
        _full((1, D)),
        _full((D, ncol)),
        _full((1, MLA_KV_RANK)),
        _full((MLA_KV_RANK, 1024)),
        _full((1, MLA_Q_RANK)),
        _full((MLA_Q_RANK, 1024)),
        _full((1, 256)),
        _full((1, 128)),
        _full((1, 128)),
        _full((1, 128)),
        _full((1, 128)),
        pl.BlockSpec((TM, LANES), lambda b, i: (i + t_off, 0)),
        pl.BlockSpec((TM, LANES), lambda b, i: (i + t_off, 0)),
    ]
    out_shape = [
        jax.ShapeDtypeStruct((B, MLA_HEADS, T, 256), BF16),
        jax.ShapeDtypeStruct((B, MLA_HEADS, T, 256), BF16),
        jax.ShapeDtypeStruct((B, MLA_HEADS, T, 256), BF16),
        jax.ShapeDtypeStruct((B, SWA_HEADS, T, 128), BF16),
        jax.ShapeDtypeStruct((B, SWA_KV_HEADS, T, 128), BF16),
        jax.ShapeDtypeStruct((B, SWA_KV_HEADS, T, 128), BF16),
        jax.ShapeDtypeStruct((B, T, CONV_CH), F32),
    ]
    out_specs = [
        pl.BlockSpec((1, MLA_HEADS, TM, 256), head),
        pl.BlockSpec((1, MLA_HEADS, TM, 256), head),
        pl.BlockSpec((1, MLA_HEADS, TM, 256), head),
        pl.BlockSpec((1, SWA_HEADS, TM, 128), head),
        pl.BlockSpec((1, SWA_KV_HEADS, TM, 128), head),
        pl.BlockSpec((1, SWA_KV_HEADS, TM, 128), head),
        pl.BlockSpec((1, TM, CONV_CH), row),
    ]
    return pl.pallas_call(
        functools.partial(_prep_kernel, t_off=t_off),
        grid=(B, nt),
        in_specs=in_specs,
        out_specs=out_specs,
        out_shape=out_shape,
        compiler_params=_cp(("arbitrary", "arbitrary")),
        name="prep",
    )(first, rest, modsel, lw["n1"], lw["w_in"], lw["kvg"], lw["w_ukv"], lw["qg"], lw["w_uq"], lw["gq"], lw["gkn"], lw["gkp"],
      lw["gsq"], lw["gsk"], cos_t, sin_t)


KEY_CHUNK = 256


def _mixer_kernel(sink_ref, q_ref, k_ref, v_ref, u_ref, cw_ref, cb_ref, cg_ref, cbb_ref, sq_ref, sk_ref, sv_ref,
                  o_ref, oc_ref, os_ref, s_ref, p_ref, cbuf_ref, *, q_off, n_keys):
    qi = pl.program_id(1) + q_off

    def attend(nk):
        for h in range(MLA_HEADS):
            q = q_ref[0, h]
            macc = jnp.full((TM, LANES), -jnp.inf, F32)
            for c in range(nk // KEY_CHUNK):
                k = k_ref[0, h, c * KEY_CHUNK:(c + 1) * KEY_CHUNK, :]
                s = lax.dot_general(q, k, (((1,), (1,)), ((), ())), preferred_element_type=F32)
                s_ref[h, :, c * KEY_CHUNK:(c + 1) * KEY_CHUNK] = s
                for j in range(KEY_CHUNK // LANES):
                    macc = jnp.maximum(macc, s[:, j * LANES:(j + 1) * LANES])
            m = jnp.max(macc, axis=-1, keepdims=True)
            for c in range(nk // KEY_CHUNK):
                p = jnp.exp2(s_ref[h, :, c * KEY_CHUNK:(c + 1) * KEY_CHUNK] - m)
                p_ref[h, :, c * KEY_CHUNK:(c + 1) * KEY_CHUNK] = p.astype(BF16)
            ol = jnp.dot(p_ref[h, :, 0:nk], v_ref[0, h, 0:nk, :], preferred_element_type=F32)
            o = ol[:, 0:MLA_V] / ol[:, MLA_V:MLA_V + 1]
            o_ref[0, :, 128 * h:128 * h + 128] = o.astype(BF16)

    def tile(is_ctx):
        _conv_tile(qi, n_keys, u_ref, cw_ref, cb_ref, cg_ref, cbb_ref, oc_ref, cbuf_ref)
        attend(TM if is_ctx else n_keys)
        _swa_tile(is_ctx, qi, n_keys, sink_ref, sq_ref, sk_ref, sv_ref, os_ref)

    if q_off == 0:
        pl.when(qi == 0)(functools.partial(tile, True))
        pl.when(qi > 0)(functools.partial(tile, False))
    else:
        tile(False)


def _mixers(qm, km, vm, u, qs, ks, vs, lw, t_off):
    B, H, T, _ = qm.shape
    C = u.shape[2]
    HS, G = qs.shape[1], ks.shape[1]
    nt = T // TM - t_off
    qtile = lambda b, i: (b, 0, i + t_off, 0)
    whole = lambda b, i: (b, 0, 0, 0)
    otile = lambda b, i: (b, i, 0)
    return pl.pallas_call(
        functools.partial(_mixer_kernel, q_off=t_off, n_keys=T),
        grid=(B, nt),
        in_specs=[
            pl.BlockSpec(memory_space=pltpu.SMEM),
            pl.BlockSpec((1, H, TM, 256), qtile),
            pl.BlockSpec((1, H, T, 256), whole),
            pl.BlockSpec((1, H, T, 256), whole),
            pl.BlockSpec((1, T, C), lambda b, i: (b, 0, 0)),
            _full((CONV_WIDTH, C)),
            _full((1, C)),
            _full((1, C)),
            _full((1, C)),
            pl.BlockSpec((1, HS, TM, 128), qtile),
            pl.BlockSpec((1, G, T, 128), whole),
            pl.BlockSpec((1, G, T, 128), whole),
        ],
        out_specs=[pl.BlockSpec((1, TM, H * MLA_V), otile), pl.BlockSpec((1, TM, C), otile),
                   pl.BlockSpec((1, TM, HS * 128), otile)],
        out_shape=[jax.ShapeDtypeStruct((B, nt * TM, H * MLA_V), BF16), jax.ShapeDtypeStruct((B, nt * TM, C), BF16),
                   jax.ShapeDtypeStruct((B, nt * TM, HS * 128), BF16)],
        scratch_shapes=[pltpu.VMEM((H, TM, T), F32), pltpu.VMEM((H, TM, T), BF16),
                        pltpu.VMEM((TM + 2 * CONV_HALO, C), F32)],
        compiler_params=_cp(("arbitrary", "arbitrary")),
        name="mixers",
    )(lw["sink"], qm, km, vm, u, lw["conv_w"], lw["conv_b"], lw["conv_g"], lw["conv_bb"], qs, ks, vs)


SWA_SPAN = TM + 2 * SWA_WINDOW


def _swa_tile(is_ctx, qi, n_rows, sink_ref, q_ref, k_ref, v_ref, o_ref):
    nt_dims = (((1,), (1,)), ((), ()))

    def latent():
        start = jnp.clip(qi * TM - SWA_WINDOW, 0, n_rows - SWA_SPAN)
        start = pl.multiple_of(start, SWA_WINDOW)
        qpos = qi * TM + lax.broadcasted_iota(I32, (TM, SWA_SPAN), 0)
        kpos = start + lax.broadcasted_iota(I32, (TM, SWA_SPAN), 1)
        valid = (jnp.abs(qpos - kpos) <= SWA_WINDOW) & (kpos >= TM)
        for h in range(SWA_HEADS):
            g = h // (SWA_HEADS // SWA_KV_HEADS)
            q = q_ref[0, h]
            kl = k_ref[0, g, pl.ds(start, SWA_SPAN), :]
            vl = v_ref[0, g, pl.ds(start, SWA_SPAN), :]
            kc = k_ref[0, g, 0:TM, :]
            vc = v_ref[0, g, 0:TM, :]
            sl = lax.dot_general(q, kl, nt_dims, preferred_element_type=F32)
            sl = jnp.where(valid, sl, NEG)
            scx = lax.dot_general(q, kc, nt_dims, preferred_element_type=F32)
            sink = sink_ref[h]
            m = jnp.maximum(jnp.maximum(jnp.max(sl, axis=-1, keepdims=True), jnp.max(scx, axis=-1, keepdims=True)), sink)
            pl_ = jnp.exp(sl - m)
            pc = jnp.exp(scx - m)
            l = jnp.sum(pl_, axis=-1, keepdims=True) + jnp.sum(pc, axis=-1, keepdims=True) + jnp.exp(sink - m)
            o = (jnp.dot(pl_.astype(BF16), vl, preferred_element_type=F32)
                 + jnp.dot(pc.astype(BF16), vc, preferred_element_type=F32)) / l
            o_ref[0, :, 128 * h:128 * h + 128] = o.astype(BF16)

    def context():
        for h in range(SWA_HEADS):
            g = h // (SWA_HEADS // SWA_KV_HEADS)
            q = q_ref[0, h]
            kc = k_ref[0, g, 0:TM, :]
            vc = v_ref[0, g, 0:TM, :]
            scx = lax.dot_general(q, kc, nt_dims, preferred_element_type=F32)
            sink = sink_ref[h]
            m = jnp.maximum(jnp.max(scx, axis=-1, keepdims=True), sink)
            pc = jnp.exp(scx - m)
            l = jnp.sum(pc, axis=-1, keepdims=True) + jnp.exp(sink - m)
            o = jnp.dot(pc.astype(BF16), vc, preferred_element_type=F32) / l
            o_ref[0, :, 128 * h:128 * h + 128] = o.astype(BF16)

    if is_ctx:
        context()
    else:
        latent()


def _conv_tile(i, n_rows, u_ref, w_ref, b_ref, g_ref, bb_ref, o_ref, buf_ref):
    nt_all = n_rows // TM
    start = pl.multiple_of(i * TM, TM)
    ps = pl.multiple_of(jnp.maximum(start - CONV_HALO, 0), 8)
    ns = pl.multiple_of(jnp.minimum(start + TM, n_rows - CONV_HALO), 8)
    keep_prev = jnp.where(i <= 1, 0.0, 1.0)
    keep_next = jnp.where((i == 0) | (i == nt_all - 1), 0.0, 1.0)
    buf_ref[0:CONV_HALO, :] = u_ref[0, pl.ds(ps, CONV_HALO), :] * keep_prev
    buf_ref[CONV_HALO:CONV_HALO + TM, :] = u_ref[0, pl.ds(start, TM), :]
    buf_ref[CONV_HALO + TM:2 * CONV_HALO + TM, :] = u_ref[0, pl.ds(ns, CONV_HALO), :] * keep_next
    off = CONV_HALO - CONV_WIDTH // 2
    accs = [None] * 4
    for j in range(CONV_WIDTH):
        term = buf_ref[off + j:off + j + TM, :] * w_ref[j:j + 1, :]
        accs[j % 4] = term if accs[j % 4] is None else accs[j % 4] + term
    y = (accs[0] + accs[1]) + (accs[2] + accs[3]) + b_ref[...]
    mu = jnp.mean(y, axis=-1, keepdims=True)
    d = y - mu
    var = jnp.mean(d * d, axis=-1, keepdims=True)
    z = d * lax.rsqrt(var + EPS) * g_ref[...] + bb_ref[...]
    o_ref[0] = (z * jax.nn.sigmoid(z)).astype(BF16)


def _out_kernel(om_ref, oc_ref, os_ref, xa_ref, xb_ref, mod_ref, w1_ref, w2_ref, w3_ref, n2_ref, wrh_ref, wrl_ref, br_ref,
                tri_ref, upper_ref, xn_ref, hf_ref, pos_ref, post_ref, gt_ref, meta_ref, cnt_ref, run_ref, *, t_off):
    first = (pl.program_id(0) == 0) & (pl.program_id(1) == 0)

    @pl.when(first)
    def _():
        run_ref[...] = jnp.zeros_like(run_ref)

    d = functools.partial(jnp.dot, preferred_element_type=F32)
    mix = d(om_ref[0], w1_ref[...]) + d(oc_ref[0], w2_ref[...]) + d(os_ref[0], w3_ref[...])
    mod = mod_ref[0, 0]
    g1, sh2, sc2 = mod[2:3], mod[3:4], mod[4:5]
    xn = _stream_tile(xa_ref, xb_ref, t_off) + g1 * mix
    xn_ref[0] = xn
    hf = xn * lax.rsqrt(jnp.mean(xn * xn, axis=-1, keepdims=True) + EPS) * n2_ref[...]
    hf = hf * (1.0 + sc2) + sh2
    hf_ref[0] = hf.astype(BF16)

    hi, lo = _split(hf)
    logits = d(hi, wrh_ref[...]) + d(hi, wrl_ref[...]) + d(lo, wrh_ref[...]) + br_ref[...]
    lane = lax.broadcasted_iota(I32, (TM, LANES), 1)
    lane_f = lane.astype(F32)
    l = logits
    ohs, vals = [], []
    for _ in range(TOP_K):
        m = jnp.max(l, axis=-1, keepdims=True)
        idx = jnp.min(jnp.where(l == m, lane_f, float(LANES)), axis=-1, keepdims=True)
        oh = lane_f == idx
        ohs.append(oh)
        vals.append(m)
        l = jnp.where(oh, -jnp.inf, l)
    ex = [jnp.exp(v - vals[0]) for v in vals]
    den = ex[0] + ex[1] + ex[2] + ex[3]
    gates = [e / den for e in ex]

    oa = jnp.zeros((TM, LANES), F32)
    for oh in ohs:
        oa = oa + oh.astype(F32)
    hist = jnp.sum(oa, axis=0, keepdims=True)
    slot_rows = jnp.floor((hist + (RUN_CHUNK - 1)) * (1.0 / RUN_CHUNK)) * RUN_CHUNK
    slot_off = d(jnp.broadcast_to(slot_rows, (8, LANES)).astype(BF16), upper_ref[...])
    where_ = d(tri_ref[...], oa.astype(BF16)) + slot_off[0:1, :]
    poss = [jnp.sum(jnp.where(oh, where_, 0.0), axis=-1, keepdims=True).astype(I32) for oh in ohs]

    p_out = jnp.zeros((TM, LANES), I32)
    g_out = jnp.zeros((TM, LANES), F32)
    for k in range(TOP_K):
        p_out = jnp.where(lane == k, poss[k], p_out)
        g_out = jnp.where(lane == k, gates[k], g_out)
    pos_ref[0] = p_out
    post_ref[0] = jnp.transpose(p_out)[0:8, :]
    gt_ref[0] = g_out

    srow = lax.broadcasted_iota(I32, (8, LANES), 0)
    meta = jnp.where(srow == 0, hist, jnp.where(srow == 1, run_ref[...], jnp.where(srow == 2, slot_off, 0.0)))
    meta_ref[0] = meta.astype(I32)
    run_ref[...] = run_ref[...] + jnp.floor((hist + (ROW_ALIGN - 1)) * (1.0 / ROW_ALIGN)) * ROW_ALIGN
    cnt_ref[...] = run_ref[...]


def _outproj(om, oc, osw, first, rest, shift, modsel, lw, tri, upper, t_off):
    B, _, D = first.shape
    T = rest.shape[1] + shift * TM
    nt = T // TM - t_off
    row = lambda b, i: (b, i, 0)
    Tq = nt * TM
    in_specs = [
        pl.BlockSpec((1, TM, 512), row),
        pl.BlockSpec((1, TM, 256), row),
        pl.BlockSpec((1, TM, 512), row),
    ] + _stream_specs(D, t_off, shift) + [
        pl.BlockSpec((1, 1, 6, D), lambda b, i: (b, jnp.minimum(i + t_off, 1), 0, 0)),
        _full((512, D)),
        _full((256, D)),
        _full((512, D)),
        _full((1, D)),
        _full((D, LANES)),
        _full((D, LANES)),
        _full((1, LANES)),
        _full((TM, TM)),
        _full((LANES, LANES)),
    ]
    out_shape = [
        jax.ShapeDtypeStruct((B, Tq, D), F32),
        jax.ShapeDtypeStruct((B, Tq, D), BF16),
        jax.ShapeDtypeStruct((B, Tq, LANES), I32),
        jax.ShapeDtypeStruct((B * nt, 8, TM), I32),
        jax.ShapeDtypeStruct((B, Tq, LANES), F32),
        jax.ShapeDtypeStruct((B * nt, 8, LANES), I32),
        jax.ShapeDtypeStruct((8, LANES), F32),
    ]
    out_specs = [
        pl.BlockSpec((1, TM, D), row),
        pl.BlockSpec((1, TM, D), row),
        pl.BlockSpec((1, TM, LANES), row),
        pl.BlockSpec((1, 8, TM), lambda b, i: (b * nt + i, 0, 0)),
        pl.BlockSpec((1, TM, LANES), row),
        pl.BlockSpec((1, 8, LANES), lambda b, i: (b * nt + i, 0, 0)),
        pl.BlockSpec((8, LANES), lambda b, i: (0, 0)),
    ]
    return pl.pallas_call(
        functools.partial(_out_kernel, t_off=t_off),
        grid=(B, nt),
        in_specs=in_specs,
        out_specs=out_specs,
        out_shape=out_shape,
        scratch_shapes=[pltpu.VMEM((8, LANES), F32)],
        compiler_params=_cp(("arbitrary", "arbitrary")),
        name="outproj_router",
    )(om, oc, osw, first, rest, modsel, lw["w_o1"], lw["w_o2"], lw["w_o3"], lw["n2"], lw["wr_hi"], lw["wr_lo"], lw["br"], tri,
      upper)


SUB = 8
TILE_SUBLANES = 8
ROW_ALIGN = TILE_SUBLANES // SUB
RUN_CHUNK = 16
SLOT_ROWS = TM * TOP_K + N_EXPERTS * RUN_CHUNK


def _rows(ref, row0, nrows):
    start = row0 * SUB if isinstance(row0, int) else pl.multiple_of(row0 * SUB, TILE_SUBLANES)
    return ref.at[pl.ds(start, nrows * SUB), :]


def _to_tiles(ref, val, nrows):
    for j in range(SUB):
        ref[pl.ds(j, nrows, stride=SUB), :] = val[:, LANES * j:LANES * (j + 1)]


def _from_tiles(ref, nrows):
    return jnp.concatenate([ref[pl.ds(j, nrows, stride=SUB), :] for j in range(SUB)], axis=1)


def _slot_matrix(pos, weights):
    col = lax.broadcasted_iota(I32, (TM, SLOT_ROWS), 1)
    m = jnp.zeros((TM, SLOT_ROWS), F32)
    for k in range(TOP_K):
        m = jnp.where(col == pos[:, k:k + 1], weights[k], m)
    return m.astype(BF16)


def _run_copies(meta_ref, pst_ref, buf_ref, hbm_ref, sem, to_hbm):
    total = jnp.int32(0)
    for e in range(N_EXPERTS):
        n = meta_ref[0, 0, e]
        nch = (n + (RUN_CHUNK - 1)) // RUN_CHUNK
        seg0 = pst_ref[e] + meta_ref[0, 1, e]
        slot0 = meta_ref[0, 2, e]

        def chunk(c, carry):
            a = _rows(buf_ref, slot0 + c * RUN_CHUNK, RUN_CHUNK)
            b = _rows(hbm_ref, seg0 + c * RUN_CHUNK, RUN_CHUNK)
            (pltpu.make_async_copy(a, b, sem) if to_hbm else pltpu.make_async_copy(b, a, sem)).start(priority=e % 2)
            return carry

        lax.fori_loop(0, nch, chunk, 0)
        total = total + nch
    return total


def _run_waits(total, buf_ref, hbm_ref, sem, to_hbm):
    a = _rows(buf_ref, 0, RUN_CHUNK)
    b = _rows(hbm_ref, 0, RUN_CHUNK)

    def one(c, carry):
        (pltpu.make_async_copy(a, b, sem) if to_hbm else pltpu.make_async_copy(b, a, sem)).wait()
        return carry

    lax.fori_loop(0, total, one, 0)


def _run_total(meta_ref):
    total = jnp.int32(0)
    for e in range(N_EXPERTS):
        total = total + (meta_ref[0, 0, e] + (RUN_CHUNK - 1)) // RUN_CHUNK
    return total


def _disp_kernel(pst_ref, cnt_ref, pad_ref, na_ref, meta_ref, hf_ref, post_ref, xb_ref, srt_ref, zero_ref, tot_ref,
                 sem, zsem, *, n_blocks):
    step = pl.program_id(0) * pl.num_programs(1) + pl.program_id(1)
    n_steps = pl.num_programs(0) * pl.num_programs(1)
    slot = step % 2
    first = step == 0

    @pl.when(first)
    def _():
        zero_ref[...] = jnp.zeros_like(zero_ref)

        def zblock(j):
            return pltpu.make_async_copy(zero_ref, _rows(xb_ref, j * EXPERT_BLOCK, EXPERT_BLOCK), zsem)

        def zb_start(j, carry):
            zblock(j).start()
            return carry

        def zb_wait(j, carry):
            zblock(j).wait()
            return carry

        lax.fori_loop(na_ref[0], n_blocks, zb_start, 0)
        lax.fori_loop(na_ref[0], n_blocks, zb_wait, 0)

        for e in range(N_EXPERTS):
            lo = pst_ref[e] + cnt_ref[e]
            nrow = pad_ref[e] - cnt_ref[e]
            nz = nrow // RUN_CHUNK
            lo1 = lo + nz * RUN_CHUNK
            n1 = (nrow - nz * RUN_CHUNK) // ROW_ALIGN

            def zchunk(c):
                return pltpu.make_async_copy(_rows(zero_ref, 0, RUN_CHUNK), _rows(xb_ref, lo + c * RUN_CHUNK, RUN_CHUNK), zsem)

            def zrow(r):
                return pltpu.make_async_copy(_rows(zero_ref, 0, ROW_ALIGN), _rows(xb_ref, lo1 + r * ROW_ALIGN, ROW_ALIGN), zsem)

            def zc_start(c, carry):
                zchunk(c).start()
                return carry

            def zc_wait(c, carry):
                zchunk(c).wait()
                return carry

            def zr_start(r, carry):
                zrow(r).start()
                return carry

            def zr_wait(r, carry):
                zrow(r).wait()
                return carry

            lax.fori_loop(0, nz, zc_start, 0)
            lax.fori_loop(0, n1, zr_start, 0)
            lax.fori_loop(0, nz, zc_wait, 0)
            lax.fori_loop(0, n1, zr_wait, 0)

    srow = lax.broadcasted_iota(I32, (SLOT_ROWS, TM), 0)
    post = post_ref[0]
    perm = jnp.zeros((SLOT_ROWS, TM), F32)
    for k in range(TOP_K):
        perm = jnp.where(srow == post[k:k + 1, :], 1.0, perm)
    srt = jnp.dot(perm.astype(BF16), hf_ref[0], preferred_element_type=F32)
    _to_tiles(srt_ref.at[slot], srt, SLOT_ROWS)

    @pl.when(step > 0)
    def _():
        _run_waits(tot_ref[1 - slot], srt_ref.at[1 - slot], xb_ref, sem.at[1 - slot], True)

    total = _run_copies(meta_ref, pst_ref, srt_ref.at[slot], xb_ref, sem.at[slot], True)
    tot_ref[slot] = total

    @pl.when(step == n_steps - 1)
    def _():
        _run_waits(total, srt_ref.at[slot], xb_ref, sem.at[slot], True)


def _dispatch(pstart, counts, padded, n_act, meta, hf, pos, n_buf):
    B, Tq, D = hf.shape
    nt = Tq // TM
    assert D == SUB * LANES
    grid_spec = pltpu.PrefetchScalarGridSpec(
        num_scalar_prefetch=4,
        grid=(B, nt),
        in_specs=[
            pl.BlockSpec((1, 8, LANES), lambda b, i, *_: (b * nt + i, 0, 0), memory_space=pltpu.SMEM),
            pl.BlockSpec((1, TM, D), lambda b, i, *_: (b, i, 0)),
            pl.BlockSpec((1, 8, TM), lambda b, i, *_: (b * nt + i, 0, 0)),
        ],
        out_specs=pl.BlockSpec(memory_space=pl.ANY),
        scratch_shapes=[pltpu.VMEM((2, SLOT_ROWS * SUB, LANES), F32), pltpu.VMEM((EXPERT_BLOCK * SUB, LANES), F32),
                        pltpu.SMEM((2,), I32), pltpu.SemaphoreType.DMA((2,)), pltpu.SemaphoreType.DMA],
    )
    return pl.pallas_call(
        functools.partial(_disp_kernel, n_blocks=n_buf // EXPERT_BLOCK),
        grid_spec=grid_spec,
        out_shape=jax.ShapeDtypeStruct((n_buf * SUB, LANES), F32),
        compiler_params=_cp(("arbitrary", "arbitrary")),
        name="dispatch",
    )(pstart, counts, padded, n_act, meta, hf, pos)


W_ROWS = 512
W_DELAY = 2


def _exp_kernel(be_ref, bx_ref, na_ref, nx_ref, ps_ref, pr_ref, npv_ref, x_ref, b1a_ref, b1b_ref, b2_ref, w1_hbm, w2_hbm,
                o_ref, st1, st2, w1a_s, w1b_s, w2_s, sem, *, e_off):
    j = pl.program_id(0)
    active = j < na_ref[0]
    e_cur = be_ref[j]
    p = pr_ref[e_cur]
    k = bx_ref[j] - ps_ref[e_cur]
    nxt = nx_ref[e_cur]
    n_prev = npv_ref[e_cur]
    n_slices = st1.shape[0] // W_ROWS

    def copies(e):
        return (pltpu.make_async_copy(w1_hbm.at[e], st1, sem.at[0]), pltpu.make_async_copy(w2_hbm.at[e], st2, sem.at[1]))

    def fetch(e):
        for cp in copies(e):
            cp.start(priority=1)

    def fetch_wait():
        for cp in copies(0):
            cp.wait()

    def convert(slot, s):
        r0 = pl.multiple_of(s * W_ROWS, W_ROWS)
        lane = lax.broadcasted_iota(I32, (W_ROWS, LANES), 1)
        lo = lane < 64
        idx = jnp.where(lo, 2 * lane, 2 * (lane - 64) + 1)
        for c in range(st1.shape[1] // (2 * LANES)):
            a = st1[pl.ds(r0, W_ROWS), 2 * LANES * c:2 * LANES * c + LANES]
            b = st1[pl.ds(r0, W_ROWS), 2 * LANES * c + LANES:2 * LANES * (c + 1)]
            pa = jnp.take_along_axis(a, idx, axis=1)
            pb = jnp.take_along_axis(b, idx, axis=1)
            ev = jnp.where(lo, pa, pltpu.roll(pb, 64, axis=1))
            od = jnp.where(lo, pltpu.roll(pa, 64, axis=1), pb)
            w1a_s[slot, pl.ds(r0, W_ROWS), LANES * c:LANES * (c + 1)] = ev.astype(BF16)
            w1b_s[slot, pl.ds(r0, W_ROWS), LANES * c:LANES * (c + 1)] = od.astype(BF16)
        w2_s[slot, pl.ds(r0, W_ROWS), :] = st2[pl.ds(r0, W_ROWS), :].astype(BF16)

    def convert_range(slot, s0):
        def one(s, carry):
            convert(slot, s)
            return carry

        lax.fori_loop(s0, n_slices, one, 0)

    @pl.when(j == 0)
    def _():
        fetch(be_ref[0] + e_off)
        fetch_wait()
        convert_range(0, 0)

    @pl.when(active & (k == 0) & (j > 0))
    def _():
        @pl.when(n_prev <= W_DELAY)
        def _():
            fetch_wait()

        convert_range(p, jnp.clip(n_prev - W_DELAY, 0, n_slices))

    @pl.when(active & (k == 0) & (nxt >= 0))
    def _():
        fetch(nxt + e_off)

    conv = active & (nxt >= 0) & (k >= W_DELAY) & (k < W_DELAY + n_slices)

    @pl.when(conv & (k == W_DELAY))
    def _():
        fetch_wait()

    def ffn(with_convert):
        if with_convert:
            convert(1 - p, k - W_DELAY)
        x = _from_tiles(x_ref, EXPERT_BLOCK).astype(BF16)
        ug = jnp.dot(x, w1a_s[p], preferred_element_type=F32) + b1a_ref[0]
        ul = jnp.dot(x, w1b_s[p], preferred_element_type=F32) + b1b_ref[0]
        xg = jnp.minimum(ug, SWIGLU_LIMIT)
        xl = jnp.clip(ul, -SWIGLU_LIMIT, SWIGLU_LIMIT)
        act = xg * jax.nn.sigmoid(SWIGLU_ALPHA * xg) * (xl + 1.0)
        y = jnp.dot(act.astype(BF16), w2_s[p], preferred_element_type=F32) + b2_ref[0]
        _to_tiles(o_ref, y, EXPERT_BLOCK)

    pl.when(conv)(functools.partial(ffn, True))
    pl.when(active & jnp.logical_not(conv))(functools.partial(ffn, False))

    @pl.when(jnp.logical_not(active))
    def _():
        o_ref[...] = jnp.zeros_like(o_ref)


def _experts(sched, xb, lw):
    nb = xb.shape[0] // (EXPERT_BLOCK * SUB)
    De, D = lw["w2"].shape[1:]
    assert De == D and D % W_ROWS == 0
    wmap = lambda j, be, *_: (be[j], 0, 0)
    grid_spec = pltpu.PrefetchScalarGridSpec(
        num_scalar_prefetch=len(sched),
        grid=(nb,),
        in_specs=[
            pl.BlockSpec((EXPERT_BLOCK * SUB, LANES), lambda j, be, bx, *_: (bx[j], 0)),
            pl.BlockSpec((1, 1, De), wmap),
            pl.BlockSpec((1, 1, De), wmap),
            pl.BlockSpec((1, 1, D), wmap),
            pl.BlockSpec(memory_space=pl.ANY),
            pl.BlockSpec(memory_space=pl.ANY),
        ],
        out_specs=pl.BlockSpec((EXPERT_BLOCK * SUB, LANES), lambda j, *_: (j, 0)),
        scratch_shapes=[pltpu.VMEM((D, 2 * De), F32), pltpu.VMEM((De, D), F32),
                        pltpu.VMEM((2, D, De), BF16), pltpu.VMEM((2, D, De), BF16), pltpu.VMEM((2, De, D), BF16),
                        pltpu.SemaphoreType.DMA((2,))],
    )
    return pl.pallas_call(
        functools.partial(_exp_kernel, e_off=lw["e_off"]),
        grid_spec=grid_spec,
        out_shape=jax.ShapeDtypeStruct(xb.shape, F32),
        compiler_params=pltpu.CompilerParams(dimension_semantics=("arbitrary",), vmem_limit_bytes=EXPERT_VMEM_LIMIT),
        name="experts",
    )(*sched, xb, lw["b1a"], lw["b1b"], lw["b2"], lw["w1"], lw["w2"])


def _comb_kernel(pst_ref, meta_ref, meta_next_ref, pos_ref, gt_ref, xn_ref, mod_ref, yb_ref, o_ref, buf_ref, sem):
    step = pl.program_id(0) * pl.num_programs(1) + pl.program_id(1)
    n_steps = pl.num_programs(0) * pl.num_programs(1)
    slot = step % 2

    @pl.when(step == 0)
    def _():
        buf_ref[...] = jnp.zeros_like(buf_ref)
        _run_copies(meta_ref, pst_ref, buf_ref.at[0], yb_ref, sem.at[0], False)

    @pl.when(step < n_steps - 1)
    def _():
        _run_copies(meta_next_ref, pst_ref, buf_ref.at[1 - slot], yb_ref, sem.at[1 - slot], False)

    gt = gt_ref[0]
    g = _slot_matrix(pos_ref[0], [gt[:, k:k + 1] for k in range(TOP_K)])
    _run_waits(_run_total(meta_ref), buf_ref.at[slot], yb_ref, sem.at[slot], False)
    rows = _from_tiles(buf_ref.at[slot], SLOT_ROWS).astype(BF16)
    y = jnp.dot(g, rows, preferred_element_type=F32)
    g2 = mod_ref[0, 0][5:6]
    o_ref[0] = xn_ref[0] + g2 * y


def _combine(pstart, meta, pos, gates, xn, modsel, yb, t_off):
    B, Tq, D = xn.shape
    nt = Tq // TM
    grid_spec = pltpu.PrefetchScalarGridSpec(
        num_scalar_prefetch=1,
        grid=(B, nt),
        in_specs=[
            pl.BlockSpec((1, 8, LANES), lambda b, i, *_: (b * nt + i, 0, 0), memory_space=pltpu.SMEM),
            pl.BlockSpec((1, 8, LANES), lambda b, i, *_: (jnp.minimum(b * nt + i + 1, B * nt - 1), 0, 0),
                         memory_space=pltpu.SMEM),
            pl.BlockSpec((1, TM, LANES), lambda b, i, *_: (b, i, 0)),
            pl.BlockSpec((1, TM, LANES), lambda b, i, *_: (b, i, 0)),
            pl.BlockSpec((1, TM, D), lambda b, i, *_: (b, i, 0)),
            pl.BlockSpec((1, 1, 6, D), lambda b, i, *_: (b, jnp.minimum(i + t_off, 1), 0, 0)),
            pl.BlockSpec(memory_space=pl.ANY),
        ],
        out_specs=pl.BlockSpec((1, TM, D), lambda b, i, *_: (b, i, 0)),
        scratch_shapes=[pltpu.VMEM((2, SLOT_ROWS * SUB, LANES), F32), pltpu.SemaphoreType.DMA((2,))],
    )
    return pl.pallas_call(
        _comb_kernel,
        grid_spec=grid_spec,
        out_shape=jax.ShapeDtypeStruct((B, Tq, D), F32),
        compiler_params=_cp(("arbitrary", "arbitrary")),
        name="combine",
    )(pstart, meta, meta, pos, gates, xn, modsel, yb)


def _take_cols(w, cols):
    cols = np.asarray(cols)
    out = jnp.take(w, jnp.asarray(np.maximum(cols, 0)), axis=-1)
    return jnp.where(jnp.asarray(cols >= 0), out, 0.0)


def _in_cols():
    pi = _PI
    cols = list(range(0, 256))
    cols += [256 + i for i in range(64)] + [256 + pi[i] for i in range(64)]
    for g in range(SWA_KV_HEADS):
        base = 320 + 64 * g
        cols += [base + i for i in range(64)] + [base + pi[i] for i in range(64)]
    for g in range(SWA_KV_HEADS):
        base = 448 + 64 * g
        cols += [base + i for i in range(64)] + [-1] * 64
    cols += list(range(576, 960))
    for h in range(SWA_HEADS):
        base = 960 + 64 * h
        cols += [base + i for i in range(64)] + [base + pi[i] for i in range(64)]
    cols += list(range(1216, 1728))
    return cols


def _layer_weights(l, a):
    pi = _PI
    lw = {}
    lw["n1"] = a["norm1_g"][l][None, :]
    lw["n2"] = a["norm2_g"][l][None, :]
    lw["w_in"] = _take_cols(a["w_in"][l], _in_cols()).astype(BF16)
    lw["kvg"] = a["mla_kv_norm"][l][None, :]
    lw["qg"] = a["mla_q_norm"][l][None, :]
    uq_cols = []
    for h in range(MLA_HEADS):
        base = MLA_QK * h
        uq_cols += [base + i for i in range(128)] + [base + 128 + i for i in range(64)] + [base + 128 + pi[i] for i in range(64)]
    lw["w_uq"] = _take_cols(a["mla_w_uq"][l], uq_cols).astype(BF16)
    ukv_cols = [256 * h + i for h in range(MLA_HEADS) for i in range(128)]
    ukv_cols += [256 * h + 128 + i for h in range(MLA_HEADS) for i in range(128)]
    lw["w_ukv"] = _take_cols(a["mla_w_ukv"][l], ukv_cols).astype(BF16)
    gq = a["mla_q_head_norm"][l]
    lw["gq"] = (jnp.concatenate([gq[:128], gq[128:], gq[128:][pi]]) * (MLA_QK ** -0.5 * LOG2E))[None, :]
    gk = a["mla_k_head_norm"][l]
    lw["gkn"] = gk[:128][None, :]
    lw["gkp"] = jnp.concatenate([gk[128:], gk[128:][pi]])[None, :]
    sq = a["swa_q_norm"][l]
    lw["gsq"] = (jnp.concatenate([sq, sq[pi]]) * (SWA_HEAD_DIM ** -0.5))[None, :]
    sk = a["swa_k_norm"][l]
    lw["gsk"] = jnp.concatenate([sk, sk[pi]])[None, :]
    lw["conv_w"] = a["conv_w"][l]
    lw["conv_b"] = a["conv_b"][l][None, :]
    lw["conv_g"] = a["conv_ln_g"][l][None, :]
    lw["conv_bb"] = a["conv_ln_b"][l][None, :]
    lw["sink"] = a["swa_sink"][l]
    wo = a["w_out"][l]
    lw["w_o1"] = wo[0:512].astype(BF16)
    lw["w_o2"] = wo[512:768].astype(BF16)
    o3 = wo[768:1024].reshape(SWA_HEADS, SWA_HEAD_DIM, -1)
    lw["w_o3"] = jnp.concatenate([o3, jnp.zeros_like(o3)], axis=1).reshape(SWA_HEADS * 128, -1).astype(BF16)
    wr = jnp.pad(a["router_w"][l], ((0, 0), (0, LANES - N_EXPERTS)))
    lw["wr_hi"] = wr.astype(BF16)
    lw["wr_lo"] = (wr - lw["wr_hi"].astype(F32)).astype(BF16)
    lw["br"] = jnp.pad(a["router_b"][l], (0, LANES - N_EXPERTS), constant_values=NEG)[None, :]
    lw["w1"] = a["exp_w1"].reshape((-1,) + a["exp_w1"].shape[2:])
    lw["e_off"] = l * a["exp_w1"].shape[1]
    b1 = a["exp_b1"][l]
    lw["b1a"] = b1[:, None, 0::2]
    lw["b1b"] = b1[:, None, 1::2]
    lw["w2"] = a["exp_w2"].reshape((-1,) + a["exp_w2"].shape[2:])
    lw["b2"] = a["exp_b2"][l][:, None, :]
    return lw


def _rope_tables(n_ctx, n_lat):
    q = MLA_ROPE // 4
    n = jnp.arange(n_lat, dtype=I32)
    row = (n // GRID_W).astype(F32)
    col = (n % GRID_W).astype(F32)
    inv = ROPE_BASE ** (-jnp.arange(q, dtype=F32) / q)
    ang_r = row[:, None] * inv
    ang_c = col[:, None] * inv
    cos = jnp.concatenate([jnp.cos(ang_r), jnp.cos(ang_r), jnp.cos(ang_c), jnp.cos(ang_c)], axis=1)
    sin = jnp.concatenate([-jnp.sin(ang_r), jnp.sin(ang_r), -jnp.sin(ang_c), jnp.sin(ang_c)], axis=1)
    cos = jnp.concatenate([jnp.ones((n_ctx, 64), F32), cos], axis=0)
    sin = jnp.concatenate([jnp.zeros((n_ctx, 64), F32), sin], axis=0)
    z = jnp.zeros_like(cos)
    return jnp.concatenate([cos, z], axis=1), jnp.concatenate([sin, z], axis=1)


def _routing_tables(cnt_f, n_blocks):
    counts = cnt_f[0, :N_EXPERTS].astype(I32)
    padded = (counts + (RUN_CHUNK - 1) + EXPERT_BLOCK - 1) // EXPERT_BLOCK * EXPERT_BLOCK
    padded = jnp.where(counts > 0, padded, 0)
    pend = jnp.cumsum(padded)
    pstart = pend - padded
    n_act = pend[-1] // EXPERT_BLOCK
    blk = jnp.minimum(jnp.arange(n_blocks, dtype=I32), n_act - 1)
    blk_e = jnp.sum((pend[None, :] <= (blk * EXPERT_BLOCK)[:, None]).astype(I32), axis=1)
    blk_e = jnp.minimum(blk_e, N_EXPERTS - 1)
    nbk = padded // EXPERT_BLOCK
    has = nbk > 0
    ids = jnp.arange(N_EXPERTS, dtype=I32)
    later = has[None, :] & (ids[None, :] > ids[:, None])
    nxt_e = jnp.min(jnp.where(later, ids[None, :], N_EXPERTS), axis=1)
    nxt_e = jnp.where(nxt_e == N_EXPERTS, -1, nxt_e)
    earlier = has[None, :] & (ids[None, :] < ids[:, None])
    prv_e = jnp.max(jnp.where(earlier, ids[None, :], -1), axis=1)
    nb_prev_e = jnp.sum(jnp.where(ids[None, :] == prv_e[:, None], nbk[None, :], 0), axis=1)
    set_e = (jnp.cumsum(has.astype(I32)) - 1) % 2
    sched = (blk_e, blk, n_act.reshape(1), nxt_e, pstart // EXPERT_BLOCK, set_e, nb_prev_e)
    sched = tuple(s.astype(I32) for s in sched)
    return counts, padded.astype(I32), pstart.astype(I32), sched


def kernel(x, c, ctx, c_ctx, norm1_g, norm2_g, w_ada, b_ada, w_in, mla_q_norm, mla_kv_norm, mla_w_uq, mla_w_ukv, mla_q_head_norm, mla_k_head_norm, conv_w, conv_b, conv_ln_g, conv_ln_b, swa_q_norm, swa_k_norm, swa_sink, w_out, router_w, router_b, exp_w1, exp_b1, exp_w2, exp_b2):
    a = dict(norm1_g=norm1_g, norm2_g=norm2_g, w_in=w_in, mla_q_norm=mla_q_norm, mla_kv_norm=mla_kv_norm,
             mla_w_uq=mla_w_uq, mla_w_ukv=mla_w_ukv, mla_q_head_norm=mla_q_head_norm, mla_k_head_norm=mla_k_head_norm,
             conv_w=conv_w, conv_b=conv_b, conv_ln_g=conv_ln_g, conv_ln_b=conv_ln_b, swa_q_norm=swa_q_norm,
             swa_k_norm=swa_k_norm, swa_sink=swa_sink, w_out=w_out, router_w=router_w, router_b=router_b,
             exp_w1=exp_w1, exp_b1=exp_b1, exp_w2=exp_w2, exp_b2=exp_b2)
    B, S, D = x.shape
    n_ctx = ctx.shape[1]
    depth = w_ada.shape[0]
    assert n_ctx == TM and S % TM == 0 and B + 1 <= 16
    T = n_ctx + S

    s_in = jnp.zeros((16, D), F32).at[:B].set(c).at[B].set(c_ctx)
    mods = _ada(s_in, w_ada, b_ada)
    cos_t, sin_t = _rope_tables(n_ctx, S)
    tri = jnp.tril(jnp.ones((TM, TM), F32), -1).astype(BF16)
    upper = jnp.triu(jnp.ones((LANES, LANES), F32), 1).astype(BF16)

    first, rest, shift = ctx, x, 1
    for l in range(depth):
        last = l == depth - 1
        t_off = 1 if last else 0
        lw = _layer_weights(l, a)
        m = mods[l].reshape(16, 6, D)
        modsel = jnp.stack([jnp.broadcast_to(m[B], (B, 6, D)), m[:B]], axis=1)

        qm, km, vm, qs, ks, vs, u = _prep(first, rest, shift, modsel, lw, cos_t, sin_t, 0)
        om, oc, osw = _mixers(qm, km, vm, u, qs, ks, vs, lw, t_off)
        xn, hf, pos, pos_t, gt_o, meta, cnt = _outproj(om, oc, osw, first, rest, shift, modsel, lw, tri, upper, t_off)

        n_tok = B * (T - t_off * TM)
        nk = n_tok * TOP_K
        n_align = (n_tok // TM) * N_EXPERTS * (ROW_ALIGN - 1)
        n_buf = -(-(nk + n_align + N_EXPERTS * (RUN_CHUNK - 1 + EXPERT_BLOCK - 1)) // EXPERT_BLOCK) * EXPERT_BLOCK
        counts, padded, pstart, sched = _routing_tables(cnt, n_buf // EXPERT_BLOCK)
        xb = _dispatch(pstart, counts, padded, sched[2], meta, hf, pos_t, n_buf)
        yb = _experts(sched, xb, lw)
        xu = _combine(pstart, meta, pos, gt_o, xn, modsel, yb, t_off)
        first, rest, shift = xu, xu, 0
    return xu
```

```python
import functools

import numpy as np
import jax
import jax.numpy as jnp
from jax import lax
from jax.experimental import pallas as pl
from jax.experimental.pallas import tpu as pltpu

F32 = jnp.float32
BF16 = jnp.bfloat16
I32 = jnp.int32

GRID_W = 64
ROPE_BASE = 10000.0
EPS = 1e-6
MLA_HEADS = 4
MLA_NOPE = 128
MLA_ROPE = 64
MLA_V = 128
MLA_QK = MLA_NOPE + MLA_ROPE
MLA_Q_RANK = 384
MLA_KV_RANK = 256
CONV_CH = 256
CONV_WIDTH = 31
SWA_HEADS = 4
SWA_KV_HEADS = 2
SWA_HEAD_DIM = 64
SWA_WINDOW = 128
N_EXPERTS = 32
TOP_K = 4
SWIGLU_LIMIT = 7.0
SWIGLU_ALPHA = 1.702
EXPERT_BLOCK = 512

LANES = 128
TM = 256
CONV_HALO = 16
VMEM_LIMIT = 48 * 1024 * 1024
MIXER_VMEM_LIMIT = 56 * 1024 * 1024
EXPERT_VMEM_LIMIT = 56 * 1024 * 1024
NEG = -1e30
LOG2E = 1.4426950408889634

_PI = np.array([i + 16 if (i % 32) < 16 else i - 16 for i in range(64)])


def _cp(sem):
    return pltpu.CompilerParams(dimension_semantics=sem, vmem_limit_bytes=VMEM_LIMIT)


def _full(shape):
    n = len(shape)
    return pl.BlockSpec(shape, lambda *a, _n=n: (0,) * _n)


def _split(x):
    hi = x.astype(BF16)
    lo = (x - hi.astype(F32)).astype(BF16)
    return hi, lo


def _dot3(a, b):
    ah, al = _split(a)
    bh, bl = _split(b)
    d = functools.partial(jnp.dot, preferred_element_type=F32)
    return d(ah, bh) + d(ah, bl) + d(al, bh)


def _ada_kernel(s_ref, w_ref, b_ref, o_ref):
    s = s_ref[...]
    s = s * jax.nn.sigmoid(s)
    o_ref[0] = _dot3(s, w_ref[0]) + b_ref[0]


def _ada(s_in, w_ada, b_ada):
    L, D, N = w_ada.shape
    tn = 1536
    return pl.pallas_call(
        _ada_kernel,
        grid=(L, N // tn),
        in_specs=[
            pl.BlockSpec((16, D), lambda l, j: (0, 0)),
            pl.BlockSpec((1, D, tn), lambda l, j: (l, 0, j)),
            pl.BlockSpec((1, 1, tn), lambda l, j: (l, 0, j)),
        ],
        out_specs=pl.BlockSpec((1, 16, tn), lambda l, j: (l, 0, j)),
        out_shape=jax.ShapeDtypeStruct((L, 16, N), F32),
        compiler_params=_cp(("arbitrary", "arbitrary")),
        name="ada",
    )(s_in, w_ada, b_ada.reshape(L, 1, N))


def _rope(x, c, s):
    return x * c + pltpu.roll(x, 64, axis=1) * s


def _stream_specs(D, t_off, shift):
    return [pl.BlockSpec((1, TM, D), lambda b, i, *_: (b, 0, 0)),
            pl.BlockSpec((1, TM, D), lambda b, i, *_: (b, jnp.maximum(i + t_off - shift, 0), 0))]


def _stream_tile(first_ref, rest_ref, t_off):
    return jnp.where(pl.program_id(1) + t_off == 0, first_ref[0], rest_ref[0])


def _prep_kernel(xa_ref, xb_ref, mod_ref, n1_ref, win_ref, kvg_ref, wukv_ref, qg_ref, wuq_ref, gq_ref, gkn_ref, gkp_ref,
                 gsq_ref, gsk_ref, cos_ref, sin_ref, qm_ref, km_ref, vm_ref, qs_ref, ks_ref, vs_ref, u_ref, *, t_off):
    x = _stream_tile(xa_ref, xb_ref, t_off)
    mod = mod_ref[0, 0]
    sh, sc = mod[0:1], mod[1:2]
    y = x * lax.rsqrt(jnp.mean(x * x, axis=-1, keepdims=True) + EPS) * n1_ref[...]
    h = y * (1.0 + sc) + sh
    p = jnp.dot(h.astype(BF16), win_ref[...], preferred_element_type=F32)
    c = cos_ref[...]
    s = sin_ref[...]
    lane = lax.broadcasted_iota(I32, (TM, LANES), 1)

    def ss_lo(v):
        return 0.5 * jnp.sum(v * v, axis=-1, keepdims=True)

    ckv = p[:, 0:256]
    ckvn = ckv * lax.rsqrt(jnp.mean(ckv * ckv, axis=-1, keepdims=True) + EPS) * kvg_ref[...]
    kv = jnp.dot(ckvn.astype(BF16), wukv_ref[...], preferred_element_type=F32)
    kpe = p[:, 256:384]
    ss_pe = ss_lo(kpe)
    kpe_rot = _rope(kpe * gkp_ref[...], c, s)
    for hh in range(MLA_HEADS):
        kn = kv[:, 128 * hh:128 * hh + 128]
        r = lax.rsqrt((jnp.sum(kn * kn, axis=-1, keepdims=True) + ss_pe) * (1.0 / MLA_QK) + EPS)
        km_ref[0, hh, :, 0:128] = (kn * r * gkn_ref[...]).astype(BF16)
        km_ref[0, hh, :, 128:256] = (kpe_rot * r).astype(BF16)
        vm_ref[0, hh, :, 0:128] = kv[:, 512 + 128 * hh:640 + 128 * hh].astype(BF16)
        vm_ref[0, hh, :, 128:256] = (lane == 0).astype(BF16)

    cq = p[:, 896:1280]
    cqn = cq * lax.rsqrt(jnp.mean(cq * cq, axis=-1, keepdims=True) + EPS) * qg_ref[...]
    q = jnp.dot(cqn.astype(BF16), wuq_ref[...], preferred_element_type=F32)
    gq = gq_ref[...]
    for hh in range(MLA_HEADS):
        qn = q[:, 256 * hh:256 * hh + 128]
        qp = q[:, 256 * hh + 128:256 * hh + 256]
        r = lax.rsqrt((jnp.sum(qn * qn, axis=-1, keepdims=True) + ss_lo(qp)) * (1.0 / MLA_QK) + EPS)
        qm_ref[0, hh, :, 0:128] = (qn * r * gq[:, 0:128]).astype(BF16)
        qm_ref[0, hh, :, 128:256] = _rope(qp * r * gq[:, 128:256], c, s).astype(BF16)

    for g in range(SWA_KV_HEADS):
        xk = p[:, 384 + 128 * g:512 + 128 * g]
        r = lax.rsqrt(ss_lo(xk) * (1.0 / SWA_HEAD_DIM) + EPS)
        ks_ref[0, g] = _rope(xk * r * gsk_ref[...], c, s).astype(BF16)
        vs_ref[0, g] = p[:, 640 + 128 * g:768 + 128 * g].astype(BF16)
    for hh in range(SWA_HEADS):
        xq = p[:, 1280 + 128 * hh:1408 + 128 * hh]
        r = lax.rsqrt(ss_lo(xq) * (1.0 / SWA_HEAD_DIM) + EPS)
        qs_ref[0, hh] = _rope(xq * r * gsq_ref[...], c, s).astype(BF16)

    u_ref[0] = p[:, 1792:2048] * jax.nn.sigmoid(p[:, 2048:2304])


def _prep(first, rest, shift, modsel, lw, cos_t, sin_t, t_off):
    B, _, D = first.shape
    T = rest.shape[1] + shift * TM
    nt = T // TM - t_off
    ncol = lw["w_in"].shape[1]
    row = lambda b, i: (b, i + t_off, 0)
    head = lambda b, i: (b, 0, i + t_off, 0)
    in_specs = _stream_specs(D, t_off, shift) + [
        pl.BlockSpec((1, 1, 6, D), lambda b, i: (b, jnp.minimum(i + t_off, 1), 0, 0)),
        _full((1, D)),
        _full((D, ncol)),
        _full((1, MLA_KV_RANK)),
        _full((MLA_KV_RANK, 1024)),
        _full((1, MLA_Q_RANK)),
        _full((MLA_Q_RANK, 1024)),
        _full((1, 256)),
        _full((1, 128)),
        _full((1, 128)),
        _full((1, 128)),
        _full((1, 128)),
        pl.BlockSpec((TM, LANES), lambda b, i: (i + t_off, 0)),
        pl.BlockSpec((TM, LANES), lambda b, i: (i + t_off, 0)),
    ]
    out_shape = [
        jax.ShapeDtypeStruct((B, MLA_HEADS, T, 256), BF16),
        jax.ShapeDtypeStruct((B, MLA_HEADS, T, 256), BF16),
        jax.ShapeDtypeStruct((B, MLA_HEADS, T, 256), BF16),
        jax.ShapeDtypeStruct((B, SWA_HEADS, T, 128), BF16),
        jax.ShapeDtypeStruct((B, SWA_KV_HEADS, T, 128), BF16),
        jax.ShapeDtypeStruct((B, SWA_KV_HEADS, T, 128), BF16),
        jax.ShapeDtypeStruct((B, T, CONV_CH), F32),
    ]
    out_specs = [
        pl.BlockSpec((1, MLA_HEADS, TM, 256), head),
        pl.BlockSpec((1, MLA_HEADS, TM, 256), head),
        pl.BlockSpec((1, MLA_HEADS, TM, 256), head),
        pl.BlockSpec((1, SWA_HEADS, TM, 128), head),
        pl.BlockSpec((1, SWA_KV_HEADS, TM, 128), head),
        pl.BlockSpec((1, SWA_KV_HEADS, TM, 128), head),
        pl.BlockSpec((1, TM, CONV_CH), row),
    ]
    return pl.pallas_call(
        functools.partial(_prep_kernel, t_off=t_off),
        grid=(B, nt),
        in_specs=in_specs,
        out_specs=out_specs,
        out_shape=out_shape,
        compiler_params=_cp(("arbitrary", "arbitrary")),
        name="prep",
    )(first, rest, modsel, lw["n1"], lw["w_in"], lw["kvg"], lw["w_ukv"], lw["qg"], lw["w_uq"], lw["gq"], lw["gkn"], lw["gkp"],
      lw["gsq"], lw["gsk"], cos_t, sin_t)


KEY_CHUNK = 256


def _mixer_kernel(sink_ref, q_ref, k_ref, v_ref, u_ref, cw_ref, cb_ref, cg_ref, cbb_ref, sq_ref, sk_ref, sv_ref,
                  xa_ref, xb_ref, mod_ref, w1_ref, w2_ref, w3_ref, n2_ref, wrh_ref, wrl_ref, br_ref, tri_ref, upper_ref,
                  xn_ref, hf_ref, pos_ref, post_ref, gt_ref, meta_ref, cnt_ref,
                  o_ref, oc_ref, os_ref, s_ref, p_ref, cbuf_ref, run_ref, *, q_off, n_keys):
    @pl.when((pl.program_id(0) == 0) & (pl.program_id(1) == 0))
    def _():
        run_ref[...] = jnp.zeros_like(run_ref)

    qi = pl.program_id(1) + q_off

    def attend(nk):
        for h in range(MLA_HEADS):
            q = q_ref[0, h]
            macc = jnp.full((TM, LANES), -jnp.inf, F32)
            for c in range(nk // KEY_CHUNK):
                k = k_ref[0, h, c * KEY_CHUNK:(c + 1) * KEY_CHUNK, :]
                s = lax.dot_general(q, k, (((1,), (1,)), ((), ())), preferred_element_type=F32)
                s_ref[:,c * KEY_CHUNK:(c + 1) * KEY_CHUNK] = s
                for j in range(KEY_CHUNK // LANES):
                    macc = jnp.maximum(macc, s[:, j * LANES:(j + 1) * LANES])
            m = jnp.max(macc, axis=-1, keepdims=True)
            for c in range(nk // KEY_CHUNK):
                p = jnp.exp2(s_ref[:,c * KEY_CHUNK:(c + 1) * KEY_CHUNK] - m)
                p_ref[:,c * KEY_CHUNK:(c + 1) * KEY_CHUNK] = p.astype(BF16)
            ol = jnp.dot(p_ref[:,0:nk], v_ref[0, h, 0:nk, :], preferred_element_type=F32)
            o = ol[:, 0:MLA_V] / ol[:, MLA_V:MLA_V + 1]
            o_ref[0, :, 128 * h:128 * h + 128] = o.astype(BF16)

    def tile(is_ctx):
        _conv_tile(qi, n_keys, u_ref, cw_ref, cb_ref, cg_ref, cbb_ref, oc_ref, cbuf_ref)
        attend(TM if is_ctx else n_keys)
        _swa_tile(is_ctx, qi, n_keys, sink_ref, sq_ref, sk_ref, sv_ref, os_ref)
        _out_tile(o_ref, oc_ref, os_ref, xa_ref, xb_ref, mod_ref, w1_ref, w2_ref, w3_ref, n2_ref, wrh_ref, wrl_ref, br_ref,
                  tri_ref, upper_ref, xn_ref, hf_ref, pos_ref, post_ref, gt_ref, meta_ref, cnt_ref, run_ref, q_off)

    if q_off == 0:
        pl.when(qi == 0)(functools.partial(tile, True))
        pl.when(qi > 0)(functools.partial(tile, False))
    else:
        tile(False)


def _mixers(qm, km, vm, u, qs, ks, vs, first, rest, shift, modsel, lw, tri, upper, t_off):
    B, H, T, _ = qm.shape
    C = u.shape[2]
    D = first.shape[2]
    HS, G = qs.shape[1], ks.shape[1]
    nt = T // TM - t_off
    Tq = nt * TM
    qtile = lambda b, i: (b, 0, i + t_off, 0)
    whole = lambda b, i: (b, 0, 0, 0)
    row = lambda b, i: (b, i, 0)
    per_tile = lambda b, i: (b * nt + i, 0, 0)
    in_specs = [
        pl.BlockSpec(memory_space=pltpu.SMEM),
        pl.BlockSpec((1, H, TM, 256), qtile),
        pl.BlockSpec((1, H, T, 256), whole),
        pl.BlockSpec((1, H, T, 256), whole),
        pl.BlockSpec((1, T, C), lambda b, i: (b, 0, 0)),
        _full((CONV_WIDTH, C)),
        _full((1, C)),
        _full((1, C)),
        _full((1, C)),
        pl.BlockSpec((1, HS, TM, 128), qtile),
        pl.BlockSpec((1, G, T, 128), whole),
        pl.BlockSpec((1, G, T, 128), whole),
    ] + _stream_specs(D, t_off, shift) + [
        pl.BlockSpec((1, 1, 6, D), lambda b, i: (b, jnp.minimum(i + t_off, 1), 0, 0)),
        _full((512, D)),
        _full((256, D)),
        _full((512, D)),
        _full((1, D)),
        _full((D, LANES)),
        _full((D, LANES)),
        _full((1, LANES)),
        _full((TM, TM)),
        _full((LANES, LANES)),
    ]
    out_shape = [
        jax.ShapeDtypeStruct((B, Tq, D), F32),
        jax.ShapeDtypeStruct((B, Tq, D), BF16),
        jax.ShapeDtypeStruct((B, Tq, LANES), I32),
        jax.ShapeDtypeStruct((B * nt, 8, TM), I32),
        jax.ShapeDtypeStruct((B, Tq, LANES), F32),
        jax.ShapeDtypeStruct((B * nt, 8, LANES), I32),
        jax.ShapeDtypeStruct((8, LANES), F32),
    ]
    out_specs = [
        pl.BlockSpec((1, TM, D), row),
        pl.BlockSpec((1, TM, D), row),
        pl.BlockSpec((1, TM, LANES), row),
        pl.BlockSpec((1, 8, TM), per_tile),
        pl.BlockSpec((1, TM, LANES), row),
        pl.BlockSpec((1, 8, LANES), per_tile),
        pl.BlockSpec((8, LANES), lambda b, i: (0, 0)),
    ]
    scratch_shapes = [
        pltpu.VMEM((1, TM, H * MLA_V), BF16),
        pltpu.VMEM((1, TM, C), BF16),
        pltpu.VMEM((1, TM, HS * 128), BF16),
        pltpu.VMEM((TM, T), F32),
        pltpu.VMEM((TM, T), BF16),
        pltpu.VMEM((TM + 2 * CONV_HALO, C), F32),
        pltpu.VMEM((8, LANES), F32),
    ]
    return pl.pallas_call(
        functools.partial(_mixer_kernel, q_off=t_off, n_keys=T),
        grid=(B, nt),
        in_specs=in_specs,
        out_specs=out_specs,
        out_shape=out_shape,
        scratch_shapes=scratch_shapes,
        compiler_params=pltpu.CompilerParams(dimension_semantics=("arbitrary", "arbitrary"),
                                             vmem_limit_bytes=MIXER_VMEM_LIMIT),
        name="mixers",
    )(lw["sink"], qm, km, vm, u, lw["conv_w"], lw["conv_b"], lw["conv_g"], lw["conv_bb"], qs, ks, vs, first, rest, modsel,
      lw["w_o1"], lw["w_o2"], lw["w_o3"], lw["n2"], lw["wr_hi"], lw["wr_lo"], lw["br"], tri, upper)


SWA_SPAN = TM + 2 * SWA_WINDOW


def _swa_tile(is_ctx, qi, n_rows, sink_ref, q_ref, k_ref, v_ref, o_ref):
    nt_dims = (((1,), (1,)), ((), ()))

    def latent():
        start = jnp.clip(qi * TM - SWA_WINDOW, 0, n_rows - SWA_SPAN)
        start = pl.multiple_of(start, SWA_WINDOW)
        qpos = qi * TM + lax.broadcasted_iota(I32, (TM, SWA_SPAN), 0)
        kpos = start + lax.broadcasted_iota(I32, (TM, SWA_SPAN), 1)
        valid = (jnp.abs(qpos - kpos) <= SWA_WINDOW) & (kpos >= TM)
        for h in range(SWA_HEADS):
            g = h // (SWA_HEADS // SWA_KV_HEADS)
            q = q_ref[0, h]
            kl = k_ref[0, g, pl.ds(start, SWA_SPAN), :]
            vl = v_ref[0, g, pl.ds(start, SWA_SPAN), :]
            kc = k_ref[0, g, 0:TM, :]
            vc = v_ref[0, g, 0:TM, :]
            sl = lax.dot_general(q, kl, nt_dims, preferred_element_type=F32)
            sl = jnp.where(valid, sl, NEG)
            scx = lax.dot_general(q, kc, nt_dims, preferred_element_type=F32)
            sink = sink_ref[h]
            m = jnp.maximum(jnp.maximum(jnp.max(sl, axis=-1, keepdims=True), jnp.max(scx, axis=-1, keepdims=True)), sink)
            pl_ = jnp.exp(sl - m)
            pc = jnp.exp(scx - m)
            l = jnp.sum(pl_, axis=-1, keepdims=True) + jnp.sum(pc, axis=-1, keepdims=True) + jnp.exp(sink - m)
            o = (jnp.dot(pl_.astype(BF16), vl, preferred_element_type=F32)
                 + jnp.dot(pc.astype(BF16), vc, preferred_element_type=F32)) / l
            o_ref[0, :, 128 * h:128 * h + 128] = o.astype(BF16)

    def context():
        for h in range(SWA_HEADS):
            g = h // (SWA_HEADS // SWA_KV_HEADS)
            q = q_ref[0, h]
            kc = k_ref[0, g, 0:TM, :]
            vc = v_ref[0, g, 0:TM, :]
            scx = lax.dot_general(q, kc, nt_dims, preferred_element_type=F32)
            sink = sink_ref[h]
            m = jnp.maximum(jnp.max(scx, axis=-1, keepdims=True), sink)
            pc = jnp.exp(scx - m)
            l = jnp.sum(pc, axis=-1, keepdims=True) + jnp.exp(sink - m)
            o = jnp.dot(pc.astype(BF16), vc, preferred_element_type=F32) / l
            o_ref[0, :, 128 * h:128 * h + 128] = o.astype(BF16)

    if is_ctx:
        context()
    else:
        latent()


def _conv_tile(i, n_rows, u_ref, w_ref, b_ref, g_ref, bb_ref, o_ref, buf_ref):
    nt_all = n_rows // TM
    start = pl.multiple_of(i * TM, TM)
    ps = pl.multiple_of(jnp.maximum(start - CONV_HALO, 0), 8)
    ns = pl.multiple_of(jnp.minimum(start + TM, n_rows - CONV_HALO), 8)
    keep_prev = jnp.where(i <= 1, 0.0, 1.0)
    keep_next = jnp.where((i == 0) | (i == nt_all - 1), 0.0, 1.0)
    buf_ref[0:CONV_HALO, :] = u_ref[0, pl.ds(ps, CONV_HALO), :] * keep_prev
    buf_ref[CONV_HALO:CONV_HALO + TM, :] = u_ref[0, pl.ds(start, TM), :]
    buf_ref[CONV_HALO + TM:2 * CONV_HALO + TM, :] = u_ref[0, pl.ds(ns, CONV_HALO), :] * keep_next
    off = CONV_HALO - CONV_WIDTH // 2
    accs = [None] * 4
    for j in range(CONV_WIDTH):
        term = buf_ref[off + j:off + j + TM, :] * w_ref[j:j + 1, :]
        accs[j % 4] = term if accs[j % 4] is None else accs[j % 4] + term
    y = (accs[0] + accs[1]) + (accs[2] + accs[3]) + b_ref[...]
    mu = jnp.mean(y, axis=-1, keepdims=True)
    d = y - mu
    var = jnp.mean(d * d, axis=-1, keepdims=True)
    z = d * lax.rsqrt(var + EPS) * g_ref[...] + bb_ref[...]
    o_ref[0] = (z * jax.nn.sigmoid(z)).astype(BF16)


def _out_tile(om_ref, oc_ref, os_ref, xa_ref, xb_ref, mod_ref, w1_ref, w2_ref, w3_ref, n2_ref, wrh_ref, wrl_ref, br_ref,
              tri_ref, upper_ref, xn_ref, hf_ref, pos_ref, post_ref, gt_ref, meta_ref, cnt_ref, run_ref, t_off):
    d = functools.partial(jnp.dot, preferred_element_type=F32)
    mix = d(om_ref[0], w1_ref[...]) + d(oc_ref[0], w2_ref[...]) + d(os_ref[0], w3_ref[...])
    mod = mod_ref[0, 0]
    g1, sh2, sc2 = mod[2:3], mod[3:4], mod[4:5]
    xn = _stream_tile(xa_ref, xb_ref, t_off) + g1 * mix
    xn_ref[0] = xn
    hf = xn * lax.rsqrt(jnp.mean(xn * xn, axis=-1, keepdims=True) + EPS) * n2_ref[...]
    hf = hf * (1.0 + sc2) + sh2
    hf_ref[0] = hf.astype(BF16)

    hi, lo = _split(hf)
    logits = d(hi, wrh_ref[...]) + d(hi, wrl_ref[...]) + d(lo, wrh_ref[...]) + br_ref[...]
    lane = lax.broadcasted_iota(I32, (TM, LANES), 1)
    lane_f = lane.astype(F32)
    l = logits
    ohs, vals = [], []
    for _ in range(TOP_K):
        m = jnp.max(l, axis=-1, keepdims=True)
        idx = jnp.min(jnp.where(l == m, lane_f, float(LANES)), axis=-1, keepdims=True)
        oh = lane_f == idx
        ohs.append(oh)
        vals.append(m)
        l = jnp.where(oh, -jnp.inf, l)
    ex = [jnp.exp(v - vals[0]) for v in vals]
    den = ex[0] + ex[1] + ex[2] + ex[3]
    gates = [e / den for e in ex]

    oa = jnp.zeros((TM, LANES), F32)
    for oh in ohs:
        oa = oa + oh.astype(F32)
    hist = jnp.sum(oa, axis=0, keepdims=True)
    slot_rows = jnp.floor((hist + (RUN_CHUNK - 1)) * (1.0 / RUN_CHUNK)) * RUN_CHUNK
    slot_off = d(jnp.broadcast_to(slot_rows, (8, LANES)).astype(BF16), upper_ref[...])
    where_ = d(tri_ref[...], oa.astype(BF16)) + slot_off[0:1, :]
    poss = [jnp.sum(jnp.where(oh, where_, 0.0), axis=-1, keepdims=True).astype(I32) for oh in ohs]

    p_out = jnp.zeros((TM, LANES), I32)
    g_out = jnp.zeros((TM, LANES), F32)
    for k in range(TOP_K):
        p_out = jnp.where(lane == k, poss[k], p_out)
        g_out = jnp.where(lane == k, gates[k], g_out)
    pos_ref[0] = p_out
    post_ref[0] = jnp.transpose(p_out)[0:8, :]
    gt_ref[0] = g_out

    srow = lax.broadcasted_iota(I32, (8, LANES), 0)
    meta = jnp.where(srow == 0, hist, jnp.where(srow == 1, run_ref[...], jnp.where(srow == 2, slot_off, 0.0)))
    meta_ref[0] = meta.astype(I32)
    run_ref[...] = run_ref[...] + jnp.floor((hist + (ROW_ALIGN - 1)) * (1.0 / ROW_ALIGN)) * ROW_ALIGN
    cnt_ref[...] = run_ref[...]


SUB = 8
TILE_SUBLANES = 8
ROW_ALIGN = TILE_SUBLANES // SUB
RUN_CHUNK = 16
SLOT_ROWS = TM * TOP_K + N_EXPERTS * RUN_CHUNK


def _rows(ref, row0, nrows):
    start = row0 * SUB if isinstance(row0, int) else pl.multiple_of(row0 * SUB, TILE_SUBLANES)
    return ref.at[pl.ds(start, nrows * SUB), :]


def _to_tiles(ref, val, nrows):
    for j in range(SUB):
        ref[pl.ds(j, nrows, stride=SUB), :] = val[:, LANES * j:LANES * (j + 1)]


def _from_tiles(ref, nrows):
    return jnp.concatenate([ref[pl.ds(j, nrows, stride=SUB), :] for j in range(SUB)], axis=1)


def _slot_matrix(pos, weights):
    col = lax.broadcasted_iota(I32, (TM, SLOT_ROWS), 1)
    m = jnp.zeros((TM, SLOT_ROWS), F32)
    for k in range(TOP_K):
        m = jnp.where(col == pos[:, k:k + 1], weights[k], m)
    return m.astype(BF16)


def _run_copies(meta_ref, pst_ref, buf_ref, hbm_ref, sem, to_hbm):
    total = jnp.int32(0)
    for e in range(N_EXPERTS):
        n = meta_ref[0, 0, e]
        nch = (n + (RUN_CHUNK - 1)) // RUN_CHUNK
        seg0 = pst_ref[e] + meta_ref[0, 1, e]
        slot0 = meta_ref[0, 2, e]

        def chunk(c, carry):
            a = _rows(buf_ref, slot0 + c * RUN_CHUNK, RUN_CHUNK)
            b = _rows(hbm_ref, seg0 + c * RUN_CHUNK, RUN_CHUNK)
            (pltpu.make_async_copy(a, b, sem) if to_hbm else pltpu.make_async_copy(b, a, sem)).start(priority=e % 2)
            return carry

        lax.fori_loop(0, nch, chunk, 0)
        total = total + nch
    return total


def _run_waits(total, buf_ref, hbm_ref, sem, to_hbm):
    a = _rows(buf_ref, 0, RUN_CHUNK)
    b = _rows(hbm_ref, 0, RUN_CHUNK)

    def one(c, carry):
        (pltpu.make_async_copy(a, b, sem) if to_hbm else pltpu.make_async_copy(b, a, sem)).wait()
        return carry

    lax.fori_loop(0, total, one, 0)


def _run_total(meta_ref):
    total = jnp.int32(0)
    for e in range(N_EXPERTS):
        total = total + (meta_ref[0, 0, e] + (RUN_CHUNK - 1)) // RUN_CHUNK
    return total


def _disp_kernel(pst_ref, cnt_ref, pad_ref, na_ref, meta_ref, hf_ref, post_ref, xb_ref, srt_ref, zero_ref, tot_ref,
                 sem, zsem, *, n_blocks):
    step = pl.program_id(0) * pl.num_programs(1) + pl.program_id(1)
    n_steps = pl.num_programs(0) * pl.num_programs(1)
    slot = step % 2
    first = step == 0

    @pl.when(first)
    def _():
        zero_ref[...] = jnp.zeros_like(zero_ref)

        def zblock(j):
            return pltpu.make_async_copy(zero_ref, _rows(xb_ref, j * EXPERT_BLOCK, EXPERT_BLOCK), zsem)

        def zb_start(j, carry):
            zblock(j).start()
            return carry

        def zb_wait(j, carry):
            zblock(j).wait()
            return carry

        lax.fori_loop(na_ref[0], n_blocks, zb_start, 0)
        lax.fori_loop(na_ref[0], n_blocks, zb_wait, 0)

        for e in range(N_EXPERTS):
            lo = pst_ref[e] + cnt_ref[e]
            nrow = pad_ref[e] - cnt_ref[e]
            nz = nrow // RUN_CHUNK
            lo1 = lo + nz * RUN_CHUNK
            n1 = (nrow - nz * RUN_CHUNK) // ROW_ALIGN

            def zchunk(c):
                return pltpu.make_async_copy(_rows(zero_ref, 0, RUN_CHUNK), _rows(xb_ref, lo + c * RUN_CHUNK, RUN_CHUNK), zsem)

            def zrow(r):
                return pltpu.make_async_copy(_rows(zero_ref, 0, ROW_ALIGN), _rows(xb_ref, lo1 + r * ROW_ALIGN, ROW_ALIGN), zsem)

            def zc_start(c, carry):
                zchunk(c).start()
                return carry

            def zc_wait(c, carry):
                zchunk(c).wait()
                return carry

            def zr_start(r, carry):
                zrow(r).start()
                return carry

            def zr_wait(r, carry):
                zrow(r).wait()
                return carry

            lax.fori_loop(0, nz, zc_start, 0)
            lax.fori_loop(0, n1, zr_start, 0)
            lax.fori_loop(0, nz, zc_wait, 0)
            lax.fori_loop(0, n1, zr_wait, 0)

    srow = lax.broadcasted_iota(I32, (SLOT_ROWS, TM), 0)
    post = post_ref[0]
    perm = jnp.zeros((SLOT_ROWS, TM), F32)
    for k in range(TOP_K):
        perm = jnp.where(srow == post[k:k + 1, :], 1.0, perm)
    srt = jnp.dot(perm.astype(BF16), hf_ref[0], preferred_element_type=F32)
    _to_tiles(srt_ref.at[slot], srt, SLOT_ROWS)

    @pl.when(step > 0)
    def _():
        _run_waits(tot_ref[1 - slot], srt_ref.at[1 - slot], xb_ref, sem.at[1 - slot], True)

    total = _run_copies(meta_ref, pst_ref, srt_ref.at[slot], xb_ref, sem.at[slot], True)
    tot_ref[slot] = total

    @pl.when(step == n_steps - 1)
    def _():
        _run_waits(total, srt_ref.at[slot], xb_ref, sem.at[slot], True)


def _dispatch(pstart, counts, padded, n_act, meta, hf, pos, n_buf):
    B, Tq, D = hf.shape
    nt = Tq // TM
    assert D == SUB * LANES
    grid_spec = pltpu.PrefetchScalarGridSpec(
        num_scalar_prefetch=4,
        grid=(B, nt),
        in_specs=[
            pl.BlockSpec((1, 8, LANES), lambda b, i, *_: (b * nt + i, 0, 0), memory_space=pltpu.SMEM),
            pl.BlockSpec((1, TM, D), lambda b, i, *_: (b, i, 0)),
            pl.BlockSpec((1, 8, TM), lambda b, i, *_: (b * nt + i, 0, 0)),
        ],
        out_specs=pl.BlockSpec(memory_space=pl.ANY),
        scratch_shapes=[pltpu.VMEM((2, SLOT_ROWS * SUB, LANES), F32), pltpu.VMEM((EXPERT_BLOCK * SUB, LANES), F32),
                        pltpu.SMEM((2,), I32), pltpu.SemaphoreType.DMA((2,)), pltpu.SemaphoreType.DMA],
    )
    return pl.pallas_call(
        functools.partial(_disp_kernel, n_blocks=n_buf // EXPERT_BLOCK),
        grid_spec=grid_spec,
        out_shape=jax.ShapeDtypeStruct((n_buf * SUB, LANES), F32),
        compiler_params=_cp(("arbitrary", "arbitrary")),
        name="dispatch",
    )(pstart, counts, padded, n_act, meta, hf, pos)


W_ROWS = 512
W_DELAY = 2


def _exp_kernel(be_ref, bx_ref, na_ref, nx_ref, ps_ref, pr_ref, npv_ref, x_ref, b1a_ref, b1b_ref, b2_ref, w1_hbm, w2_hbm,
                o_ref, st1, st2, w1a_s, w1b_s, w2_s, sem, *, e_off):
    j = pl.program_id(0)
    active = j < na_ref[0]
    e_cur = be_ref[j]
    p = pr_ref[e_cur]
    k = bx_ref[j] - ps_ref[e_cur]
    nxt = nx_ref[e_cur]
    n_prev = npv_ref[e_cur]
    n_slices = st1.shape[0] // W_ROWS

    def copies(e):
        return (pltpu.make_async_copy(w1_hbm.at[e], st1, sem.at[0]), pltpu.make_async_copy(w2_hbm.at[e], st2, sem.at[1]))

    def fetch(e):
        for cp in copies(e):
            cp.start(priority=1)

    def fetch_wait():
        for cp in copies(0):
            cp.wait()

    def convert(slot, s):
        r0 = pl.multiple_of(s * W_ROWS, W_ROWS)
        lane = lax.broadcasted_iota(I32, (W_ROWS, LANES), 1)
        lo = lane < 64
        idx = jnp.where(lo, 2 * lane, 2 * (lane - 64) + 1)
        for c in range(st1.shape[1] // (2 * LANES)):
            a = st1[pl.ds(r0, W_ROWS), 2 * LANES * c:2 * LANES * c + LANES]
            b = st1[pl.ds(r0, W_ROWS), 2 * LANES * c + LANES:2 * LANES * (c + 1)]
            pa = jnp.take_along_axis(a, idx, axis=1)
            pb = jnp.take_along_axis(b, idx, axis=1)
            ev = jnp.where(lo, pa, pltpu.roll(pb, 64, axis=1))
            od = jnp.where(lo, pltpu.roll(pa, 64, axis=1), pb)
            w1a_s[slot, pl.ds(r0, W_ROWS), LANES * c:LANES * (c + 1)] = ev.astype(BF16)
            w1b_s[slot, pl.ds(r0, W_ROWS), LANES * c:LANES * (c + 1)] = od.astype(BF16)
        w2_s[slot, pl.ds(r0, W_ROWS), :] = st2[pl.ds(r0, W_ROWS), :].astype(BF16)

    def convert_range(slot, s0):
        def one(s, carry):
            convert(slot, s)
            return carry

        lax.fori_loop(s0, n_slices, one, 0)

    @pl.when(j == 0)
    def _():
        fetch(be_ref[0] + e_off)
        fetch_wait()
        convert_range(0, 0)

    @pl.when(active & (k == 0) & (j > 0))
    def _():
        @pl.when(n_prev <= W_DELAY)
        def _():
            fetch_wait()

        convert_range(p, jnp.clip(n_prev - W_DELAY, 0, n_slices))

    @pl.when(active & (k == 0) & (nxt >= 0))
    def _():
        fetch(nxt + e_off)

    conv = active & (nxt >= 0) & (k >= W_DELAY) & (k < W_DELAY + n_slices)

    @pl.when(conv & (k == W_DELAY))
    def _():
        fetch_wait()

    def ffn(with_convert):
        if with_convert:
            convert(1 - p, k - W_DELAY)
        x = _from_tiles(x_ref, EXPERT_BLOCK).astype(BF16)
        ug = jnp.dot(x, w1a_s[p], preferred_element_type=F32) + b1a_ref[0]
        ul = jnp.dot(x, w1b_s[p], preferred_element_type=F32) + b1b_ref[0]
        xg = jnp.minimum(ug, SWIGLU_LIMIT)
        xl = jnp.clip(ul, -SWIGLU_LIMIT, SWIGLU_LIMIT)
        act = xg * jax.nn.sigmoid(SWIGLU_ALPHA * xg) * (xl + 1.0)
        y = jnp.dot(act.astype(BF16), w2_s[p], preferred_element_type=F32) + b2_ref[0]
        _to_tiles(o_ref, y, EXPERT_BLOCK)

    pl.when(conv)(functools.partial(ffn, True))
    pl.when(active & jnp.logical_not(conv))(functools.partial(ffn, False))

    @pl.when(jnp.logical_not(active))
    def _():
        o_ref[...] = jnp.zeros_like(o_ref)


def _experts(sched, xb, lw):
    nb = xb.shape[0] // (EXPERT_BLOCK * SUB)
    De, D = lw["w2"].shape[1:]
    assert De == D and D % W_ROWS == 0
    wmap = lambda j, be, *_: (be[j], 0, 0)
    grid_spec = pltpu.PrefetchScalarGridSpec(
        num_scalar_prefetch=len(sched),
        grid=(nb,),
        in_specs=[
            pl.BlockSpec((EXPERT_BLOCK * SUB, LANES), lambda j, be, bx, *_: (bx[j], 0)),
            pl.BlockSpec((1, 1, De), wmap),
            pl.BlockSpec((1, 1, De), wmap),
            pl.BlockSpec((1, 1, D), wmap),
            pl.BlockSpec(memory_space=pl.ANY),
            pl.BlockSpec(memory_space=pl.ANY),
        ],
        out_specs=pl.BlockSpec((EXPERT_BLOCK * SUB, LANES), lambda j, *_: (j, 0)),
        scratch_shapes=[pltpu.VMEM((D, 2 * De), F32), pltpu.VMEM((De, D), F32),
                        pltpu.VMEM((2, D, De), BF16), pltpu.VMEM((2, D, De), BF16), pltpu.VMEM((2, De, D), BF16),
                        pltpu.SemaphoreType.DMA((2,))],
    )
    return pl.pallas_call(
        functools.partial(_exp_kernel, e_off=lw["e_off"]),
        grid_spec=grid_spec,
        out_shape=jax.ShapeDtypeStruct(xb.shape, F32),
        compiler_params=pltpu.CompilerParams(dimension_semantics=("arbitrary",), vmem_limit_bytes=EXPERT_VMEM_LIMIT),
        name="experts",
    )(*sched, xb, lw["b1a"], lw["b1b"], lw["b2"], lw["w1"], lw["w2"])


def _comb_kernel(pst_ref, meta_ref, meta_next_ref, pos_ref, gt_ref, xn_ref, mod_ref, yb_ref, o_ref, buf_ref, sem):
    step = pl.program_id(0) * pl.num_programs(1) + pl.program_id(1)
    n_steps = pl.num_programs(0) * pl.num_programs(1)
    slot = step % 2

    @pl.when(step == 0)
    def _():
        buf_ref[...] = jnp.zeros_like(buf_ref)
        _run_copies(meta_ref, pst_ref, buf_ref.at[0], yb_ref, sem.at[0], False)

    @pl.when(step < n_steps - 1)
    def _():
        _run_copies(meta_next_ref, pst_ref, buf_ref.at[1 - slot], yb_ref, sem.at[1 - slot], False)

    gt = gt_ref[0]
    g = _slot_matrix(pos_ref[0], [gt[:, k:k + 1] for k in range(TOP_K)])
    _run_waits(_run_total(meta_ref), buf_ref.at[slot], yb_ref, sem.at[slot], False)
    rows = _from_tiles(buf_ref.at[slot], SLOT_ROWS).astype(BF16)
    y = jnp.dot(g, rows, preferred_element_type=F32)
    g2 = mod_ref[0, 0][5:6]
    o_ref[0] = xn_ref[0] + g2 * y


def _combine(pstart, meta, pos, gates, xn, modsel, yb, t_off):
    B, Tq, D = xn.shape
    nt = Tq // TM
    grid_spec = pltpu.PrefetchScalarGridSpec(
        num_scalar_prefetch=1,
        grid=(B, nt),
        in_specs=[
            pl.BlockSpec((1, 8, LANES), lambda b, i, *_: (b * nt + i, 0, 0), memory_space=pltpu.SMEM),
            pl.BlockSpec((1, 8, LANES), lambda b, i, *_: (jnp.minimum(b * nt + i + 1, B * nt - 1), 0, 0),
                         memory_space=pltpu.SMEM),
            pl.BlockSpec((1, TM, LANES), lambda b, i, *_: (b, i, 0)),
            pl.BlockSpec((1, TM, LANES), lambda b, i, *_: (b, i, 0)),
            pl.BlockSpec((1, TM, D), lambda b, i, *_: (b, i, 0)),
            pl.BlockSpec((1, 1, 6, D), lambda b, i, *_: (b, jnp.minimum(i + t_off, 1), 0, 0)),
            pl.BlockSpec(memory_space=pl.ANY),
        ],
        out_specs=pl.BlockSpec((1, TM, D), lambda b, i, *_: (b, i, 0)),
        scratch_shapes=[pltpu.VMEM((2, SLOT_ROWS * SUB, LANES), F32), pltpu.SemaphoreType.DMA((2,))],
    )
    return pl.pallas_call(
        _comb_kernel,
        grid_spec=grid_spec,
        out_shape=jax.ShapeDtypeStruct((B, Tq, D), F32),
        compiler_params=_cp(("arbitrary", "arbitrary")),
        name="combine",
    )(pstart, meta, meta, pos, gates, xn, modsel, yb)


def _take_cols(w, cols):
    cols = np.asarray(cols)
    out = jnp.take(w, jnp.asarray(np.maximum(cols, 0)), axis=-1)
    return jnp.where(jnp.asarray(cols >= 0), out, 0.0)


def _in_cols():
    pi = _PI
    cols = list(range(0, 256))
    cols += [256 + i for i in range(64)] + [256 + pi[i] for i in range(64)]
    for g in range(SWA_KV_HEADS):
        base = 320 + 64 * g
        cols += [base + i for i in range(64)] + [base + pi[i] for i in range(64)]
    for g in range(SWA_KV_HEADS):
        base = 448 + 64 * g
        cols += [base + i for i in range(64)] + [-1] * 64
    cols += list(range(576, 960))
    for h in range(SWA_HEADS):
        base = 960 + 64 * h
        cols += [base + i for i in range(64)] + [base + pi[i] for i in range(64)]
    cols += list(range(1216, 1728))
    return cols


def _layer_weights(l, a):
    pi = _PI
    lw = {}
    lw["n1"] = a["norm1_g"][l][None, :]
    lw["n2"] = a["norm2_g"][l][None, :]
    lw["w_in"] = _take_cols(a["w_in"][l], _in_cols()).astype(BF16)
    lw["kvg"] = a["mla_kv_norm"][l][None, :]
    lw["qg"] = a["mla_q_norm"][l][None, :]
    uq_cols = []
    for h in range(MLA_HEADS):
        base = MLA_QK * h
        uq_cols += [base + i for i in range(128)] + [base + 128 + i for i in range(64)] + [base + 128 + pi[i] for i in range(64)]
    lw["w_uq"] = _take_cols(a["mla_w_uq"][l], uq_cols).astype(BF16)
    ukv_cols = [256 * h + i for h in range(MLA_HEADS) for i in range(128)]
    ukv_cols += [256 * h + 128 + i for h in range(MLA_HEADS) for i in range(128)]
    lw["w_ukv"] = _take_cols(a["mla_w_ukv"][l], ukv_cols).astype(BF16)
    gq = a["mla_q_head_norm"][l]
    lw["gq"] = (jnp.concatenate([gq[:128], gq[128:], gq[128:][pi]]) * (MLA_QK ** -0.5 * LOG2E))[None, :]
    gk = a["mla_k_head_norm"][l]
    lw["gkn"] = gk[:128][None, :]
    lw["gkp"] = jnp.concatenate([gk[128:], gk[128:][pi]])[None, :]
    sq = a["swa_q_norm"][l]
    lw["gsq"] = (jnp.concatenate([sq, sq[pi]]) * (SWA_HEAD_DIM ** -0.5))[None, :]
    sk = a["swa_k_norm"][l]
    lw["gsk"] = jnp.concatenate([sk, sk[pi]])[None, :]
    lw["conv_w"] = a["conv_w"][l]
    lw["conv_b"] = a["conv_b"][l][None, :]
    lw["conv_g"] = a["conv_ln_g"][l][None, :]
    lw["conv_bb"] = a["conv_ln_b"][l][None, :]
    lw["sink"] = a["swa_sink"][l]
    wo = a["w_out"][l]
    lw["w_o1"] = wo[0:512].astype(BF16)
    lw["w_o2"] = wo[512:768].astype(BF16)
    o3 = wo[768:1024].reshape(SWA_HEADS, SWA_HEAD_DIM, -1)
    lw["w_o3"] = jnp.concatenate([o3, jnp.zeros_like(o3)], axis=1).reshape(SWA_HEADS * 128, -1).astype(BF16)
    wr = jnp.pad(a["router_w"][l], ((0, 0), (0, LANES - N_EXPERTS)))
    lw["wr_hi"] = wr.astype(BF16)
    lw["wr_lo"] = (wr - lw["wr_hi"].astype(F32)).astype(BF16)
    lw["br"] = jnp.pad(a["router_b"][l], (0, LANES - N_EXPERTS), constant_values=NEG)[None, :]
    lw["w1"] = a["exp_w1"].reshape((-1,) + a["exp_w1"].shape[2:])
    lw["e_off"] = l * a["exp_w1"].shape[1]
    b1 = a["exp_b1"][l]
    lw["b1a"] = b1[:, None, 0::2]
    lw["b1b"] = b1[:, None, 1::2]
    lw["w2"] = a["exp_w2"].reshape((-1,) + a["exp_w2"].shape[2:])
    lw["b2"] = a["exp_b2"][l][:, None, :]
    return lw


def _rope_tables(n_ctx, n_lat):
    q = MLA_ROPE // 4
    n = jnp.arange(n_lat, dtype=I32)
    row = (n // GRID_W).astype(F32)
    col = (n % GRID_W).astype(F32)
    inv = ROPE_BASE ** (-jnp.arange(q, dtype=F32) / q)
    ang_r = row[:, None] * inv
    ang_c = col[:, None] * inv
    cos = jnp.concatenate([jnp.cos(ang_r), jnp.cos(ang_r), jnp.cos(ang_c), jnp.cos(ang_c)], axis=1)
    sin = jnp.concatenate([-jnp.sin(ang_r), jnp.sin(ang_r), -jnp.sin(ang_c), jnp.sin(ang_c)], axis=1)
    cos = jnp.concatenate([jnp.ones((n_ctx, 64), F32), cos], axis=0)
    sin = jnp.concatenate([jnp.zeros((n_ctx, 64), F32), sin], axis=0)
    z = jnp.zeros_like(cos)
    return jnp.concatenate([cos, z], axis=1), jnp.concatenate([sin, z], axis=1)


def _routing_tables(cnt_f, n_blocks):
    counts = cnt_f[0, :N_EXPERTS].astype(I32)
    padded = (counts + (RUN_CHUNK - 1) + EXPERT_BLOCK - 1) // EXPERT_BLOCK * EXPERT_BLOCK
    padded = jnp.where(counts > 0, padded, 0)
    pend = jnp.cumsum(padded)
    pstart = pend - padded
    n_act = pend[-1] // EXPERT_BLOCK
    blk = jnp.minimum(jnp.arange(n_blocks, dtype=I32), n_act - 1)
    blk_e = jnp.sum((pend[None, :] <= (blk * EXPERT_BLOCK)[:, None]).astype(I32), axis=1)
    blk_e = jnp.minimum(blk_e, N_EXPERTS - 1)
    nbk = padded // EXPERT_BLOCK
    has = nbk > 0
    ids = jnp.arange(N_EXPERTS, dtype=I32)
    later = has[None, :] & (ids[None, :] > ids[:, None])
    nxt_e = jnp.min(jnp.where(later, ids[None, :], N_EXPERTS), axis=1)
    nxt_e = jnp.where(nxt_e == N_EXPERTS, -1, nxt_e)
    earlier = has[None, :] & (ids[None, :] < ids[:, None])
    prv_e = jnp.max(jnp.where(earlier, ids[None, :], -1), axis=1)
    nb_prev_e = jnp.sum(jnp.where(ids[None, :] == prv_e[:, None], nbk[None, :], 0), axis=1)
    set_e = (jnp.cumsum(has.astype(I32)) - 1) % 2
    sched = (blk_e, blk, n_act.reshape(1), nxt_e, pstart // EXPERT_BLOCK, set_e, nb_prev_e)
    sched = tuple(s.astype(I32) for s in sched)
    return counts, padded.astype(I32), pstart.astype(I32), sched


def kernel(x, c, ctx, c_ctx, norm1_g, norm2_g, w_ada, b_ada, w_in, mla_q_norm, mla_kv_norm, mla_w_uq, mla_w_ukv, mla_q_head_norm, mla_k_head_norm, conv_w, conv_b, conv_ln_g, conv_ln_b, swa_q_norm, swa_k_norm, swa_sink, w_out, router_w, router_b, exp_w1, exp_b1, exp_w2, exp_b2):
    a = dict(norm1_g=norm1_g, norm2_g=norm2_g, w_in=w_in, mla_q_norm=mla_q_norm, mla_kv_norm=mla_kv_norm,
             mla_w_uq=mla_w_uq, mla_w_ukv=mla_w_ukv, mla_q_head_norm=mla_q_head_norm, mla_k_head_norm=mla_k_head_norm,
             conv_w=conv_w, conv_b=conv_b, conv_ln_g=conv_ln_g, conv_ln_b=conv_ln_b, swa_q_norm=swa_q_norm,
             swa_k_norm=swa_k_norm, swa_sink=swa_sink, w_out=w_out, router_w=router_w, router_b=router_b,
             exp_w1=exp_w1, exp_b1=exp_b1, exp_w2=exp_w2, exp_b2=exp_b2)
    B, S, D = x.shape
    n_ctx = ctx.shape[1]
    depth = w_ada.shape[0]
    assert n_ctx == TM and S % TM == 0 and B + 1 <= 16
    T = n_ctx + S

    s_in = jnp.zeros((16, D), F32).at[:B].set(c).at[B].set(c_ctx)
    mods = _ada(s_in, w_ada, b_ada)
    cos_t, sin_t = _rope_tables(n_ctx, S)
    tri = jnp.tril(jnp.ones((TM, TM), F32), -1).astype(BF16)
    upper = jnp.triu(jnp.ones((LANES, LANES), F32), 1).astype(BF16)

    first, rest, shift = ctx, x, 1
    for l in range(depth):
        last = l == depth - 1
        t_off = 1 if last else 0
        lw = _layer_weights(l, a)
        m = mods[l].reshape(16, 6, D)
        modsel = jnp.stack([jnp.broadcast_to(m[B], (B, 6, D)), m[:B]], axis=1)

        qm, km, vm, qs, ks, vs, u = _prep(first, rest, shift, modsel, lw, cos_t, sin_t, 0)
        xn, hf, pos, pos_t, gt_o, meta, cnt = _mixers(qm, km, vm, u, qs, ks, vs, first, rest, shift, modsel, lw, tri,
                                                      upper, t_off)

        n_tok = B * (T - t_off * TM)
        nk = n_tok * TOP_K
        n_align = (n_tok // TM) * N_EXPERTS * (ROW_ALIGN - 1)
        n_buf = -(-(nk + n_align + N_EXPERTS * (RUN_CHUNK - 1 + EXPERT_BLOCK - 1)) // EXPERT_BLOCK) * EXPERT_BLOCK
        counts, padded, pstart, sched = _routing_tables(cnt, n_buf // EXPERT_BLOCK)
        xb = _dispatch(pstart, counts, padded, sched[2], meta, hf, pos_t, n_buf)
        yb = _experts(sched, xb, lw)
        xu = _combine(pstart, meta, pos, gt_o, xn, modsel, yb, t_off)
        first, rest, shift = xu, xu, 0
    return xu
```

```python
import functools

import jax
import jax.numpy as jnp
from jax import lax
from jax.experimental import pallas as pl
from jax.experimental.pallas import tpu as pltpu

F32 = jnp.float32
BF16 = jnp.bfloat16
I32 = jnp.int32

GRID_W = 64
ROPE_BASE = 10000.0
EPS = 1e-6
MLA_HEADS = 4
MLA_NOPE = 128
MLA_ROPE = 64
MLA_V = 128
MLA_QK = MLA_NOPE + MLA_ROPE
MLA_Q_RANK = 384
MLA_KV_RANK = 256
CONV_CH = 256
CONV_WIDTH = 31
SWA_HEADS = 4
SWA_KV_HEADS = 2
SWA_HEAD_DIM = 64
SWA_WINDOW = 128
N_EXPERTS = 32
TOP_K = 4
SWIGLU_LIMIT = 7.0
SWIGLU_ALPHA = 1.702
EXPERT_BLOCK = 512

LANES = 128
TM = 256
CONV_HALO = 16
VMEM_LIMIT = 48 * 1024 * 1024
MIXER_VMEM_LIMIT = 56 * 1024 * 1024
EXPERT_VMEM_LIMIT = 56 * 1024 * 1024
NEG = -1e30
LOG2E = 1.4426950408889634


def _cp(sem):
    return pltpu.CompilerParams(dimension_semantics=sem, vmem_limit_bytes=VMEM_LIMIT)


def _full(shape):
    n = len(shape)
    return pl.BlockSpec(shape, lambda *a, _n=n: (0,) * _n)


def _split(x):
    hi = x.astype(BF16)
    lo = (x - hi.astype(F32)).astype(BF16)
    return hi, lo


def _dot3(a, b):
    ah, al = _split(a)
    bh, bl = _split(b)
    d = functools.partial(jnp.dot, preferred_element_type=F32)
    return d(ah, bh) + d(ah, bl) + d(al, bh)


def _ada_kernel(s_ref, w_ref, b_ref, o_ref):
    s = s_ref[...]
    s = s * jax.nn.sigmoid(s)
    o_ref[0] = _dot3(s, w_ref[0]) + b_ref[0]


def _ada(s_in, w_ada, b_ada):
    L, D, N = w_ada.shape
    tn = 1536
    return pl.pallas_call(
        _ada_kernel,
        grid=(L, N // tn),
        in_specs=[
            pl.BlockSpec((16, D), lambda l, j: (0, 0)),
            pl.BlockSpec((1, D, tn), lambda l, j: (l, 0, j)),
            pl.BlockSpec((1, 1, tn), lambda l, j: (l, 0, j)),
        ],
        out_specs=pl.BlockSpec((1, 16, tn), lambda l, j: (l, 0, j)),
        out_shape=jax.ShapeDtypeStruct((L, 16, N), F32),
        compiler_params=_cp(("arbitrary", "arbitrary")),
        name="ada",
    )(s_in, w_ada, b_ada.reshape(L, 1, N))


def _rope(x, c, s):
    return x * c + pltpu.roll(x, 64, axis=1) * s


def _stream_specs(D, t_off, shift):
    return [pl.BlockSpec((1, TM, D), lambda b, i, *_: (b, 0, 0)),
            pl.BlockSpec((1, TM, D), lambda b, i, *_: (b, jnp.maximum(i + t_off - shift, 0), 0))]


def _stream_tile(first_ref, rest_ref, t_off):
    return jnp.where(pl.program_id(1) + t_off == 0, first_ref[0], rest_ref[0])


def _prep_kernel(xa_ref, xb_ref, mod_ref, n1_ref, win_ref, kvg_ref, wukv_ref, qg_ref, wuq_ref, gq_ref, gkn_ref, gkp_ref,
                 gsq_ref, gsk_ref, cos_ref, sin_ref, qm_ref, km_ref, vm_ref, qs_ref, ks_ref, vs_ref, u_ref, *, t_off):
    x = _stream_tile(xa_ref, xb_ref, t_off)
    mod = mod_ref[0, 0]
    sh, sc = mod[0:1], mod[1:2]
    y = x * lax.rsqrt(jnp.mean(x * x, axis=-1, keepdims=True) + EPS) * n1_ref[...]
    h = y * (1.0 + sc) + sh
    p = jnp.dot(h.astype(BF16), win_ref[...], preferred_element_type=F32)
    c = cos_ref[...]
    s = sin_ref[...]
    lane = lax.broadcasted_iota(I32, (TM, LANES), 1)

    def ss_lo(v):
        return 0.5 * jnp.sum(v * v, axis=-1, keepdims=True)

    ckv = p[:, 0:256]
    ckvn = ckv * lax.rsqrt(jnp.mean(ckv * ckv, axis=-1, keepdims=True) + EPS) * kvg_ref[...]
    kv = jnp.dot(ckvn.astype(BF16), wukv_ref[...], preferred_element_type=F32)
    kpe = p[:, 256:384]
    ss_pe = ss_lo(kpe)
    kpe_rot = _rope(kpe * gkp_ref[...], c, s)
    for hh in range(MLA_HEADS):
        kn = kv[:, 128 * hh:128 * hh + 128]
        r = lax.rsqrt((jnp.sum(kn * kn, axis=-1, keepdims=True) + ss_pe) * (1.0 / MLA_QK) + EPS)
        km_ref[0, hh, :, 0:128] = (kn * r * gkn_ref[...]).astype(BF16)
        km_ref[0, hh, :, 128:256] = (kpe_rot * r).astype(BF16)
        vm_ref[0, hh, :, 0:128] = kv[:, 512 + 128 * hh:640 + 128 * hh].astype(BF16)
        vm_ref[0, hh, :, 128:256] = (lane == 0).astype(BF16)

    cq = p[:, 896:1280]
    cqn = cq * lax.rsqrt(jnp.mean(cq * cq, axis=-1, keepdims=True) + EPS) * qg_ref[...]
    q = jnp.dot(cqn.astype(BF16), wuq_ref[...], preferred_element_type=F32)
    gq = gq_ref[...]
    for hh in range(MLA_HEADS):
        qn = q[:, 256 * hh:256 * hh + 128]
        qp = q[:, 256 * hh + 128:256 * hh + 256]
        r = lax.rsqrt((jnp.sum(qn * qn, axis=-1, keepdims=True) + ss_lo(qp)) * (1.0 / MLA_QK) + EPS)
        qm_ref[0, hh, :, 0:128] = (qn * r * gq[:, 0:128]).astype(BF16)
        qm_ref[0, hh, :, 128:256] = _rope(qp * r * gq[:, 128:256], c, s).astype(BF16)

    for g in range(SWA_KV_HEADS):
        xk = p[:, 384 + 128 * g:512 + 128 * g]
        r = lax.rsqrt(ss_lo(xk) * (1.0 / SWA_HEAD_DIM) + EPS)
        ks_ref[0, g] = _rope(xk * r * gsk_ref[...], c, s).astype(BF16)
        vs_ref[0, g] = p[:, 640 + 128 * g:768 + 128 * g].astype(BF16)
    for hh in range(SWA_HEADS):
        xq = p[:, 1280 + 128 * hh:1408 + 128 * hh]
        r = lax.rsqrt(ss_lo(xq) * (1.0 / SWA_HEAD_DIM) + EPS)
        qs_ref[0, hh] = _rope(xq * r * gsq_ref[...], c, s).astype(BF16)

    u_ref[0] = p[:, 1792:2048] * jax.nn.sigmoid(p[:, 2048:2304])


def _prep(first, rest, shift, modsel, lw, cos_t, sin_t, t_off):
    B, _, D = first.shape
    T = rest.shape[1] + shift * TM
    nt = T // TM - t_off
    ncol = lw["w_in"].shape[1]
    row = lambda b, i: (b, i + t_off, 0)
    head = lambda b, i: (b, 0, i + t_off, 0)
    in_specs = _stream_specs(D, t_off, shift) + [
        pl.BlockSpec((1, 1, 6, D), lambda b, i: (b, jnp.minimum(i + t_off, 1), 0, 0)),
        _full((1, D)),
        _full((D, ncol)),
        _full((1, MLA_KV_RANK)),
        _full((MLA_KV_RANK, 1024)),
        _full((1, MLA_Q_RANK)),
        _full((MLA_Q_RANK, 1024)),
        _full((1, 256)),
        _full((1, 128)),
        _full((1, 128)),
        _full((1, 128)),
        _full((1, 128)),
        pl.BlockSpec((TM, LANES), lambda b, i: (i + t_off, 0)),
        pl.BlockSpec((TM, LANES), lambda b, i: (i + t_off, 0)),
    ]
    out_shape = [
        jax.ShapeDtypeStruct((B, MLA_HEADS, T, 256), BF16),
        jax.ShapeDtypeStruct((B, MLA_HEADS, T, 256), BF16),
        jax.ShapeDtypeStruct((B, MLA_HEADS, T, 256), BF16),
        jax.ShapeDtypeStruct((B, SWA_HEADS, T, 128), BF16),
        jax.ShapeDtypeStruct((B, SWA_KV_HEADS, T, 128), BF16),
        jax.ShapeDtypeStruct((B, SWA_KV_HEADS, T, 128), BF16),
        jax.ShapeDtypeStruct((B, T, CONV_CH), F32),
    ]
    out_specs = [
        pl.BlockSpec((1, MLA_HEADS, TM, 256), head),
        pl.BlockSpec((1, MLA_HEADS, TM, 256), head),
        pl.BlockSpec((1, MLA_HEADS, TM, 256), head),
        pl.BlockSpec((1, SWA_HEADS, TM, 128), head),
        pl.BlockSpec((1, SWA_KV_HEADS, TM, 128), head),
        pl.BlockSpec((1, SWA_KV_HEADS, TM, 128), head),
        pl.BlockSpec((1, TM, CONV_CH), row),
    ]
    return pl.pallas_call(
        functools.partial(_prep_kernel, t_off=t_off),
        grid=(B, nt),
        in_specs=in_specs,
        out_specs=out_specs,
        out_shape=out_shape,
        compiler_params=_cp(("arbitrary", "arbitrary")),
        name="prep",
    )(first, rest, modsel, lw["n1"], lw["w_in"], lw["kvg"], lw["w_ukv"], lw["qg"], lw["w_uq"], lw["gq"], lw["gkn"], lw["gkp"],
      lw["gsq"], lw["gsk"], cos_t, sin_t)


KEY_CHUNK = 256


def _mixer_kernel(sink_ref, q_ref, k_ref, v_ref, u_ref, cw_ref, cb_ref, cg_ref, cbb_ref, sq_ref, sk_ref, sv_ref,
                  xa_ref, xb_ref, mod_ref, w1_ref, w2_ref, w3_ref, n2_ref, wrh_ref, wrl_ref, br_ref, tri_ref, upper_ref,
                  xn_ref, hf_ref, pos_ref, post_ref, gt_ref, meta_ref, cnt_ref,
                  o_ref, oc_ref, os_ref, s_ref, p_ref, cbuf_ref, run_ref, *, q_off, n_keys):
    @pl.when((pl.program_id(0) == 0) & (pl.program_id(1) == 0))
    def _():
        run_ref[...] = jnp.zeros_like(run_ref)

    qi = pl.program_id(1) + q_off

    def attend(nk):
        for h in range(MLA_HEADS):
            q = q_ref[0, h]
            macc = jnp.full((TM, LANES), -jnp.inf, F32)
            for c in range(nk // KEY_CHUNK):
                k = k_ref[0, h, c * KEY_CHUNK:(c + 1) * KEY_CHUNK, :]
                s = lax.dot_general(q, k, (((1,), (1,)), ((), ())), preferred_element_type=F32)
                s_ref[:,c * KEY_CHUNK:(c + 1) * KEY_CHUNK] = s
                for j in range(KEY_CHUNK // LANES):
                    macc = jnp.maximum(macc, s[:, j * LANES:(j + 1) * LANES])
            m = jnp.max(macc, axis=-1, keepdims=True)
            for c in range(nk // KEY_CHUNK):
                p = jnp.exp2(s_ref[:,c * KEY_CHUNK:(c + 1) * KEY_CHUNK] - m)
                p_ref[:,c * KEY_CHUNK:(c + 1) * KEY_CHUNK] = p.astype(BF16)
            ol = jnp.dot(p_ref[:,0:nk], v_ref[0, h, 0:nk, :], preferred_element_type=F32)
            o = ol[:, 0:MLA_V] / ol[:, MLA_V:MLA_V + 1]
            o_ref[0, :, 128 * h:128 * h + 128] = o.astype(BF16)

    def tile(is_ctx):
        _conv_tile(qi, n_keys, u_ref, cw_ref, cb_ref, cg_ref, cbb_ref, oc_ref, cbuf_ref)
        attend(TM if is_ctx else n_keys)
        _swa_tile(is_ctx, qi, n_keys, sink_ref, sq_ref, sk_ref, sv_ref, os_ref)
        _out_tile(o_ref, oc_ref, os_ref, xa_ref, xb_ref, mod_ref, w1_ref, w2_ref, w3_ref, n2_ref, wrh_ref, wrl_ref, br_ref,
                  tri_ref, upper_ref, xn_ref, hf_ref, pos_ref, post_ref, gt_ref, meta_ref, cnt_ref, run_ref, q_off)

    if q_off == 0:
        pl.when(qi == 0)(functools.partial(tile, True))
        pl.when(qi > 0)(functools.partial(tile, False))
    else:
        tile(False)


def _mixers(qm, km, vm, u, qs, ks, vs, first, rest, shift, modsel, lw, tri, upper, t_off):
    B, H, T, _ = qm.shape
    C = u.shape[2]
    D = first.shape[2]
    HS, G = qs.shape[1], ks.shape[1]
    nt = T // TM - t_off
    Tq = nt * TM
    qtile = lambda b, i: (b, 0, i + t_off, 0)
    whole = lambda b, i: (b, 0, 0, 0)
    row = lambda b, i: (b, i, 0)
    per_tile = lambda b, i: (b * nt + i, 0, 0)
    in_specs = [
        pl.BlockSpec(memory_space=pltpu.SMEM),
        pl.BlockSpec((1, H, TM, 256), qtile),
        pl.BlockSpec((1, H, T, 256), whole),
        pl.BlockSpec((1, H, T, 256), whole),
        pl.BlockSpec((1, T, C), lambda b, i: (b, 0, 0)),
        _full((CONV_WIDTH, C)),
        _full((1, C)),
        _full((1, C)),
        _full((1, C)),
        pl.BlockSpec((1, HS, TM, 128), qtile),
        pl.BlockSpec((1, G, T, 128), whole),
        pl.BlockSpec((1, G, T, 128), whole),
    ] + _stream_specs(D, t_off, shift) + [
        pl.BlockSpec((1, 1, 6, D), lambda b, i: (b, jnp.minimum(i + t_off, 1), 0, 0)),
        _full((512, D)),
        _full((256, D)),
        _full((512, D)),
        _full((1, D)),
        _full((D, LANES)),
        _full((D, LANES)),
        _full((1, LANES)),
        _full((TM, TM)),
        _full((LANES, LANES)),
    ]
    out_shape = [
        jax.ShapeDtypeStruct((B, Tq, D), F32),
        jax.ShapeDtypeStruct((B, Tq, D), BF16),
        jax.ShapeDtypeStruct((B, Tq, LANES), I32),
        jax.ShapeDtypeStruct((B * nt, 8, TM), I32),
        jax.ShapeDtypeStruct((B, Tq, LANES), F32),
        jax.ShapeDtypeStruct((B * nt, 8, LANES), I32),
        jax.ShapeDtypeStruct((8, LANES), F32),
    ]
    out_specs = [
        pl.BlockSpec((1, TM, D), row),
        pl.BlockSpec((1, TM, D), row),
        pl.BlockSpec((1, TM, LANES), row),
        pl.BlockSpec((1, 8, TM), per_tile),
        pl.BlockSpec((1, TM, LANES), row),
        pl.BlockSpec((1, 8, LANES), per_tile),
        pl.BlockSpec((8, LANES), lambda b, i: (0, 0)),
    ]
    scratch_shapes = [
        pltpu.VMEM((1, TM, H * MLA_V), BF16),
        pltpu.VMEM((1, TM, C), BF16),
        pltpu.VMEM((1, TM, HS * 128), BF16),
        pltpu.VMEM((TM, T), F32),
        pltpu.VMEM((TM, T), BF16),
        pltpu.VMEM((TM + 2 * CONV_HALO, C), F32),
        pltpu.VMEM((8, LANES), F32),
    ]
    return pl.pallas_call(
        functools.partial(_mixer_kernel, q_off=t_off, n_keys=T),
        grid=(B, nt),
        in_specs=in_specs,
        out_specs=out_specs,
        out_shape=out_shape,
        scratch_shapes=scratch_shapes,
        compiler_params=pltpu.CompilerParams(dimension_semantics=("arbitrary", "arbitrary"),
                                             vmem_limit_bytes=MIXER_VMEM_LIMIT),
        name="mixers",
    )(lw["sink"], qm, km, vm, u, lw["conv_w"], lw["conv_b"], lw["conv_g"], lw["conv_bb"], qs, ks, vs, first, rest, modsel,
      lw["w_o1"], lw["w_o2"], lw["w_o3"], lw["n2"], lw["wr_hi"], lw["wr_lo"], lw["br"], tri, upper)


SWA_SPAN = TM + 2 * SWA_WINDOW


def _swa_tile(is_ctx, qi, n_rows, sink_ref, q_ref, k_ref, v_ref, o_ref):
    nt_dims = (((1,), (1,)), ((), ()))

    def latent():
        start = jnp.clip(qi * TM - SWA_WINDOW, 0, n_rows - SWA_SPAN)
        start = pl.multiple_of(start, SWA_WINDOW)
        qpos = qi * TM + lax.broadcasted_iota(I32, (TM, SWA_SPAN), 0)
        kpos = start + lax.broadcasted_iota(I32, (TM, SWA_SPAN), 1)
        valid = (jnp.abs(qpos - kpos) <= SWA_WINDOW) & (kpos >= TM)
        for h in range(SWA_HEADS):
            g = h // (SWA_HEADS // SWA_KV_HEADS)
            q = q_ref[0, h]
            kl = k_ref[0, g, pl.ds(start, SWA_SPAN), :]
            vl = v_ref[0, g, pl.ds(start, SWA_SPAN), :]
            kc = k_ref[0, g, 0:TM, :]
            vc = v_ref[0, g, 0:TM, :]
            sl = lax.dot_general(q, kl, nt_dims, preferred_element_type=F32)
            sl = jnp.where(valid, sl, NEG)
            scx = lax.dot_general(q, kc, nt_dims, preferred_element_type=F32)
            sink = sink_ref[h]
            m = jnp.maximum(jnp.maximum(jnp.max(sl, axis=-1, keepdims=True), jnp.max(scx, axis=-1, keepdims=True)), sink)
            pl_ = jnp.exp(sl - m)
            pc = jnp.exp(scx - m)
            l = jnp.sum(pl_, axis=-1, keepdims=True) + jnp.sum(pc, axis=-1, keepdims=True) + jnp.exp(sink - m)
            o = (jnp.dot(pl_.astype(BF16), vl, preferred_element_type=F32)
                 + jnp.dot(pc.astype(BF16), vc, preferred_element_type=F32)) / l
            o_ref[0, :, 128 * h:128 * h + 128] = o.astype(BF16)

    def context():
        for h in range(SWA_HEADS):
            g = h // (SWA_HEADS // SWA_KV_HEADS)
            q = q_ref[0, h]
            kc = k_ref[0, g, 0:TM, :]
            vc = v_ref[0, g, 0:TM, :]
            scx = lax.dot_general(q, kc, nt_dims, preferred_element_type=F32)
            sink = sink_ref[h]
            m = jnp.maximum(jnp.max(scx, axis=-1, keepdims=True), sink)
            pc = jnp.exp(scx - m)
            l = jnp.sum(pc, axis=-1, keepdims=True) + jnp.exp(sink - m)
            o = jnp.dot(pc.astype(BF16), vc, preferred_element_type=F32) / l
            o_ref[0, :, 128 * h:128 * h + 128] = o.astype(BF16)

    if is_ctx:
        context()
    else:
        latent()


def _conv_tile(i, n_rows, u_ref, w_ref, b_ref, g_ref, bb_ref, o_ref, buf_ref):
    nt_all = n_rows // TM
    start = pl.multiple_of(i * TM, TM)
    ps = pl.multiple_of(jnp.maximum(start - CONV_HALO, 0), 8)
    ns = pl.multiple_of(jnp.minimum(start + TM, n_rows - CONV_HALO), 8)
    keep_prev = jnp.where(i <= 1, 0.0, 1.0)
    keep_next = jnp.where((i == 0) | (i == nt_all - 1), 0.0, 1.0)
    buf_ref[0:CONV_HALO, :] = u_ref[0, pl.ds(ps, CONV_HALO), :] * keep_prev
    buf_ref[CONV_HALO:CONV_HALO + TM, :] = u_ref[0, pl.ds(start, TM), :]
    buf_ref[CONV_HALO + TM:2 * CONV_HALO + TM, :] = u_ref[0, pl.ds(ns, CONV_HALO), :] * keep_next
    off = CONV_HALO - CONV_WIDTH // 2
    accs = [None] * 4
    for j in range(CONV_WIDTH):
        term = buf_ref[off + j:off + j + TM, :] * w_ref[j:j + 1, :]
        accs[j % 4] = term if accs[j % 4] is None else accs[j % 4] + term
    y = (accs[0] + accs[1]) + (accs[2] + accs[3]) + b_ref[...]
    mu = jnp.mean(y, axis=-1, keepdims=True)
    d = y - mu
    var = jnp.mean(d * d, axis=-1, keepdims=True)
    z = d * lax.rsqrt(var + EPS) * g_ref[...] + bb_ref[...]
    o_ref[0] = (z * jax.nn.sigmoid(z)).astype(BF16)


def _out_tile(om_ref, oc_ref, os_ref, xa_ref, xb_ref, mod_ref, w1_ref, w2_ref, w3_ref, n2_ref, wrh_ref, wrl_ref, br_ref,
              tri_ref, upper_ref, xn_ref, hf_ref, pos_ref, post_ref, gt_ref, meta_ref, cnt_ref, run_ref, t_off):
    d = functools.partial(jnp.dot, preferred_element_type=F32)
    mix = d(om_ref[0], w1_ref[...]) + d(oc_ref[0], w2_ref[...]) + d(os_ref[0], w3_ref[...])
    mod = mod_ref[0, 0]
    g1, sh2, sc2 = mod[2:3], mod[3:4], mod[4:5]
    xn = _stream_tile(xa_ref, xb_ref, t_off) + g1 * mix
    xn_ref[0] = xn
    hf = xn * lax.rsqrt(jnp.mean(xn * xn, axis=-1, keepdims=True) + EPS) * n2_ref[...]
    hf = hf * (1.0 + sc2) + sh2
    hf_ref[0] = hf.astype(BF16)

    hi, lo = _split(hf)
    logits = d(hi, wrh_ref[...]) + d(hi, wrl_ref[...]) + d(lo, wrh_ref[...]) + br_ref[...]
    lane = lax.broadcasted_iota(I32, (TM, LANES), 1)
    lane_f = lane.astype(F32)
    l = logits
    ohs, vals = [], []
    for _ in range(TOP_K):
        m = jnp.max(l, axis=-1, keepdims=True)
        idx = jnp.min(jnp.where(l == m, lane_f, float(LANES)), axis=-1, keepdims=True)
        oh = lane_f == idx
        ohs.append(oh)
        vals.append(m)
        l = jnp.where(oh, -jnp.inf, l)
    ex = [jnp.exp(v - vals[0]) for v in vals]
    den = ex[0] + ex[1] + ex[2] + ex[3]
    gates = [e / den for e in ex]

    oa = jnp.zeros((TM, LANES), F32)
    for oh in ohs:
        oa = oa + oh.astype(F32)
    hist = jnp.sum(oa, axis=0, keepdims=True)
    slot_rows = jnp.floor((hist + (RUN_CHUNK - 1)) * (1.0 / RUN_CHUNK)) * RUN_CHUNK
    slot_off = d(jnp.broadcast_to(slot_rows, (8, LANES)).astype(BF16), upper_ref[...])
    where_ = d(tri_ref[...], oa.astype(BF16)) + slot_off[0:1, :]
    poss = [jnp.sum(jnp.where(oh, where_, 0.0), axis=-1, keepdims=True).astype(I32) for oh in ohs]

    p_out = jnp.zeros((TM, LANES), I32)
    g_out = jnp.zeros((TM, LANES), F32)
    for k in range(TOP_K):
        p_out = jnp.where(lane == k, poss[k], p_out)
        g_out = jnp.where(lane == k, gates[k], g_out)
    pos_ref[0] = p_out
    post_ref[0] = jnp.transpose(p_out)[0:8, :]
    gt_ref[0] = g_out

    srow = lax.broadcasted_iota(I32, (8, LANES), 0)
    meta = jnp.where(srow == 0, hist, jnp.where(srow == 1, run_ref[...], jnp.where(srow == 2, slot_off, 0.0)))
    meta_ref[0] = meta.astype(I32)
    run_ref[...] = run_ref[...] + jnp.floor((hist + (ROW_ALIGN - 1)) * (1.0 / ROW_ALIGN)) * ROW_ALIGN
    cnt_ref[...] = run_ref[...]


SUB = 8
TILE_SUBLANES = 8
ROW_ALIGN = TILE_SUBLANES // SUB
RUN_CHUNK = 16
SLOT_ROWS = TM * TOP_K + N_EXPERTS * RUN_CHUNK


def _rows(ref, row0, nrows):
    start = row0 * SUB if isinstance(row0, int) else pl.multiple_of(row0 * SUB, TILE_SUBLANES)
    return ref.at[pl.ds(start, nrows * SUB), :]


def _to_tiles(ref, val, nrows):
    for j in range(SUB):
        ref[pl.ds(j, nrows, stride=SUB), :] = val[:, LANES * j:LANES * (j + 1)]


def _from_tiles(ref, nrows):
    return jnp.concatenate([ref[pl.ds(j, nrows, stride=SUB), :] for j in range(SUB)], axis=1)


def _slot_matrix(pos, weights):
    col = lax.broadcasted_iota(I32, (TM, SLOT_ROWS), 1)
    m = jnp.zeros((TM, SLOT_ROWS), F32)
    for k in range(TOP_K):
        m = jnp.where(col == pos[:, k:k + 1], weights[k], m)
    return m.astype(BF16)


def _run_copies(meta_ref, pst_ref, buf_ref, hbm_ref, sem, to_hbm):
    total = jnp.int32(0)
    for e in range(N_EXPERTS):
        n = meta_ref[0, 0, e]
        nch = (n + (RUN_CHUNK - 1)) // RUN_CHUNK
        seg0 = pst_ref[e] + meta_ref[0, 1, e]
        slot0 = meta_ref[0, 2, e]

        def chunk(c, carry):
            a = _rows(buf_ref, slot0 + c * RUN_CHUNK, RUN_CHUNK)
            b = _rows(hbm_ref, seg0 + c * RUN_CHUNK, RUN_CHUNK)
            (pltpu.make_async_copy(a, b, sem) if to_hbm else pltpu.make_async_copy(b, a, sem)).start(priority=e % 2)
            return carry

        lax.fori_loop(0, nch, chunk, 0)
        total = total + nch
    return total


def _run_waits(total, buf_ref, hbm_ref, sem, to_hbm):
    a = _rows(buf_ref, 0, RUN_CHUNK)
    b = _rows(hbm_ref, 0, RUN_CHUNK)

    def one(c, carry):
        (pltpu.make_async_copy(a, b, sem) if to_hbm else pltpu.make_async_copy(b, a, sem)).wait()
        return carry

    lax.fori_loop(0, total, one, 0)


def _run_total(meta_ref):
    total = jnp.int32(0)
    for e in range(N_EXPERTS):
        total = total + (meta_ref[0, 0, e] + (RUN_CHUNK - 1)) // RUN_CHUNK
    return total


def _disp_kernel(pst_ref, cnt_ref, pad_ref, na_ref, meta_ref, hf_ref, post_ref, xb_ref, srt_ref, zero_ref, tot_ref,
                 sem, zsem, *, n_blocks):
    step = pl.program_id(0) * pl.num_programs(1) + pl.program_id(1)
    n_steps = pl.num_programs(0) * pl.num_programs(1)
    slot = step % 2
    first = step == 0

    @pl.when(first)
    def _():
        zero_ref[...] = jnp.zeros_like(zero_ref)

        def zblock(j):
            return pltpu.make_async_copy(zero_ref, _rows(xb_ref, j * EXPERT_BLOCK, EXPERT_BLOCK), zsem)

        def zb_start(j, carry):
            zblock(j).start()
            return carry

        def zb_wait(j, carry):
            zblock(j).wait()
            return carry

        lax.fori_loop(na_ref[0], n_blocks, zb_start, 0)
        lax.fori_loop(na_ref[0], n_blocks, zb_wait, 0)

        for e in range(N_EXPERTS):
            lo = pst_ref[e] + cnt_ref[e]
            nrow = pad_ref[e] - cnt_ref[e]
            nz = nrow // RUN_CHUNK
            lo1 = lo + nz * RUN_CHUNK
            n1 = (nrow - nz * RUN_CHUNK) // ROW_ALIGN

            def zchunk(c):
                return pltpu.make_async_copy(_rows(zero_ref, 0, RUN_CHUNK), _rows(xb_ref, lo + c * RUN_CHUNK, RUN_CHUNK), zsem)

            def zrow(r):
                return pltpu.make_async_copy(_rows(zero_ref, 0, ROW_ALIGN), _rows(xb_ref, lo1 + r * ROW_ALIGN, ROW_ALIGN), zsem)

            def zc_start(c, carry):
                zchunk(c).start()
                return carry

            def zc_wait(c, carry):
                zchunk(c).wait()
                return carry

            def zr_start(r, carry):
                zrow(r).start()
                return carry

            def zr_wait(r, carry):
                zrow(r).wait()
                return carry

            lax.fori_loop(0, nz, zc_start, 0)
            lax.fori_loop(0, n1, zr_start, 0)
            lax.fori_loop(0, nz, zc_wait, 0)
            lax.fori_loop(0, n1, zr_wait, 0)

    srow = lax.broadcasted_iota(I32, (SLOT_ROWS, TM), 0)
    post = post_ref[0]
    perm = jnp.zeros((SLOT_ROWS, TM), F32)
    for k in range(TOP_K):
        perm = jnp.where(srow == post[k:k + 1, :], 1.0, perm)
    srt = jnp.dot(perm.astype(BF16), hf_ref[0], preferred_element_type=F32)
    _to_tiles(srt_ref.at[slot], srt, SLOT_ROWS)

    @pl.when(step > 0)
    def _():
        _run_waits(tot_ref[1 - slot], srt_ref.at[1 - slot], xb_ref, sem.at[1 - slot], True)

    total = _run_copies(meta_ref, pst_ref, srt_ref.at[slot], xb_ref, sem.at[slot], True)
    tot_ref[slot] = total

    @pl.when(step == n_steps - 1)
    def _():
        _run_waits(total, srt_ref.at[slot], xb_ref, sem.at[slot], True)


def _dispatch(pstart, counts, padded, n_act, meta, hf, pos, n_buf):
    B, Tq, D = hf.shape
    nt = Tq // TM
    assert D == SUB * LANES
    grid_spec = pltpu.PrefetchScalarGridSpec(
        num_scalar_prefetch=4,
        grid=(B, nt),
        in_specs=[
            pl.BlockSpec((1, 8, LANES), lambda b, i, *_: (b * nt + i, 0, 0), memory_space=pltpu.SMEM),
            pl.BlockSpec((1, TM, D), lambda b, i, *_: (b, i, 0)),
            pl.BlockSpec((1, 8, TM), lambda b, i, *_: (b * nt + i, 0, 0)),
        ],
        out_specs=pl.BlockSpec(memory_space=pl.ANY),
        scratch_shapes=[pltpu.VMEM((2, SLOT_ROWS * SUB, LANES), F32), pltpu.VMEM((EXPERT_BLOCK * SUB, LANES), F32),
                        pltpu.SMEM((2,), I32), pltpu.SemaphoreType.DMA((2,)), pltpu.SemaphoreType.DMA],
    )
    return pl.pallas_call(
        functools.partial(_disp_kernel, n_blocks=n_buf // EXPERT_BLOCK),
        grid_spec=grid_spec,
        out_shape=jax.ShapeDtypeStruct((n_buf * SUB, LANES), F32),
        compiler_params=_cp(("arbitrary", "arbitrary")),
        name="dispatch",
    )(pstart, counts, padded, n_act, meta, hf, pos)


W_ROWS = 512
W_DELAY = 2


def _exp_kernel(be_ref, bx_ref, na_ref, nx_ref, ps_ref, pr_ref, npv_ref, x_ref, b1a_ref, b1b_ref, b2_ref, w1_hbm, w2_hbm,
                o_ref, st1, st2, w1a_s, w1b_s, w2_s, sem, *, e_off):
    j = pl.program_id(0)
    active = j < na_ref[0]
    e_cur = be_ref[j]
    p = pr_ref[e_cur]
    k = bx_ref[j] - ps_ref[e_cur]
    nxt = nx_ref[e_cur]
    n_prev = npv_ref[e_cur]
    n_slices = st1.shape[0] // W_ROWS

    def copies(e):
        return (pltpu.make_async_copy(w1_hbm.at[e], st1, sem.at[0]), pltpu.make_async_copy(w2_hbm.at[e], st2, sem.at[1]))

    def fetch(e):
        for cp in copies(e):
            cp.start(priority=1)

    def fetch_wait():
        for cp in copies(0):
            cp.wait()

    def convert(slot, s):
        r0 = pl.multiple_of(s * W_ROWS, W_ROWS)
        lane = lax.broadcasted_iota(I32, (W_ROWS, LANES), 1)
        lo = lane < 64
        idx = jnp.where(lo, 2 * lane, 2 * (lane - 64) + 1)
        for c in range(st1.shape[1] // (2 * LANES)):
            a = st1[pl.ds(r0, W_ROWS), 2 * LANES * c:2 * LANES * c + LANES]
            b = st1[pl.ds(r0, W_ROWS), 2 * LANES * c + LANES:2 * LANES * (c + 1)]
            pa = jnp.take_along_axis(a, idx, axis=1)
            pb = jnp.take_along_axis(b, idx, axis=1)
            ev = jnp.where(lo, pa, pltpu.roll(pb, 64, axis=1))
            od = jnp.where(lo, pltpu.roll(pa, 64, axis=1), pb)
            w1a_s[slot, pl.ds(r0, W_ROWS), LANES * c:LANES * (c + 1)] = ev.astype(BF16)
            w1b_s[slot, pl.ds(r0, W_ROWS), LANES * c:LANES * (c + 1)] = od.astype(BF16)
        w2_s[slot, pl.ds(r0, W_ROWS), :] = st2[pl.ds(r0, W_ROWS), :].astype(BF16)

    def convert_range(slot, s0):
        def one(s, carry):
            convert(slot, s)
            return carry

        lax.fori_loop(s0, n_slices, one, 0)

    @pl.when(j == 0)
    def _():
        fetch(be_ref[0] + e_off)
        fetch_wait()
        convert_range(0, 0)

    @pl.when(active & (k == 0) & (j > 0))
    def _():
        @pl.when(n_prev <= W_DELAY)
        def _():
            fetch_wait()

        convert_range(p, jnp.clip(n_prev - W_DELAY, 0, n_slices))

    @pl.when(active & (k == 0) & (nxt >= 0))
    def _():
        fetch(nxt + e_off)

    conv = active & (nxt >= 0) & (k >= W_DELAY) & (k < W_DELAY + n_slices)

    @pl.when(conv & (k == W_DELAY))
    def _():
        fetch_wait()

    def ffn(with_convert):
        if with_convert:
            convert(1 - p, k - W_DELAY)
        x = _from_tiles(x_ref, EXPERT_BLOCK).astype(BF16)
        ug = jnp.dot(x, w1a_s[p], preferred_element_type=F32) + b1a_ref[0]
        ul = jnp.dot(x, w1b_s[p], preferred_element_type=F32) + b1b_ref[0]
        xg = jnp.minimum(ug, SWIGLU_LIMIT)
        xl = jnp.clip(ul, -SWIGLU_LIMIT, SWIGLU_LIMIT)
        act = xg * jax.nn.sigmoid(SWIGLU_ALPHA * xg) * (xl + 1.0)
        y = jnp.dot(act.astype(BF16), w2_s[p], preferred_element_type=F32) + b2_ref[0]
        _to_tiles(o_ref, y, EXPERT_BLOCK)

    pl.when(conv)(functools.partial(ffn, True))
    pl.when(active & jnp.logical_not(conv))(functools.partial(ffn, False))

    @pl.when(jnp.logical_not(active))
    def _():
        o_ref[...] = jnp.zeros_like(o_ref)


def _experts(sched, xb, lw):
    nb = xb.shape[0] // (EXPERT_BLOCK * SUB)
    De, D = lw["w2"].shape[1:]
    assert De == D and D % W_ROWS == 0
    wmap = lambda j, be, *_: (be[j], 0, 0)
    grid_spec = pltpu.PrefetchScalarGridSpec(
        num_scalar_prefetch=len(sched),
        grid=(nb,),
        in_specs=[
            pl.BlockSpec((EXPERT_BLOCK * SUB, LANES), lambda j, be, bx, *_: (bx[j], 0)),
            pl.BlockSpec((1, 1, De), wmap),
            pl.BlockSpec((1, 1, De), wmap),
            pl.BlockSpec((1, 1, D), wmap),
            pl.BlockSpec(memory_space=pl.ANY),
            pl.BlockSpec(memory_space=pl.ANY),
        ],
        out_specs=pl.BlockSpec((EXPERT_BLOCK * SUB, LANES), lambda j, *_: (j, 0)),
        scratch_shapes=[pltpu.VMEM((D, 2 * De), F32), pltpu.VMEM((De, D), F32),
                        pltpu.VMEM((2, D, De), BF16), pltpu.VMEM((2, D, De), BF16), pltpu.VMEM((2, De, D), BF16),
                        pltpu.SemaphoreType.DMA((2,))],
    )
    return pl.pallas_call(
        functools.partial(_exp_kernel, e_off=lw["e_off"]),
        grid_spec=grid_spec,
        out_shape=jax.ShapeDtypeStruct(xb.shape, F32),
        compiler_params=pltpu.CompilerParams(dimension_semantics=("arbitrary",), vmem_limit_bytes=EXPERT_VMEM_LIMIT),
        name="experts",
    )(*sched, xb, lw["b1a"], lw["b1b"], lw["b2"], lw["w1"], lw["w2"])


def _comb_kernel(pst_ref, meta_ref, meta_next_ref, pos_ref, gt_ref, xn_ref, mod_ref, yb_ref, o_ref, buf_ref, sem):
    step = pl.program_id(0) * pl.num_programs(1) + pl.program_id(1)
    n_steps = pl.num_programs(0) * pl.num_programs(1)
    slot = step % 2

    @pl.when(step == 0)
    def _():
        buf_ref[...] = jnp.zeros_like(buf_ref)
        _run_copies(meta_ref, pst_ref, buf_ref.at[0], yb_ref, sem.at[0], False)

    @pl.when(step < n_steps - 1)
    def _():
        _run_copies(meta_next_ref, pst_ref, buf_ref.at[1 - slot], yb_ref, sem.at[1 - slot], False)

    gt = gt_ref[0]
    g = _slot_matrix(pos_ref[0], [gt[:, k:k + 1] for k in range(TOP_K)])
    _run_waits(_run_total(meta_ref), buf_ref.at[slot], yb_ref, sem.at[slot], False)
    rows = _from_tiles(buf_ref.at[slot], SLOT_ROWS).astype(BF16)
    y = jnp.dot(g, rows, preferred_element_type=F32)
    g2 = mod_ref[0, 0][5:6]
    o_ref[0] = xn_ref[0] + g2 * y


def _combine(pstart, meta, pos, gates, xn, modsel, yb, t_off):
    B, Tq, D = xn.shape
    nt = Tq // TM
    grid_spec = pltpu.PrefetchScalarGridSpec(
        num_scalar_prefetch=1,
        grid=(B, nt),
        in_specs=[
            pl.BlockSpec((1, 8, LANES), lambda b, i, *_: (b * nt + i, 0, 0), memory_space=pltpu.SMEM),
            pl.BlockSpec((1, 8, LANES), lambda b, i, *_: (jnp.minimum(b * nt + i + 1, B * nt - 1), 0, 0),
                         memory_space=pltpu.SMEM),
            pl.BlockSpec((1, TM, LANES), lambda b, i, *_: (b, i, 0)),
            pl.BlockSpec((1, TM, LANES), lambda b, i, *_: (b, i, 0)),
            pl.BlockSpec((1, TM, D), lambda b, i, *_: (b, i, 0)),
            pl.BlockSpec((1, 1, 6, D), lambda b, i, *_: (b, jnp.minimum(i + t_off, 1), 0, 0)),
            pl.BlockSpec(memory_space=pl.ANY),
        ],
        out_specs=pl.BlockSpec((1, TM, D), lambda b, i, *_: (b, i, 0)),
        scratch_shapes=[pltpu.VMEM((2, SLOT_ROWS * SUB, LANES), F32), pltpu.SemaphoreType.DMA((2,))],
    )
    return pl.pallas_call(
        _comb_kernel,
        grid_spec=grid_spec,
        out_shape=jax.ShapeDtypeStruct((B, Tq, D), F32),
        compiler_params=_cp(("arbitrary", "arbitrary")),
        name="combine",
    )(pstart, meta, meta, pos, gates, xn, modsel, yb)


def _cols(w, lo, n):
    return w[..., lo:lo + n]


def _slot64(w, lo):
    partner = [_cols(w, lo + 16, 16), _cols(w, lo, 16), _cols(w, lo + 48, 16), _cols(w, lo + 32, 16)]
    return [_cols(w, lo, 64)] + partner


def _layer_weights(l, a):
    lw = {}
    lw["n1"] = a["norm1_g"][l][None, :]
    lw["n2"] = a["norm2_g"][l][None, :]
    w = a["w_in"][l]
    zeros64 = jnp.zeros(w.shape[:-1] + (64,), w.dtype)
    parts = [_cols(w, 0, 256)] + _slot64(w, 256)
    for g in range(SWA_KV_HEADS):
        parts += _slot64(w, 320 + 64 * g)
    for g in range(SWA_KV_HEADS):
        parts += [_cols(w, 448 + 64 * g, 64), zeros64]
    parts += [_cols(w, 576, 384)]
    for h in range(SWA_HEADS):
        parts += _slot64(w, 960 + 64 * h)
    parts += [_cols(w, 1216, 512)]
    lw["w_in"] = jnp.concatenate(parts, axis=-1).astype(BF16)
    lw["kvg"] = a["mla_kv_norm"][l][None, :]
    lw["qg"] = a["mla_q_norm"][l][None, :]
    w = a["mla_w_uq"][l]
    parts = []
    for h in range(MLA_HEADS):
        parts += [_cols(w, MLA_QK * h, 128)] + _slot64(w, MLA_QK * h + 128)
    lw["w_uq"] = jnp.concatenate(parts, axis=-1).astype(BF16)
    w = a["mla_w_ukv"][l]
    parts = [_cols(w, 256 * h, 128) for h in range(MLA_HEADS)] + [_cols(w, 256 * h + 128, 128) for h in range(MLA_HEADS)]
    lw["w_ukv"] = jnp.concatenate(parts, axis=-1).astype(BF16)
    gq = a["mla_q_head_norm"][l]
    lw["gq"] = (jnp.concatenate([gq[:128]] + _slot64(gq, 128)) * (MLA_QK ** -0.5 * LOG2E))[None, :]
    gk = a["mla_k_head_norm"][l]
    lw["gkn"] = gk[:128][None, :]
    lw["gkp"] = jnp.concatenate(_slot64(gk, 128))[None, :]
    lw["gsq"] = (jnp.concatenate(_slot64(a["swa_q_norm"][l], 0)) * (SWA_HEAD_DIM ** -0.5))[None, :]
    lw["gsk"] = jnp.concatenate(_slot64(a["swa_k_norm"][l], 0))[None, :]
    lw["conv_w"] = a["conv_w"][l]
    lw["conv_b"] = a["conv_b"][l][None, :]
    lw["conv_g"] = a["conv_ln_g"][l][None, :]
    lw["conv_bb"] = a["conv_ln_b"][l][None, :]
    lw["sink"] = a["swa_sink"][l]
    wo = a["w_out"][l]
    lw["w_o1"] = wo[0:512].astype(BF16)
    lw["w_o2"] = wo[512:768].astype(BF16)
    o3 = wo[768:1024].reshape(SWA_HEADS, SWA_HEAD_DIM, -1)
    lw["w_o3"] = jnp.concatenate([o3, jnp.zeros_like(o3)], axis=1).reshape(SWA_HEADS * 128, -1).astype(BF16)
    wr = jnp.pad(a["router_w"][l], ((0, 0), (0, LANES - N_EXPERTS)))
    lw["wr_hi"] = wr.astype(BF16)
    lw["wr_lo"] = (wr - lw["wr_hi"].astype(F32)).astype(BF16)
    lw["br"] = jnp.pad(a["router_b"][l], (0, LANES - N_EXPERTS), constant_values=NEG)[None, :]
    lw["w1"] = a["exp_w1"].reshape((-1,) + a["exp_w1"].shape[2:])
    lw["e_off"] = l * a["exp_w1"].shape[1]
    b1 = a["exp_b1"][l]
    lw["b1a"] = b1[:, None, 0::2]
    lw["b1b"] = b1[:, None, 1::2]
    lw["w2"] = a["exp_w2"].reshape((-1,) + a["exp_w2"].shape[2:])
    lw["b2"] = a["exp_b2"][l][:, None, :]
    return lw


def _rope_tables(n_ctx, n_lat):
    q = MLA_ROPE // 4
    n = jnp.arange(n_lat, dtype=I32)
    row = (n // GRID_W).astype(F32)
    col = (n % GRID_W).astype(F32)
    inv = ROPE_BASE ** (-jnp.arange(q, dtype=F32) / q)
    ang_r = row[:, None] * inv
    ang_c = col[:, None] * inv
    cos = jnp.concatenate([jnp.cos(ang_r), jnp.cos(ang_r), jnp.cos(ang_c), jnp.cos(ang_c)], axis=1)
    sin = jnp.concatenate([-jnp.sin(ang_r), jnp.sin(ang_r), -jnp.sin(ang_c), jnp.sin(ang_c)], axis=1)
    cos = jnp.concatenate([jnp.ones((n_ctx, 64), F32), cos], axis=0)
    sin = jnp.concatenate([jnp.zeros((n_ctx, 64), F32), sin], axis=0)
    z = jnp.zeros_like(cos)
    return jnp.concatenate([cos, z], axis=1), jnp.concatenate([sin, z], axis=1)


def _routing_tables(cnt_f, n_blocks):
    counts = cnt_f[0, :N_EXPERTS].astype(I32)
    padded = (counts + (RUN_CHUNK - 1) + EXPERT_BLOCK - 1) // EXPERT_BLOCK * EXPERT_BLOCK
    padded = jnp.where(counts > 0, padded, 0)
    pend = jnp.cumsum(padded)
    pstart = pend - padded
    n_act = pend[-1] // EXPERT_BLOCK
    blk = jnp.minimum(jnp.arange(n_blocks, dtype=I32), n_act - 1)
    blk_e = jnp.sum((pend[None, :] <= (blk * EXPERT_BLOCK)[:, None]).astype(I32), axis=1)
    blk_e = jnp.minimum(blk_e, N_EXPERTS - 1)
    nbk = padded // EXPERT_BLOCK
    has = nbk > 0
    ids = jnp.arange(N_EXPERTS, dtype=I32)
    later = has[None, :] & (ids[None, :] > ids[:, None])
    nxt_e = jnp.min(jnp.where(later, ids[None, :], N_EXPERTS), axis=1)
    nxt_e = jnp.where(nxt_e == N_EXPERTS, -1, nxt_e)
    earlier = has[None, :] & (ids[None, :] < ids[:, None])
    prv_e = jnp.max(jnp.where(earlier, ids[None, :], -1), axis=1)
    nb_prev_e = jnp.sum(jnp.where(ids[None, :] == prv_e[:, None], nbk[None, :], 0), axis=1)
    set_e = (jnp.cumsum(has.astype(I32)) - 1) % 2
    sched = (blk_e, blk, n_act.reshape(1), nxt_e, pstart // EXPERT_BLOCK, set_e, nb_prev_e)
    sched = tuple(s.astype(I32) for s in sched)
    return counts, padded.astype(I32), pstart.astype(I32), sched


def kernel(x, c, ctx, c_ctx, norm1_g, norm2_g, w_ada, b_ada, w_in, mla_q_norm, mla_kv_norm, mla_w_uq, mla_w_ukv, mla_q_head_norm, mla_k_head_norm, conv_w, conv_b, conv_ln_g, conv_ln_b, swa_q_norm, swa_k_norm, swa_sink, w_out, router_w, router_b, exp_w1, exp_b1, exp_w2, exp_b2):
    a = dict(norm1_g=norm1_g, norm2_g=norm2_g, w_in=w_in, mla_q_norm=mla_q_norm, mla_kv_norm=mla_kv_norm,
             mla_w_uq=mla_w_uq, mla_w_ukv=mla_w_ukv, mla_q_head_norm=mla_q_head_norm, mla_k_head_norm=mla_k_head_norm,
             conv_w=conv_w, conv_b=conv_b, conv_ln_g=conv_ln_g, conv_ln_b=conv_ln_b, swa_q_norm=swa_q_norm,
             swa_k_norm=swa_k_norm, swa_sink=swa_sink, w_out=w_out, router_w=router_w, router_b=router_b,
             exp_w1=exp_w1, exp_b1=exp_b1, exp_w2=exp_w2, exp_b2=exp_b2)
    B, S, D = x.shape
    n_ctx = ctx.shape[1]
    depth = w_ada.shape[0]
    assert n_ctx == TM and S % TM == 0 and B + 1 <= 16
    T = n_ctx + S

    s_in = jnp.zeros((16, D), F32).at[:B].set(c).at[B].set(c_ctx)
    mods = _ada(s_in, w_ada, b_ada)
    cos_t, sin_t = _rope_tables(n_ctx, S)
    tri = jnp.tril(jnp.ones((TM, TM), F32), -1).astype(BF16)
    upper = jnp.triu(jnp.ones((LANES, LANES), F32), 1).astype(BF16)

    first, rest, shift = ctx, x, 1
    for l in range(depth):
        last = l == depth - 1
        t_off = 1 if last else 0
        lw = _layer_weights(l, a)
        m = mods[l].reshape(16, 6, D)
        modsel = jnp.stack([jnp.broadcast_to(m[B], (B, 6, D)), m[:B]], axis=1)

        qm, km, vm, qs, ks, vs, u = _prep(first, rest, shift, modsel, lw, cos_t, sin_t, 0)
        xn, hf, pos, pos_t, gt_o, meta, cnt = _mixers(qm, km, vm, u, qs, ks, vs, first, rest, shift, modsel, lw, tri,
                                                      upper, t_off)

        n_tok = B * (T - t_off * TM)
        nk = n_tok * TOP_K
        n_align = (n_tok // TM) * N_EXPERTS * (ROW_ALIGN - 1)
        n_buf = -(-(nk + n_align + N_EXPERTS * (RUN_CHUNK - 1 + EXPERT_BLOCK - 1)) // EXPERT_BLOCK) * EXPERT_BLOCK
        counts, padded, pstart, sched = _routing_tables(cnt, n_buf // EXPERT_BLOCK)
        xb = _dispatch(pstart, counts, padded, sched[2], meta, hf, pos_t, n_buf)
        yb = _experts(sched, xb, lw)
        xu = _combine(pstart, meta, pos, gt_o, xn, modsel, yb, t_off)
        first, rest, shift = xu, xu, 0
    return xu
```

```python
import functools

import jax
import jax.numpy as jnp
from jax import lax
from jax.experimental import pallas as pl
from jax.experimental.pallas import tpu as pltpu

F32 = jnp.float32
BF16 = jnp.bfloat16
I32 = jnp.int32

GRID_W = 64
ROPE_BASE = 10000.0
EPS = 1e-6
MLA_HEADS = 4
MLA_NOPE = 128
MLA_ROPE = 64
MLA_V = 128
MLA_QK = MLA_NOPE + MLA_ROPE
MLA_Q_RANK = 384
MLA_KV_RANK = 256
CONV_CH = 256
CONV_WIDTH = 31
SWA_HEADS = 4
SWA_KV_HEADS = 2
SWA_HEAD_DIM = 64
SWA_WINDOW = 128
N_EXPERTS = 32
TOP_K = 4
SWIGLU_LIMIT = 7.0
SWIGLU_ALPHA = 1.702
EXPERT_BLOCK = 512

LANES = 128
TM = 256
CONV_HALO = 16
VMEM_LIMIT = 48 * 1024 * 1024
MIXER_VMEM_LIMIT = 56 * 1024 * 1024
EXPERT_VMEM_LIMIT = 56 * 1024 * 1024
NEG = -1e30
LOG2E = 1.4426950408889634


def _cp(sem):
    return pltpu.CompilerParams(dimension_semantics=sem, vmem_limit_bytes=VMEM_LIMIT)


def _full(shape):
    n = len(shape)
    return pl.BlockSpec(shape, lambda *a, _n=n: (0,) * _n)


def _split(x):
    hi = x.astype(BF16)
    lo = (x - hi.astype(F32)).astype(BF16)
    return hi, lo


def _dot3(a, b):
    ah, al = _split(a)
    bh, bl = _split(b)
    d = functools.partial(jnp.dot, preferred_element_type=F32)
    return d(ah, bh) + d(ah, bl) + d(al, bh)


def _ada_kernel(s_ref, w_ref, b_ref, o_ref):
    s = s_ref[...]
    s = s * jax.nn.sigmoid(s)
    o_ref[0] = _dot3(s, w_ref[0]) + b_ref[0]


def _ada(s_in, w_ada, b_ada):
    L, D, N = w_ada.shape
    tn = 1536
    return pl.pallas_call(
        _ada_kernel,
        grid=(L, N // tn),
        in_specs=[
            pl.BlockSpec((16, D), lambda l, j: (0, 0)),
            pl.BlockSpec((1, D, tn), lambda l, j: (l, 0, j)),
            pl.BlockSpec((1, 1, tn), lambda l, j: (l, 0, j)),
        ],
        out_specs=pl.BlockSpec((1, 16, tn), lambda l, j: (l, 0, j)),
        out_shape=jax.ShapeDtypeStruct((L, 16, N), F32),
        compiler_params=_cp(("arbitrary", "arbitrary")),
        name="ada",
    )(s_in, w_ada, b_ada.reshape(L, 1, N))


def _rope(x, c, s):
    return x * c + pltpu.roll(x, 64, axis=1) * s


def _stream_specs(D, t_off, shift):
    return [pl.BlockSpec((1, TM, D), lambda b, i, *_: (b, 0, 0)),
            pl.BlockSpec((1, TM, D), lambda b, i, *_: (b, jnp.maximum(i + t_off - shift, 0), 0))]


def _stream_tile(first_ref, rest_ref, t_off):
    return jnp.where(pl.program_id(1) + t_off == 0, first_ref[0], rest_ref[0])


def _prep_kernel(xa_ref, xb_ref, mod_ref, n1_ref, win_ref, kvg_ref, wukv_ref, qg_ref, wuq_ref, gq_ref, gkn_ref, gkp_ref,
                 gsq_ref, gsk_ref, cos_ref, sin_ref, qm_ref, km_ref, vm_ref, qs_ref, ks_ref, vs_ref, u_ref, *, t_off):
    x = _stream_tile(xa_ref, xb_ref, t_off)
    mod = mod_ref[0, 0]
    sh, sc = mod[0:1], mod[1:2]
    y = x * lax.rsqrt(jnp.mean(x * x, axis=-1, keepdims=True) + EPS) * n1_ref[...]
    h = y * (1.0 + sc) + sh
    p = jnp.dot(h.astype(BF16), win_ref[...], preferred_element_type=F32)
    c = cos_ref[...]
    s = sin_ref[...]
    lane = lax.broadcasted_iota(I32, (TM, LANES), 1)

    def ss_lo(v):
        return 0.5 * jnp.sum(v * v, axis=-1, keepdims=True)

    ckv = p[:, 0:256]
    ckvn = ckv * lax.rsqrt(jnp.mean(ckv * ckv, axis=-1, keepdims=True) + EPS) * kvg_ref[...]
    kv = jnp.dot(ckvn.astype(BF16), wukv_ref[...], preferred_element_type=F32)
    kpe = p[:, 256:384]
    ss_pe = ss_lo(kpe)
    kpe_rot = _rope(kpe * gkp_ref[...], c, s)
    for hh in range(MLA_HEADS):
        kn = kv[:, 128 * hh:128 * hh + 128]
        r = lax.rsqrt((jnp.sum(kn * kn, axis=-1, keepdims=True) + ss_pe) * (1.0 / MLA_QK) + EPS)
        km_ref[0, hh, :, 0:128] = (kn * r * gkn_ref[...]).astype(BF16)
        km_ref[0, hh, :, 128:256] = (kpe_rot * r).astype(BF16)
        vm_ref[0, hh, :, 0:128] = kv[:, 512 + 128 * hh:640 + 128 * hh].astype(BF16)
        vm_ref[0, hh, :, 128:256] = (lane == 0).astype(BF16)

    cq = p[:, 896:1280]
    cqn = cq * lax.rsqrt(jnp.mean(cq * cq, axis=-1, keepdims=True) + EPS) * qg_ref[...]
    q = jnp.dot(cqn.astype(BF16), wuq_ref[...], preferred_element_type=F32)
    gq = gq_ref[...]
    for hh in range(MLA_HEADS):
        qn = q[:, 256 * hh:256 * hh + 128]
        qp = q[:, 256 * hh + 128:256 * hh + 256]
        r = lax.rsqrt((jnp.sum(qn * qn, axis=-1, keepdims=True) + ss_lo(qp)) * (1.0 / MLA_QK) + EPS)
        qm_ref[0, hh, :, 0:128] = (qn * r * gq[:, 0:128]).astype(BF16)
        qm_ref[0, hh, :, 128:256] = _rope(qp * r * gq[:, 128:256], c, s).astype(BF16)

    for g in range(SWA_KV_HEADS):
        xk = p[:, 384 + 128 * g:512 + 128 * g]
        r = lax.rsqrt(ss_lo(xk) * (1.0 / SWA_HEAD_DIM) + EPS)
        ks_ref[0, g] = _rope(xk * r * gsk_ref[...], c, s).astype(BF16)
        vs_ref[0, g] = p[:, 640 + 128 * g:768 + 128 * g].astype(BF16)
    for hh in range(SWA_HEADS):
        xq = p[:, 1280 + 128 * hh:1408 + 128 * hh]
        r = lax.rsqrt(ss_lo(xq) * (1.0 / SWA_HEAD_DIM) + EPS)
        qs_ref[0, hh] = _rope(xq * r * gsq_ref[...], c, s).astype(BF16)

    u_ref[0] = p[:, 1792:2048] * jax.nn.sigmoid(p[:, 2048:2304])


def _prep(first, rest, shift, modsel, lw, cos_t, sin_t, t_off):
    B, _, D = first.shape
    T = rest.shape[1] + shift * TM
    nt = T // TM - t_off
    ncol = lw["w_in"].shape[1]
    row = lambda b, i: (b, i + t_off, 0)
    head = lambda b, i: (b, 0, i + t_off, 0)
    in_specs = _stream_specs(D, t_off, shift) + [
        pl.BlockSpec((1, 1, 6, D), lambda b, i: (b, jnp.minimum(i + t_off, 1), 0, 0)),
        _full((1, D)),
        _full((D, ncol)),
        _full((1, MLA_KV_RANK)),
        _full((MLA_KV_RANK, 1024)),
        _full((1, MLA_Q_RANK)),
        _full((MLA_Q_RANK, 1024)),
        _full((1, 256)),
        _full((1, 128)),
        _full((1, 128)),
        _full((1, 128)),
        _full((1, 128)),
        pl.BlockSpec((TM, LANES), lambda b, i: (i + t_off, 0)),
        pl.BlockSpec((TM, LANES), lambda b, i: (i + t_off, 0)),
    ]
    out_shape = [
        jax.ShapeDtypeStruct((B, MLA_HEADS, T, 256), BF16),
        jax.ShapeDtypeStruct((B, MLA_HEADS, T, 256), BF16),
        jax.ShapeDtypeStruct((B, MLA_HEADS, T, 256), BF16),
        jax.ShapeDtypeStruct((B, SWA_HEADS, T, 128), BF16),
        jax.ShapeDtypeStruct((B, SWA_KV_HEADS, T, 128), BF16),
        jax.ShapeDtypeStruct((B, SWA_KV_HEADS, T, 128), BF16),
        jax.ShapeDtypeStruct((B, T, CONV_CH), F32),
    ]
    out_specs = [
        pl.BlockSpec((1, MLA_HEADS, TM, 256), head),
        pl.BlockSpec((1, MLA_HEADS, TM, 256), head),
        pl.BlockSpec((1, MLA_HEADS, TM, 256), head),
        pl.BlockSpec((1, SWA_HEADS, TM, 128), head),
        pl.BlockSpec((1, SWA_KV_HEADS, TM, 128), head),
        pl.BlockSpec((1, SWA_KV_HEADS, TM, 128), head),
        pl.BlockSpec((1, TM, CONV_CH), row),
    ]
    return pl.pallas_call(
        functools.partial(_prep_kernel, t_off=t_off),
        grid=(B, nt),
        in_specs=in_specs,
        out_specs=out_specs,
        out_shape=out_shape,
        compiler_params=_cp(("arbitrary", "arbitrary")),
        name="prep",
    )(first, rest, modsel, lw["n1"], lw["w_in"], lw["kvg"], lw["w_ukv"], lw["qg"], lw["w_uq"], lw["gq"], lw["gkn"], lw["gkp"],
      lw["gsq"], lw["gsk"], cos_t, sin_t)


KEY_CHUNK = 256


def _mixer_kernel(sink_ref, q_ref, k_ref, v_ref, u_ref, cw_ref, cb_ref, cg_ref, cbb_ref, sq_ref, sk_ref, sv_ref,
                  xa_ref, xb_ref, mod_ref, w1_ref, w2_ref, w3_ref, n2_ref, wrh_ref, wrl_ref, br_ref, tri_ref, upper_ref,
                  xn_ref, hf_ref, pos_ref, post_ref, gt_ref, meta_ref, cnt_ref,
                  o_ref, oc_ref, os_ref, s_ref, p_ref, cbuf_ref, run_ref, *, q_off, n_keys):
    @pl.when((pl.program_id(0) == 0) & (pl.program_id(1) == 0))
    def _():
        run_ref[...] = jnp.zeros_like(run_ref)

    qi = pl.program_id(1) + q_off

    def attend(nk):
        for h in range(MLA_HEADS):
            q = q_ref[0, h]
            macc = jnp.full((TM, LANES), -jnp.inf, F32)
            for c in range(nk // KEY_CHUNK):
                k = k_ref[0, h, c * KEY_CHUNK:(c + 1) * KEY_CHUNK, :]
                s = lax.dot_general(q, k, (((1,), (1,)), ((), ())), preferred_element_type=F32)
                s_ref[:, c * KEY_CHUNK:(c + 1) * KEY_CHUNK] = s
                for j in range(KEY_CHUNK // LANES):
                    macc = jnp.maximum(macc, s[:, j * LANES:(j + 1) * LANES])
            m = jnp.max(macc, axis=-1, keepdims=True)
            for c in range(nk // KEY_CHUNK):
                p = jnp.exp2(s_ref[:, c * KEY_CHUNK:(c + 1) * KEY_CHUNK] - m)
                p_ref[:, c * KEY_CHUNK:(c + 1) * KEY_CHUNK] = p.astype(BF16)
            ol = jnp.dot(p_ref[:, 0:nk], v_ref[0, h, 0:nk, :], preferred_element_type=F32)
            o = ol[:, 0:MLA_V] / ol[:, MLA_V:MLA_V + 1]
            o_ref[0, :, 128 * h:128 * h + 128] = o.astype(BF16)

    def tile(is_ctx):
        _conv_tile(qi, n_keys, u_ref, cw_ref, cb_ref, cg_ref, cbb_ref, oc_ref, cbuf_ref)
        attend(TM if is_ctx else n_keys)
        _swa_tile(is_ctx, qi, n_keys, sink_ref, sq_ref, sk_ref, sv_ref, os_ref)
        _out_tile(o_ref, oc_ref, os_ref, xa_ref, xb_ref, mod_ref, w1_ref, w2_ref, w3_ref, n2_ref, wrh_ref, wrl_ref, br_ref,
                  tri_ref, upper_ref, xn_ref, hf_ref, pos_ref, post_ref, gt_ref, meta_ref, cnt_ref, run_ref, q_off)

    if q_off == 0:
        pl.when(qi == 0)(functools.partial(tile, True))
        pl.when(qi > 0)(functools.partial(tile, False))
    else:
        tile(False)


def _mixers(qm, km, vm, u, qs, ks, vs, first, rest, shift, modsel, lw, tri, upper, t_off):
    B, H, T, _ = qm.shape
    C = u.shape[2]
    D = first.shape[2]
    HS, G = qs.shape[1], ks.shape[1]
    nt = T // TM - t_off
    Tq = nt * TM
    qtile = lambda b, i: (b, 0, i + t_off, 0)
    whole = lambda b, i: (b, 0, 0, 0)
    row = lambda b, i: (b, i, 0)
    per_tile = lambda b, i: (b * nt + i, 0, 0)
    in_specs = [
        pl.BlockSpec(memory_space=pltpu.SMEM),
        pl.BlockSpec((1, H, TM, 256), qtile),
        pl.BlockSpec((1, H, T, 256), whole),
        pl.BlockSpec((1, H, T, 256), whole),
        pl.BlockSpec((1, T, C), lambda b, i: (b, 0, 0)),
        _full((CONV_WIDTH, C)),
        _full((1, C)),
        _full((1, C)),
        _full((1, C)),
        pl.BlockSpec((1, HS, TM, 128), qtile),
        pl.BlockSpec((1, G, T, 128), whole),
        pl.BlockSpec((1, G, T, 128), whole),
    ] + _stream_specs(D, t_off, shift) + [
        pl.BlockSpec((1, 1, 6, D), lambda b, i: (b, jnp.minimum(i + t_off, 1), 0, 0)),
        _full((512, D)),
        _full((256, D)),
        _full((512, D)),
        _full((1, D)),
        _full((D, LANES)),
        _full((D, LANES)),
        _full((1, LANES)),
        _full((TM, TM)),
        _full((LANES, LANES)),
    ]
    out_shape = [
        jax.ShapeDtypeStruct((B, Tq, D), F32),
        jax.ShapeDtypeStruct((B, Tq, D), BF16),
        jax.ShapeDtypeStruct((B, Tq, LANES), I32),
        jax.ShapeDtypeStruct((B * nt, 8, TM), I32),
        jax.ShapeDtypeStruct((B, Tq, LANES), F32),
        jax.ShapeDtypeStruct((B * nt, 8, LANES), I32),
        jax.ShapeDtypeStruct((8, LANES), F32),
    ]
    out_specs = [
        pl.BlockSpec((1, TM, D), row),
        pl.BlockSpec((1, TM, D), row),
        pl.BlockSpec((1, TM, LANES), row),
        pl.BlockSpec((1, 8, TM), per_tile),
        pl.BlockSpec((1, TM, LANES), row),
        pl.BlockSpec((1, 8, LANES), per_tile),
        pl.BlockSpec((8, LANES), lambda b, i: (0, 0)),
    ]
    scratch_shapes = [
        pltpu.VMEM((1, TM, H * MLA_V), BF16),
        pltpu.VMEM((1, TM, C), BF16),
        pltpu.VMEM((1, TM, HS * 128), BF16),
        pltpu.VMEM((TM, T), F32),
        pltpu.VMEM((TM, T), BF16),
        pltpu.VMEM((TM + 2 * CONV_HALO, C), F32),
        pltpu.VMEM((8, LANES), F32),
    ]
    return pl.pallas_call(
        functools.partial(_mixer_kernel, q_off=t_off, n_keys=T),
        grid=(B, nt),
        in_specs=in_specs,
        out_specs=out_specs,
        out_shape=out_shape,
        scratch_shapes=scratch_shapes,
        compiler_params=pltpu.CompilerParams(dimension_semantics=("arbitrary", "arbitrary"),
                                             vmem_limit_bytes=MIXER_VMEM_LIMIT),
        name="mixers",
    )(lw["sink"], qm, km, vm, u, lw["conv_w"], lw["conv_b"], lw["conv_g"], lw["conv_bb"], qs, ks, vs, first, rest, modsel,
      lw["w_o1"], lw["w_o2"], lw["w_o3"], lw["n2"], lw["wr_hi"], lw["wr_lo"], lw["br"], tri, upper)


SWA_SPAN = TM + 2 * SWA_WINDOW


def _swa_tile(is_ctx, qi, n_rows, sink_ref, q_ref, k_ref, v_ref, o_ref):
    nt_dims = (((1,), (1,)), ((), ()))

    def latent():
        start = jnp.clip(qi * TM - SWA_WINDOW, 0, n_rows - SWA_SPAN)
        start = pl.multiple_of(start, SWA_WINDOW)
        qpos = qi * TM + lax.broadcasted_iota(I32, (TM, SWA_SPAN), 0)
        kpos = start + lax.broadcasted_iota(I32, (TM, SWA_SPAN), 1)
        valid = (jnp.abs(qpos - kpos) <= SWA_WINDOW) & (kpos >= TM)
        for h in range(SWA_HEADS):
            g = h // (SWA_HEADS // SWA_KV_HEADS)
            q = q_ref[0, h]
            kl = k_ref[0, g, pl.ds(start, SWA_SPAN), :]
            vl = v_ref[0, g, pl.ds(start, SWA_SPAN), :]
            kc = k_ref[0, g, 0:TM, :]
            vc = v_ref[0, g, 0:TM, :]
            sl = lax.dot_general(q, kl, nt_dims, preferred_element_type=F32)
            sl = jnp.where(valid, sl, NEG)
            scx = lax.dot_general(q, kc, nt_dims, preferred_element_type=F32)
            sink = sink_ref[h]
            m = jnp.maximum(jnp.maximum(jnp.max(sl, axis=-1, keepdims=True), jnp.max(scx, axis=-1, keepdims=True)), sink)
            pl_ = jnp.exp(sl - m)
            pc = jnp.exp(scx - m)
            l = jnp.sum(pl_, axis=-1, keepdims=True) + jnp.sum(pc, axis=-1, keepdims=True) + jnp.exp(sink - m)
            o = (jnp.dot(pl_.astype(BF16), vl, preferred_element_type=F32)
                 + jnp.dot(pc.astype(BF16), vc, preferred_element_type=F32)) / l
            o_ref[0, :, 128 * h:128 * h + 128] = o.astype(BF16)

    def context():
        for h in range(SWA_HEADS):
            g = h // (SWA_HEADS // SWA_KV_HEADS)
            q = q_ref[0, h]
            kc = k_ref[0, g, 0:TM, :]
            vc = v_ref[0, g, 0:TM, :]
            scx = lax.dot_general(q, kc, nt_dims, preferred_element_type=F32)
            sink = sink_ref[h]
            m = jnp.maximum(jnp.max(scx, axis=-1, keepdims=True), sink)
            pc = jnp.exp(scx - m)
            l = jnp.sum(pc, axis=-1, keepdims=True) + jnp.exp(sink - m)
            o = jnp.dot(pc.astype(BF16), vc, preferred_element_type=F32) / l
            o_ref[0, :, 128 * h:128 * h + 128] = o.astype(BF16)

    if is_ctx:
        context()
    else:
        latent()


def _conv_tile(i, n_rows, u_ref, w_ref, b_ref, g_ref, bb_ref, o_ref, buf_ref):
    nt_all = n_rows // TM
    start = pl.multiple_of(i * TM, TM)
    ps = pl.multiple_of(jnp.maximum(start - CONV_HALO, 0), 8)
    ns = pl.multiple_of(jnp.minimum(start + TM, n_rows - CONV_HALO), 8)
    keep_prev = jnp.where(i <= 1, 0.0, 1.0)
    keep_next = jnp.where((i == 0) | (i == nt_all - 1), 0.0, 1.0)
    buf_ref[0:CONV_HALO, :] = u_ref[0, pl.ds(ps, CONV_HALO), :] * keep_prev
    buf_ref[CONV_HALO:CONV_HALO + TM, :] = u_ref[0, pl.ds(start, TM), :]
    buf_ref[CONV_HALO + TM:2 * CONV_HALO + TM, :] = u_ref[0, pl.ds(ns, CONV_HALO), :] * keep_next
    off = CONV_HALO - CONV_WIDTH // 2
    accs = [None] * 4
    for j in range(CONV_WIDTH):
        term = buf_ref[off + j:off + j + TM, :] * w_ref[j:j + 1, :]
        accs[j % 4] = term if accs[j % 4] is None else accs[j % 4] + term
    y = (accs[0] + accs[1]) + (accs[2] + accs[3]) + b_ref[...]
    mu = jnp.mean(y, axis=-1, keepdims=True)
    d = y - mu
    var = jnp.mean(d * d, axis=-1, keepdims=True)
    z = d * lax.rsqrt(var + EPS) * g_ref[...] + bb_ref[...]
    o_ref[0] = (z * jax.nn.sigmoid(z)).astype(BF16)


def _out_tile(om_ref, oc_ref, os_ref, xa_ref, xb_ref, mod_ref, w1_ref, w2_ref, w3_ref, n2_ref, wrh_ref, wrl_ref, br_ref,
              tri_ref, upper_ref, xn_ref, hf_ref, pos_ref, post_ref, gt_ref, meta_ref, cnt_ref, run_ref, t_off):
    d = functools.partial(jnp.dot, preferred_element_type=F32)
    mix = d(om_ref[0], w1_ref[...]) + d(oc_ref[0], w2_ref[...]) + d(os_ref[0], w3_ref[...])
    mod = mod_ref[0, 0]
    g1, sh2, sc2 = mod[2:3], mod[3:4], mod[4:5]
    xn = _stream_tile(xa_ref, xb_ref, t_off) + g1 * mix
    xn_ref[0] = xn
    hf = xn * lax.rsqrt(jnp.mean(xn * xn, axis=-1, keepdims=True) + EPS) * n2_ref[...]
    hf = hf * (1.0 + sc2) + sh2
    hf_ref[0] = hf.astype(BF16)

    hi, lo = _split(hf)
    logits = d(hi, wrh_ref[...]) + d(hi, wrl_ref[...]) + d(lo, wrh_ref[...]) + br_ref[...]
    lane = lax.broadcasted_iota(I32, (TM, LANES), 1)
    lane_f = lane.astype(F32)
    l = logits
    ohs, vals = [], []
    for _ in range(TOP_K):
        m = jnp.max(l, axis=-1, keepdims=True)
        idx = jnp.min(jnp.where(l == m, lane_f, float(LANES)), axis=-1, keepdims=True)
        oh = lane_f == idx
        ohs.append(oh)
        vals.append(m)
        l = jnp.where(oh, -jnp.inf, l)
    ex = [jnp.exp(v - vals[0]) for v in vals]
    den = ex[0] + ex[1] + ex[2] + ex[3]
    gates = [e / den for e in ex]

    oa = jnp.zeros((TM, LANES), F32)
    for oh in ohs:
        oa = oa + oh.astype(F32)
    hist = jnp.sum(oa, axis=0, keepdims=True)
    slot_rows = jnp.floor((hist + (RUN_CHUNK - 1)) * (1.0 / RUN_CHUNK)) * RUN_CHUNK
    slot_off = d(jnp.broadcast_to(slot_rows, (8, LANES)).astype(BF16), upper_ref[...])
    where_ = d(tri_ref[...], oa.astype(BF16)) + slot_off[0:1, :]
    poss = [jnp.sum(jnp.where(oh, where_, 0.0), axis=-1, keepdims=True).astype(I32) for oh in ohs]

    p_out = jnp.zeros((TM, LANES), I32)
    g_out = jnp.zeros((TM, LANES), F32)
    for k in range(TOP_K):
        p_out = jnp.where(lane == k, poss[k], p_out)
        g_out = jnp.where(lane == k, gates[k], g_out)
    pos_ref[0] = p_out
    post_ref[0] = jnp.transpose(p_out)[0:8, :]
    gt_ref[0] = g_out

    srow = lax.broadcasted_iota(I32, (8, LANES), 0)
    meta = jnp.where(srow == 0, hist, jnp.where(srow == 1, run_ref[...], jnp.where(srow == 2, slot_off, 0.0)))
    meta_ref[0] = meta.astype(I32)
    run_ref[...] = run_ref[...] + jnp.floor((hist + (ROW_ALIGN - 1)) * (1.0 / ROW_ALIGN)) * ROW_ALIGN
    cnt_ref[...] = run_ref[...]


SUB = 8
TILE_SUBLANES = 8
ROW_ALIGN = TILE_SUBLANES // SUB
RUN_CHUNK = 16
SLOT_ROWS = TM * TOP_K + N_EXPERTS * RUN_CHUNK


def _rows(ref, row0, nrows):
    start = row0 * SUB if isinstance(row0, int) else pl.multiple_of(row0 * SUB, TILE_SUBLANES)
    return ref.at[pl.ds(start, nrows * SUB), :]


def _to_tiles(ref, val, nrows):
    for j in range(SUB):
        ref[pl.ds(j, nrows, stride=SUB), :] = val[:, LANES * j:LANES * (j + 1)]


def _from_tiles(ref, nrows):
    return jnp.concatenate([ref[pl.ds(j, nrows, stride=SUB), :] for j in range(SUB)], axis=1)


def _slot_matrix(pos, weights):
    col = lax.broadcasted_iota(I32, (TM, SLOT_ROWS), 1)
    m = jnp.zeros((TM, SLOT_ROWS), F32)
    for k in range(TOP_K):
        m = jnp.where(col == pos[:, k:k + 1], weights[k], m)
    return m.astype(BF16)


def _run_copies(meta_ref, pst_ref, buf_ref, hbm_ref, sem, to_hbm):
    total = jnp.int32(0)
    for e in range(N_EXPERTS):
        n = meta_ref[0, 0, e]
        nch = (n + (RUN_CHUNK - 1)) // RUN_CHUNK
        seg0 = pst_ref[e] + meta_ref[0, 1, e]
        slot0 = meta_ref[0, 2, e]

        def chunk(c, carry):
            a = _rows(buf_ref, slot0 + c * RUN_CHUNK, RUN_CHUNK)
            b = _rows(hbm_ref, seg0 + c * RUN_CHUNK, RUN_CHUNK)
            (pltpu.make_async_copy(a, b, sem) if to_hbm else pltpu.make_async_copy(b, a, sem)).start()
            return carry

        lax.fori_loop(0, nch, chunk, 0)
        total = total + nch
    return total


def _run_waits(total, buf_ref, hbm_ref, sem, to_hbm):
    a = _rows(buf_ref, 0, RUN_CHUNK)
    b = _rows(hbm_ref, 0, RUN_CHUNK)

    def one(c, carry):
        (pltpu.make_async_copy(a, b, sem) if to_hbm else pltpu.make_async_copy(b, a, sem)).wait()
        return carry

    lax.fori_loop(0, total, one, 0)


def _run_total(meta_ref):
    total = jnp.int32(0)
    for e in range(N_EXPERTS):
        total = total + (meta_ref[0, 0, e] + (RUN_CHUNK - 1)) // RUN_CHUNK
    return total


def _disp_kernel(pst_ref, cnt_ref, pad_ref, na_ref, meta_ref, hf_ref, post_ref, xb_ref, srt_ref, zero_ref, tot_ref,
                 sem, zsem, *, n_blocks):
    step = pl.program_id(0) * pl.num_programs(1) + pl.program_id(1)
    n_steps = pl.num_programs(0) * pl.num_programs(1)
    slot = step % 2
    first = step == 0

    @pl.when(first)
    def _():
        zero_ref[...] = jnp.zeros_like(zero_ref)

        def zblock(j):
            return pltpu.make_async_copy(zero_ref, _rows(xb_ref, j * EXPERT_BLOCK, EXPERT_BLOCK), zsem)

        def zb_start(j, carry):
            zblock(j).start()
            return carry

        def zb_wait(j, carry):
            zblock(j).wait()
            return carry

        lax.fori_loop(na_ref[0], n_blocks, zb_start, 0)
        lax.fori_loop(na_ref[0], n_blocks, zb_wait, 0)

        for e in range(N_EXPERTS):
            lo = pst_ref[e] + cnt_ref[e]
            nrow = pad_ref[e] - cnt_ref[e]
            nz = nrow // RUN_CHUNK
            lo1 = lo + nz * RUN_CHUNK
            n1 = (nrow - nz * RUN_CHUNK) // ROW_ALIGN

            def zchunk(c):
                return pltpu.make_async_copy(_rows(zero_ref, 0, RUN_CHUNK), _rows(xb_ref, lo + c * RUN_CHUNK, RUN_CHUNK), zsem)

            def zrow(r):
                return pltpu.make_async_copy(_rows(zero_ref, 0, ROW_ALIGN), _rows(xb_ref, lo1 + r * ROW_ALIGN, ROW_ALIGN), zsem)

            def zc_start(c, carry):
                zchunk(c).start()
                return carry

            def zc_wait(c, carry):
                zchunk(c).wait()
                return carry

            def zr_start(r, carry):
                zrow(r).start()
                return carry

            def zr_wait(r, carry):
                zrow(r).wait()
                return carry

            lax.fori_loop(0, nz, zc_start, 0)
            lax.fori_loop(0, n1, zr_start, 0)
            lax.fori_loop(0, nz, zc_wait, 0)
            lax.fori_loop(0, n1, zr_wait, 0)

    srow = lax.broadcasted_iota(I32, (SLOT_ROWS, TM), 0)
    post = post_ref[0]
    perm = jnp.zeros((SLOT_ROWS, TM), F32)
    for k in range(TOP_K):
        perm = jnp.where(srow == post[k:k + 1, :], 1.0, perm)
    srt = jnp.dot(perm.astype(BF16), hf_ref[0], preferred_element_type=F32)
    _to_tiles(srt_ref.at[slot], srt, SLOT_ROWS)

    @pl.when(step > 0)
    def _():
        _run_waits(tot_ref[1 - slot], srt_ref.at[1 - slot], xb_ref, sem.at[1 - slot], True)

    total = _run_copies(meta_ref, pst_ref, srt_ref.at[slot], xb_ref, sem.at[slot], True)
    tot_ref[slot] = total

    @pl.when(step == n_steps - 1)
    def _():
        _run_waits(total, srt_ref.at[slot], xb_ref, sem.at[slot], True)


def _dispatch(pstart, counts, padded, n_act, meta, hf, pos, n_buf):
    B, Tq, D = hf.shape
    nt = Tq // TM
    assert D == SUB * LANES
    grid_spec = pltpu.PrefetchScalarGridSpec(
        num_scalar_prefetch=4,
        grid=(B, nt),
        in_specs=[
            pl.BlockSpec((1, 8, LANES), lambda b, i, *_: (b * nt + i, 0, 0), memory_space=pltpu.SMEM),
            pl.BlockSpec((1, TM, D), lambda b, i, *_: (b, i, 0)),
            pl.BlockSpec((1, 8, TM), lambda b, i, *_: (b * nt + i, 0, 0)),
        ],
        out_specs=pl.BlockSpec(memory_space=pl.ANY),
        scratch_shapes=[pltpu.VMEM((2, SLOT_ROWS * SUB, LANES), F32), pltpu.VMEM((EXPERT_BLOCK * SUB, LANES), F32),
                        pltpu.SMEM((2,), I32), pltpu.SemaphoreType.DMA((2,)), pltpu.SemaphoreType.DMA],
    )
    return pl.pallas_call(
        functools.partial(_disp_kernel, n_blocks=n_buf // EXPERT_BLOCK),
        grid_spec=grid_spec,
        out_shape=jax.ShapeDtypeStruct((n_buf * SUB, LANES), F32),
        compiler_params=_cp(("arbitrary", "arbitrary")),
        name="dispatch",
    )(pstart, counts, padded, n_act, meta, hf, pos)


W_ROWS = 512
W_DELAY = 2


def _exp_kernel(be_ref, bx_ref, na_ref, nx_ref, ps_ref, pr_ref, npv_ref, x_ref, b1a_ref, b1b_ref, b2_ref, w1_hbm, w2_hbm,
                o_ref, st1, st2, w1a_s, w1b_s, w2_s, sem, *, e_off):
    j = pl.program_id(0)
    active = j < na_ref[0]
    e_cur = be_ref[j]
    p = pr_ref[e_cur]
    k = bx_ref[j] - ps_ref[e_cur]
    nxt = nx_ref[e_cur]
    n_prev = npv_ref[e_cur]
    n_slices = st1.shape[0] // W_ROWS

    def copies(e):
        return (pltpu.make_async_copy(w1_hbm.at[e], st1, sem.at[0]), pltpu.make_async_copy(w2_hbm.at[e], st2, sem.at[1]))

    def fetch(e):
        for cp in copies(e):
            cp.start()

    def fetch_wait():
        for cp in copies(0):
            cp.wait()

    def convert(slot, s):
        r0 = pl.multiple_of(s * W_ROWS, W_ROWS)
        lane = lax.broadcasted_iota(I32, (W_ROWS, LANES), 1)
        lo = lane < 64
        idx = jnp.where(lo, 2 * lane, 2 * (lane - 64) + 1)
        for c in range(st1.shape[1] // (2 * LANES)):
            a = st1[pl.ds(r0, W_ROWS), 2 * LANES * c:2 * LANES * c + LANES]
            b = st1[pl.ds(r0, W_ROWS), 2 * LANES * c + LANES:2 * LANES * (c + 1)]
            pa = jnp.take_along_axis(a, idx, axis=1)
            pb = jnp.take_along_axis(b, idx, axis=1)
            ev = jnp.where(lo, pa, pltpu.roll(pb, 64, axis=1))
            od = jnp.where(lo, pltpu.roll(pa, 64, axis=1), pb)
            w1a_s[slot, pl.ds(r0, W_ROWS), LANES * c:LANES * (c + 1)] = ev.astype(BF16)
            w1b_s[slot, pl.ds(r0, W_ROWS), LANES * c:LANES * (c + 1)] = od.astype(BF16)
        w2_s[slot, pl.ds(r0, W_ROWS), :] = st2[pl.ds(r0, W_ROWS), :].astype(BF16)

    def convert_range(slot, s0):
        def one(s, carry):
            convert(slot, s)
            return carry

        lax.fori_loop(s0, n_slices, one, 0)

    @pl.when(j == 0)
    def _():
        fetch(be_ref[0] + e_off)
        fetch_wait()
        convert_range(0, 0)

    @pl.when(active & (k == 0) & (j > 0))
    def _():
        @pl.when(n_prev <= W_DELAY)
        def _():
            fetch_wait()

        convert_range(p, jnp.clip(n_prev - W_DELAY, 0, n_slices))

    @pl.when(active & (k == 0) & (nxt >= 0))
    def _():
        fetch(nxt + e_off)

    conv = active & (nxt >= 0) & (k >= W_DELAY) & (k < W_DELAY + n_slices)

    @pl.when(conv & (k == W_DELAY))
    def _():
        fetch_wait()

    def ffn(with_convert):
        if with_convert:
            convert(1 - p, k - W_DELAY)
        x = _from_tiles(x_ref, EXPERT_BLOCK).astype(BF16)
        ug = jnp.dot(x, w1a_s[p], preferred_element_type=F32) + b1a_ref[0]
        ul = jnp.dot(x, w1b_s[p], preferred_element_type=F32) + b1b_ref[0]
        xg = jnp.minimum(ug, SWIGLU_LIMIT)
        xl = jnp.clip(ul, -SWIGLU_LIMIT, SWIGLU_LIMIT)
        act = xg * jax.nn.sigmoid(SWIGLU_ALPHA * xg) * (xl + 1.0)
        y = jnp.dot(act.astype(BF16), w2_s[p], preferred_element_type=F32) + b2_ref[0]
        _to_tiles(o_ref, y, EXPERT_BLOCK)

    pl.when(conv)(functools.partial(ffn, True))
    pl.when(active & jnp.logical_not(conv))(functools.partial(ffn, False))

    @pl.when(jnp.logical_not(active))
    def _():
        o_ref[...] = jnp.zeros_like(o_ref)


def _experts(sched, xb, lw):
    nb = xb.shape[0] // (EXPERT_BLOCK * SUB)
    De, D = lw["w2"].shape[1:]
    assert De == D and D % W_ROWS == 0
    wmap = lambda j, be, *_: (be[j], 0, 0)
    grid_spec = pltpu.PrefetchScalarGridSpec(
        num_scalar_prefetch=len(sched),
        grid=(nb,),
        in_specs=[
            pl.BlockSpec((EXPERT_BLOCK * SUB, LANES), lambda j, be, bx, *_: (bx[j], 0)),
            pl.BlockSpec((1, 1, De), wmap),
            pl.BlockSpec((1, 1, De), wmap),
            pl.BlockSpec((1, 1, D), wmap),
            pl.BlockSpec(memory_space=pl.ANY),
            pl.BlockSpec(memory_space=pl.ANY),
        ],
        out_specs=pl.BlockSpec((EXPERT_BLOCK * SUB, LANES), lambda j, *_: (j, 0)),
        scratch_shapes=[pltpu.VMEM((D, 2 * De), F32), pltpu.VMEM((De, D), F32),
                        pltpu.VMEM((2, D, De), BF16), pltpu.VMEM((2, D, De), BF16), pltpu.VMEM((2, De, D), BF16),
                        pltpu.SemaphoreType.DMA((2,))],
    )
    return pl.pallas_call(
        functools.partial(_exp_kernel, e_off=lw["e_off"]),
        grid_spec=grid_spec,
        out_shape=jax.ShapeDtypeStruct(xb.shape, F32),
        compiler_params=pltpu.CompilerParams(dimension_semantics=("arbitrary",), vmem_limit_bytes=EXPERT_VMEM_LIMIT),
        name="experts",
    )(*sched, xb, lw["b1a"], lw["b1b"], lw["b2"], lw["w1"], lw["w2"])


def _comb_kernel(pst_ref, meta_ref, meta_next_ref, pos_ref, gt_ref, xn_ref, mod_ref, yb_ref, o_ref, buf_ref, sem):
    step = pl.program_id(0) * pl.num_programs(1) + pl.program_id(1)
    n_steps = pl.num_programs(0) * pl.num_programs(1)
    slot = step % 2

    @pl.when(step == 0)
    def _():
        buf_ref[...] = jnp.zeros_like(buf_ref)
        _run_copies(meta_ref, pst_ref, buf_ref.at[0], yb_ref, sem.at[0], False)

    @pl.when(step < n_steps - 1)
    def _():
        _run_copies(meta_next_ref, pst_ref, buf_ref.at[1 - slot], yb_ref, sem.at[1 - slot], False)

    gt = gt_ref[0]
    g = _slot_matrix(pos_ref[0], [gt[:, k:k + 1] for k in range(TOP_K)])
    _run_waits(_run_total(meta_ref), buf_ref.at[slot], yb_ref, sem.at[slot], False)
    rows = _from_tiles(buf_ref.at[slot], SLOT_ROWS).astype(BF16)
    y = jnp.dot(g, rows, preferred_element_type=F32)
    g2 = mod_ref[0, 0][5:6]
    o_ref[0] = xn_ref[0] + g2 * y


def _combine(pstart, meta, pos, gates, xn, modsel, yb, t_off):
    B, Tq, D = xn.shape
    nt = Tq // TM
    grid_spec = pltpu.PrefetchScalarGridSpec(
        num_scalar_prefetch=1,
        grid=(B, nt),
        in_specs=[
            pl.BlockSpec((1, 8, LANES), lambda b, i, *_: (b * nt + i, 0, 0), memory_space=pltpu.SMEM),
            pl.BlockSpec((1, 8, LANES), lambda b, i, *_: (jnp.minimum(b * nt + i + 1, B * nt - 1), 0, 0),
                         memory_space=pltpu.SMEM),
            pl.BlockSpec((1, TM, LANES), lambda b, i, *_: (b, i, 0)),
            pl.BlockSpec((1, TM, LANES), lambda b, i, *_: (b, i, 0)),
            pl.BlockSpec((1, TM, D), lambda b, i, *_: (b, i, 0)),
            pl.BlockSpec((1, 1, 6, D), lambda b, i, *_: (b, jnp.minimum(i + t_off, 1), 0, 0)),
            pl.BlockSpec(memory_space=pl.ANY),
        ],
        out_specs=pl.BlockSpec((1, TM, D), lambda b, i, *_: (b, i, 0)),
        scratch_shapes=[pltpu.VMEM((2, SLOT_ROWS * SUB, LANES), F32), pltpu.SemaphoreType.DMA((2,))],
    )
    return pl.pallas_call(
        _comb_kernel,
        grid_spec=grid_spec,
        out_shape=jax.ShapeDtypeStruct((B, Tq, D), F32),
        compiler_params=_cp(("arbitrary", "arbitrary")),
        name="combine",
    )(pstart, meta, meta, pos, gates, xn, modsel, yb)


def _cols(w, lo, n):
    return w[..., lo:lo + n]


def _slot64(w, lo):
    partner = [_cols(w, lo + 16, 16), _cols(w, lo, 16), _cols(w, lo + 48, 16), _cols(w, lo + 32, 16)]
    return [_cols(w, lo, 64)] + partner


def _layer_weights(l, a):
    lw = {}
    lw["n1"] = a["norm1_g"][l][None, :]
    lw["n2"] = a["norm2_g"][l][None, :]
    w = a["w_in"][l]
    zeros64 = jnp.zeros(w.shape[:-1] + (64,), w.dtype)
    parts = [_cols(w, 0, 256)] + _slot64(w, 256)
    for g in range(SWA_KV_HEADS):
        parts += _slot64(w, 320 + 64 * g)
    for g in range(SWA_KV_HEADS):
        parts += [_cols(w, 448 + 64 * g, 64), zeros64]
    parts += [_cols(w, 576, 384)]
    for h in range(SWA_HEADS):
        parts += _slot64(w, 960 + 64 * h)
    parts += [_cols(w, 1216, 512)]
    lw["w_in"] = jnp.concatenate(parts, axis=-1).astype(BF16)
    lw["kvg"] = a["mla_kv_norm"][l][None, :]
    lw["qg"] = a["mla_q_norm"][l][None, :]
    w = a["mla_w_uq"][l]
    parts = []
    for h in range(MLA_HEADS):
        parts += [_cols(w, MLA_QK * h, 128)] + _slot64(w, MLA_QK * h + 128)
    lw["w_uq"] = jnp.concatenate(parts, axis=-1).astype(BF16)
    w = a["mla_w_ukv"][l]
    parts = [_cols(w, 256 * h, 128) for h in range(MLA_HEADS)] + [_cols(w, 256 * h + 128, 128) for h in range(MLA_HEADS)]
    lw["w_ukv"] = jnp.concatenate(parts, axis=-1).astype(BF16)
    gq = a["mla_q_head_norm"][l]
    lw["gq"] = (jnp.concatenate([gq[:128]] + _slot64(gq, 128)) * (MLA_QK ** -0.5 * LOG2E))[None, :]
    gk = a["mla_k_head_norm"][l]
    lw["gkn"] = gk[:128][None, :]
    lw["gkp"] = jnp.concatenate(_slot64(gk, 128))[None, :]
    lw["gsq"] = (jnp.concatenate(_slot64(a["swa_q_norm"][l], 0)) * (SWA_HEAD_DIM ** -0.5))[None, :]
    lw["gsk"] = jnp.concatenate(_slot64(a["swa_k_norm"][l], 0))[None, :]
    lw["conv_w"] = a["conv_w"][l]
    lw["conv_b"] = a["conv_b"][l][None, :]
    lw["conv_g"] = a["conv_ln_g"][l][None, :]
    lw["conv_bb"] = a["conv_ln_b"][l][None, :]
    lw["sink"] = a["swa_sink"][l]
    wo = a["w_out"][l]
    lw["w_o1"] = wo[0:512].astype(BF16)
    lw["w_o2"] = wo[512:768].astype(BF16)
    o3 = wo[768:1024].reshape(SWA_HEADS, SWA_HEAD_DIM, -1)
    lw["w_o3"] = jnp.concatenate([o3, jnp.zeros_like(o3)], axis=1).reshape(SWA_HEADS * 128, -1).astype(BF16)
    wr = jnp.pad(a["router_w"][l], ((0, 0), (0, LANES - N_EXPERTS)))
    lw["wr_hi"] = wr.astype(BF16)
    lw["wr_lo"] = (wr - lw["wr_hi"].astype(F32)).astype(BF16)
    lw["br"] = jnp.pad(a["router_b"][l], (0, LANES - N_EXPERTS), constant_values=NEG)[None, :]
    lw["w1"] = a["exp_w1"].reshape((-1,) + a["exp_w1"].shape[2:])
    lw["e_off"] = l * a["exp_w1"].shape[1]
    b1 = a["exp_b1"][l]
    lw["b1a"] = b1[:, None, 0::2]
    lw["b1b"] = b1[:, None, 1::2]
    lw["w2"] = a["exp_w2"].reshape((-1,) + a["exp_w2"].shape[2:])
    lw["b2"] = a["exp_b2"][l][:, None, :]
    return lw


def _rope_tables(n_ctx, n_lat):
    q = MLA_ROPE // 4
    n = jnp.arange(n_lat, dtype=I32)
    row = (n // GRID_W).astype(F32)
    col = (n % GRID_W).astype(F32)
    inv = ROPE_BASE ** (-jnp.arange(q, dtype=F32) / q)
    ang_r = row[:, None] * inv
    ang_c = col[:, None] * inv
    cos = jnp.concatenate([jnp.cos(ang_r), jnp.cos(ang_r), jnp.cos(ang_c), jnp.cos(ang_c)], axis=1)
    sin = jnp.concatenate([-jnp.sin(ang_r), jnp.sin(ang_r), -jnp.sin(ang_c), jnp.sin(ang_c)], axis=1)
    cos = jnp.concatenate([jnp.ones((n_ctx, 64), F32), cos], axis=0)
    sin = jnp.concatenate([jnp.zeros((n_ctx, 64), F32), sin], axis=0)
    z = jnp.zeros_like(cos)
    return jnp.concatenate([cos, z], axis=1), jnp.concatenate([sin, z], axis=1)


def _routing_tables(cnt_f, n_blocks):
    counts = cnt_f[0, :N_EXPERTS].astype(I32)
    padded = (counts + (RUN_CHUNK - 1) + EXPERT_BLOCK - 1) // EXPERT_BLOCK * EXPERT_BLOCK
    padded = jnp.where(counts > 0, padded, 0)
    pend = jnp.cumsum(padded)
    pstart = pend - padded
    n_act = pend[-1] // EXPERT_BLOCK
    blk = jnp.minimum(jnp.arange(n_blocks, dtype=I32), n_act - 1)
    blk_e = jnp.sum((pend[None, :] <= (blk * EXPERT_BLOCK)[:, None]).astype(I32), axis=1)
    blk_e = jnp.minimum(blk_e, N_EXPERTS - 1)
    nbk = padded // EXPERT_BLOCK
    has = nbk > 0
    ids = jnp.arange(N_EXPERTS, dtype=I32)
    later = has[None, :] & (ids[None, :] > ids[:, None])
    nxt_e = jnp.min(jnp.where(later, ids[None, :], N_EXPERTS), axis=1)
    nxt_e = jnp.where(nxt_e == N_EXPERTS, -1, nxt_e)
    earlier = has[None, :] & (ids[None, :] < ids[:, None])
    prv_e = jnp.max(jnp.where(earlier, ids[None, :], -1), axis=1)
    nb_prev_e = jnp.sum(jnp.where(ids[None, :] == prv_e[:, None], nbk[None, :], 0), axis=1)
    set_e = (jnp.cumsum(has.astype(I32)) - 1) % 2
    sched = (blk_e, blk, n_act.reshape(1), nxt_e, pstart // EXPERT_BLOCK, set_e, nb_prev_e)
    sched = tuple(s.astype(I32) for s in sched)
    return counts, padded.astype(I32), pstart.astype(I32), sched


def kernel(x, c, ctx, c_ctx, norm1_g, norm2_g, w_ada, b_ada, w_in, mla_q_norm, mla_kv_norm, mla_w_uq, mla_w_ukv, mla_q_head_norm, mla_k_head_norm, conv_w, conv_b, conv_ln_g, conv_ln_b, swa_q_norm, swa_k_norm, swa_sink, w_out, router_w, router_b, exp_w1, exp_b1, exp_w2, exp_b2):
    a = dict(norm1_g=norm1_g, norm2_g=norm2_g, w_in=w_in, mla_q_norm=mla_q_norm, mla_kv_norm=mla_kv_norm,
             mla_w_uq=mla_w_uq, mla_w_ukv=mla_w_ukv, mla_q_head_norm=mla_q_head_norm, mla_k_head_norm=mla_k_head_norm,
             conv_w=conv_w, conv_b=conv_b, conv_ln_g=conv_ln_g, conv_ln_b=conv_ln_b, swa_q_norm=swa_q_norm,
             swa_k_norm=swa_k_norm, swa_sink=swa_sink, w_out=w_out, router_w=router_w, router_b=router_b,
             exp_w1=exp_w1, exp_b1=exp_b1, exp_w2=exp_w2, exp_b2=exp_b2)
    B, S, D = x.shape
    n_ctx = ctx.shape[1]
    depth = w_ada.shape[0]
    assert n_ctx == TM and S % TM == 0 and B + 1 <= 16
    T = n_ctx + S

    s_in = jnp.zeros((16, D), F32).at[:B].set(c).at[B].set(c_ctx)
    mods = _ada(s_in, w_ada, b_ada)
    cos_t, sin_t = _rope_tables(n_ctx, S)
    tri = jnp.tril(jnp.ones((TM, TM), F32), -1).astype(BF16)
    upper = jnp.triu(jnp.ones((LANES, LANES), F32), 1).astype(BF16)

    first, rest, shift = ctx, x, 1
    for l in range(depth):
        last = l == depth - 1
        t_off = 1 if last else 0
        lw = _layer_weights(l, a)
        m = mods[l].reshape(16, 6, D)
        modsel = jnp.stack([jnp.broadcast_to(m[B], (B, 6, D)), m[:B]], axis=1)

        qm, km, vm, qs, ks, vs, u = _prep(first, rest, shift, modsel, lw, cos_t, sin_t, 0)
        xn, hf, pos, pos_t, gt_o, meta, cnt = _mixers(qm, km, vm, u, qs, ks, vs, first, rest, shift, modsel, lw, tri,
                                                      upper, t_off)

        n_tok = B * (T - t_off * TM)
        nk = n_tok * TOP_K
        n_align = (n_tok // TM) * N_EXPERTS * (ROW_ALIGN - 1)
        n_buf = -(-(nk + n_align + N_EXPERTS * (RUN_CHUNK - 1 + EXPERT_BLOCK - 1)) // EXPERT_BLOCK) * EXPERT_BLOCK
        counts, padded, pstart, sched = _routing_tables(cnt, n_buf // EXPERT_BLOCK)
        xb = _dispatch(pstart, counts, padded, sched[2], meta, hf, pos_t, n_buf)
        yb = _experts(sched, xb, lw)
        xu = _combine(pstart, meta, pos, gt_o, xn, modsel, yb, t_off)
        first, rest, shift = xu, xu, 0
    return xu
```

```python
import functools

import jax
import jax.numpy as jnp
from jax import lax
from jax.experimental import pallas as pl
from jax.experimental.pallas import tpu as pltpu

F32 = jnp.float32
BF16 = jnp.bfloat16
I32 = jnp.int32

GRID_W = 64
ROPE_BASE = 10000.0
EPS = 1e-6
MLA_HEADS = 4
MLA_NOPE = 128
MLA_ROPE = 64
MLA_V = 128
MLA_QK = MLA_NOPE + MLA_ROPE
MLA_Q_RANK = 384
MLA_KV_RANK = 256
CONV_CH = 256
CONV_WIDTH = 31
SWA_HEADS = 4
SWA_KV_HEADS = 2
SWA_HEAD_DIM = 64
SWA_WINDOW = 128
N_EXPERTS = 32
TOP_K = 4
SWIGLU_LIMIT = 7.0
SWIGLU_ALPHA = 1.702
EXPERT_BLOCK = 512

LANES = 128
TM = 256
CONV_HALO = 16
VMEM_LIMIT = 48 * 1024 * 1024
MIXER_VMEM_LIMIT = 56 * 1024 * 1024
EXPERT_VMEM_LIMIT = 56 * 1024 * 1024
NEG = -1e30
LOG2E = 1.4426950408889634


def _cp(sem):
    return pltpu.CompilerParams(dimension_semantics=sem, vmem_limit_bytes=VMEM_LIMIT)


def _full(shape):
    n = len(shape)
    return pl.BlockSpec(shape, lambda *a, _n=n: (0,) * _n)


def _split(x):
    hi = x.astype(BF16)
    lo = (x - hi.astype(F32)).astype(BF16)
    return hi, lo


def _dot3(a, b):
    ah, al = _split(a)
    bh, bl = _split(b)
    d = functools.partial(jnp.dot, preferred_element_type=F32)
    return d(ah, bh) + d(ah, bl) + d(al, bh)


def _ada_kernel(s_ref, w_ref, b_ref, o_ref):
    s = s_ref[...]
    s = s * jax.nn.sigmoid(s)
    o_ref[0] = _dot3(s, w_ref[0]) + b_ref[0]


def _ada(s_in, w_ada, b_ada):
    L, D, N = w_ada.shape
    tn = 1536
    return pl.pallas_call(
        _ada_kernel,
        grid=(L, N // tn),
        in_specs=[
            pl.BlockSpec((16, D), lambda l, j: (0, 0)),
            pl.BlockSpec((1, D, tn), lambda l, j: (l, 0, j)),
            pl.BlockSpec((1, 1, tn), lambda l, j: (l, 0, j)),
        ],
        out_specs=pl.BlockSpec((1, 16, tn), lambda l, j: (l, 0, j)),
        out_shape=jax.ShapeDtypeStruct((L, 16, N), F32),
        compiler_params=_cp(("arbitrary", "arbitrary")),
        name="ada",
    )(s_in, w_ada, b_ada.reshape(L, 1, N))


def _rope(x, c, s):
    return x * c + pltpu.roll(x, 64, axis=1) * s


def _stream_specs(D, t_off, shift):
    return [pl.BlockSpec((1, TM, D), lambda b, i, *_: (b, 0, 0)),
            pl.BlockSpec((1, TM, D), lambda b, i, *_: (b, jnp.maximum(i + t_off - shift, 0), 0))]


def _stream_tile(first_ref, rest_ref, t_off):
    return jnp.where(pl.program_id(1) + t_off == 0, first_ref[0], rest_ref[0])


def _prep_kernel(xa_ref, xb_ref, mod_ref, n1_ref, win_ref, kvg_ref, wukv_ref, qg_ref, wuq_ref, gq_ref, gkn_ref, gkp_ref,
                 gsq_ref, gsk_ref, cos_ref, sin_ref, qm_ref, km_ref, vm_ref, qs_ref, ks_ref, vs_ref, u_ref, *, t_off):
    x = _stream_tile(xa_ref, xb_ref, t_off)
    mod = mod_ref[0, 0]
    sh, sc = mod[0:1], mod[1:2]
    y = x * lax.rsqrt(jnp.mean(x * x, axis=-1, keepdims=True) + EPS) * n1_ref[...]
    h = y * (1.0 + sc) + sh
    p = jnp.dot(h.astype(BF16), win_ref[...], preferred_element_type=F32)
    c = cos_ref[...]
    s = sin_ref[...]
    lane = lax.broadcasted_iota(I32, (TM, LANES), 1)

    def ss_lo(v):
        return 0.5 * jnp.sum(v * v, axis=-1, keepdims=True)

    ckv = p[:, 0:256]
    ckvn = ckv * lax.rsqrt(jnp.mean(ckv * ckv, axis=-1, keepdims=True) + EPS) * kvg_ref[...]
    kv = jnp.dot(ckvn.astype(BF16), wukv_ref[...], preferred_element_type=F32)
    kpe = p[:, 256:384]
    ss_pe = ss_lo(kpe)
    kpe_rot = _rope(kpe * gkp_ref[...], c, s)
    for hh in range(MLA_HEADS):
        kn = kv[:, 128 * hh:128 * hh + 128]
        r = lax.rsqrt((jnp.sum(kn * kn, axis=-1, keepdims=True) + ss_pe) * (1.0 / MLA_QK) + EPS)
        km_ref[0, hh, :, 0:128] = (kn * r * gkn_ref[...]).astype(BF16)
        km_ref[0, hh, :, 128:256] = (kpe_rot * r).astype(BF16)
        vm_ref[0, hh, :, 0:128] = kv[:, 512 + 128 * hh:640 + 128 * hh].astype(BF16)
        vm_ref[0, hh, :, 128:256] = (lane == 0).astype(BF16)

    cq = p[:, 896:1280]
    cqn = cq * lax.rsqrt(jnp.mean(cq * cq, axis=-1, keepdims=True) + EPS) * qg_ref[...]
    q = jnp.dot(cqn.astype(BF16), wuq_ref[...], preferred_element_type=F32)
    gq = gq_ref[...]
    for hh in range(MLA_HEADS):
        qn = q[:, 256 * hh:256 * hh + 128]
        qp = q[:, 256 * hh + 128:256 * hh + 256]
        r = lax.rsqrt((jnp.sum(qn * qn, axis=-1, keepdims=True) + ss_lo(qp)) * (1.0 / MLA_QK) + EPS)
        qm_ref[0, hh, :, 0:128] = (qn * r * gq[:, 0:128]).astype(BF16)
        qm_ref[0, hh, :, 128:256] = _rope(qp * r * gq[:, 128:256], c, s).astype(BF16)

    for g in range(SWA_KV_HEADS):
        xk = p[:, 384 + 128 * g:512 + 128 * g]
        r = lax.rsqrt(ss_lo(xk) * (1.0 / SWA_HEAD_DIM) + EPS)
        ks_ref[0, g] = _rope(xk * r * gsk_ref[...], c, s).astype(BF16)
        vs_ref[0, g] = p[:, 640 + 128 * g:768 + 128 * g].astype(BF16)
    for hh in range(SWA_HEADS):
        xq = p[:, 1280 + 128 * hh:1408 + 128 * hh]
        r = lax.rsqrt(ss_lo(xq) * (1.0 / SWA_HEAD_DIM) + EPS)
        qs_ref[0, hh] = _rope(xq * r * gsq_ref[...], c, s).astype(BF16)

    u_ref[0] = p[:, 1792:2048] * jax.nn.sigmoid(p[:, 2048:2304])


def _prep(first, rest, shift, modsel, lw, cos_t, sin_t, t_off):
    B, _, D = first.shape
    T = rest.shape[1] + shift * TM
    nt = T // TM - t_off
    ncol = lw["w_in"].shape[1]
    row = lambda b, i: (b, i + t_off, 0)
    head = lambda b, i: (b, 0, i + t_off, 0)
    in_specs = _stream_specs(D, t_off, shift) + [
        pl.BlockSpec((1, 1, 6, D), lambda b, i: (b, jnp.minimum(i + t_off, 1), 0, 0)),
        _full((1, D)),
        _full((D, ncol)),
        _full((1, MLA_KV_RANK)),
        _full((MLA_KV_RANK, 1024)),
        _full((1, MLA_Q_RANK)),
        _full((MLA_Q_RANK, 1024)),
        _full((1, 256)),
        _full((1, 128)),
        _full((1, 128)),
        _full((1, 128)),
        _full((1, 128)),
        pl.BlockSpec((TM, LANES), lambda b, i: (i + t_off, 0)),
        pl.BlockSpec((TM, LANES), lambda b, i: (i + t_off, 0)),
    ]
    out_shape = [
        jax.ShapeDtypeStruct((B, MLA_HEADS, T, 256), BF16),
        jax.ShapeDtypeStruct((B, MLA_HEADS, T, 256), BF16),
        jax.ShapeDtypeStruct((B, MLA_HEADS, T, 256), BF16),
        jax.ShapeDtypeStruct((B, SWA_HEADS, T, 128), BF16),
        jax.ShapeDtypeStruct((B, SWA_KV_HEADS, T, 128), BF16),
        jax.ShapeDtypeStruct((B, SWA_KV_HEADS, T, 128), BF16),
        jax.ShapeDtypeStruct((B, T, CONV_CH), F32),
    ]
    out_specs = [
        pl.BlockSpec((1, MLA_HEADS, TM, 256), head),
        pl.BlockSpec((1, MLA_HEADS, TM, 256), head),
        pl.BlockSpec((1, MLA_HEADS, TM, 256), head),
        pl.BlockSpec((1, SWA_HEADS, TM, 128), head),
        pl.BlockSpec((1, SWA_KV_HEADS, TM, 128), head),
        pl.BlockSpec((1, SWA_KV_HEADS, TM, 128), head),
        pl.BlockSpec((1, TM, CONV_CH), row),
    ]
    return pl.pallas_call(
        functools.partial(_prep_kernel, t_off=t_off),
        grid=(B, nt),
        in_specs=in_specs,
        out_specs=out_specs,
        out_shape=out_shape,
        compiler_params=pltpu.CompilerParams(
            dimension_semantics=("arbitrary", "arbitrary"), vmem_limit_bytes=VMEM_LIMIT,
            allow_input_fusion=[i in (4, 6, 8) for i in range(16)]),
        name="prep",
    )(first, rest, modsel, lw["n1"], lw["w_in"], lw["kvg"], lw["w_ukv"], lw["qg"], lw["w_uq"], lw["gq"], lw["gkn"], lw["gkp"],
      lw["gsq"], lw["gsk"], cos_t, sin_t)


KEY_CHUNK = 256


def _mixer_kernel(sink_ref, q_ref, k_ref, v_ref, u_ref, cw_ref, cb_ref, cg_ref, cbb_ref, sq_ref, sk_ref, sv_ref,
                  xa_ref, xb_ref, mod_ref, w1_ref, w2_ref, w3_ref, n2_ref, wrh_ref, wrl_ref, br_ref, tri_ref, upper_ref,
                  xn_ref, hf_ref, pos_ref, post_ref, gt_ref, meta_ref, cnt_ref,
                  o_ref, oc_ref, os_ref, s_ref, p_ref, cbuf_ref, run_ref, *, q_off, n_keys):
    @pl.when((pl.program_id(0) == 0) & (pl.program_id(1) == 0))
    def _():
        run_ref[...] = jnp.zeros_like(run_ref)

    qi = pl.program_id(1) + q_off

    def attend(nk):
        for h in range(MLA_HEADS):
            q = q_ref[0, h]
            macc = jnp.full((TM, LANES), -jnp.inf, F32)
            for c in range(nk // KEY_CHUNK):
                k = k_ref[0, h, c * KEY_CHUNK:(c + 1) * KEY_CHUNK, :]
                s = lax.dot_general(q, k, (((1,), (1,)), ((), ())), preferred_element_type=F32)
                s_ref[:, c * KEY_CHUNK:(c + 1) * KEY_CHUNK] = s
                for j in range(KEY_CHUNK // LANES):
                    macc = jnp.maximum(macc, s[:, j * LANES:(j + 1) * LANES])
            m = jnp.max(macc, axis=-1, keepdims=True)
            for c in range(nk // KEY_CHUNK):
                p = jnp.exp2(s_ref[:, c * KEY_CHUNK:(c + 1) * KEY_CHUNK] - m)
                p_ref[:, c * KEY_CHUNK:(c + 1) * KEY_CHUNK] = p.astype(BF16)
            ol = jnp.dot(p_ref[:, 0:nk], v_ref[0, h, 0:nk, :], preferred_element_type=F32)
            o = ol[:, 0:MLA_V] / ol[:, MLA_V:MLA_V + 1]
            o_ref[0, :, 128 * h:128 * h + 128] = o.astype(BF16)

    def tile(is_ctx):
        _conv_tile(qi, n_keys, u_ref, cw_ref, cb_ref, cg_ref, cbb_ref, oc_ref, cbuf_ref)
        attend(TM if is_ctx else n_keys)
        _swa_tile(is_ctx, qi, n_keys, sink_ref, sq_ref, sk_ref, sv_ref, os_ref)
        _out_tile(o_ref, oc_ref, os_ref, xa_ref, xb_ref, mod_ref, w1_ref, w2_ref, w3_ref, n2_ref, wrh_ref, wrl_ref, br_ref,
                  tri_ref, upper_ref, xn_ref, hf_ref, pos_ref, post_ref, gt_ref, meta_ref, cnt_ref, run_ref, q_off)

    if q_off == 0:
        pl.when(qi == 0)(functools.partial(tile, True))
        pl.when(qi > 0)(functools.partial(tile, False))
    else:
        tile(False)


def _mixers(qm, km, vm, u, qs, ks, vs, first, rest, shift, modsel, lw, tri, upper, t_off):
    B, H, T, _ = qm.shape
    C = u.shape[2]
    D = first.shape[2]
    HS, G = qs.shape[1], ks.shape[1]
    nt = T // TM - t_off
    Tq = nt * TM
    qtile = lambda b, i: (b, 0, i + t_off, 0)
    whole = lambda b, i: (b, 0, 0, 0)
    row = lambda b, i: (b, i, 0)
    per_tile = lambda b, i: (b * nt + i, 0, 0)
    in_specs = [
        pl.BlockSpec(memory_space=pltpu.SMEM),
        pl.BlockSpec((1, H, TM, 256), qtile),
        pl.BlockSpec((1, H, T, 256), whole),
        pl.BlockSpec((1, H, T, 256), whole),
        pl.BlockSpec((1, T, C), lambda b, i: (b, 0, 0)),
        _full((CONV_WIDTH, C)),
        _full((1, C)),
        _full((1, C)),
        _full((1, C)),
        pl.BlockSpec((1, HS, TM, 128), qtile),
        pl.BlockSpec((1, G, T, 128), whole),
        pl.BlockSpec((1, G, T, 128), whole),
    ] + _stream_specs(D, t_off, shift) + [
        pl.BlockSpec((1, 1, 6, D), lambda b, i: (b, jnp.minimum(i + t_off, 1), 0, 0)),
        _full((512, D)),
        _full((256, D)),
        _full((512, D)),
        _full((1, D)),
        _full((D, LANES)),
        _full((D, LANES)),
        _full((1, LANES)),
        _full((TM, TM)),
        _full((LANES, LANES)),
    ]
    out_shape = [
        jax.ShapeDtypeStruct((B, Tq, D), F32),
        jax.ShapeDtypeStruct((B, Tq, D), BF16),
        jax.ShapeDtypeStruct((B, Tq, LANES), I32),
        jax.ShapeDtypeStruct((B * nt, 8, TM), I32),
        jax.ShapeDtypeStruct((B, Tq, LANES), F32),
        jax.ShapeDtypeStruct((B * nt, 8, LANES), I32),
        jax.ShapeDtypeStruct((8, LANES), F32),
    ]
    out_specs = [
        pl.BlockSpec((1, TM, D), row),
        pl.BlockSpec((1, TM, D), row),
        pl.BlockSpec((1, TM, LANES), row),
        pl.BlockSpec((1, 8, TM), per_tile),
        pl.BlockSpec((1, TM, LANES), row),
        pl.BlockSpec((1, 8, LANES), per_tile),
        pl.BlockSpec((8, LANES), lambda b, i: (0, 0)),
    ]
    scratch_shapes = [
        pltpu.VMEM((1, TM, H * MLA_V), BF16),
        pltpu.VMEM((1, TM, C), BF16),
        pltpu.VMEM((1, TM, HS * 128), BF16),
        pltpu.VMEM((TM, T), F32),
        pltpu.VMEM((TM, T), BF16),
        pltpu.VMEM((TM + 2 * CONV_HALO, C), F32),
        pltpu.VMEM((8, LANES), F32),
    ]
    return pl.pallas_call(
        functools.partial(_mixer_kernel, q_off=t_off, n_keys=T),
        grid=(B, nt),
        in_specs=in_specs,
        out_specs=out_specs,
        out_shape=out_shape,
        scratch_shapes=scratch_shapes,
        compiler_params=pltpu.CompilerParams(dimension_semantics=("arbitrary", "arbitrary"),
                                             vmem_limit_bytes=MIXER_VMEM_LIMIT),
        name="mixers",
    )(lw["sink"], qm, km, vm, u, lw["conv_w"], lw["conv_b"], lw["conv_g"], lw["conv_bb"], qs, ks, vs, first, rest, modsel,
      lw["w_o1"], lw["w_o2"], lw["w_o3"], lw["n2"], lw["wr_hi"], lw["wr_lo"], lw["br"], tri, upper)


SWA_SPAN = TM + 2 * SWA_WINDOW


def _swa_tile(is_ctx, qi, n_rows, sink_ref, q_ref, k_ref, v_ref, o_ref):
    nt_dims = (((1,), (1,)), ((), ()))

    def latent():
        start = jnp.clip(qi * TM - SWA_WINDOW, 0, n_rows - SWA_SPAN)
        start = pl.multiple_of(start, SWA_WINDOW)
        qpos = qi * TM + lax.broadcasted_iota(I32, (TM, SWA_SPAN), 0)
        kpos = start + lax.broadcasted_iota(I32, (TM, SWA_SPAN), 1)
        valid = (jnp.abs(qpos - kpos) <= SWA_WINDOW) & (kpos >= TM)
        for h in range(SWA_HEADS):
            g = h // (SWA_HEADS // SWA_KV_HEADS)
            q = q_ref[0, h]
            kl = k_ref[0, g, pl.ds(start, SWA_SPAN), :]
            vl = v_ref[0, g, pl.ds(start, SWA_SPAN), :]
            kc = k_ref[0, g, 0:TM, :]
            vc = v_ref[0, g, 0:TM, :]
            sl = lax.dot_general(q, kl, nt_dims, preferred_element_type=F32)
            sl = jnp.where(valid, sl, NEG)
            scx = lax.dot_general(q, kc, nt_dims, preferred_element_type=F32)
            sink = sink_ref[h]
            m = jnp.maximum(jnp.maximum(jnp.max(sl, axis=-1, keepdims=True), jnp.max(scx, axis=-1, keepdims=True)), sink)
            pl_ = jnp.exp(sl - m)
            pc = jnp.exp(scx - m)
            l = jnp.sum(pl_, axis=-1, keepdims=True) + jnp.sum(pc, axis=-1, keepdims=True) + jnp.exp(sink - m)
            o = (jnp.dot(pl_.astype(BF16), vl, preferred_element_type=F32)
                 + jnp.dot(pc.astype(BF16), vc, preferred_element_type=F32)) / l
            o_ref[0, :, 128 * h:128 * h + 128] = o.astype(BF16)

    def context():
        for h in range(SWA_HEADS):
            g = h // (SWA_HEADS // SWA_KV_HEADS)
            q = q_ref[0, h]
            kc = k_ref[0, g, 0:TM, :]
            vc = v_ref[0, g, 0:TM, :]
            scx = lax.dot_general(q, kc, nt_dims, preferred_element_type=F32)
            sink = sink_ref[h]
            m = jnp.maximum(jnp.max(scx, axis=-1, keepdims=True), sink)
            pc = jnp.exp(scx - m)
            l = jnp.sum(pc, axis=-1, keepdims=True) + jnp.exp(sink - m)
            o = jnp.dot(pc.astype(BF16), vc, preferred_element_type=F32) / l
            o_ref[0, :, 128 * h:128 * h + 128] = o.astype(BF16)

    if is_ctx:
        context()
    else:
        latent()


def _conv_tile(i, n_rows, u_ref, w_ref, b_ref, g_ref, bb_ref, o_ref, buf_ref):
    nt_all = n_rows // TM
    start = pl.multiple_of(i * TM, TM)
    ps = pl.multiple_of(jnp.maximum(start - CONV_HALO, 0), 8)
    ns = pl.multiple_of(jnp.minimum(start + TM, n_rows - CONV_HALO), 8)
    keep_prev = jnp.where(i <= 1, 0.0, 1.0)
    keep_next = jnp.where((i == 0) | (i == nt_all - 1), 0.0, 1.0)
    buf_ref[0:CONV_HALO, :] = u_ref[0, pl.ds(ps, CONV_HALO), :] * keep_prev
    buf_ref[CONV_HALO:CONV_HALO + TM, :] = u_ref[0, pl.ds(start, TM), :]
    buf_ref[CONV_HALO + TM:2 * CONV_HALO + TM, :] = u_ref[0, pl.ds(ns, CONV_HALO), :] * keep_next
    off = CONV_HALO - CONV_WIDTH // 2
    accs = [None] * 4
    for j in range(CONV_WIDTH):
        term = buf_ref[off + j:off + j + TM, :] * w_ref[j:j + 1, :]
        accs[j % 4] = term if accs[j % 4] is None else accs[j % 4] + term
    y = (accs[0] + accs[1]) + (accs[2] + accs[3]) + b_ref[...]
    mu = jnp.mean(y, axis=-1, keepdims=True)
    d = y - mu
    var = jnp.mean(d * d, axis=-1, keepdims=True)
    z = d * lax.rsqrt(var + EPS) * g_ref[...] + bb_ref[...]
    o_ref[0] = (z * jax.nn.sigmoid(z)).astype(BF16)


def _out_tile(om_ref, oc_ref, os_ref, xa_ref, xb_ref, mod_ref, w1_ref, w2_ref, w3_ref, n2_ref, wrh_ref, wrl_ref, br_ref,
              tri_ref, upper_ref, xn_ref, hf_ref, pos_ref, post_ref, gt_ref, meta_ref, cnt_ref, run_ref, t_off):
    d = functools.partial(jnp.dot, preferred_element_type=F32)
    mix = d(om_ref[0], w1_ref[...]) + d(oc_ref[0], w2_ref[...]) + d(os_ref[0], w3_ref[...])
    mod = mod_ref[0, 0]
    g1, sh2, sc2 = mod[2:3], mod[3:4], mod[4:5]
    xn = _stream_tile(xa_ref, xb_ref, t_off) + g1 * mix
    xn_ref[0] = xn
    hf = xn * lax.rsqrt(jnp.mean(xn * xn, axis=-1, keepdims=True) + EPS) * n2_ref[...]
    hf = hf * (1.0 + sc2) + sh2
    hf_ref[0] = hf.astype(BF16)

    hi, lo = _split(hf)
    logits = d(hi, wrh_ref[...]) + d(hi, wrl_ref[...]) + d(lo, wrh_ref[...]) + br_ref[...]
    lane = lax.broadcasted_iota(I32, (TM, LANES), 1)
    lane_f = lane.astype(F32)
    l = logits
    ohs, vals = [], []
    for _ in range(TOP_K):
        m = jnp.max(l, axis=-1, keepdims=True)
        idx = jnp.min(jnp.where(l == m, lane_f, float(LANES)), axis=-1, keepdims=True)
        oh = lane_f == idx
        ohs.append(oh)
        vals.append(m)
        l = jnp.where(oh, -jnp.inf, l)
    ex = [jnp.exp(v - vals[0]) for v in vals]
    den = ex[0] + ex[1] + ex[2] + ex[3]
    gates = [e / den for e in ex]

    oa = jnp.zeros((TM, LANES), F32)
    for oh in ohs:
        oa = oa + oh.astype(F32)
    hist = jnp.sum(oa, axis=0, keepdims=True)
    slot_rows = jnp.floor((hist + (RUN_CHUNK - 1)) * (1.0 / RUN_CHUNK)) * RUN_CHUNK
    slot_off = d(jnp.broadcast_to(slot_rows, (8, LANES)).astype(BF16), upper_ref[...])
    where_ = d(tri_ref[...], oa.astype(BF16)) + slot_off[0:1, :]
    poss = [jnp.sum(jnp.where(oh, where_, 0.0), axis=-1, keepdims=True).astype(I32) for oh in ohs]

    p_out = jnp.zeros((TM, LANES), I32)
    g_out = jnp.zeros((TM, LANES), F32)
    for k in range(TOP_K):
        p_out = jnp.where(lane == k, poss[k], p_out)
        g_out = jnp.where(lane == k, gates[k], g_out)
    pos_ref[0] = p_out
    post_ref[0] = jnp.transpose(p_out)[0:8, :]
    gt_ref[0] = g_out

    srow = lax.broadcasted_iota(I32, (8, LANES), 0)
    meta = jnp.where(srow == 0, hist, jnp.where(srow == 1, run_ref[...], jnp.where(srow == 2, slot_off, 0.0)))
    meta_ref[0] = meta.astype(I32)
    run_ref[...] = run_ref[...] + jnp.floor((hist + (ROW_ALIGN - 1)) * (1.0 / ROW_ALIGN)) * ROW_ALIGN
    cnt_ref[...] = run_ref[...]


SUB = 8
TILE_SUBLANES = 8
ROW_ALIGN = TILE_SUBLANES // SUB
RUN_CHUNK = 16
SLOT_ROWS = TM * TOP_K + N_EXPERTS * RUN_CHUNK


def _rows(ref, row0, nrows):
    start = row0 * SUB if isinstance(row0, int) else pl.multiple_of(row0 * SUB, TILE_SUBLANES)
    return ref.at[pl.ds(start, nrows * SUB), :]


def _to_tiles(ref, val, nrows):
    for j in range(SUB):
        ref[pl.ds(j, nrows, stride=SUB), :] = val[:, LANES * j:LANES * (j + 1)]


def _from_tiles(ref, nrows):
    return jnp.concatenate([ref[pl.ds(j, nrows, stride=SUB), :] for j in range(SUB)], axis=1)


def _slot_matrix(pos, weights):
    col = lax.broadcasted_iota(I32, (TM, SLOT_ROWS), 1)
    m = jnp.zeros((TM, SLOT_ROWS), F32)
    for k in range(TOP_K):
        m = jnp.where(col == pos[:, k:k + 1], weights[k], m)
    return m.astype(BF16)


def _run_copies(meta_ref, pst_ref, buf_ref, hbm_ref, sem, to_hbm):
    total = jnp.int32(0)
    for e in range(N_EXPERTS):
        n = meta_ref[0, 0, e]
        nch = (n + (RUN_CHUNK - 1)) // RUN_CHUNK
        seg0 = pst_ref[e] + meta_ref[0, 1, e]
        slot0 = meta_ref[0, 2, e]

        def chunk(c, carry):
            a = _rows(buf_ref, slot0 + c * RUN_CHUNK, RUN_CHUNK)
            b = _rows(hbm_ref, seg0 + c * RUN_CHUNK, RUN_CHUNK)
            (pltpu.make_async_copy(a, b, sem) if to_hbm else pltpu.make_async_copy(b, a, sem)).start()
            return carry

        lax.fori_loop(0, nch, chunk, 0)
        total = total + nch
    return total


def _run_waits(total, buf_ref, hbm_ref, sem, to_hbm):
    a = _rows(buf_ref, 0, RUN_CHUNK)
    b = _rows(hbm_ref, 0, RUN_CHUNK)

    def one(c, carry):
        (pltpu.make_async_copy(a, b, sem) if to_hbm else pltpu.make_async_copy(b, a, sem)).wait()
        return carry

    lax.fori_loop(0, total, one, 0)


def _run_total(meta_ref):
    total = jnp.int32(0)
    for e in range(N_EXPERTS):
        total = total + (meta_ref[0, 0, e] + (RUN_CHUNK - 1)) // RUN_CHUNK
    return total


def _disp_kernel(pst_ref, cnt_ref, pad_ref, na_ref, meta_ref, hf_ref, post_ref, xb_ref, srt_ref, zero_ref, tot_ref,
                 sem, zsem, *, n_blocks):
    step = pl.program_id(0) * pl.num_programs(1) + pl.program_id(1)
    n_steps = pl.num_programs(0) * pl.num_programs(1)
    slot = step % 2
    first = step == 0

    @pl.when(first)
    def _():
        zero_ref[...] = jnp.zeros_like(zero_ref)

        def zblock(j):
            return pltpu.make_async_copy(zero_ref, _rows(xb_ref, j * EXPERT_BLOCK, EXPERT_BLOCK), zsem)

        def zb_start(j, carry):
            zblock(j).start()
            return carry

        def zb_wait(j, carry):
            zblock(j).wait()
            return carry

        lax.fori_loop(na_ref[0], n_blocks, zb_start, 0)
        lax.fori_loop(na_ref[0], n_blocks, zb_wait, 0)

        for e in range(N_EXPERTS):
            lo = pst_ref[e] + cnt_ref[e]
            nrow = pad_ref[e] - cnt_ref[e]
            nz = nrow // RUN_CHUNK
            lo1 = lo + nz * RUN_CHUNK
            n1 = (nrow - nz * RUN_CHUNK) // ROW_ALIGN

            def zchunk(c):
                return pltpu.make_async_copy(_rows(zero_ref, 0, RUN_CHUNK), _rows(xb_ref, lo + c * RUN_CHUNK, RUN_CHUNK), zsem)

            def zrow(r):
                return pltpu.make_async_copy(_rows(zero_ref, 0, ROW_ALIGN), _rows(xb_ref, lo1 + r * ROW_ALIGN, ROW_ALIGN), zsem)

            def zc_start(c, carry):
                zchunk(c).start()
                return carry

            def zc_wait(c, carry):
                zchunk(c).wait()
                return carry

            def zr_start(r, carry):
                zrow(r).start()
                return carry

            def zr_wait(r, carry):
                zrow(r).wait()
                return carry

            lax.fori_loop(0, nz, zc_start, 0)
            lax.fori_loop(0, n1, zr_start, 0)
            lax.fori_loop(0, nz, zc_wait, 0)
            lax.fori_loop(0, n1, zr_wait, 0)

    srow = lax.broadcasted_iota(I32, (SLOT_ROWS, TM), 0)
    post = post_ref[0]
    perm = jnp.zeros((SLOT_ROWS, TM), F32)
    for k in range(TOP_K):
        perm = jnp.where(srow == post[k:k + 1, :], 1.0, perm)
    srt = jnp.dot(perm.astype(BF16), hf_ref[0], preferred_element_type=F32)
    _to_tiles(srt_ref.at[slot], srt, SLOT_ROWS)

    @pl.when(step > 0)
    def _():
        _run_waits(tot_ref[1 - slot], srt_ref.at[1 - slot], xb_ref, sem.at[1 - slot], True)

    total = _run_copies(meta_ref, pst_ref, srt_ref.at[slot], xb_ref, sem.at[slot], True)
    tot_ref[slot] = total

    @pl.when(step == n_steps - 1)
    def _():
        _run_waits(total, srt_ref.at[slot], xb_ref, sem.at[slot], True)


def _dispatch(pstart, counts, padded, n_act, meta, hf, pos, n_buf):
    B, Tq, D = hf.shape
    nt = Tq // TM
    assert D == SUB * LANES
    grid_spec = pltpu.PrefetchScalarGridSpec(
        num_scalar_prefetch=4,
        grid=(B, nt),
        in_specs=[
            pl.BlockSpec((1, 8, LANES), lambda b, i, *_: (b * nt + i, 0, 0), memory_space=pltpu.SMEM),
            pl.BlockSpec((1, TM, D), lambda b, i, *_: (b, i, 0)),
            pl.BlockSpec((1, 8, TM), lambda b, i, *_: (b * nt + i, 0, 0)),
        ],
        out_specs=pl.BlockSpec(memory_space=pl.ANY),
        scratch_shapes=[pltpu.VMEM((2, SLOT_ROWS * SUB, LANES), F32), pltpu.VMEM((EXPERT_BLOCK * SUB, LANES), F32),
                        pltpu.SMEM((2,), I32), pltpu.SemaphoreType.DMA((2,)), pltpu.SemaphoreType.DMA],
    )
    return pl.pallas_call(
        functools.partial(_disp_kernel, n_blocks=n_buf // EXPERT_BLOCK),
        grid_spec=grid_spec,
        out_shape=jax.ShapeDtypeStruct((n_buf * SUB, LANES), F32),
        compiler_params=_cp(("arbitrary", "arbitrary")),
        name="dispatch",
    )(pstart, counts, padded, n_act, meta, hf, pos)


W_ROWS = 512
W_DELAY = 2


def _exp_kernel(be_ref, bx_ref, na_ref, nx_ref, ps_ref, pr_ref, npv_ref, x_ref, b1a_ref, b1b_ref, b2_ref, w1_hbm, w2_hbm,
                o_ref, st1, st2, w1a_s, w1b_s, w2_s, sem, *, e_off):
    j = pl.program_id(0)
    active = j < na_ref[0]
    e_cur = be_ref[j]
    p = pr_ref[e_cur]
    k = bx_ref[j] - ps_ref[e_cur]
    nxt = nx_ref[e_cur]
    n_prev = npv_ref[e_cur]
    n_slices = st1.shape[0] // W_ROWS

    def copies(e):
        return (pltpu.make_async_copy(w1_hbm.at[e], st1, sem.at[0]), pltpu.make_async_copy(w2_hbm.at[e], st2, sem.at[1]))

    def fetch(e):
        for cp in copies(e):
            cp.start()

    def fetch_wait():
        for cp in copies(0):
            cp.wait()

    def convert(slot, s):
        r0 = pl.multiple_of(s * W_ROWS, W_ROWS)
        lane = lax.broadcasted_iota(I32, (W_ROWS, LANES), 1)
        lo = lane < 64
        idx = jnp.where(lo, 2 * lane, 2 * (lane - 64) + 1)
        for c in range(st1.shape[1] // (2 * LANES)):
            a = st1[pl.ds(r0, W_ROWS), 2 * LANES * c:2 * LANES * c + LANES]
            b = st1[pl.ds(r0, W_ROWS), 2 * LANES * c + LANES:2 * LANES * (c + 1)]
            pa = jnp.take_along_axis(a, idx, axis=1)
            pb = jnp.take_along_axis(b, idx, axis=1)
            ev = jnp.where(lo, pa, pltpu.roll(pb, 64, axis=1))
            od = jnp.where(lo, pltpu.roll(pa, 64, axis=1), pb)
            w1a_s[slot, pl.ds(r0, W_ROWS), LANES * c:LANES * (c + 1)] = ev.astype(BF16)
            w1b_s[slot, pl.ds(r0, W_ROWS), LANES * c:LANES * (c + 1)] = od.astype(BF16)
        w2_s[slot, pl.ds(r0, W_ROWS), :] = st2[pl.ds(r0, W_ROWS), :].astype(BF16)

    def convert_range(slot, s0):
        def one(s, carry):
            convert(slot, s)
            return carry

        lax.fori_loop(s0, n_slices, one, 0)

    @pl.when(j == 0)
    def _():
        fetch(be_ref[0] + e_off)
        fetch_wait()
        convert_range(0, 0)

    @pl.when(active & (k == 0) & (j > 0))
    def _():
        @pl.when(n_prev <= W_DELAY)
        def _():
            fetch_wait()

        convert_range(p, jnp.clip(n_prev - W_DELAY, 0, n_slices))

    @pl.when(active & (k == 0) & (nxt >= 0))
    def _():
        fetch(nxt + e_off)

    conv = active & (nxt >= 0) & (k >= W_DELAY) & (k < W_DELAY + n_slices)

    @pl.when(conv & (k == W_DELAY))
    def _():
        fetch_wait()

    def ffn(with_convert):
        if with_convert:
            convert(1 - p, k - W_DELAY)
        x = _from_tiles(x_ref, EXPERT_BLOCK).astype(BF16)
        ug = jnp.dot(x, w1a_s[p], preferred_element_type=F32) + b1a_ref[0]
        ul = jnp.dot(x, w1b_s[p], preferred_element_type=F32) + b1b_ref[0]
        xg = jnp.minimum(ug, SWIGLU_LIMIT)
        xl = jnp.clip(ul, -SWIGLU_LIMIT, SWIGLU_LIMIT)
        act = xg * jax.nn.sigmoid(SWIGLU_ALPHA * xg) * (xl + 1.0)
        y = jnp.dot(act.astype(BF16), w2_s[p], preferred_element_type=F32) + b2_ref[0]
        _to_tiles(o_ref, y, EXPERT_BLOCK)

    pl.when(conv)(functools.partial(ffn, True))
    pl.when(active & jnp.logical_not(conv))(functools.partial(ffn, False))

    @pl.when(jnp.logical_not(active))
    def _():
        o_ref[...] = jnp.zeros_like(o_ref)


def _experts(sched, xb, lw):
    nb = xb.shape[0] // (EXPERT_BLOCK * SUB)
    De, D = lw["w2"].shape[1:]
    assert De == D and D % W_ROWS == 0
    wmap = lambda j, be, *_: (be[j], 0, 0)
    grid_spec = pltpu.PrefetchScalarGridSpec(
        num_scalar_prefetch=len(sched),
        grid=(nb,),
        in_specs=[
            pl.BlockSpec((EXPERT_BLOCK * SUB, LANES), lambda j, be, bx, *_: (bx[j], 0)),
            pl.BlockSpec((1, 1, De), wmap),
            pl.BlockSpec((1, 1, De), wmap),
            pl.BlockSpec((1, 1, D), wmap),
            pl.BlockSpec(memory_space=pl.ANY),
            pl.BlockSpec(memory_space=pl.ANY),
        ],
        out_specs=pl.BlockSpec((EXPERT_BLOCK * SUB, LANES), lambda j, *_: (j, 0)),
        scratch_shapes=[pltpu.VMEM((D, 2 * De), F32), pltpu.VMEM((De, D), F32),
                        pltpu.VMEM((2, D, De), BF16), pltpu.VMEM((2, D, De), BF16), pltpu.VMEM((2, De, D), BF16),
                        pltpu.SemaphoreType.DMA((2,))],
    )
    return pl.pallas_call(
        functools.partial(_exp_kernel, e_off=lw["e_off"]),
        grid_spec=grid_spec,
        out_shape=jax.ShapeDtypeStruct(xb.shape, F32),
        compiler_params=pltpu.CompilerParams(dimension_semantics=("arbitrary",), vmem_limit_bytes=EXPERT_VMEM_LIMIT),
        name="experts",
    )(*sched, xb, lw["b1a"], lw["b1b"], lw["b2"], lw["w1"], lw["w2"])


def _comb_kernel(pst_ref, meta_ref, meta_next_ref, pos_ref, gt_ref, xn_ref, mod_ref, yb_ref, o_ref, buf_ref, sem):
    step = pl.program_id(0) * pl.num_programs(1) + pl.program_id(1)
    n_steps = pl.num_programs(0) * pl.num_programs(1)
    slot = step % 2

    @pl.when(step == 0)
    def _():
        buf_ref[...] = jnp.zeros_like(buf_ref)
        _run_copies(meta_ref, pst_ref, buf_ref.at[0], yb_ref, sem.at[0], False)

    @pl.when(step < n_steps - 1)
    def _():
        _run_copies(meta_next_ref, pst_ref, buf_ref.at[1 - slot], yb_ref, sem.at[1 - slot], False)

    gt = gt_ref[0]
    g = _slot_matrix(pos_ref[0], [gt[:, k:k + 1] for k in range(TOP_K)])
    _run_waits(_run_total(meta_ref), buf_ref.at[slot], yb_ref, sem.at[slot], False)
    rows = _from_tiles(buf_ref.at[slot], SLOT_ROWS).astype(BF16)
    y = jnp.dot(g, rows, preferred_element_type=F32)
    g2 = mod_ref[0, 0][5:6]
    o_ref[0] = xn_ref[0] + g2 * y


def _combine(pstart, meta, pos, gates, xn, modsel, yb, t_off):
    B, Tq, D = xn.shape
    nt = Tq // TM
    grid_spec = pltpu.PrefetchScalarGridSpec(
        num_scalar_prefetch=1,
        grid=(B, nt),
        in_specs=[
            pl.BlockSpec((1, 8, LANES), lambda b, i, *_: (b * nt + i, 0, 0), memory_space=pltpu.SMEM),
            pl.BlockSpec((1, 8, LANES), lambda b, i, *_: (jnp.minimum(b * nt + i + 1, B * nt - 1), 0, 0),
                         memory_space=pltpu.SMEM),
            pl.BlockSpec((1, TM, LANES), lambda b, i, *_: (b, i, 0)),
            pl.BlockSpec((1, TM, LANES), lambda b, i, *_: (b, i, 0)),
            pl.BlockSpec((1, TM, D), lambda b, i, *_: (b, i, 0)),
            pl.BlockSpec((1, 1, 6, D), lambda b, i, *_: (b, jnp.minimum(i + t_off, 1), 0, 0)),
            pl.BlockSpec(memory_space=pl.ANY),
        ],
        out_specs=pl.BlockSpec((1, TM, D), lambda b, i, *_: (b, i, 0)),
        scratch_shapes=[pltpu.VMEM((2, SLOT_ROWS * SUB, LANES), F32), pltpu.SemaphoreType.DMA((2,))],
    )
    return pl.pallas_call(
        _comb_kernel,
        grid_spec=grid_spec,
        out_shape=jax.ShapeDtypeStruct((B, Tq, D), F32),
        compiler_params=_cp(("arbitrary", "arbitrary")),
        name="combine",
    )(pstart, meta, meta, pos, gates, xn, modsel, yb)


def _cols(w, lo, n):
    return w[..., lo:lo + n]


def _slot64(w, lo):
    partner = [_cols(w, lo + 16, 16), _cols(w, lo, 16), _cols(w, lo + 48, 16), _cols(w, lo + 32, 16)]
    return [_cols(w, lo, 64)] + partner


def _layer_weights(l, a):
    lw = {}
    lw["n1"] = a["norm1_g"][l][None, :]
    lw["n2"] = a["norm2_g"][l][None, :]
    w = a["w_in"][l]
    zeros64 = jnp.zeros(w.shape[:-1] + (64,), w.dtype)
    parts = [_cols(w, 0, 256)] + _slot64(w, 256)
    for g in range(SWA_KV_HEADS):
        parts += _slot64(w, 320 + 64 * g)
    for g in range(SWA_KV_HEADS):
        parts += [_cols(w, 448 + 64 * g, 64), zeros64]
    parts += [_cols(w, 576, 384)]
    for h in range(SWA_HEADS):
        parts += _slot64(w, 960 + 64 * h)
    parts += [_cols(w, 1216, 512)]
    lw["w_in"] = jnp.concatenate(parts, axis=-1).astype(BF16)
    lw["kvg"] = a["mla_kv_norm"][l][None, :]
    lw["qg"] = a["mla_q_norm"][l][None, :]
    w = a["mla_w_uq"][l]
    parts = []
    for h in range(MLA_HEADS):
        parts += [_cols(w, MLA_QK * h, 128)] + _slot64(w, MLA_QK * h + 128)
    lw["w_uq"] = jnp.concatenate(parts, axis=-1).astype(BF16)
    w = a["mla_w_ukv"][l]
    parts = [_cols(w, 256 * h, 128) for h in range(MLA_HEADS)] + [_cols(w, 256 * h + 128, 128) for h in range(MLA_HEADS)]
    lw["w_ukv"] = jnp.concatenate(parts, axis=-1).astype(BF16)
    gq = a["mla_q_head_norm"][l]
    lw["gq"] = (jnp.concatenate([gq[:128]] + _slot64(gq, 128)) * (MLA_QK ** -0.5 * LOG2E))[None, :]
    gk = a["mla_k_head_norm"][l]
    lw["gkn"] = gk[:128][None, :]
    lw["gkp"] = jnp.concatenate(_slot64(gk, 128))[None, :]
    lw["gsq"] = (jnp.concatenate(_slot64(a["swa_q_norm"][l], 0)) * (SWA_HEAD_DIM ** -0.5))[None, :]
    lw["gsk"] = jnp.concatenate(_slot64(a["swa_k_norm"][l], 0))[None, :]
    lw["conv_w"] = a["conv_w"][l]
    lw["conv_b"] = a["conv_b"][l][None, :]
    lw["conv_g"] = a["conv_ln_g"][l][None, :]
    lw["conv_bb"] = a["conv_ln_b"][l][None, :]
    lw["sink"] = a["swa_sink"][l]
    wo = a["w_out"][l]
    lw["w_o1"] = wo[0:512].astype(BF16)
    lw["w_o2"] = wo[512:768].astype(BF16)
    o3 = wo[768:1024].reshape(SWA_HEADS, SWA_HEAD_DIM, -1)
    lw["w_o3"] = jnp.concatenate([o3, jnp.zeros_like(o3)], axis=1).reshape(SWA_HEADS * 128, -1).astype(BF16)
    wr = jnp.pad(a["router_w"][l], ((0, 0), (0, LANES - N_EXPERTS)))
    lw["wr_hi"] = wr.astype(BF16)
    lw["wr_lo"] = (wr - lw["wr_hi"].astype(F32)).astype(BF16)
    lw["br"] = jnp.pad(a["router_b"][l], (0, LANES - N_EXPERTS), constant_values=NEG)[None, :]
    lw["w1"] = a["exp_w1"].reshape((-1,) + a["exp_w1"].shape[2:])
    lw["e_off"] = l * a["exp_w1"].shape[1]
    b1 = a["exp_b1"][l]
    lw["b1a"] = b1[:, None, 0::2]
    lw["b1b"] = b1[:, None, 1::2]
    lw["w2"] = a["exp_w2"].reshape((-1,) + a["exp_w2"].shape[2:])
    lw["b2"] = a["exp_b2"][l][:, None, :]
    return lw


def _rope_tables(n_ctx, n_lat):
    q = MLA_ROPE // 4
    n = jnp.arange(n_lat, dtype=I32)
    row = (n // GRID_W).astype(F32)
    col = (n % GRID_W).astype(F32)
    inv = ROPE_BASE ** (-jnp.arange(q, dtype=F32) / q)
    ang_r = row[:, None] * inv
    ang_c = col[:, None] * inv
    cos = jnp.concatenate([jnp.cos(ang_r), jnp.cos(ang_r), jnp.cos(ang_c), jnp.cos(ang_c)], axis=1)
    sin = jnp.concatenate([-jnp.sin(ang_r), jnp.sin(ang_r), -jnp.sin(ang_c), jnp.sin(ang_c)], axis=1)
    cos = jnp.concatenate([jnp.ones((n_ctx, 64), F32), cos], axis=0)
    sin = jnp.concatenate([jnp.zeros((n_ctx, 64), F32), sin], axis=0)
    z = jnp.zeros_like(cos)
    return jnp.concatenate([cos, z], axis=1), jnp.concatenate([sin, z], axis=1)


def _routing_tables(cnt_f, n_blocks):
    counts = cnt_f[0, :N_EXPERTS].astype(I32)
    padded = (counts + (RUN_CHUNK - 1) + EXPERT_BLOCK - 1) // EXPERT_BLOCK * EXPERT_BLOCK
    padded = jnp.where(counts > 0, padded, 0)
    pend = jnp.cumsum(padded)
    pstart = pend - padded
    n_act = pend[-1] // EXPERT_BLOCK
    blk = jnp.minimum(jnp.arange(n_blocks, dtype=I32), n_act - 1)
    blk_e = jnp.sum((pend[None, :] <= (blk * EXPERT_BLOCK)[:, None]).astype(I32), axis=1)
    blk_e = jnp.minimum(blk_e, N_EXPERTS - 1)
    nbk = padded // EXPERT_BLOCK
    has = nbk > 0
    ids = jnp.arange(N_EXPERTS, dtype=I32)
    later = has[None, :] & (ids[None, :] > ids[:, None])
    nxt_e = jnp.min(jnp.where(later, ids[None, :], N_EXPERTS), axis=1)
    nxt_e = jnp.where(nxt_e == N_EXPERTS, -1, nxt_e)
    earlier = has[None, :] & (ids[None, :] < ids[:, None])
    prv_e = jnp.max(jnp.where(earlier, ids[None, :], -1), axis=1)
    nb_prev_e = jnp.sum(jnp.where(ids[None, :] == prv_e[:, None], nbk[None, :], 0), axis=1)
    set_e = (jnp.cumsum(has.astype(I32)) - 1) % 2
    sched = (blk_e, blk, n_act.reshape(1), nxt_e, pstart // EXPERT_BLOCK, set_e, nb_prev_e)
    sched = tuple(s.astype(I32) for s in sched)
    return counts, padded.astype(I32), pstart.astype(I32), sched


def kernel(x, c, ctx, c_ctx, norm1_g, norm2_g, w_ada, b_ada, w_in, mla_q_norm, mla_kv_norm, mla_w_uq, mla_w_ukv, mla_q_head_norm, mla_k_head_norm, conv_w, conv_b, conv_ln_g, conv_ln_b, swa_q_norm, swa_k_norm, swa_sink, w_out, router_w, router_b, exp_w1, exp_b1, exp_w2, exp_b2):
    a = dict(norm1_g=norm1_g, norm2_g=norm2_g, w_in=w_in, mla_q_norm=mla_q_norm, mla_kv_norm=mla_kv_norm,
             mla_w_uq=mla_w_uq, mla_w_ukv=mla_w_ukv, mla_q_head_norm=mla_q_head_norm, mla_k_head_norm=mla_k_head_norm,
             conv_w=conv_w, conv_b=conv_b, conv_ln_g=conv_ln_g, conv_ln_b=conv_ln_b, swa_q_norm=swa_q_norm,
             swa_k_norm=swa_k_norm, swa_sink=swa_sink, w_out=w_out, router_w=router_w, router_b=router_b,
             exp_w1=exp_w1, exp_b1=exp_b1, exp_w2=exp_w2, exp_b2=exp_b2)
    B, S, D = x.shape
    n_ctx = ctx.shape[1]
    depth = w_ada.shape[0]
    assert n_ctx == TM and S % TM == 0 and B + 1 <= 16
    T = n_ctx + S

    s_in = jnp.zeros((16, D), F32).at[:B].set(c).at[B].set(c_ctx)
    mods = _ada(s_in, w_ada, b_ada)
    cos_t, sin_t = _rope_tables(n_ctx, S)
    tri = jnp.tril(jnp.ones((TM, TM), F32), -1).astype(BF16)
    upper = jnp.triu(jnp.ones((LANES, LANES), F32), 1).astype(BF16)

    first, rest, shift = ctx, x, 1
    for l in range(depth):
        last = l == depth - 1
        t_off = 1 if last else 0
        lw = _layer_weights(l, a)
        m = mods[l].reshape(16, 6, D)
        modsel = jnp.stack([jnp.broadcast_to(m[B], (B, 6, D)), m[:B]], axis=1)

        qm, km, vm, qs, ks, vs, u = _prep(first, rest, shift, modsel, lw, cos_t, sin_t, 0)
        xn, hf, pos, pos_t, gt_o, meta, cnt = _mixers(qm, km, vm, u, qs, ks, vs, first, rest, shift, modsel, lw, tri,
                                                      upper, t_off)

        n_tok = B * (T - t_off * TM)
        nk = n_tok * TOP_K
        n_align = (n_tok // TM) * N_EXPERTS * (ROW_ALIGN - 1)
        n_buf = -(-(nk + n_align + N_EXPERTS * (RUN_CHUNK - 1 + EXPERT_BLOCK - 1)) // EXPERT_BLOCK) * EXPERT_BLOCK
        counts, padded, pstart, sched = _routing_tables(cnt, n_buf // EXPERT_BLOCK)
        xb = _dispatch(pstart, counts, padded, sched[2], meta, hf, pos_t, n_buf)
        yb = _experts(sched, xb, lw)
        xu = _combine(pstart, meta, pos, gt_o, xn, modsel, yb, t_off)
        first, rest, shift = xu, xu, 0
    return xu
```
